```python
import jax, jax.numpy as jnp
from jax import lax
import numpy as np

D_MODEL = 2048
BATCH = 1
SEQ = 8192
DEPTH = 1

GMLP_WIDTH = 1024
GMLP_GROUPS = 8
GMLP_GROUP_DIM = GMLP_WIDTH // GMLP_GROUPS
CHUNK = 128
ATTN_HEADS = 8
HEAD_DIM = 128
ATTN_WIDTH = ATTN_HEADS * HEAD_DIM
Q_LORA_RANK = 512
IDX_HEADS = 16
IDX_HEAD_DIM = 64
IDX_ROPE_DIM = 32
INDEX_TOPK = 256
QUERY_BLOCK = 128
ROPE_THETA = 10000.0
FFN_HIDDEN = -(-8 * D_MODEL // (3 * 256)) * 256
N_MOD = 6
NORM_EPS = 1e-6

COL_UV = 0
COL_QLAT = COL_UV + 2 * GMLP_WIDTH
COL_K = COL_QLAT + Q_LORA_RANK
COL_V = COL_K + ATTN_WIDTH
COL_KIDX = COL_V + ATTN_WIDTH
COL_IDXW = COL_KIDX + IDX_HEAD_DIM
COL_GATE = COL_IDXW + IDX_HEADS
IN_COLS = COL_GATE + 2 * D_MODEL

kernel_name = "hybrid_gmlp_dsa_gated_block"


def rms_norm(x, g):
    xf = x.astype(jnp.float32)
    y = xf * lax.rsqrt(jnp.mean(xf * xf, axis=-1, keepdims=True) + NORM_EPS)
    return (y * g.astype(jnp.float32)).astype(x.dtype)


def layer_norm(x, g, b):
    xf = x.astype(jnp.float32)
    mu = jnp.mean(xf, axis=-1, keepdims=True)
    xc = xf - mu
    var = jnp.mean(xc * xc, axis=-1, keepdims=True)
    return (xc * lax.rsqrt(var + NORM_EPS) * g.astype(jnp.float32) + b.astype(jnp.float32)).astype(x.dtype)


def rope_tables(seq, dim):
    inv = 1.0 / (ROPE_THETA ** (jnp.arange(0, dim, 2, dtype=jnp.float32) / dim))
    ang = jnp.arange(seq, dtype=jnp.float32)[:, None] * inv[None, :]
    return jnp.cos(ang), jnp.sin(ang)


def apply_rope(x, cos, sin):
    xf = x.astype(jnp.float32)
    half = xf.shape[-1] // 2
    x1, x2 = xf[..., :half], xf[..., half:]
    c = cos[None, :, None, :]
    s = sin[None, :, None, :]
    return jnp.concatenate([x1 * c - x2 * s, x2 * c + x1 * s], axis=-1).astype(x.dtype)


def apply_partial_rope(x, cos, sin):
    return jnp.concatenate([apply_rope(x[..., :IDX_ROPE_DIM], cos, sin), x[..., IDX_ROPE_DIM:]], axis=-1)


def gmlp_spatial_gating(uv, ln_g, ln_b, w_s, b_s):
    B, S, _ = uv.shape
    z = jax.nn.gelu(uv)
    u, v = z[..., :GMLP_WIDTH], z[..., GMLP_WIDTH:]
    v = layer_norm(v, ln_g, ln_b)
    v = v.reshape(B, S // CHUNK, CHUNK, GMLP_GROUPS, GMLP_GROUP_DIM)
    causal = jnp.tril(jnp.ones((CHUNK, CHUNK), dtype=bool))
    w = jnp.where(causal[None], w_s, 0).astype(v.dtype)
    sv = jnp.einsum('gts,bcsgd->bctgd', w, v) + b_s.T.astype(v.dtype)[None, None, :, :, None]
    return u * sv.reshape(B, S, GMLP_WIDTH)


def dsa_attention(q, k, v, q_idx, k_idx, idx_w):
    B, S, H, Dh = q.shape
    topk = min(INDEX_TOPK, S // 4)
    n_blocks = S // QUERY_BLOCK
    key_pos = jnp.arange(S)
    scale = HEAD_DIM ** -0.5
    gather = jax.vmap(lambda kk, ii: kk[ii])

    def one_block(i):
        t0 = i * QUERY_BLOCK
        qb = lax.dynamic_slice_in_dim(q, t0, QUERY_BLOCK, axis=1)
        qib = lax.dynamic_slice_in_dim(q_idx, t0, QUERY_BLOCK, axis=1)
        wb = lax.dynamic_slice_in_dim(idx_w, t0, QUERY_BLOCK, axis=1)
        q_pos = t0 + jnp.arange(QUERY_BLOCK)
        causal = key_pos[None, :] <= q_pos[:, None]
        logits = jnp.einsum('bqhd,bsd->bqhs', qib, k_idx).astype(jnp.float32)
        score = jnp.einsum('bqhs,bqh->bqs', jax.nn.relu(logits), wb.astype(jnp.float32))
        score = jnp.where(causal[None], score, -jnp.inf)
        _, sel = lax.top_k(score, topk)
        valid = sel <= q_pos[None, :, None]
        k_sel = gather(k, sel)
        v_sel = gather(v, sel)
        att = jnp.einsum('bqhd,bqkhd->bhqk', qb, k_sel).astype(jnp.float32) * scale
        att = jnp.where(valid[:, None], att, -jnp.inf)
        p = jax.nn.softmax(att, axis=-1).astype(v.dtype)
        return jnp.einsum('bhqk,bqkhd->bqhd', p, v_sel)

    out = lax.map(one_block, jnp.arange(n_blocks))
    return out.transpose(1, 0, 2, 3, 4).reshape(B, S, H * Dh)


def setup_inputs(seed: int = 0) -> dict:
    key = jax.random.key(seed)
    ks = iter(jax.random.split(key, 32))
    f32 = jnp.float32

    def nrm(shape, scale):
        return jax.random.normal(next(ks), shape, f32) * scale

    def gain(shape):
        return 1.0 + 0.1 * jax.random.normal(next(ks), shape, f32)

    L = DEPTH
    return {
        "x": nrm((BATCH, SEQ, D_MODEL), 1.0),
        "c": nrm((BATCH, D_MODEL), 1.0),
        "w_mod": nrm((L, D_MODEL, N_MOD * D_MODEL), 0.5 * D_MODEL ** -0.5),
        "b_mod": nrm((L, N_MOD * D_MODEL), 0.01),
        "g_pre_mix": gain((L, D_MODEL)),
        "g_post_mix": gain((L, D_MODEL)),
        "w_in": nrm((L, D_MODEL, IN_COLS), D_MODEL ** -0.5),
        "gmlp_ln_g": gain((L, GMLP_WIDTH)),
        "gmlp_ln_b": nrm((L, GMLP_WIDTH), 0.02),
        "gmlp_w_s": nrm((L, GMLP_GROUPS, CHUNK, CHUNK), CHUNK ** -0.5),
        "gmlp_b_s": gain((L, GMLP_GROUPS, CHUNK)),
        "q_lat_norm_g": gain((L, Q_LORA_RANK)),
        "w_q_up": nrm((L, Q_LORA_RANK, ATTN_WIDTH), Q_LORA_RANK ** -0.5),
        "w_qidx_up": nrm((L, Q_LORA_RANK, IDX_HEADS * IDX_HEAD_DIM), Q_LORA_RANK ** -0.5),
        "kidx_ln_g": gain((L, IDX_HEAD_DIM)),
        "kidx_ln_b": nrm((L, IDX_HEAD_DIM), 0.02),
        "w_proj_a": nrm((L, GMLP_WIDTH, D_MODEL), GMLP_WIDTH ** -0.5),
        "w_proj_b": nrm((L, ATTN_WIDTH, D_MODEL), ATTN_WIDTH ** -0.5),
        "w_out": nrm((L, D_MODEL, D_MODEL), D_MODEL ** -0.5),
        "g_pre_ffn": gain((L, D_MODEL)),
        "g_post_ffn": gain((L, D_MODEL)),
        "w_ffn_gate": nrm((L, D_MODEL, FFN_HIDDEN), D_MODEL ** -0.5),
        "w_ffn_up": nrm((L, D_MODEL, FFN_HIDDEN), D_MODEL ** -0.5),
        "w_ffn_down": nrm((L, FFN_HIDDEN, D_MODEL), FFN_HIDDEN ** -0.5),
    }


def reference(x, c, w_mod, b_mod, g_pre_mix, g_post_mix, w_in, gmlp_ln_g, gmlp_ln_b, gmlp_w_s, gmlp_b_s,
              q_lat_norm_g, w_q_up, w_qidx_up, kidx_ln_g, kidx_ln_b, w_proj_a, w_proj_b, w_out,
              g_pre_ffn, g_post_ffn, w_ffn_gate, w_ffn_up, w_ffn_down):
    B, S, D = x.shape
    cos_a, sin_a = rope_tables(S, HEAD_DIM)
    cos_i, sin_i = rope_tables(S, IDX_ROPE_DIM)
    idx_w_scale = (IDX_HEADS ** -0.5) * (IDX_HEAD_DIM ** -0.5)

    for l in range(DEPTH):
        mod = (c @ w_mod[l] + b_mod[l]).reshape(B, N_MOD, D)
        shift_m, scale_m, gate_m = mod[:, 0, None], mod[:, 1, None], mod[:, 2, None]
        shift_f, scale_f, gate_f = mod[:, 3, None], mod[:, 4, None], mod[:, 5, None]

        h = rms_norm(x, g_pre_mix[l]) * (1 + scale_m) + shift_m
        proj = h @ w_in[l]

        y_a = gmlp_spatial_gating(proj[..., COL_UV:COL_QLAT], gmlp_ln_g[l], gmlp_ln_b[l],
                                  gmlp_w_s[l], gmlp_b_s[l])

        q_lat = rms_norm(proj[..., COL_QLAT:COL_K], q_lat_norm_g[l])
        q = apply_rope((q_lat @ w_q_up[l]).reshape(B, S, ATTN_HEADS, HEAD_DIM), cos_a, sin_a)
        k = apply_rope(proj[..., COL_K:COL_V].reshape(B, S, ATTN_HEADS, HEAD_DIM), cos_a, sin_a)
        v = proj[..., COL_V:COL_KIDX].reshape(B, S, ATTN_HEADS, HEAD_DIM)
        q_idx = apply_partial_rope((q_lat @ w_qidx_up[l]).reshape(B, S, IDX_HEADS, IDX_HEAD_DIM), cos_i, sin_i)
        k_idx = layer_norm(proj[..., COL_KIDX:COL_IDXW], kidx_ln_g[l], kidx_ln_b[l])
        k_idx = apply_partial_rope(k_idx[:, :, None, :], cos_i, sin_i)[:, :, 0, :]
        idx_w = proj[..., COL_IDXW:COL_GATE] * idx_w_scale
        y_b = dsa_attention(q, k, v, q_idx, k_idx, idx_w)

        gates = jax.nn.sigmoid(proj[..., COL_GATE:])
        merged = gates[..., :D] * (y_a @ w_proj_a[l]) + gates[..., D:] * (y_b @ w_proj_b[l])
        mix_out = merged @ w_out[l]
        x = x + gate_m * rms_norm(mix_out, g_post_mix[l])

        h2 = rms_norm(x, g_pre_ffn[l]) * (1 + scale_f) + shift_f
        f = (jax.nn.silu(h2 @ w_ffn_gate[l]) * (h2 @ w_ffn_up[l])) @ w_ffn_down[l]
        x = x + gate_f * rms_norm(f, g_post_ffn[l])

    return x
```

```python
import functools

import jax
import jax.numpy as jnp
from jax import lax
from jax.experimental import pallas as pl
from jax.experimental.pallas import tpu as pltpu

F32 = jnp.float32
BF16 = jnp.bfloat16

D_MODEL = 2048
GMLP_WIDTH = 1024
GMLP_GROUPS = 8
CHUNK = 128
ATTN_HEADS = 8
HEAD_DIM = 128
ATTN_WIDTH = ATTN_HEADS * HEAD_DIM
Q_LORA_RANK = 512
IDX_HEADS = 16
IDX_HEAD_DIM = 64
IDX_ROPE_DIM = 32
INDEX_TOPK = 256
ROPE_THETA = 10000.0
N_MOD = 6
NORM_EPS = 1e-6

COL_QLAT = 2 * GMLP_WIDTH
COL_K = COL_QLAT + Q_LORA_RANK
COL_V = COL_K + ATTN_WIDTH
COL_KIDX = COL_V + ATTN_WIDTH
COL_IDXW = COL_KIDX + IDX_HEAD_DIM
COL_GATE = COL_IDXW + IDX_HEADS

LANES = 128
VMEM_LIMIT = 56 * 1024 * 1024

MASK_BIAS = -1e30
INT_MIN = -2 ** 31
KEY_NEG_INF = -2139095041

IDX_TQ = 128
KEY_CHUNK = 512
ATT_TQ = 256


def _params(sem=None):
    return pltpu.CompilerParams(dimension_semantics=sem, vmem_limit_bytes=VMEM_LIMIT)


def _mod_body(cb_ref, w_ref, b_ref, o_ref):
    cb = cb_ref[...]
    tn = o_ref.shape[1]
    parts = [jnp.sum(w_ref[:, p * LANES:(p + 1) * LANES] * cb, axis=0, keepdims=True)
             for p in range(tn // LANES)]
    o_ref[...] = jnp.concatenate(parts, axis=1) + b_ref[...]


def _modulation(c, w_mod, b_mod):
    k, n = w_mod.shape
    tn = 1024
    cb = jnp.broadcast_to(c.reshape(k, 1), (k, LANES))
    return pl.pallas_call(
        _mod_body,
        grid=(n // tn,),
        in_specs=[pl.BlockSpec((k, LANES), lambda j: (0, 0)),
                  pl.BlockSpec((k, tn), lambda j: (0, j)),
                  pl.BlockSpec((1, tn), lambda j: (0, j))],
        out_specs=pl.BlockSpec((1, tn), lambda j: (0, j)),
        out_shape=jax.ShapeDtypeStruct((1, n), F32),
        compiler_params=_params(("arbitrary",)),
        name="modulation",
    )(cb, w_mod, b_mod.reshape(1, n))


def _rms(x, g):
    return x * lax.rsqrt(jnp.mean(x * x, axis=-1, keepdims=True) + NORM_EPS) * g


def _prenorm_body(x_ref, g_ref, scale_ref, shift_ref, o_ref):
    h = _rms(x_ref[...], g_ref[...]) * (1.0 + scale_ref[...]) + shift_ref[...]
    o_ref[...] = h.astype(o_ref.dtype)


def _prenorm(x, g, scale, shift):
    m, d = x.shape
    tm = 512
    row = pl.BlockSpec((1, d), lambda i: (0, 0))
    return pl.pallas_call(
        _prenorm_body,
        grid=(m // tm,),
        in_specs=[pl.BlockSpec((tm, d), lambda i: (i, 0)), row, row, row],
        out_specs=pl.BlockSpec((tm, d), lambda i: (i, 0)),
        out_shape=jax.ShapeDtypeStruct((m, d), BF16),
        compiler_params=_params(("arbitrary",)),
        name="prenorm",
    )(x, g, scale, shift)


def _mm_body(*refs, n_x, pairs, n_extra, epilogue):
    x_refs = refs[:n_x]
    w_refs = refs[n_x:n_x + len(pairs)]
    e_refs = refs[n_x + len(pairs):n_x + len(pairs) + n_extra]
    o_refs = refs[n_x + len(pairs) + n_extra:]
    accs = [jnp.dot(x_refs[xi][...], w_ref[...], preferred_element_type=F32)
            for xi, w_ref in zip(pairs, w_refs)]
    epilogue(accs, e_refs, o_refs)


def _matmul(xs, ws, pairs, extras, outs, epilogue, *, tm, tn, name):
    m, _ = xs[0].shape
    n = ws[0].shape[1]
    in_specs = [pl.BlockSpec((tm, x.shape[1]), lambda i, j: (i, 0)) for x in xs]
    in_specs += [pl.BlockSpec((w.shape[0], tn), lambda i, j: (0, j)) for w in ws]
    in_specs += [pl.BlockSpec(bs, im) for _, bs, im in extras]
    body = functools.partial(_mm_body, n_x=len(xs), pairs=tuple(pairs), n_extra=len(extras),
                             epilogue=epilogue)
    return pl.pallas_call(
        body,
        grid=(m // tm, n // tn),
        in_specs=in_specs,
        out_specs=[pl.BlockSpec(bs, im) for _, _, bs, im in outs],
        out_shape=[jax.ShapeDtypeStruct(s, d) for s, d, _, _ in outs],
        compiler_params=_params(("arbitrary", "arbitrary")),
        name=name,
    )(*xs, *ws, *[a for a, _, _ in extras])


def _tile(i, j):
    return (i, j)


def _rowblk(i, j):
    return (i, 0)


def _colblk(i, j):
    return (0, j)


def _ep_gelu(accs, e_refs, o_refs):
    o_refs[0][...] = jax.nn.gelu(accs[0]).astype(o_refs[0].dtype)


def _ep_gelu_layernorm(accs, e_refs, o_refs):
    z = jax.nn.gelu(accs[0])
    mu = jnp.mean(z, axis=-1, keepdims=True)
    zc = z - mu
    var = jnp.mean(zc * zc, axis=-1, keepdims=True)
    y = zc * lax.rsqrt(var + NORM_EPS) * e_refs[0][...] + e_refs[1][...]
    o_refs[0][...] = y.astype(o_refs[0].dtype)


def _ep_rmsnorm(accs, e_refs, o_refs):
    o_refs[0][...] = _rms(accs[0], e_refs[0][...]).astype(o_refs[0].dtype)


def _ep_cast(accs, e_refs, o_refs):
    o_refs[0][...] = accs[0].astype(o_refs[0].dtype)


def _ep_sigmoid(accs, e_refs, o_refs):
    o_refs[0][...] = jax.nn.sigmoid(accs[0]).astype(o_refs[0].dtype)


def _ep_rope(accs, e_refs, o_refs, *, scale):
    cc = e_refs[0][...]
    ss = e_refs[1][...]
    acc = accs[0]
    for h in range(acc.shape[1] // HEAD_DIM):
        xh = acc[:, h * HEAD_DIM:(h + 1) * HEAD_DIM]
        r = xh * cc + pltpu.roll(xh, HEAD_DIM // 2, 1) * ss
        if scale != 1.0:
            r = r * scale
        o_refs[0][:, h * HEAD_DIM:(h + 1) * HEAD_DIM] = r.astype(o_refs[0].dtype)


def _partial_rope(x, c, a, b):
    half = IDX_ROPE_DIM // 2
    return x * c + pltpu.roll(x, LANES - half, 1) * a + pltpu.roll(x, half, 1) * b


def _ep_partial_rope(accs, e_refs, o_refs):
    c, a, b = e_refs[0][...], e_refs[1][...], e_refs[2][...]
    acc = accs[0]
    for p in range(acc.shape[1] // LANES):
        xp = acc[:, p * LANES:(p + 1) * LANES]
        o_refs[0][:, p * LANES:(p + 1) * LANES] = _partial_rope(xp, c, a, b).astype(o_refs[0].dtype)


def _ep_index_keys(accs, e_refs, o_refs, *, w_scale):
    g, bb = e_refs[0][...], e_refs[1][...]
    c, a, b = e_refs[2][...], e_refs[3][...], e_refs[4][...]
    acc = accs[0]
    lane = lax.broadcasted_iota(jnp.int32, acc.shape, 1)
    is_key = lane < IDX_HEAD_DIM
    mu = jnp.sum(jnp.where(is_key, acc, 0.0), axis=-1, keepdims=True) / IDX_HEAD_DIM
    xc = jnp.where(is_key, acc - mu, 0.0)
    var = jnp.sum(xc * xc, axis=-1, keepdims=True) / IDX_HEAD_DIM
    y = xc * lax.rsqrt(var + NORM_EPS) * g + bb
    y = _partial_rope(y, c, a, b)
    o_refs[0][...] = jnp.where(is_key, y, acc * w_scale)


def _ep_merge(accs, e_refs, o_refs):
    o = e_refs[0][...].astype(F32) * accs[0] + e_refs[1][...].astype(F32) * accs[1]
    o_refs[0][...] = o.astype(o_refs[0].dtype)


def _ep_mix_residual(accs, e_refs, o_refs):
    x_ref, g_post, gate, g_pre, scale, shift = e_refs
    x1 = x_ref[...] + gate[...] * _rms(accs[0], g_post[...])
    o_refs[0][...] = x1
    h2 = _rms(x1, g_pre[...]) * (1.0 + scale[...]) + shift[...]
    o_refs[1][...] = h2.astype(o_refs[1].dtype)


def _ep_swiglu(accs, e_refs, o_refs):
    o_refs[0][...] = (jax.nn.silu(accs[0]) * accs[1]).astype(o_refs[0].dtype)


def _simple_mm(x, w, epilogue, extras, out_dtype, *, tm, tn, name):
    m, n = x.shape[0], w.shape[1]
    return _matmul([x], [w], [0], extras, [((m, n), out_dtype, (tm, tn), _tile)], epilogue,
                   tm=tm, tn=tn, name=name)[0]


def _gmlp_body(u_ref, v_ref, w_ref, b_ref, o_ref):
    t = lax.broadcasted_iota(jnp.int32, (CHUNK, CHUNK), 0)
    s = lax.broadcasted_iota(jnp.int32, (CHUNK, CHUNK), 1)
    causal = s <= t
    for g in range(GMLP_GROUPS):
        w = jnp.where(causal, w_ref[g], 0.0).astype(BF16)
        cols = slice(g * LANES, (g + 1) * LANES)
        bias = b_ref[:, cols]
        for c in range(u_ref.shape[0] // CHUNK):
            rows = slice(c * CHUNK, (c + 1) * CHUNK)
            sv = jnp.dot(w, v_ref[rows, cols], preferred_element_type=F32) + bias
            o_ref[rows, cols] = (u_ref[rows, cols].astype(F32) * sv).astype(o_ref.dtype)


def _gmlp(u, vn, w_s, bias):
    m, n = u.shape
    tm = 512
    blk = pl.BlockSpec((tm, n), lambda i: (i, 0))
    return pl.pallas_call(
        _gmlp_body,
        grid=(m // tm,),
        in_specs=[blk, blk,
                  pl.BlockSpec(w_s.shape, lambda i: (0, 0, 0)),
                  pl.BlockSpec(bias.shape, lambda i: (0, 0))],
        out_specs=blk,
        out_shape=jax.ShapeDtypeStruct((m, n), BF16),
        compiler_params=_params(("arbitrary",)),
        name="gmlp_gating",
    )(u, vn, w_s, bias)


def _indexer_body(q_ref, w_ref, kt_ref, bias_ref, key_ref, wb_ref):
    tq = q_ref.shape[0]
    n_chunks_total, _, kc = bias_ref.shape
    i = pl.program_id(0)
    t0 = i * tq
    n_chunks = (t0 + tq - 1) // kc + 1
    sub = kc // LANES

    for h in range(IDX_HEADS):
        wb_ref[h] = jnp.broadcast_to(w_ref[:, h:h + 1], (tq, LANES))

    q_pos = t0 + lax.broadcasted_iota(jnp.int32, (tq, kc), 0)
    k_off = lax.broadcasted_iota(jnp.int32, (tq, kc), 1)

    heads_per_group = LANES // IDX_HEAD_DIM

    def score_chunk(j, carry):
        k0 = j * kc
        parts = [jnp.zeros((tq, LANES), F32) for _ in range(sub)]
        for h in range(IDX_HEADS):
            grp, r = divmod(h, heads_per_group)
            logit = jnp.dot(q_ref[:, grp * LANES:(grp + 1) * LANES], kt_ref[j, r],
                            preferred_element_type=F32)
            wbh = wb_ref[h]
            for p in range(sub):
                parts[p] = parts[p] + jnp.maximum(logit[:, p * LANES:(p + 1) * LANES], 0.0) * wbh
        score = jnp.concatenate(parts, axis=1)
        score = jnp.where(k0 + k_off <= q_pos, score, -jnp.inf)
        bits = lax.bitcast_convert_type(score, jnp.int32)
        key_ref[j] = bits ^ ((bits >> 31) & jnp.int32(0x7FFFFFFF))
        return carry

    lax.fori_loop(0, n_chunks, score_chunk, 0)

    def search_bit(b, prefix):
        cand = prefix ^ jnp.left_shift(jnp.int32(1), 31 - b)
        cand_b = jnp.broadcast_to(cand, (tq, LANES))

        def count_chunk(j, acc):
            keys = key_ref[j]
            for p in range(sub):
                acc = acc + jnp.where(keys[:, p * LANES:(p + 1) * LANES] >= cand_b, 1.0, 0.0)
            return acc

        acc = lax.fori_loop(0, n_chunks, count_chunk, jnp.zeros((tq, LANES), F32))
        total = jnp.sum(acc, axis=1, keepdims=True)
        return jnp.where(total >= float(INDEX_TOPK), cand, prefix)

    kth = lax.fori_loop(0, 32, search_bit, jnp.full((tq, 1), INT_MIN, jnp.int32))
    kth = jnp.maximum(kth, KEY_NEG_INF + 1)
    kth_b = jnp.broadcast_to(kth, (tq, kc))

    def write_chunk(j, carry):
        bias_ref[j] = jnp.where(key_ref[j] >= kth_b, 0.0, MASK_BIAS).astype(bias_ref.dtype)
        return carry

    lax.fori_loop(0, n_chunks, write_chunk, 0)

    def fill_chunk(j, carry):
        bias_ref[j] = jnp.full((tq, kc), MASK_BIAS, bias_ref.dtype)
        return carry

    lax.fori_loop(n_chunks, n_chunks_total, fill_chunk, 0)


def _indexer_mask(q_idx, idx_w, k_idx_t):
    s = q_idx.shape[0]
    tq, kc = IDX_TQ, KEY_CHUNK
    return pl.pallas_call(
        _indexer_body,
        grid=(s // tq,),
        in_specs=[pl.BlockSpec((tq, q_idx.shape[1]), lambda i: (i, 0)),
                  pl.BlockSpec((tq, IDX_HEADS), lambda i: (i, 0)),
                  pl.BlockSpec(k_idx_t.shape, lambda i: (0, 0, 0, 0))],
        out_specs=pl.BlockSpec((s // kc, tq, kc), lambda i: (0, i, 0)),
        out_shape=jax.ShapeDtypeStruct((s // kc, s, kc), BF16),
        scratch_shapes=[pltpu.VMEM((s // kc, tq, kc), jnp.int32),
                        pltpu.VMEM((IDX_HEADS, tq, LANES), F32)],
        compiler_params=_params(("arbitrary",)),
        name="indexer_mask",
    )(q_idx, idx_w, k_idx_t)


def _attention_body(q_ref, k_ref, v_ref, bias_ref, o_ref, m_ref, l_ref, acc_ref):
    tq = q_ref.shape[0]
    kc = bias_ref.shape[2]
    i = pl.program_id(0)
    n_chunks = (i * tq + tq - 1) // kc + 1
    sub = kc // LANES

    m_ref[...] = jnp.full(m_ref.shape, MASK_BIAS, F32)
    l_ref[...] = jnp.zeros(l_ref.shape, F32)
    acc_ref[...] = jnp.zeros(acc_ref.shape, F32)

    def chunk(j, carry):
        k0 = pl.multiple_of(j * kc, kc)
        bias = bias_ref[j].astype(F32)
        for h in range(ATTN_HEADS):
            cols = slice(h * HEAD_DIM, (h + 1) * HEAD_DIM)
            s = lax.dot_general(q_ref[:, cols], k_ref[pl.ds(k0, kc), cols],
                                (((1,), (1,)), ((), ())), preferred_element_type=F32) + bias
            m_prev = m_ref[h]
            m_next = jnp.maximum(m_prev, jnp.max(s, axis=1, keepdims=True))
            p = jnp.exp(s - jnp.concatenate([m_next] * sub, axis=1))
            alpha = jnp.exp(m_prev - m_next)
            l_ref[h] = alpha * l_ref[h] + jnp.sum(p, axis=1, keepdims=True)
            m_ref[h] = m_next
            pv = jnp.dot(p.astype(v_ref.dtype), v_ref[pl.ds(k0, kc), cols],
                         preferred_element_type=F32)
            acc_ref[h] = alpha * acc_ref[h] + pv
        return carry

    lax.fori_loop(0, n_chunks, chunk, 0)

    for h in range(ATTN_HEADS):
        o_ref[:, h * HEAD_DIM:(h + 1) * HEAD_DIM] = (acc_ref[h] / l_ref[h]).astype(o_ref.dtype)


def _attention(q, k, v, bias):
    s, width = q.shape
    tq = ATT_TQ
    n_chunks, _, kc = bias.shape
    resident = dict(pipeline_mode=pl.Buffered(1))
    return pl.pallas_call(
        _attention_body,
        grid=(s // tq,),
        in_specs=[pl.BlockSpec((tq, width), lambda i: (i, 0)),
                  pl.BlockSpec((s, width), lambda i: (0, 0), **resident),
                  pl.BlockSpec((s, width), lambda i: (0, 0), **resident),
                  pl.BlockSpec((n_chunks, tq, kc), lambda i: (0, i, 0))],
        out_specs=pl.BlockSpec((tq, width), lambda i: (i, 0)),
        out_shape=jax.ShapeDtypeStruct((s, width), BF16),
        scratch_shapes=[pltpu.VMEM((ATTN_HEADS, tq, LANES), F32),
                        pltpu.VMEM((ATTN_HEADS, tq, LANES), F32),
                        pltpu.VMEM((ATTN_HEADS, tq, HEAD_DIM), F32)],
        compiler_params=_params(("arbitrary",)),
        name="masked_attention",
    )(q, k, v, bias)


def _ffn_down_body(a_ref, w_ref, x_ref, g_ref, gate_ref, o_ref, acc_ref):
    k = pl.program_id(1)

    @pl.when(k == 0)
    def _():
        acc_ref[...] = jnp.zeros(acc_ref.shape, F32)

    acc_ref[...] += jnp.dot(a_ref[...], w_ref[...], preferred_element_type=F32)

    @pl.when(k == pl.num_programs(1) - 1)
    def _():
        o_ref[...] = x_ref[...] + gate_ref[...] * _rms(acc_ref[...], g_ref[...])


def _ffn_down(a, w, x1, g_post, gate):
    m, kdim = a.shape
    d = w.shape[1]
    tm, tk = 512, kdim // 4
    row = pl.BlockSpec((1, d), lambda i, k: (0, 0))
    return pl.pallas_call(
        _ffn_down_body,
        grid=(m // tm, kdim // tk),
        in_specs=[pl.BlockSpec((tm, tk), lambda i, k: (i, k)),
                  pl.BlockSpec((tk, d), lambda i, k: (k, 0)),
                  pl.BlockSpec((tm, d), lambda i, k: (i, 0)), row, row],
        out_specs=pl.BlockSpec((tm, d), lambda i, k: (i, 0)),
        out_shape=jax.ShapeDtypeStruct((m, d), F32),
        scratch_shapes=[pltpu.VMEM((tm, d), F32)],
        compiler_params=_params(("arbitrary", "arbitrary")),
        name="ffn_down_residual",
    )(a, w, x1, g_post, gate)


def _rope_angles(seq, dim):
    inv = 1.0 / (ROPE_THETA ** (jnp.arange(0, dim, 2, dtype=F32) / dim))
    ang = jnp.arange(seq, dtype=F32)[:, None] * inv[None, :]
    return jnp.cos(ang), jnp.sin(ang)


def _attn_rope_tables(seq):
    cos, sin = _rope_angles(seq, HEAD_DIM)
    return jnp.concatenate([cos, cos], axis=1), jnp.concatenate([-sin, sin], axis=1)


def _index_rope_tables(seq):
    cos, sin = _rope_angles(seq, IDX_ROPE_DIM)
    half = IDX_ROPE_DIM // 2
    rest = IDX_HEAD_DIM - IDX_ROPE_DIM
    zeros = lambda n: jnp.zeros((seq, n), F32)
    c = jnp.concatenate([cos, cos, jnp.ones((seq, rest), F32)], axis=1)
    a = jnp.concatenate([-sin, zeros(IDX_HEAD_DIM - half)], axis=1)
    b = jnp.concatenate([zeros(half), sin, zeros(rest)], axis=1)
    rep = LANES // IDX_HEAD_DIM
    return tuple(jnp.tile(t, (1, rep)) for t in (c, a, b))


def _layer(x, mod, g_pre_mix, g_post_mix, w_in, gmlp_ln_g, gmlp_ln_b, gmlp_w_s, gmlp_b_s,
           q_lat_norm_g, w_q_up, w_qidx_up, kidx_ln_g, kidx_ln_b, w_proj_a, w_proj_b, w_out,
           g_pre_ffn, g_post_ffn, w_ffn_gate, w_ffn_up, w_ffn_down, tables):
    s, d = x.shape
    cc, ss, ic, ia, ib = tables
    row = lambda a: a.reshape(1, -1)
    shift_m, scale_m, gate_m, shift_f, scale_f, gate_f = (mod[:, n * d:(n + 1) * d] for n in range(N_MOD))

    w_u = w_in[:, :GMLP_WIDTH].astype(BF16)
    w_v = w_in[:, GMLP_WIDTH:COL_QLAT].astype(BF16)
    w_ql = w_in[:, COL_QLAT:COL_K].astype(BF16)
    w_k = w_in[:, COL_K:COL_V].astype(BF16)
    w_val = w_in[:, COL_V:COL_KIDX].astype(BF16)
    w_ix = jnp.pad(w_in[:, COL_KIDX:COL_GATE], ((0, 0), (0, LANES - (COL_GATE - COL_KIDX)))).astype(BF16)
    w_gate = w_in[:, COL_GATE:].astype(BF16)

    h = _prenorm(x, row(g_pre_mix), scale_m, shift_m)

    tm = 1024
    rope_extras = [(cc, (tm, HEAD_DIM), _rowblk), (ss, (tm, HEAD_DIM), _rowblk)]
    irope_extras = [(t, (tm, LANES), _rowblk) for t in (ic, ia, ib)]

    u = _simple_mm(h, w_u, _ep_gelu, [], BF16, tm=tm, tn=512, name="in_proj_u")
    vn = _simple_mm(h, w_v, _ep_gelu_layernorm,
                    [(row(gmlp_ln_g), (1, GMLP_WIDTH), _colblk), (row(gmlp_ln_b), (1, GMLP_WIDTH), _colblk)],
                    BF16, tm=512, tn=GMLP_WIDTH, name="in_proj_v_ln")
    gmlp_bias = jnp.repeat(gmlp_b_s.T, LANES, axis=1)
    y_a = _gmlp(u, vn, gmlp_w_s, gmlp_bias)

    q_lat = _simple_mm(h, w_ql, _ep_rmsnorm, [(row(q_lat_norm_g), (1, Q_LORA_RANK), _colblk)],
                       BF16, tm=tm, tn=Q_LORA_RANK, name="in_proj_qlat")
    q = _simple_mm(q_lat, w_q_up.astype(BF16), functools.partial(_ep_rope, scale=HEAD_DIM ** -0.5),
                   rope_extras, BF16, tm=tm, tn=512, name="q_up_rope")
    k = _simple_mm(h, w_k, functools.partial(_ep_rope, scale=1.0), rope_extras, BF16,
                   tm=tm, tn=512, name="in_proj_k_rope")
    v = _simple_mm(h, w_val, _ep_cast, [], BF16, tm=tm, tn=512, name="in_proj_v")
    q_idx = _simple_mm(q_lat, w_qidx_up.astype(BF16), _ep_partial_rope, irope_extras, BF16,
                       tm=tm, tn=512, name="qidx_up_rope")
    pad = lambda a: jnp.pad(a, (0, LANES - a.shape[0])).reshape(1, LANES)
    idx_w_scale = (IDX_HEADS ** -0.5) * (IDX_HEAD_DIM ** -0.5)
    kx = _simple_mm(h, w_ix, functools.partial(_ep_index_keys, w_scale=idx_w_scale),
                    [(pad(kidx_ln_g), (1, LANES), _colblk), (pad(kidx_ln_b), (1, LANES), _colblk)] + irope_extras,
                    F32, tm=tm, tn=LANES, name="in_proj_index_keys")
    kt = kx[:, :IDX_HEAD_DIM].astype(BF16).reshape(s // KEY_CHUNK, KEY_CHUNK, IDX_HEAD_DIM).transpose(0, 2, 1)
    zeros = jnp.zeros_like(kt)
    k_idx_t = jnp.stack([jnp.concatenate([kt, zeros], axis=1), jnp.concatenate([zeros, kt], axis=1)], axis=1)
    idx_w = kx[:, IDX_HEAD_DIM:IDX_HEAD_DIM + IDX_HEADS]
    bias = _indexer_mask(q_idx, idx_w, k_idx_t)
    y_b = _attention(q, k, v, bias)

    gates = _simple_mm(h, w_gate, _ep_sigmoid, [], BF16, tm=tm, tn=512, name="in_proj_gates")
    n_gate_blocks = d // 512
    merged = _matmul([y_a, y_b], [w_proj_a.astype(BF16), w_proj_b.astype(BF16)], [0, 1],
                     [(gates, (tm, 512), _tile),
                      (gates, (tm, 512), lambda i, j: (i, j + n_gate_blocks))],
                     [((s, d), BF16, (tm, 512), _tile)], _ep_merge, tm=tm, tn=512, name="branch_merge")[0]
    tm_full = 256
    vec = lambda a: (a, (1, d), _colblk)
    x1, h2 = _matmul([merged], [w_out.astype(BF16)], [0],
                     [(x, (tm_full, d), _rowblk), vec(row(g_post_mix)), vec(gate_m),
                      vec(row(g_pre_ffn)), vec(scale_f), vec(shift_f)],
                     [((s, d), F32, (tm_full, d), _tile), ((s, d), BF16, (tm_full, d), _tile)],
                     _ep_mix_residual, tm=tm_full, tn=d, name="out_proj_residual")

    ffn_hidden = w_ffn_gate.shape[1]
    act = _matmul([h2], [w_ffn_gate.astype(BF16), w_ffn_up.astype(BF16)], [0, 0], [],
                  [((s, ffn_hidden), BF16, (tm, 512), _tile)], _ep_swiglu, tm=tm, tn=512, name="ffn_up")[0]
    return _ffn_down(act, w_ffn_down.astype(BF16), x1, row(g_post_ffn), gate_f)


def kernel(x, c, w_mod, b_mod, g_pre_mix, g_post_mix, w_in, gmlp_ln_g, gmlp_ln_b, gmlp_w_s, gmlp_b_s, q_lat_norm_g, w_q_up, w_qidx_up, kidx_ln_g, kidx_ln_b, w_proj_a, w_proj_b, w_out, g_pre_ffn, g_post_ffn, w_ffn_gate, w_ffn_up, w_ffn_down):
    batch, seq, d = x.shape
    assert batch == 1 and d == D_MODEL
    tables = _attn_rope_tables(seq) + _index_rope_tables(seq)
    y = x[0]
    for l in range(w_mod.shape[0]):
        mod = _modulation(c, w_mod[l], b_mod[l])
        y = _layer(y, mod, g_pre_mix[l], g_post_mix[l], w_in[l], gmlp_ln_g[l], gmlp_ln_b[l],
                   gmlp_w_s[l], gmlp_b_s[l], q_lat_norm_g[l], w_q_up[l], w_qidx_up[l],
                   kidx_ln_g[l], kidx_ln_b[l], w_proj_a[l], w_proj_b[l], w_out[l],
                   g_pre_ffn[l], g_post_ffn[l], w_ffn_gate[l], w_ffn_up[l], w_ffn_down[l], tables)
    return y[None]
```

```python
import functools

import jax
import jax.numpy as jnp
from jax import lax
from jax.experimental import pallas as pl
from jax.experimental.pallas import tpu as pltpu

F32 = jnp.float32
BF16 = jnp.bfloat16

D_MODEL = 2048
GMLP_WIDTH = 1024
GMLP_GROUPS = 8
CHUNK = 128
ATTN_HEADS = 8
HEAD_DIM = 128
ATTN_WIDTH = ATTN_HEADS * HEAD_DIM
Q_LORA_RANK = 512
IDX_HEADS = 16
IDX_HEAD_DIM = 64
IDX_ROPE_DIM = 32
INDEX_TOPK = 256
ROPE_THETA = 10000.0
N_MOD = 6
NORM_EPS = 1e-6

COL_QLAT = 2 * GMLP_WIDTH
COL_K = COL_QLAT + Q_LORA_RANK
COL_V = COL_K + ATTN_WIDTH
COL_KIDX = COL_V + ATTN_WIDTH
COL_IDXW = COL_KIDX + IDX_HEAD_DIM
COL_GATE = COL_IDXW + IDX_HEADS

LANES = 128
VMEM_LIMIT = 56 * 1024 * 1024

MASK_BIAS = -1e30
INT_MIN = -2 ** 31
KEY_NEG_INF = -2139095041

IDX_TQ = 128
KEY_CHUNK = 512
ATT_TQ = 256


def _params(sem=None):
    return pltpu.CompilerParams(dimension_semantics=sem, vmem_limit_bytes=VMEM_LIMIT)


def _mod_body(cb_ref, w_ref, b_ref, o_ref):
    cb = cb_ref[...]
    tn = o_ref.shape[1]
    parts = [jnp.sum(w_ref[:, p * LANES:(p + 1) * LANES] * cb, axis=0, keepdims=True)
             for p in range(tn // LANES)]
    o_ref[...] = jnp.concatenate(parts, axis=1) + b_ref[...]


def _modulation(c, w_mod, b_mod):
    k, n = w_mod.shape
    tn = 1024
    cb = jnp.broadcast_to(c.reshape(k, 1), (k, LANES))
    return pl.pallas_call(
        _mod_body,
        grid=(n // tn,),
        in_specs=[pl.BlockSpec((k, LANES), lambda j: (0, 0)),
                  pl.BlockSpec((k, tn), lambda j: (0, j)),
                  pl.BlockSpec((1, tn), lambda j: (0, j))],
        out_specs=pl.BlockSpec((1, tn), lambda j: (0, j)),
        out_shape=jax.ShapeDtypeStruct((1, n), F32),
        compiler_params=_params(("arbitrary",)),
        name="modulation",
    )(cb, w_mod, b_mod.reshape(1, n))


def _rms(x, g):
    return x * lax.rsqrt(jnp.mean(x * x, axis=-1, keepdims=True) + NORM_EPS) * g


def _prenorm_body(x_ref, g_ref, scale_ref, shift_ref, o_ref):
    h = _rms(x_ref[...], g_ref[...]) * (1.0 + scale_ref[...]) + shift_ref[...]
    o_ref[...] = h.astype(o_ref.dtype)


def _prenorm(x, g, scale, shift):
    m, d = x.shape
    tm = 512
    row = pl.BlockSpec((1, d), lambda i: (0, 0))
    return pl.pallas_call(
        _prenorm_body,
        grid=(m // tm,),
        in_specs=[pl.BlockSpec((tm, d), lambda i: (i, 0)), row, row, row],
        out_specs=pl.BlockSpec((tm, d), lambda i: (i, 0)),
        out_shape=jax.ShapeDtypeStruct((m, d), BF16),
        compiler_params=_params(("arbitrary",)),
        name="prenorm",
    )(x, g, scale, shift)


def _mm_body(*refs, n_x, pairs, n_extra, n_out, epilogue):
    n_w = len(pairs)
    x_refs = refs[:n_x]
    w_refs = refs[n_x:n_x + n_w]
    e_refs = refs[n_x + n_w:n_x + n_w + n_extra]
    o_refs = refs[n_x + n_w + n_extra:n_x + n_w + n_extra + n_out]
    wb_refs = refs[n_x + n_w + n_extra + n_out:]

    @pl.when(pl.program_id(1) == 0)
    def _():
        for w_ref, wb_ref in zip(w_refs, wb_refs):
            wb_ref[...] = w_ref[...].astype(wb_ref.dtype)

    accs = [jnp.dot(x_refs[xi][...], wb_ref[...], preferred_element_type=F32)
            for xi, wb_ref in zip(pairs, wb_refs)]
    epilogue(accs, e_refs, o_refs)


def _matmul(xs, ws, pairs, extras, outs, epilogue, *, n, tm, tn, name):
    m = xs[0].shape[0]
    n_col = n // tn
    w_mode = dict(pipeline_mode=pl.Buffered(1)) if n_col == 1 else {}
    in_specs = [pl.BlockSpec((tm, x.shape[1]), lambda j, i: (i, 0)) for x in xs]
    in_specs += [pl.BlockSpec((w.shape[0], tn), functools.partial(_w_index, first=first), **w_mode)
                 for w, first in ws]
    in_specs += [pl.BlockSpec(bs, im) for _, bs, im in extras]
    body = functools.partial(_mm_body, n_x=len(xs), pairs=tuple(pairs), n_extra=len(extras),
                             n_out=len(outs), epilogue=epilogue)
    return pl.pallas_call(
        body,
        grid=(n_col, m // tm),
        in_specs=in_specs,
        out_specs=[pl.BlockSpec(bs, im) for _, _, bs, im in outs],
        out_shape=[jax.ShapeDtypeStruct(s, d) for s, d, _, _ in outs],
        scratch_shapes=[pltpu.VMEM((w.shape[0], tn), BF16) for w, _ in ws],
        compiler_params=_params(("arbitrary", "arbitrary")),
        name=name,
    )(*xs, *[w for w, _ in ws], *[a for a, _, _ in extras])


def _w_index(j, i, *, first):
    return (0, first + j)


def _tile(j, i):
    return (i, j)


def _rowblk(j, i):
    return (i, 0)


def _colblk(j, i):
    return (0, j)


def _ep_gelu(accs, e_refs, o_refs):
    o_refs[0][...] = jax.nn.gelu(accs[0]).astype(o_refs[0].dtype)


def _ep_gelu_layernorm(accs, e_refs, o_refs):
    z = jax.nn.gelu(accs[0])
    mu = jnp.mean(z, axis=-1, keepdims=True)
    zc = z - mu
    var = jnp.mean(zc * zc, axis=-1, keepdims=True)
    y = zc * lax.rsqrt(var + NORM_EPS) * e_refs[0][...] + e_refs[1][...]
    o_refs[0][...] = y.astype(o_refs[0].dtype)


def _ep_rmsnorm(accs, e_refs, o_refs):
    o_refs[0][...] = _rms(accs[0], e_refs[0][...]).astype(o_refs[0].dtype)


def _ep_cast(accs, e_refs, o_refs):
    o_refs[0][...] = accs[0].astype(o_refs[0].dtype)


def _ep_sigmoid(accs, e_refs, o_refs):
    o_refs[0][...] = jax.nn.sigmoid(accs[0]).astype(o_refs[0].dtype)


def _ep_rope(accs, e_refs, o_refs, *, scale):
    cc = e_refs[0][...]
    ss = e_refs[1][...]
    acc = accs[0]
    for h in range(acc.shape[1] // HEAD_DIM):
        xh = acc[:, h * HEAD_DIM:(h + 1) * HEAD_DIM]
        r = xh * cc + pltpu.roll(xh, HEAD_DIM // 2, 1) * ss
        if scale != 1.0:
            r = r * scale
        o_refs[0][:, h * HEAD_DIM:(h + 1) * HEAD_DIM] = r.astype(o_refs[0].dtype)


def _partial_rope(x, c, a, b):
    half = IDX_ROPE_DIM // 2
    return x * c + pltpu.roll(x, LANES - half, 1) * a + pltpu.roll(x, half, 1) * b


def _ep_partial_rope(accs, e_refs, o_refs):
    c, a, b = e_refs[0][...], e_refs[1][...], e_refs[2][...]
    acc = accs[0]
    for p in range(acc.shape[1] // LANES):
        xp = acc[:, p * LANES:(p + 1) * LANES]
        o_refs[0][:, p * LANES:(p + 1) * LANES] = _partial_rope(xp, c, a, b).astype(o_refs[0].dtype)


def _ep_index_keys(accs, e_refs, o_refs, *, w_scale):
    g, bb = e_refs[0][...], e_refs[1][...]
    c, a, b = e_refs[2][...], e_refs[3][...], e_refs[4][...]
    acc = accs[0]
    lane = lax.broadcasted_iota(jnp.int32, acc.shape, 1)
    is_key = lane < IDX_HEAD_DIM
    mu = jnp.sum(jnp.where(is_key, acc, 0.0), axis=-1, keepdims=True) / IDX_HEAD_DIM
    xc = jnp.where(is_key, acc - mu, 0.0)
    var = jnp.sum(xc * xc, axis=-1, keepdims=True) / IDX_HEAD_DIM
    y = xc * lax.rsqrt(var + NORM_EPS) * g + bb
    y = _partial_rope(y, c, a, b)
    o_refs[0][...] = jnp.where(is_key, y, acc * w_scale)


def _ep_merge(accs, e_refs, o_refs):
    o = e_refs[0][...].astype(F32) * accs[0] + e_refs[1][...].astype(F32) * accs[1]
    o_refs[0][...] = o.astype(o_refs[0].dtype)


def _ep_mix_residual(accs, e_refs, o_refs):
    x_ref, g_post, gate, g_pre, scale, shift = e_refs
    x1 = x_ref[...] + gate[...] * _rms(accs[0], g_post[...])
    o_refs[0][...] = x1
    h2 = _rms(x1, g_pre[...]) * (1.0 + scale[...]) + shift[...]
    o_refs[1][...] = h2.astype(o_refs[1].dtype)


def _ep_swiglu(accs, e_refs, o_refs):
    o_refs[0][...] = (jax.nn.silu(accs[0]) * accs[1]).astype(o_refs[0].dtype)


def _simple_mm(x, w, epilogue, extras, out_dtype, *, n, tm, tn, name, first_col=0):
    m = x.shape[0]
    return _matmul([x], [(w, first_col // tn)], [0], extras, [((m, n), out_dtype, (tm, tn), _tile)],
                   epilogue, n=n, tm=tm, tn=tn, name=name)[0]


def _gmlp_body(u_ref, v_ref, w_ref, b_ref, o_ref):
    t = lax.broadcasted_iota(jnp.int32, (CHUNK, CHUNK), 0)
    s = lax.broadcasted_iota(jnp.int32, (CHUNK, CHUNK), 1)
    causal = s <= t
    for g in range(GMLP_GROUPS):
        w = jnp.where(causal, w_ref[g], 0.0).astype(BF16)
        cols = slice(g * LANES, (g + 1) * LANES)
        bias = b_ref[:, cols]
        for c in range(u_ref.shape[0] // CHUNK):
            rows = slice(c * CHUNK, (c + 1) * CHUNK)
            sv = jnp.dot(w, v_ref[rows, cols], preferred_element_type=F32) + bias
            o_ref[rows, cols] = (u_ref[rows, cols].astype(F32) * sv).astype(o_ref.dtype)


def _gmlp(u, vn, w_s, bias):
    m, n = u.shape
    tm = 512
    blk = pl.BlockSpec((tm, n), lambda i: (i, 0))
    return pl.pallas_call(
        _gmlp_body,
        grid=(m // tm,),
        in_specs=[blk, blk,
                  pl.BlockSpec(w_s.shape, lambda i: (0, 0, 0)),
                  pl.BlockSpec(bias.shape, lambda i: (0, 0))],
        out_specs=blk,
        out_shape=jax.ShapeDtypeStruct((m, n), BF16),
        compiler_params=_params(("arbitrary",)),
        name="gmlp_gating",
    )(u, vn, w_s, bias)


def _indexer_body(q_ref, w_ref, kt_ref, bias_ref, key_ref, wb_ref):
    tq = q_ref.shape[0]
    n_chunks_total, _, kc = bias_ref.shape
    i = pl.program_id(0)
    t0 = i * tq
    n_chunks = (t0 + tq - 1) // kc + 1
    sub = kc // LANES

    for h in range(IDX_HEADS):
        wb_ref[h] = jnp.broadcast_to(w_ref[:, h:h + 1], (tq, LANES))

    q_pos = t0 + lax.broadcasted_iota(jnp.int32, (tq, kc), 0)
    k_off = lax.broadcasted_iota(jnp.int32, (tq, kc), 1)

    heads_per_group = LANES // IDX_HEAD_DIM

    def score_chunk(j, carry):
        k0 = j * kc
        parts = [jnp.zeros((tq, LANES), F32) for _ in range(sub)]
        for h in range(IDX_HEADS):
            grp, r = divmod(h, heads_per_group)
            logit = jnp.dot(q_ref[:, grp * LANES:(grp + 1) * LANES], kt_ref[j, r],
                            preferred_element_type=F32)
            wbh = wb_ref[h]
            for p in range(sub):
                parts[p] = parts[p] + jnp.maximum(logit[:, p * LANES:(p + 1) * LANES], 0.0) * wbh
        score = jnp.concatenate(parts, axis=1)
        score = jnp.where(k0 + k_off <= q_pos, score, -jnp.inf)
        bits = lax.bitcast_convert_type(score, jnp.int32)
        key_ref[j] = bits ^ ((bits >> 31) & jnp.int32(0x7FFFFFFF))
        return carry

    lax.fori_loop(0, n_chunks, score_chunk, 0)

    def search_bit(b, prefix):
        cand = prefix ^ jnp.left_shift(jnp.int32(1), 31 - b)
        cand_b = jnp.broadcast_to(cand, (tq, LANES))

        def count_chunk(j, acc):
            keys = key_ref[j]
            for p in range(sub):
                acc = acc + jnp.where(keys[:, p * LANES:(p + 1) * LANES] >= cand_b, 1.0, 0.0)
            return acc

        acc = lax.fori_loop(0, n_chunks, count_chunk, jnp.zeros((tq, LANES), F32))
        total = jnp.sum(acc, axis=1, keepdims=True)
        return jnp.where(total >= float(INDEX_TOPK), cand, prefix)

    kth = lax.fori_loop(0, 32, search_bit, jnp.full((tq, 1), INT_MIN, jnp.int32))
    kth = jnp.maximum(kth, KEY_NEG_INF + 1)
    kth_b = jnp.broadcast_to(kth, (tq, kc))

    def write_chunk(j, carry):
        bias_ref[j] = jnp.where(key_ref[j] >= kth_b, 0.0, MASK_BIAS).astype(bias_ref.dtype)
        return carry

    lax.fori_loop(0, n_chunks, write_chunk, 0)

    def fill_chunk(j, carry):
        bias_ref[j] = jnp.full((tq, kc), MASK_BIAS, bias_ref.dtype)
        return carry

    lax.fori_loop(n_chunks, n_chunks_total, fill_chunk, 0)


def _indexer_mask(q_idx, idx_w, k_idx_t):
    s = q_idx.shape[0]
    tq, kc = IDX_TQ, KEY_CHUNK
    return pl.pallas_call(
        _indexer_body,
        grid=(s // tq,),
        in_specs=[pl.BlockSpec((tq, q_idx.shape[1]), lambda i: (i, 0)),
                  pl.BlockSpec((tq, IDX_HEADS), lambda i: (i, 0)),
                  pl.BlockSpec(k_idx_t.shape, lambda i: (0, 0, 0, 0))],
        out_specs=pl.BlockSpec((s // kc, tq, kc), lambda i: (0, i, 0)),
        out_shape=jax.ShapeDtypeStruct((s // kc, s, kc), BF16),
        scratch_shapes=[pltpu.VMEM((s // kc, tq, kc), jnp.int32),
                        pltpu.VMEM((IDX_HEADS, tq, LANES), F32)],
        compiler_params=_params(("arbitrary",)),
        name="indexer_mask",
    )(q_idx, idx_w, k_idx_t)


def _attention_body(q_ref, k_ref, v_ref, bias_ref, o_ref, m_ref, l_ref, acc_ref):
    tq = q_ref.shape[0]
    kc = bias_ref.shape[2]
    i = pl.program_id(0)
    n_chunks = (i * tq + tq - 1) // kc + 1
    sub = kc // LANES

    m_ref[...] = jnp.full(m_ref.shape, MASK_BIAS, F32)
    l_ref[...] = jnp.zeros(l_ref.shape, F32)
    acc_ref[...] = jnp.zeros(acc_ref.shape, F32)

    def chunk(j, carry):
        k0 = pl.multiple_of(j * kc, kc)
        bias = bias_ref[j].astype(F32)
        for h in range(ATTN_HEADS):
            cols = slice(h * HEAD_DIM, (h + 1) * HEAD_DIM)
            s = lax.dot_general(q_ref[:, cols], k_ref[pl.ds(k0, kc), cols],
                                (((1,), (1,)), ((), ())), preferred_element_type=F32) + bias
            m_prev = m_ref[h]
            m_next = jnp.maximum(m_prev, jnp.max(s, axis=1, keepdims=True))
            p = jnp.exp(s - jnp.concatenate([m_next] * sub, axis=1))
            alpha = jnp.exp(m_prev - m_next)
            l_ref[h] = alpha * l_ref[h] + jnp.sum(p, axis=1, keepdims=True)
            m_ref[h] = m_next
            pv = jnp.dot(p.astype(v_ref.dtype), v_ref[pl.ds(k0, kc), cols],
                         preferred_element_type=F32)
            acc_ref[h] = alpha * acc_ref[h] + pv
        return carry

    lax.fori_loop(0, n_chunks, chunk, 0)

    for h in range(ATTN_HEADS):
        o_ref[:, h * HEAD_DIM:(h + 1) * HEAD_DIM] = (acc_ref[h] / l_ref[h]).astype(o_ref.dtype)


def _attention(q, k, v, bias):
    s, width = q.shape
    tq = ATT_TQ
    n_chunks, _, kc = bias.shape
    resident = dict(pipeline_mode=pl.Buffered(1))
    return pl.pallas_call(
        _attention_body,
        grid=(s // tq,),
        in_specs=[pl.BlockSpec((tq, width), lambda i: (i, 0)),
                  pl.BlockSpec((s, width), lambda i: (0, 0), **resident),
                  pl.BlockSpec((s, width), lambda i: (0, 0), **resident),
                  pl.BlockSpec((n_chunks, tq, kc), lambda i: (0, i, 0))],
        out_specs=pl.BlockSpec((tq, width), lambda i: (i, 0)),
        out_shape=jax.ShapeDtypeStruct((s, width), BF16),
        scratch_shapes=[pltpu.VMEM((ATTN_HEADS, tq, LANES), F32),
                        pltpu.VMEM((ATTN_HEADS, tq, LANES), F32),
                        pltpu.VMEM((ATTN_HEADS, tq, HEAD_DIM), F32)],
        compiler_params=_params(("arbitrary",)),
        name="masked_attention",
    )(q, k, v, bias)


def _ffn_down_body(a_ref, w_ref, x_ref, g_ref, gate_ref, o_ref, acc_ref):
    k = pl.program_id(1)

    @pl.when(k == 0)
    def _():
        acc_ref[...] = jnp.zeros(acc_ref.shape, F32)

    acc_ref[...] += jnp.dot(a_ref[...], w_ref[...], preferred_element_type=F32)

    @pl.when(k == pl.num_programs(1) - 1)
    def _():
        o_ref[...] = x_ref[...] + gate_ref[...] * _rms(acc_ref[...], g_ref[...])


def _ffn_down(a, w, x1, g_post, gate):
    m, kdim = a.shape
    d = w.shape[1]
    tm, tk = 512, kdim // 4
    row = pl.BlockSpec((1, d), lambda i, k: (0, 0))
    return pl.pallas_call(
        _ffn_down_body,
        grid=(m // tm, kdim // tk),
        in_specs=[pl.BlockSpec((tm, tk), lambda i, k: (i, k)),
                  pl.BlockSpec((tk, d), lambda i, k: (k, 0)),
                  pl.BlockSpec((tm, d), lambda i, k: (i, 0)), row, row],
        out_specs=pl.BlockSpec((tm, d), lambda i, k: (i, 0)),
        out_shape=jax.ShapeDtypeStruct((m, d), F32),
        scratch_shapes=[pltpu.VMEM((tm, d), F32)],
        compiler_params=_params(("arbitrary", "arbitrary")),
        name="ffn_down_residual",
    )(a, w, x1, g_post, gate)


def _rope_angles(seq, dim):
    inv = 1.0 / (ROPE_THETA ** (jnp.arange(0, dim, 2, dtype=F32) / dim))
    ang = jnp.arange(seq, dtype=F32)[:, None] * inv[None, :]
    return jnp.cos(ang), jnp.sin(ang)


def _attn_rope_tables(seq):
    cos, sin = _rope_angles(seq, HEAD_DIM)
    return jnp.concatenate([cos, cos], axis=1), jnp.concatenate([-sin, sin], axis=1)


def _index_rope_tables(seq):
    cos, sin = _rope_angles(seq, IDX_ROPE_DIM)
    half = IDX_ROPE_DIM // 2
    rest = IDX_HEAD_DIM - IDX_ROPE_DIM
    zeros = lambda n: jnp.zeros((seq, n), F32)
    c = jnp.concatenate([cos, cos, jnp.ones((seq, rest), F32)], axis=1)
    a = jnp.concatenate([-sin, zeros(IDX_HEAD_DIM - half)], axis=1)
    b = jnp.concatenate([zeros(half), sin, zeros(rest)], axis=1)
    rep = LANES // IDX_HEAD_DIM
    return tuple(jnp.tile(t, (1, rep)) for t in (c, a, b))


def _layer(x, mod, g_pre_mix, g_post_mix, w_in, gmlp_ln_g, gmlp_ln_b, gmlp_w_s, gmlp_b_s,
           q_lat_norm_g, w_q_up, w_qidx_up, kidx_ln_g, kidx_ln_b, w_proj_a, w_proj_b, w_out,
           g_pre_ffn, g_post_ffn, w_ffn_gate, w_ffn_up, w_ffn_down, tables):
    s, d = x.shape
    cc, ss, ic, ia, ib = tables
    row = lambda a: a.reshape(1, -1)
    shift_m, scale_m, gate_m, shift_f, scale_f, gate_f = (mod[:, n * d:(n + 1) * d] for n in range(N_MOD))

    h = _prenorm(x, row(g_pre_mix), scale_m, shift_m)

    tm = 1024
    rope_extras = [(cc, (tm, HEAD_DIM), _rowblk), (ss, (tm, HEAD_DIM), _rowblk)]
    irope_extras = [(t, (tm, LANES), _rowblk) for t in (ic, ia, ib)]

    u = _simple_mm(h, w_in, _ep_gelu, [], BF16, n=GMLP_WIDTH, tm=tm, tn=512, name="in_proj_u")
    vn = _simple_mm(h, w_in, _ep_gelu_layernorm,
                    [(row(gmlp_ln_g), (1, GMLP_WIDTH), _colblk), (row(gmlp_ln_b), (1, GMLP_WIDTH), _colblk)],
                    BF16, n=GMLP_WIDTH, first_col=GMLP_WIDTH, tm=512, tn=GMLP_WIDTH, name="in_proj_v_ln")
    gmlp_bias = jnp.repeat(gmlp_b_s.T, LANES, axis=1)
    y_a = _gmlp(u, vn, gmlp_w_s, gmlp_bias)

    q_lat = _simple_mm(h, w_in, _ep_rmsnorm, [(row(q_lat_norm_g), (1, Q_LORA_RANK), _colblk)],
                       BF16, n=Q_LORA_RANK, first_col=COL_QLAT, tm=tm, tn=Q_LORA_RANK, name="in_proj_qlat")
    q = _simple_mm(q_lat, w_q_up, functools.partial(_ep_rope, scale=HEAD_DIM ** -0.5),
                   rope_extras, BF16, n=ATTN_WIDTH, tm=tm, tn=512, name="q_up_rope")
    k = _simple_mm(h, w_in, functools.partial(_ep_rope, scale=1.0), rope_extras, BF16,
                   n=ATTN_WIDTH, first_col=COL_K, tm=tm, tn=512, name="in_proj_k_rope")
    v = _simple_mm(h, w_in, _ep_cast, [], BF16, n=ATTN_WIDTH, first_col=COL_V, tm=tm, tn=512,
                   name="in_proj_v")
    q_idx = _simple_mm(q_lat, w_qidx_up, _ep_partial_rope, irope_extras, BF16,
                       n=IDX_HEADS * IDX_HEAD_DIM, tm=tm, tn=512, name="qidx_up_rope")
    pad = lambda a: jnp.pad(a, (0, LANES - a.shape[0])).reshape(1, LANES)
    idx_w_scale = (IDX_HEADS ** -0.5) * (IDX_HEAD_DIM ** -0.5)
    kx = _simple_mm(h, w_in, functools.partial(_ep_index_keys, w_scale=idx_w_scale),
                    [(pad(kidx_ln_g), (1, LANES), _colblk), (pad(kidx_ln_b), (1, LANES), _colblk)] + irope_extras,
                    F32, n=LANES, first_col=COL_KIDX, tm=tm, tn=LANES, name="in_proj_index_keys")
    kt = kx[:, :IDX_HEAD_DIM].astype(BF16).reshape(s // KEY_CHUNK, KEY_CHUNK, IDX_HEAD_DIM).transpose(0, 2, 1)
    zeros = jnp.zeros_like(kt)
    k_idx_t = jnp.stack([jnp.concatenate([kt, zeros], axis=1), jnp.concatenate([zeros, kt], axis=1)], axis=1)
    idx_w = kx[:, IDX_HEAD_DIM:IDX_HEAD_DIM + IDX_HEADS]
    bias = _indexer_mask(q_idx, idx_w, k_idx_t)
    y_b = _attention(q, k, v, bias)

    gates = _simple_mm(h, w_in[:, COL_GATE:], _ep_sigmoid, [], BF16, n=2 * d, tm=tm, tn=512,
                       name="in_proj_gates")
    n_gate_blocks = d // 512
    merged = _matmul([y_a, y_b], [(w_proj_a, 0), (w_proj_b, 0)], [0, 1],
                     [(gates, (tm, 512), _tile),
                      (gates, (tm, 512), lambda j, i: (i, j + n_gate_blocks))],
                     [((s, d), BF16, (tm, 512), _tile)], _ep_merge, n=d, tm=tm, tn=512,
                     name="branch_merge")[0]
    tm_full = 256
    vec = lambda a: (a, (1, d), _colblk)
    x1, h2 = _matmul([merged], [(w_out, 0)], [0],
                     [(x, (tm_full, d), _rowblk), vec(row(g_post_mix)), vec(gate_m),
                      vec(row(g_pre_ffn)), vec(scale_f), vec(shift_f)],
                     [((s, d), F32, (tm_full, d), _tile), ((s, d), BF16, (tm_full, d), _tile)],
                     _ep_mix_residual, n=d, tm=tm_full, tn=d, name="out_proj_residual")

    ffn_hidden = w_ffn_gate.shape[1]
    act = _matmul([h2], [(w_ffn_gate, 0), (w_ffn_up, 0)], [0, 0], [],
                  [((s, ffn_hidden), BF16, (tm, 512), _tile)], _ep_swiglu, n=ffn_hidden, tm=tm, tn=512,
                  name="ffn_up")[0]
    return _ffn_down(act, w_ffn_down.astype(BF16), x1, row(g_post_ffn), gate_f)


def kernel(x, c, w_mod, b_mod, g_pre_mix, g_post_mix, w_in, gmlp_ln_g, gmlp_ln_b, gmlp_w_s, gmlp_b_s, q_lat_norm_g, w_q_up, w_qidx_up, kidx_ln_g, kidx_ln_b, w_proj_a, w_proj_b, w_out, g_pre_ffn, g_post_ffn, w_ffn_gate, w_ffn_up, w_ffn_down):
    batch, seq, d = x.shape
    assert batch == 1 and d == D_MODEL
    tables = _attn_rope_tables(seq) + _index_rope_tables(seq)
    y = x[0]
    for l in range(w_mod.shape[0]):
        mod = _modulation(c, w_mod[l], b_mod[l])
        y = _layer(y, mod, g_pre_mix[l], g_post_mix[l], w_in[l], gmlp_ln_g[l], gmlp_ln_b[l],
                   gmlp_w_s[l], gmlp_b_s[l], q_lat_norm_g[l], w_q_up[l], w_qidx_up[l],
                   kidx_ln_g[l], kidx_ln_b[l], w_proj_a[l], w_proj_b[l], w_out[l],
                   g_pre_ffn[l], g_post_ffn[l], w_ffn_gate[l], w_ffn_up[l], w_ffn_down[l], tables)
    return y[None]
```

```python
import functools
import math

import jax
import jax.numpy as jnp
from jax import lax
from jax.experimental import pallas as pl
from jax.experimental.pallas import tpu as pltpu

F32 = jnp.float32
BF16 = jnp.bfloat16

D_MODEL = 2048
GMLP_WIDTH = 1024
GMLP_GROUPS = 8
CHUNK = 128
ATTN_HEADS = 8
HEAD_DIM = 128
ATTN_WIDTH = ATTN_HEADS * HEAD_DIM
Q_LORA_RANK = 512
IDX_HEADS = 16
IDX_HEAD_DIM = 64
IDX_ROPE_DIM = 32
INDEX_TOPK = 256
ROPE_THETA = 10000.0
N_MOD = 6
NORM_EPS = 1e-6

COL_QLAT = 2 * GMLP_WIDTH
COL_K = COL_QLAT + Q_LORA_RANK
COL_V = COL_K + ATTN_WIDTH
COL_KIDX = COL_V + ATTN_WIDTH
COL_IDXW = COL_KIDX + IDX_HEAD_DIM
COL_GATE = COL_IDXW + IDX_HEADS

LANES = 128
SUBLANES = 8
VMEM_LIMIT = 56 * 1024 * 1024

MASK_BIAS = -1e30
INT_MIN = -2 ** 31
KEY_NEG_INF = -2139095041

SEL_TQ = 256
IDX_KC = 128
ATT_KC = 256


def _params(sem=None):
    return pltpu.CompilerParams(dimension_semantics=sem, vmem_limit_bytes=VMEM_LIMIT)


def _mod_body(cb_ref, w_ref, b_ref, o_ref):
    cb = cb_ref[...]
    tn = o_ref.shape[1]
    parts = [jnp.sum(w_ref[:, p * LANES:(p + 1) * LANES] * cb, axis=0, keepdims=True)
             for p in range(tn // LANES)]
    o_ref[...] = jnp.concatenate(parts, axis=1) + b_ref[...]


def _modulation(c, w_mod, b_mod):
    k, n = w_mod.shape
    tn = 1024
    cb = jnp.broadcast_to(c.reshape(k, 1), (k, LANES))
    return pl.pallas_call(
        _mod_body,
        grid=(n // tn,),
        in_specs=[pl.BlockSpec((k, LANES), lambda j: (0, 0)),
                  pl.BlockSpec((k, tn), lambda j: (0, j)),
                  pl.BlockSpec((1, tn), lambda j: (0, j))],
        out_specs=pl.BlockSpec((1, tn), lambda j: (0, j)),
        out_shape=jax.ShapeDtypeStruct((1, n), F32),
        compiler_params=_params(("arbitrary",)),
        name="modulation",
    )(cb, w_mod, b_mod.reshape(1, n))


def _rms(x, g):
    return x * lax.rsqrt(jnp.mean(x * x, axis=-1, keepdims=True) + NORM_EPS) * g


def _prenorm_body(x_ref, g_ref, scale_ref, shift_ref, o_ref):
    h = _rms(x_ref[...], g_ref[...]) * (1.0 + scale_ref[...]) + shift_ref[...]
    o_ref[...] = h.astype(o_ref.dtype)


def _prenorm(x, g, scale, shift):
    m, d = x.shape
    tm = 512
    row = pl.BlockSpec((1, d), lambda i: (0, 0))
    return pl.pallas_call(
        _prenorm_body,
        grid=(m // tm,),
        in_specs=[pl.BlockSpec((tm, d), lambda i: (i, 0)), row, row, row],
        out_specs=pl.BlockSpec((tm, d), lambda i: (i, 0)),
        out_shape=jax.ShapeDtypeStruct((m, d), BF16),
        compiler_params=_params(("arbitrary",)),
        name="prenorm",
    )(x, g, scale, shift)


def _mm_body(*refs, n_x, pairs, w_kn, out_t, n_extra, n_out, epilogue):
    n_w = len(pairs)
    x_refs = refs[:n_x]
    w_refs = refs[n_x:n_x + n_w]
    e_refs = refs[n_x + n_w:n_x + n_w + n_extra]
    o_refs = refs[n_x + n_w + n_extra:n_x + n_w + n_extra + n_out]
    wb_refs = refs[n_x + n_w + n_extra + n_out:]

    @pl.when(pl.program_id(1) == 0)
    def _():
        for w_ref, wb_ref, kn in zip(w_refs, wb_refs, w_kn):
            w = w_ref[...]
            if not kn and not out_t:
                w = w.T
            wb_ref[...] = w.astype(wb_ref.dtype)

    if out_t:
        accs = [lax.dot_general(wb_ref[...], x_refs[xi][...], (((1,), (1,)), ((), ())),
                                preferred_element_type=F32)
                for xi, wb_ref in zip(pairs, wb_refs)]
    else:
        accs = [jnp.dot(x_refs[xi][...], wb_ref[...], preferred_element_type=F32)
                for xi, wb_ref in zip(pairs, wb_refs)]
    epilogue(accs, e_refs, o_refs)


def _matmul(xs, ws, pairs, extras, outs, epilogue, *, n, tm, tn, name, out_t=False):
    m = xs[0].shape[0]
    n_col = n // tn
    w_mode = dict(pipeline_mode=pl.Buffered(1)) if n_col == 1 else {}
    in_specs = [pl.BlockSpec((tm, x.shape[1]), lambda j, i: (i, 0)) for x in xs]
    scratch = []
    for w, first, kn in ws:
        assert not (kn and out_t)
        if kn:
            kdim = w.shape[0]
            in_specs.append(pl.BlockSpec((kdim, tn), functools.partial(_w_cols, first=first), **w_mode))
        else:
            kdim = w.shape[1]
            in_specs.append(pl.BlockSpec((tn, kdim), functools.partial(_w_rows, first=first), **w_mode))
        scratch.append(pltpu.VMEM((tn, kdim) if out_t else (kdim, tn), BF16))
    in_specs += [pl.BlockSpec(bs, im) for _, bs, im in extras]
    body = functools.partial(_mm_body, n_x=len(xs), pairs=tuple(pairs), w_kn=tuple(kn for _, _, kn in ws),
                             out_t=out_t, n_extra=len(extras), n_out=len(outs), epilogue=epilogue)
    return pl.pallas_call(
        body,
        grid=(n_col, m // tm),
        in_specs=in_specs,
        out_specs=[pl.BlockSpec(bs, im) for _, _, bs, im in outs],
        out_shape=[jax.ShapeDtypeStruct(s, d) for s, d, _, _ in outs],
        scratch_shapes=scratch,
        compiler_params=_params(("arbitrary", "arbitrary")),
        name=name,
    )(*xs, *[w for w, _, _ in ws], *[a for a, _, _ in extras])


def _w_cols(j, i, *, first):
    return (0, first + j)


def _w_rows(j, i, *, first):
    return (first + j, 0)


def _tile(j, i):
    return (i, j)


def _tile_t(j, i):
    return (j, i)


def _rowblk(j, i):
    return (i, 0)


def _colblk(j, i):
    return (0, j)


def _colblk_t(j, i):
    return (0, i)


def _ep_gelu(accs, e_refs, o_refs):
    o_refs[0][...] = jax.nn.gelu(accs[0]).astype(o_refs[0].dtype)


def _ep_gelu_layernorm(accs, e_refs, o_refs):
    z = jax.nn.gelu(accs[0])
    mu = jnp.mean(z, axis=-1, keepdims=True)
    zc = z - mu
    var = jnp.mean(zc * zc, axis=-1, keepdims=True)
    y = zc * lax.rsqrt(var + NORM_EPS) * e_refs[0][...] + e_refs[1][...]
    o_refs[0][...] = y.astype(o_refs[0].dtype)


def _ep_rmsnorm(accs, e_refs, o_refs):
    o_refs[0][...] = _rms(accs[0], e_refs[0][...]).astype(o_refs[0].dtype)


def _ep_cast(accs, e_refs, o_refs):
    o_refs[0][...] = accs[0].astype(o_refs[0].dtype)


def _ep_sigmoid(accs, e_refs, o_refs):
    o_refs[0][...] = jax.nn.sigmoid(accs[0]).astype(o_refs[0].dtype)


def _ep_rope(accs, e_refs, o_refs):
    cc = e_refs[0][...]
    ss = e_refs[1][...]
    acc = accs[0]
    for h in range(acc.shape[1] // HEAD_DIM):
        xh = acc[:, h * HEAD_DIM:(h + 1) * HEAD_DIM]
        r = xh * cc + pltpu.roll(xh, HEAD_DIM // 2, 1) * ss
        o_refs[0][:, h * HEAD_DIM:(h + 1) * HEAD_DIM] = r.astype(o_refs[0].dtype)


def _ep_rope_t(accs, e_refs, o_refs, *, scale):
    cc = e_refs[0][...]
    ss = e_refs[1][...]
    acc = accs[0]
    half = HEAD_DIM // 2
    for h in range(acc.shape[0] // HEAD_DIM):
        xh = acc[h * HEAD_DIM:(h + 1) * HEAD_DIM, :]
        swapped = jnp.concatenate([xh[half:, :], xh[:half, :]], axis=0)
        r = (xh * cc + swapped * ss) * scale
        o_refs[0][h * HEAD_DIM:(h + 1) * HEAD_DIM, :] = r.astype(o_refs[0].dtype)


def _ep_partial_rope_t(accs, e_refs, o_refs):
    c = e_refs[0][...]
    s = e_refs[1][...]
    acc = accs[0]
    half = IDX_ROPE_DIM // 2
    for h in range(acc.shape[0] // IDX_HEAD_DIM):
        r0 = h * IDX_HEAD_DIM
        x1 = acc[r0:r0 + half, :]
        x2 = acc[r0 + half:r0 + 2 * half, :]
        o_refs[0][r0:r0 + half, :] = (x1 * c - x2 * s).astype(o_refs[0].dtype)
        o_refs[0][r0 + half:r0 + 2 * half, :] = (x2 * c + x1 * s).astype(o_refs[0].dtype)
        o_refs[0][r0 + 2 * half:r0 + IDX_HEAD_DIM, :] = (
            acc[r0 + 2 * half:r0 + IDX_HEAD_DIM, :].astype(o_refs[0].dtype))


def _partial_rope(x, c, a, b):
    half = IDX_ROPE_DIM // 2
    return x * c + pltpu.roll(x, LANES - half, 1) * a + pltpu.roll(x, half, 1) * b


def _ep_index_keys(accs, e_refs, o_refs, *, w_scale):
    g, bb = e_refs[0][...], e_refs[1][...]
    c, a, b = e_refs[2][...], e_refs[3][...], e_refs[4][...]
    acc = accs[0]
    lane = lax.broadcasted_iota(jnp.int32, acc.shape, 1)
    is_key = lane < IDX_HEAD_DIM
    mu = jnp.sum(jnp.where(is_key, acc, 0.0), axis=-1, keepdims=True) / IDX_HEAD_DIM
    xc = jnp.where(is_key, acc - mu, 0.0)
    var = jnp.sum(xc * xc, axis=-1, keepdims=True) / IDX_HEAD_DIM
    y = xc * lax.rsqrt(var + NORM_EPS) * g + bb
    y = _partial_rope(y, c, a, b)
    o_refs[0][...] = jnp.where(is_key, y, acc * w_scale)


def _ep_merge(accs, e_refs, o_refs):
    o = e_refs[0][...].astype(F32) * accs[0] + e_refs[1][...].astype(F32) * accs[1]
    o_refs[0][...] = o.astype(o_refs[0].dtype)


def _ep_mix_residual(accs, e_refs, o_refs):
    x_ref, g_post, gate, g_pre, scale, shift = e_refs
    x1 = x_ref[...] + gate[...] * _rms(accs[0], g_post[...])
    o_refs[0][...] = x1
    h2 = _rms(x1, g_pre[...]) * (1.0 + scale[...]) + shift[...]
    o_refs[1][...] = h2.astype(o_refs[1].dtype)


def _ep_swiglu(accs, e_refs, o_refs):
    o_refs[0][...] = (jax.nn.silu(accs[0]) * accs[1]).astype(o_refs[0].dtype)


def _simple_mm(x, w, epilogue, extras, out_dtype, *, n, tm, tn, name, first=0, kn=True, out_t=False):
    m = x.shape[0]
    out = ((n, m), out_dtype, (tn, tm), _tile_t) if out_t else ((m, n), out_dtype, (tm, tn), _tile)
    return _matmul([x], [(w, first // tn, kn)], [0], extras, [out], epilogue, n=n, tm=tm, tn=tn,
                   name=name, out_t=out_t)[0]


def _gmlp_body(u_ref, v_ref, w_ref, b_ref, o_ref):
    t = lax.broadcasted_iota(jnp.int32, (CHUNK, CHUNK), 0)
    s = lax.broadcasted_iota(jnp.int32, (CHUNK, CHUNK), 1)
    causal = s <= t
    for g in range(GMLP_GROUPS):
        w = jnp.where(causal, w_ref[g], 0.0).astype(BF16)
        cols = slice(g * LANES, (g + 1) * LANES)
        bias = b_ref[:, cols]
        for c in range(u_ref.shape[0] // CHUNK):
            rows = slice(c * CHUNK, (c + 1) * CHUNK)
            sv = jnp.dot(w, v_ref[rows, cols], preferred_element_type=F32) + bias
            o_ref[rows, cols] = (u_ref[rows, cols].astype(F32) * sv).astype(o_ref.dtype)


def _gmlp(u, vn, w_s, bias):
    m, n = u.shape
    tm = 512
    blk = pl.BlockSpec((tm, n), lambda i: (i, 0))
    return pl.pallas_call(
        _gmlp_body,
        grid=(m // tm,),
        in_specs=[blk, blk,
                  pl.BlockSpec(w_s.shape, lambda i: (0, 0, 0)),
                  pl.BlockSpec(bias.shape, lambda i: (0, 0))],
        out_specs=blk,
        out_shape=jax.ShapeDtypeStruct((m, n), BF16),
        compiler_params=_params(("arbitrary",)),
        name="gmlp_gating",
    )(u, vn, w_s, bias)


def _indexer_body(qt_ref, wt_ref, k2_ref, bias_ref, key_ref):
    tq = qt_ref.shape[1]
    s_total = bias_ref.shape[0]
    i = pl.program_id(0)
    n_keys = (i + 1) * tq
    heads_per_group = LANES // IDX_HEAD_DIM

    q_pos = i * tq + lax.broadcasted_iota(jnp.int32, (IDX_KC, tq), 1)
    k_off = lax.broadcasted_iota(jnp.int32, (IDX_KC, tq), 0)

    def score_chunk(j, carry):
        k0 = pl.multiple_of(j * IDX_KC, IDX_KC)
        score = jnp.zeros((IDX_KC, tq), F32)
        for h in range(IDX_HEADS):
            grp, r = divmod(h, heads_per_group)
            logit = jnp.dot(k2_ref[r, pl.ds(k0, IDX_KC), :], qt_ref[grp * LANES:(grp + 1) * LANES, :],
                            preferred_element_type=F32)
            score = score + jnp.maximum(logit, 0.0) * wt_ref[h:h + 1, :]
        score = jnp.where(k0 + k_off <= q_pos, score, -jnp.inf)
        bits = lax.bitcast_convert_type(score, jnp.int32)
        key_ref[pl.ds(k0, IDX_KC), :] = bits ^ ((bits >> 31) & jnp.int32(0x7FFFFFFF))
        return carry

    lax.fori_loop(0, n_keys // IDX_KC, score_chunk, 0)

    n_blocks = n_keys // ATT_KC
    acc_rows = 4 * SUBLANES

    def search_bit(b, prefix):
        cand = prefix ^ jnp.left_shift(jnp.int32(1), 31 - b)

        def count_block(j, acc):
            k0 = pl.multiple_of(j * ATT_KC, ATT_KC)
            hit = jnp.where(key_ref[pl.ds(k0, ATT_KC), :] >= cand, 1.0, 0.0)
            return acc + jnp.sum(hit.reshape(ATT_KC // acc_rows, acc_rows, tq), axis=0)

        acc = lax.fori_loop(0, n_blocks, count_block, jnp.zeros((acc_rows, tq), F32))
        total = jnp.sum(acc, axis=0, keepdims=True)
        return jnp.where(total >= float(INDEX_TOPK), cand, prefix)

    kth = lax.fori_loop(0, 32, search_bit, jnp.full((1, tq), INT_MIN, jnp.int32))
    kth = jnp.maximum(kth, KEY_NEG_INF + 1)

    def write_block(j, carry):
        k0 = pl.multiple_of(j * ATT_KC, ATT_KC)
        sel = key_ref[pl.ds(k0, ATT_KC), :] >= kth
        bias_ref[pl.ds(k0, ATT_KC), :] = jnp.where(sel, 0.0, MASK_BIAS).astype(bias_ref.dtype)
        return carry

    lax.fori_loop(0, n_blocks, write_block, 0)

    def fill_block(j, carry):
        k0 = pl.multiple_of(j * ATT_KC, ATT_KC)
        bias_ref[pl.ds(k0, ATT_KC), :] = jnp.full((ATT_KC, tq), MASK_BIAS, bias_ref.dtype)
        return carry

    lax.fori_loop(n_blocks, s_total // ATT_KC, fill_block, 0)


def _indexer_mask(q_idx_t, idx_w_t, k2):
    s = q_idx_t.shape[1]
    tq = SEL_TQ
    return pl.pallas_call(
        _indexer_body,
        grid=(s // tq,),
        in_specs=[pl.BlockSpec((q_idx_t.shape[0], tq), lambda i: (0, i)),
                  pl.BlockSpec((IDX_HEADS, tq), lambda i: (0, i)),
                  pl.BlockSpec(k2.shape, lambda i: (0, 0, 0))],
        out_specs=pl.BlockSpec((s, tq), lambda i: (0, i)),
        out_shape=jax.ShapeDtypeStruct((s, s), BF16),
        scratch_shapes=[pltpu.VMEM((s, tq), jnp.int32)],
        compiler_params=_params(("arbitrary",)),
        name="indexer_mask",
    )(q_idx_t, idx_w_t, k2)


def _attention_body(qt_ref, k_ref, vt_ref, bias_ref, o_ref, m_ref, l_ref, acc_ref):
    tq = qt_ref.shape[1]
    i = pl.program_id(0)
    n_blocks = (i + 1) * tq // ATT_KC

    m_ref[...] = jnp.full(m_ref.shape, MASK_BIAS, F32)
    l_ref[...] = jnp.zeros(l_ref.shape, F32)
    acc_ref[...] = jnp.zeros(acc_ref.shape, F32)

    def block(j, carry):
        k0 = pl.multiple_of(j * ATT_KC, ATT_KC)
        bias = bias_ref[pl.ds(k0, ATT_KC), :].astype(F32)
        for h in range(ATTN_HEADS):
            rows = slice(h * HEAD_DIM, (h + 1) * HEAD_DIM)
            s = jnp.dot(k_ref[pl.ds(k0, ATT_KC), rows], qt_ref[rows, :],
                        preferred_element_type=F32) + bias
            m_prev = m_ref[h]
            m_next = jnp.maximum(m_prev, jnp.max(s, axis=0, keepdims=True))
            p = jnp.exp2(s - m_next)
            alpha = jnp.exp2(m_prev - m_next)
            l_ref[h] = alpha * l_ref[h] + jnp.sum(p, axis=0, keepdims=True)
            m_ref[h] = m_next
            pv = jnp.dot(vt_ref[rows, pl.ds(k0, ATT_KC)], p.astype(vt_ref.dtype),
                         preferred_element_type=F32)
            acc_ref[h] = alpha * acc_ref[h] + pv
        return carry

    lax.fori_loop(0, n_blocks, block, 0)

    for h in range(ATTN_HEADS):
        out_t = acc_ref[h] / l_ref[h]
        o_ref[:, h * HEAD_DIM:(h + 1) * HEAD_DIM] = out_t.T.astype(o_ref.dtype)


def _attention(q_t, k, v_t, bias_t):
    width, s = q_t.shape
    tq = SEL_TQ
    resident = dict(pipeline_mode=pl.Buffered(1))
    return pl.pallas_call(
        _attention_body,
        grid=(s // tq,),
        in_specs=[pl.BlockSpec((width, tq), lambda i: (0, i)),
                  pl.BlockSpec((s, width), lambda i: (0, 0), **resident),
                  pl.BlockSpec((width, s), lambda i: (0, 0), **resident),
                  pl.BlockSpec((s, tq), lambda i: (0, i))],
        out_specs=pl.BlockSpec((tq, width), lambda i: (i, 0)),
        out_shape=jax.ShapeDtypeStruct((s, width), BF16),
        scratch_shapes=[pltpu.VMEM((ATTN_HEADS, 1, tq), F32),
                        pltpu.VMEM((ATTN_HEADS, 1, tq), F32),
                        pltpu.VMEM((ATTN_HEADS, HEAD_DIM, tq), F32)],
        compiler_params=_params(("arbitrary",)),
        name="masked_attention",
    )(q_t, k, v_t, bias_t)


def _ffn_down_body(a_ref, w_ref, x_ref, g_ref, gate_ref, o_ref, acc_ref):
    k = pl.program_id(1)

    @pl.when(k == 0)
    def _():
        acc_ref[...] = jnp.zeros(acc_ref.shape, F32)

    acc_ref[...] += jnp.dot(a_ref[...], w_ref[...], preferred_element_type=F32)

    @pl.when(k == pl.num_programs(1) - 1)
    def _():
        o_ref[...] = x_ref[...] + gate_ref[...] * _rms(acc_ref[...], g_ref[...])


def _ffn_down(a, w, x1, g_post, gate):
    m, kdim = a.shape
    d = w.shape[1]
    tm, tk = 512, kdim // 4
    row = pl.BlockSpec((1, d), lambda i, k: (0, 0))
    return pl.pallas_call(
        _ffn_down_body,
        grid=(m // tm, kdim // tk),
        in_specs=[pl.BlockSpec((tm, tk), lambda i, k: (i, k)),
                  pl.BlockSpec((tk, d), lambda i, k: (k, 0)),
                  pl.BlockSpec((tm, d), lambda i, k: (i, 0)), row, row],
        out_specs=pl.BlockSpec((tm, d), lambda i, k: (i, 0)),
        out_shape=jax.ShapeDtypeStruct((m, d), F32),
        scratch_shapes=[pltpu.VMEM((tm, d), F32)],
        compiler_params=_params(("arbitrary", "arbitrary")),
        name="ffn_down_residual",
    )(a, w, x1, g_post, gate)


def _rope_angles(seq, dim):
    inv = 1.0 / (ROPE_THETA ** (jnp.arange(0, dim, 2, dtype=F32) / dim))
    ang = jnp.arange(seq, dtype=F32)[:, None] * inv[None, :]
    return jnp.cos(ang), jnp.sin(ang)


def _attn_rope_tables(seq):
    cos, sin = _rope_angles(seq, HEAD_DIM)
    return jnp.concatenate([cos, cos], axis=1), jnp.concatenate([-sin, sin], axis=1)


def _index_rope_tables(seq):
    cos, sin = _rope_angles(seq, IDX_ROPE_DIM)
    half = IDX_ROPE_DIM // 2
    rest = IDX_HEAD_DIM - IDX_ROPE_DIM
    zeros = lambda n: jnp.zeros((seq, n), F32)
    c = jnp.concatenate([cos, cos, jnp.ones((seq, rest), F32)], axis=1)
    a = jnp.concatenate([-sin, zeros(IDX_HEAD_DIM - half)], axis=1)
    b = jnp.concatenate([zeros(half), sin, zeros(rest)], axis=1)
    rep = LANES // IDX_HEAD_DIM
    return tuple(jnp.tile(t, (1, rep)) for t in (c, a, b)) + (cos.T, sin.T)


def _layer(x, mod, g_pre_mix, g_post_mix, w_in_t, gmlp_ln_g, gmlp_ln_b, gmlp_w_s, gmlp_b_s,
           q_lat_norm_g, w_q_up, w_qidx_up, kidx_ln_g, kidx_ln_b, w_proj_a, w_proj_b, w_out,
           g_pre_ffn, g_post_ffn, w_ffn_gate, w_ffn_up, w_ffn_down, tables):
    s, d = x.shape
    cc, ss, ic, ia, ib, icos_t, isin_t = tables
    row = lambda a: a.reshape(1, -1)
    shift_m, scale_m, gate_m, shift_f, scale_f, gate_f = (mod[:, n * d:(n + 1) * d] for n in range(N_MOD))

    h = _prenorm(x, row(g_pre_mix), scale_m, shift_m)

    tm = 1024
    in_proj = functools.partial(_simple_mm, h, w_in_t, kn=False)

    u = in_proj(_ep_gelu, [], BF16, n=GMLP_WIDTH, tm=tm, tn=512, name="in_proj_u")
    vn = in_proj(_ep_gelu_layernorm,
                 [(row(gmlp_ln_g), (1, GMLP_WIDTH), _colblk), (row(gmlp_ln_b), (1, GMLP_WIDTH), _colblk)],
                 BF16, n=GMLP_WIDTH, first=GMLP_WIDTH, tm=512, tn=GMLP_WIDTH, name="in_proj_v_ln")
    gmlp_bias = jnp.repeat(gmlp_b_s.T, LANES, axis=1)
    y_a = _gmlp(u, vn, gmlp_w_s, gmlp_bias)

    q_lat = in_proj(_ep_rmsnorm, [(row(q_lat_norm_g), (1, Q_LORA_RANK), _colblk)],
                    BF16, n=Q_LORA_RANK, first=COL_QLAT, tm=tm, tn=Q_LORA_RANK, name="in_proj_qlat")
    q_scale = HEAD_DIM ** -0.5 * math.log2(math.e)
    q_t = _simple_mm(q_lat, w_q_up.T, functools.partial(_ep_rope_t, scale=q_scale),
                     [(cc.T, (HEAD_DIM, tm), _colblk_t), (ss.T, (HEAD_DIM, tm), _colblk_t)],
                     BF16, n=ATTN_WIDTH, tm=tm, tn=512, name="q_up_rope", kn=False, out_t=True)
    k = in_proj(_ep_rope, [(cc, (tm, HEAD_DIM), _rowblk), (ss, (tm, HEAD_DIM), _rowblk)], BF16,
                n=ATTN_WIDTH, first=COL_K, tm=tm, tn=512, name="in_proj_k_rope")
    v_t = in_proj(_ep_cast, [], BF16, n=ATTN_WIDTH, first=COL_V, tm=tm, tn=512, name="in_proj_v",
                  out_t=True)
    half = IDX_ROPE_DIM // 2
    q_idx_t = _simple_mm(q_lat, w_qidx_up.T, _ep_partial_rope_t,
                         [(icos_t, (half, tm), _colblk_t), (isin_t, (half, tm), _colblk_t)],
                         BF16, n=IDX_HEADS * IDX_HEAD_DIM, tm=tm, tn=512, name="qidx_up_rope",
                         kn=False, out_t=True)
    pad = lambda a: jnp.pad(a, (0, LANES - a.shape[0])).reshape(1, LANES)
    idx_w_scale = (IDX_HEADS ** -0.5) * (IDX_HEAD_DIM ** -0.5)
    kx = in_proj(functools.partial(_ep_index_keys, w_scale=idx_w_scale),
                 [(pad(kidx_ln_g), (1, LANES), _colblk), (pad(kidx_ln_b), (1, LANES), _colblk)]
                 + [(t, (tm, LANES), _rowblk) for t in (ic, ia, ib)],
                 F32, n=LANES, first=COL_KIDX, tm=tm, tn=LANES, name="in_proj_index_keys")
    k_idx = kx[:, :IDX_HEAD_DIM].astype(BF16)
    zeros = jnp.zeros_like(k_idx)
    k2 = jnp.stack([jnp.concatenate([k_idx, zeros], axis=1), jnp.concatenate([zeros, k_idx], axis=1)])
    idx_w_t = kx[:, IDX_HEAD_DIM:IDX_HEAD_DIM + IDX_HEADS].T
    bias_t = _indexer_mask(q_idx_t, idx_w_t, k2)
    y_b = _attention(q_t, k, v_t, bias_t)

    gates = _simple_mm(h, w_in_t[COL_GATE:], _ep_sigmoid, [], BF16, n=2 * d, tm=tm, tn=512,
                       name="in_proj_gates", kn=False)
    n_gate_blocks = d // 512
    merged = _matmul([y_a, y_b], [(w_proj_a, 0, True), (w_proj_b, 0, True)], [0, 1],
                     [(gates, (tm, 512), _tile),
                      (gates, (tm, 512), lambda j, i: (i, j + n_gate_blocks))],
                     [((s, d), BF16, (tm, 512), _tile)], _ep_merge, n=d, tm=tm, tn=512,
                     name="branch_merge")[0]
    tm_full = 256
    vec = lambda a: (a, (1, d), _colblk)
    x1, h2 = _matmul([merged], [(w_out, 0, True)], [0],
                     [(x, (tm_full, d), _rowblk), vec(row(g_post_mix)), vec(gate_m),
                      vec(row(g_pre_ffn)), vec(scale_f), vec(shift_f)],
                     [((s, d), F32, (tm_full, d), _tile), ((s, d), BF16, (tm_full, d), _tile)],
                     _ep_mix_residual, n=d, tm=tm_full, tn=d, name="out_proj_residual")

    ffn_hidden = w_ffn_gate.shape[1]
    act = _matmul([h2], [(w_ffn_gate, 0, True), (w_ffn_up, 0, True)], [0, 0], [],
                  [((s, ffn_hidden), BF16, (tm, 512), _tile)], _ep_swiglu, n=ffn_hidden, tm=tm, tn=512,
                  name="ffn_up")[0]
    return _ffn_down(act, w_ffn_down.astype(BF16), x1, row(g_post_ffn), gate_f)


def kernel(x, c, w_mod, b_mod, g_pre_mix, g_post_mix, w_in, gmlp_ln_g, gmlp_ln_b, gmlp_w_s, gmlp_b_s, q_lat_norm_g, w_q_up, w_qidx_up, kidx_ln_g, kidx_ln_b, w_proj_a, w_proj_b, w_out, g_pre_ffn, g_post_ffn, w_ffn_gate, w_ffn_up, w_ffn_down):
    batch, seq, d = x.shape
    assert batch == 1 and d == D_MODEL
    tables = _attn_rope_tables(seq) + _index_rope_tables(seq)
    y = x[0]
    for l in range(w_mod.shape[0]):
        mod = _modulation(c, w_mod[l], b_mod[l])
        y = _layer(y, mod, g_pre_mix[l], g_post_mix[l], w_in[l].T, gmlp_ln_g[l], gmlp_ln_b[l],
                   gmlp_w_s[l], gmlp_b_s[l], q_lat_norm_g[l], w_q_up[l], w_qidx_up[l],
                   kidx_ln_g[l], kidx_ln_b[l], w_proj_a[l], w_proj_b[l], w_out[l],
                   g_pre_ffn[l], g_post_ffn[l], w_ffn_gate[l], w_ffn_up[l], w_ffn_down[l], tables)
    return y[None]
```

```python
import functools
import math

import jax
import jax.numpy as jnp
from jax import lax
from jax.experimental import pallas as pl
from jax.experimental.pallas import tpu as pltpu

F32 = jnp.float32
BF16 = jnp.bfloat16

D_MODEL = 2048
GMLP_WIDTH = 1024
GMLP_GROUPS = 8
CHUNK = 128
ATTN_HEADS = 8
HEAD_DIM = 128
ATTN_WIDTH = ATTN_HEADS * HEAD_DIM
Q_LORA_RANK = 512
IDX_HEADS = 16
IDX_HEAD_DIM = 64
IDX_ROPE_DIM = 32
INDEX_TOPK = 256
ROPE_THETA = 10000.0
N_MOD = 6
NORM_EPS = 1e-6

COL_QLAT = 2 * GMLP_WIDTH
COL_K = COL_QLAT + Q_LORA_RANK
COL_V = COL_K + ATTN_WIDTH
COL_KIDX = COL_V + ATTN_WIDTH
COL_IDXW = COL_KIDX + IDX_HEAD_DIM
COL_GATE = COL_IDXW + IDX_HEADS

LANES = 128
SUBLANES = 8
VMEM_LIMIT = 56 * 1024 * 1024

MASK_BIAS = -1e30
INT_MIN = -2 ** 31
KEY_NEG_INF = -2139095041

SEL_TQ = 256
IDX_KC = 128
KEY_BITS = 32
PLANE_KEYS = KEY_BITS * SUBLANES
ATT_KC = 2 * PLANE_KEYS


def _params(sem=None):
    return pltpu.CompilerParams(dimension_semantics=sem, vmem_limit_bytes=VMEM_LIMIT)


def _mod_body(cb_ref, w_ref, b_ref, o_ref):
    cb = cb_ref[...]
    tn = o_ref.shape[1]
    parts = [jnp.sum(w_ref[:, p * LANES:(p + 1) * LANES] * cb, axis=0, keepdims=True)
             for p in range(tn // LANES)]
    o_ref[...] = jnp.concatenate(parts, axis=1) + b_ref[...]


def _modulation(c, w_mod, b_mod):
    k, n = w_mod.shape
    tn = 1024
    cb = jnp.broadcast_to(c.reshape(k, 1), (k, LANES))
    return pl.pallas_call(
        _mod_body,
        grid=(n // tn,),
        in_specs=[pl.BlockSpec((k, LANES), lambda j: (0, 0)),
                  pl.BlockSpec((k, tn), lambda j: (0, j)),
                  pl.BlockSpec((1, tn), lambda j: (0, j))],
        out_specs=pl.BlockSpec((1, tn), lambda j: (0, j)),
        out_shape=jax.ShapeDtypeStruct((1, n), F32),
        compiler_params=_params(("arbitrary",)),
        name="modulation",
    )(cb, w_mod, b_mod.reshape(1, n))


def _rms(x, g):
    return x * lax.rsqrt(jnp.mean(x * x, axis=-1, keepdims=True) + NORM_EPS) * g


def _prenorm_body(x_ref, g_ref, scale_ref, shift_ref, o_ref):
    h = _rms(x_ref[...], g_ref[...]) * (1.0 + scale_ref[...]) + shift_ref[...]
    o_ref[...] = h.astype(o_ref.dtype)


def _prenorm(x, g, scale, shift):
    m, d = x.shape
    tm = 512
    row = pl.BlockSpec((1, d), lambda i: (0, 0))
    return pl.pallas_call(
        _prenorm_body,
        grid=(m // tm,),
        in_specs=[pl.BlockSpec((tm, d), lambda i: (i, 0)), row, row, row],
        out_specs=pl.BlockSpec((tm, d), lambda i: (i, 0)),
        out_shape=jax.ShapeDtypeStruct((m, d), BF16),
        compiler_params=_params(("arbitrary",)),
        name="prenorm",
    )(x, g, scale, shift)


def _mm_body(*refs, n_x, pairs, w_kn, out_t, n_extra, n_out, epilogue):
    n_w = len(pairs)
    x_refs = refs[:n_x]
    w_refs = refs[n_x:n_x + n_w]
    e_refs = refs[n_x + n_w:n_x + n_w + n_extra]
    o_refs = refs[n_x + n_w + n_extra:n_x + n_w + n_extra + n_out]
    wb_refs = refs[n_x + n_w + n_extra + n_out:]

    @pl.when(pl.program_id(1) == 0)
    def _():
        for w_ref, wb_ref, kn in zip(w_refs, wb_refs, w_kn):
            w = w_ref[...]
            if not kn and not out_t:
                w = w.T
            wb_ref[...] = w.astype(wb_ref.dtype)

    if out_t:
        accs = [lax.dot_general(wb_ref[...], x_refs[xi][...], (((1,), (1,)), ((), ())),
                                preferred_element_type=F32)
                for xi, wb_ref in zip(pairs, wb_refs)]
    else:
        accs = [jnp.dot(x_refs[xi][...], wb_ref[...], preferred_element_type=F32)
                for xi, wb_ref in zip(pairs, wb_refs)]
    epilogue(accs, e_refs, o_refs)


def _matmul(xs, ws, pairs, extras, outs, epilogue, *, n, tm, tn, name, out_t=False):
    m = xs[0].shape[0]
    n_col = n // tn
    w_mode = dict(pipeline_mode=pl.Buffered(1)) if n_col == 1 else {}
    in_specs = [pl.BlockSpec((tm, x.shape[1]), lambda j, i: (i, 0)) for x in xs]
    scratch = []
    for w, first, kn in ws:
        assert not (kn and out_t)
        if kn:
            kdim = w.shape[0]
            in_specs.append(pl.BlockSpec((kdim, tn), functools.partial(_w_cols, first=first), **w_mode))
        else:
            kdim = w.shape[1]
            in_specs.append(pl.BlockSpec((tn, kdim), functools.partial(_w_rows, first=first), **w_mode))
        scratch.append(pltpu.VMEM((tn, kdim) if out_t else (kdim, tn), BF16))
    in_specs += [pl.BlockSpec(bs, im) for _, bs, im in extras]
    body = functools.partial(_mm_body, n_x=len(xs), pairs=tuple(pairs), w_kn=tuple(kn for _, _, kn in ws),
                             out_t=out_t, n_extra=len(extras), n_out=len(outs), epilogue=epilogue)
    return pl.pallas_call(
        body,
        grid=(n_col, m // tm),
        in_specs=in_specs,
        out_specs=[pl.BlockSpec(bs, im) for _, _, bs, im in outs],
        out_shape=[jax.ShapeDtypeStruct(s, d) for s, d, _, _ in outs],
        scratch_shapes=scratch,
        compiler_params=_params(("arbitrary", "arbitrary")),
        name=name,
    )(*xs, *[w for w, _, _ in ws], *[a for a, _, _ in extras])


def _w_cols(j, i, *, first):
    return (0, first + j)


def _w_rows(j, i, *, first):
    return (first + j, 0)


def _tile(j, i):
    return (i, j)


def _tile_t(j, i):
    return (j, i)


def _rowblk(j, i):
    return (i, 0)


def _colblk(j, i):
    return (0, j)


def _colblk_t(j, i):
    return (0, i)


def _ep_gelu(accs, e_refs, o_refs):
    o_refs[0][...] = jax.nn.gelu(accs[0]).astype(o_refs[0].dtype)


def _ep_gelu_layernorm(accs, e_refs, o_refs):
    z = jax.nn.gelu(accs[0])
    mu = jnp.mean(z, axis=-1, keepdims=True)
    zc = z - mu
    var = jnp.mean(zc * zc, axis=-1, keepdims=True)
    y = zc * lax.rsqrt(var + NORM_EPS) * e_refs[0][...] + e_refs[1][...]
    o_refs[0][...] = y.astype(o_refs[0].dtype)


def _ep_rmsnorm(accs, e_refs, o_refs):
    o_refs[0][...] = _rms(accs[0], e_refs[0][...]).astype(o_refs[0].dtype)


def _ep_cast(accs, e_refs, o_refs):
    o_refs[0][...] = accs[0].astype(o_refs[0].dtype)


def _ep_sigmoid(accs, e_refs, o_refs):
    o_refs[0][...] = jax.nn.sigmoid(accs[0]).astype(o_refs[0].dtype)


def _ep_rope(accs, e_refs, o_refs, *, scale):
    cc = e_refs[0][...]
    ss = e_refs[1][...]
    acc = accs[0]
    for h in range(acc.shape[1] // HEAD_DIM):
        xh = acc[:, h * HEAD_DIM:(h + 1) * HEAD_DIM]
        r = xh * cc + pltpu.roll(xh, HEAD_DIM // 2, 1) * ss
        if scale != 1.0:
            r = r * scale
        o_refs[0][:, h * HEAD_DIM:(h + 1) * HEAD_DIM] = r.astype(o_refs[0].dtype)


def _ep_partial_rope_t(accs, e_refs, o_refs):
    c = e_refs[0][...]
    s = e_refs[1][...]
    acc = accs[0]
    half = IDX_ROPE_DIM // 2
    for h in range(acc.shape[0] // IDX_HEAD_DIM):
        r0 = h * IDX_HEAD_DIM
        x1 = acc[r0:r0 + half, :]
        x2 = acc[r0 + half:r0 + 2 * half, :]
        o_refs[0][r0:r0 + half, :] = (x1 * c - x2 * s).astype(o_refs[0].dtype)
        o_refs[0][r0 + half:r0 + 2 * half, :] = (x2 * c + x1 * s).astype(o_refs[0].dtype)
        o_refs[0][r0 + 2 * half:r0 + IDX_HEAD_DIM, :] = (
            acc[r0 + 2 * half:r0 + IDX_HEAD_DIM, :].astype(o_refs[0].dtype))


def _partial_rope(x, c, a, b):
    half = IDX_ROPE_DIM // 2
    return x * c + pltpu.roll(x, LANES - half, 1) * a + pltpu.roll(x, half, 1) * b


def _ep_index_keys(accs, e_refs, o_refs, *, w_scale):
    g, bb = e_refs[0][...], e_refs[1][...]
    c, a, b = e_refs[2][...], e_refs[3][...], e_refs[4][...]
    acc = accs[0]
    lane = lax.broadcasted_iota(jnp.int32, acc.shape, 1)
    is_key = lane < IDX_HEAD_DIM
    mu = jnp.sum(jnp.where(is_key, acc, 0.0), axis=-1, keepdims=True) / IDX_HEAD_DIM
    xc = jnp.where(is_key, acc - mu, 0.0)
    var = jnp.sum(xc * xc, axis=-1, keepdims=True) / IDX_HEAD_DIM
    y = xc * lax.rsqrt(var + NORM_EPS) * g + bb
    y = _partial_rope(y, c, a, b)
    o_refs[0][...] = jnp.where(is_key, y, acc * w_scale)


def _ep_merge(accs, e_refs, o_refs):
    o = e_refs[0][...].astype(F32) * accs[0] + e_refs[1][...].astype(F32) * accs[1]
    o_refs[0][...] = o.astype(o_refs[0].dtype)


def _ep_mix_residual(accs, e_refs, o_refs):
    x_ref, g_post, gate, g_pre, scale, shift = e_refs
    x1 = x_ref[...] + gate[...] * _rms(accs[0], g_post[...])
    o_refs[0][...] = x1
    h2 = _rms(x1, g_pre[...]) * (1.0 + scale[...]) + shift[...]
    o_refs[1][...] = h2.astype(o_refs[1].dtype)


def _ep_swiglu(accs, e_refs, o_refs):
    o_refs[0][...] = (jax.nn.silu(accs[0]) * accs[1]).astype(o_refs[0].dtype)


def _simple_mm(x, w, epilogue, extras, out_dtype, *, n, tm, tn, name, first=0, kn=True, out_t=False):
    m = x.shape[0]
    out = ((n, m), out_dtype, (tn, tm), _tile_t) if out_t else ((m, n), out_dtype, (tm, tn), _tile)
    return _matmul([x], [(w, first // tn, kn)], [0], extras, [out], epilogue, n=n, tm=tm, tn=tn,
                   name=name, out_t=out_t)[0]


def _gmlp_body(u_ref, v_ref, w_ref, b_ref, o_ref):
    t = lax.broadcasted_iota(jnp.int32, (CHUNK, CHUNK), 0)
    s = lax.broadcasted_iota(jnp.int32, (CHUNK, CHUNK), 1)
    causal = s <= t
    for g in range(GMLP_GROUPS):
        w = jnp.where(causal, w_ref[g], 0.0).astype(BF16)
        cols = slice(g * LANES, (g + 1) * LANES)
        bias = b_ref[:, cols]
        for c in range(u_ref.shape[0] // CHUNK):
            rows = slice(c * CHUNK, (c + 1) * CHUNK)
            sv = jnp.dot(w, v_ref[rows, cols], preferred_element_type=F32) + bias
            o_ref[rows, cols] = (u_ref[rows, cols].astype(F32) * sv).astype(o_ref.dtype)


def _gmlp(u, vn, w_s, bias):
    m, n = u.shape
    tm = 512
    blk = pl.BlockSpec((tm, n), lambda i: (i, 0))
    return pl.pallas_call(
        _gmlp_body,
        grid=(m // tm,),
        in_specs=[blk, blk,
                  pl.BlockSpec(w_s.shape, lambda i: (0, 0, 0)),
                  pl.BlockSpec(bias.shape, lambda i: (0, 0))],
        out_specs=blk,
        out_shape=jax.ShapeDtypeStruct((m, n), BF16),
        compiler_params=_params(("arbitrary",)),
        name="gmlp_gating",
    )(u, vn, w_s, bias)


def _bit_planes(words):
    a = list(words)
    j, mask = KEY_BITS // 2, 0x0000FFFF
    while j:
        for k in range(KEY_BITS):
            if k & j:
                continue
            t = (a[k] ^ (a[k + j] >> j)) & jnp.int32(mask)
            a[k] = a[k] ^ t
            a[k + j] = a[k + j] ^ (t << j)
        j //= 2
        mask = (mask ^ (mask << j)) & 0xFFFFFFFF
    return a


def _indexer_body(qt_ref, wt_ref, k2_ref, bias_ref, key_ref, plane_ref, live_ref):
    tq = qt_ref.shape[1]
    n_chunks_total = bias_ref.shape[0]
    i = pl.program_id(0)
    n_keys = (i + 1) * tq
    n_blocks = n_keys // PLANE_KEYS
    heads_per_group = LANES // IDX_HEAD_DIM

    @pl.when(i == 0)
    def _():
        plane_ref[...] = jnp.zeros(plane_ref.shape, jnp.int32)

    q_pos = i * tq + lax.broadcasted_iota(jnp.int32, (IDX_KC, tq), 1)
    k_off = lax.broadcasted_iota(jnp.int32, (IDX_KC, tq), 0)

    def score_chunk(j, carry):
        k0 = pl.multiple_of(j * IDX_KC, IDX_KC)
        score = jnp.zeros((IDX_KC, tq), F32)
        for h in range(IDX_HEADS):
            grp, r = divmod(h, heads_per_group)
            logit = jnp.dot(k2_ref[r, pl.ds(k0, IDX_KC), :], qt_ref[grp * LANES:(grp + 1) * LANES, :],
                            preferred_element_type=F32)
            score = score + jnp.maximum(logit, 0.0) * wt_ref[h:h + 1, :]
        score = jnp.where(k0 + k_off <= q_pos, score, -jnp.inf)
        bits = lax.bitcast_convert_type(score, jnp.int32)
        key_ref[pl.ds(k0, IDX_KC), :] = bits ^ ((bits >> 31) & jnp.int32(0x7FFFFFFF))
        return carry

    lax.fori_loop(0, n_keys // IDX_KC, score_chunk, 0)

    @pl.when(n_blocks % (ATT_KC // PLANE_KEYS) == 1)
    def _():
        pad0 = pl.multiple_of(n_keys, PLANE_KEYS)
        key_ref[pl.ds(pad0, PLANE_KEYS), :] = jnp.full((PLANE_KEYS, tq), KEY_NEG_INF, jnp.int32)

    def slice_block(blk, carry):
        k0 = pl.multiple_of(blk * PLANE_KEYS, PLANE_KEYS)
        keys = key_ref[pl.ds(k0, PLANE_KEYS), :]
        planes = _bit_planes([keys[c * SUBLANES:(c + 1) * SUBLANES, :] for c in range(KEY_BITS)])
        r0 = pl.multiple_of(blk * SUBLANES, SUBLANES)
        plane_ref[0, pl.ds(r0, SUBLANES), :] = ~planes[0]
        for r in range(1, KEY_BITS):
            plane_ref[r, pl.ds(r0, SUBLANES), :] = planes[r]
        return carry

    lax.fori_loop(0, n_blocks, slice_block, 0)

    plane_rows = live_ref.shape[0]
    row_id = lax.broadcasted_iota(jnp.int32, (plane_rows, tq), 0)
    live_ref[...] = jnp.where(row_id < n_blocks * SUBLANES, jnp.int32(-1), jnp.int32(0))

    def search_bit(r, carry):
        need, kth = carry
        plane = plane_ref[r]
        live = live_ref[...]
        hit = live & plane
        ones = lax.population_count(hit)
        part = jnp.sum(ones.reshape(plane_rows // SUBLANES, SUBLANES, tq), axis=0)
        total = jnp.sum(part.astype(F32), axis=0, keepdims=True)
        take = total >= need
        live_ref[...] = jnp.where(take, hit, live & ~plane)
        need = jnp.where(take, need, need - total)
        kth = jnp.where(take, kth | jnp.left_shift(jnp.int32(1), KEY_BITS - 1 - r), kth)
        return need, kth

    _, kth = lax.fori_loop(0, KEY_BITS, search_bit,
                           (jnp.full((1, tq), float(INDEX_TOPK), F32), jnp.zeros((1, tq), jnp.int32)))
    kth = kth ^ jnp.int32(INT_MIN)
    kth = jnp.maximum(kth, KEY_NEG_INF + 1)

    halves = ATT_KC // PLANE_KEYS
    n_chunks = (n_blocks + halves - 1) // halves

    def write_chunk(c, carry):
        parts = []
        for hh in range(halves):
            k0 = pl.multiple_of(c * ATT_KC + hh * PLANE_KEYS, PLANE_KEYS)
            sel = key_ref[pl.ds(k0, PLANE_KEYS), :] >= kth
            parts.append(jnp.where(sel, 0.0, MASK_BIAS).T)
        bias_ref[c] = jnp.concatenate(parts, axis=1).astype(bias_ref.dtype)
        return carry

    lax.fori_loop(0, n_chunks, write_chunk, 0)

    def fill_chunk(c, carry):
        bias_ref[c] = jnp.full(bias_ref.shape[1:], MASK_BIAS, bias_ref.dtype)
        return carry

    lax.fori_loop(n_chunks, n_chunks_total, fill_chunk, 0)


def _indexer_mask(q_idx_t, idx_w_t, k2):
    s = q_idx_t.shape[1]
    tq = SEL_TQ
    assert tq % PLANE_KEYS == 0 and s % (KEY_BITS * SUBLANES) == 0
    return pl.pallas_call(
        _indexer_body,
        grid=(s // tq,),
        in_specs=[pl.BlockSpec((q_idx_t.shape[0], tq), lambda i: (0, i)),
                  pl.BlockSpec((IDX_HEADS, tq), lambda i: (0, i)),
                  pl.BlockSpec(k2.shape, lambda i: (0, 0, 0))],
        out_specs=pl.BlockSpec((s // ATT_KC, tq, ATT_KC), lambda i: (0, i, 0)),
        out_shape=jax.ShapeDtypeStruct((s // ATT_KC, s, ATT_KC), BF16),
        scratch_shapes=[pltpu.VMEM((s, tq), jnp.int32),
                        pltpu.VMEM((KEY_BITS, s // KEY_BITS, tq), jnp.int32),
                        pltpu.VMEM((s // KEY_BITS, tq), jnp.int32)],
        compiler_params=_params(("arbitrary",)),
        name="indexer_mask",
    )(q_idx_t, idx_w_t, k2)


def _attention_body(q_ref, k_ref, v_ref, bias_ref, o_ref, m_ref, l_ref, acc_ref):
    tq = q_ref.shape[0]
    kc = bias_ref.shape[2]
    i = pl.program_id(0)
    n_chunks = (i * tq + tq - 1) // kc + 1
    sub = kc // LANES

    m_ref[...] = jnp.full(m_ref.shape, MASK_BIAS, F32)
    l_ref[...] = jnp.zeros(l_ref.shape, F32)
    acc_ref[...] = jnp.zeros(acc_ref.shape, F32)

    def chunk(j, carry):
        k0 = pl.multiple_of(j * kc, kc)
        bias = bias_ref[j].astype(F32)
        for h in range(ATTN_HEADS):
            cols = slice(h * HEAD_DIM, (h + 1) * HEAD_DIM)
            s = lax.dot_general(q_ref[:, cols], k_ref[pl.ds(k0, kc), cols],
                                (((1,), (1,)), ((), ())), preferred_element_type=F32) + bias
            m_prev = m_ref[h]
            m_next = jnp.maximum(m_prev, jnp.max(s, axis=1, keepdims=True))
            p = jnp.exp2(s - jnp.concatenate([m_next] * sub, axis=1))
            alpha = jnp.exp2(m_prev - m_next)
            l_ref[h] = alpha * l_ref[h] + jnp.sum(p, axis=1, keepdims=True)
            m_ref[h] = m_next
            pv = jnp.dot(p.astype(v_ref.dtype), v_ref[pl.ds(k0, kc), cols],
                         preferred_element_type=F32)
            acc_ref[h] = alpha * acc_ref[h] + pv
        return carry

    lax.fori_loop(0, n_chunks, chunk, 0)

    for h in range(ATTN_HEADS):
        o_ref[:, h * HEAD_DIM:(h + 1) * HEAD_DIM] = (acc_ref[h] / l_ref[h]).astype(o_ref.dtype)


def _attention(q, k, v, bias):
    s, width = q.shape
    tq = SEL_TQ
    n_chunks, _, kc = bias.shape
    resident = dict(pipeline_mode=pl.Buffered(1))
    return pl.pallas_call(
        _attention_body,
        grid=(s // tq,),
        in_specs=[pl.BlockSpec((tq, width), lambda i: (i, 0)),
                  pl.BlockSpec((s, width), lambda i: (0, 0), **resident),
                  pl.BlockSpec((s, width), lambda i: (0, 0), **resident),
                  pl.BlockSpec((n_chunks, tq, kc), lambda i: (0, i, 0))],
        out_specs=pl.BlockSpec((tq, width), lambda i: (i, 0)),
        out_shape=jax.ShapeDtypeStruct((s, width), BF16),
        scratch_shapes=[pltpu.VMEM((ATTN_HEADS, tq, LANES), F32),
                        pltpu.VMEM((ATTN_HEADS, tq, LANES), F32),
                        pltpu.VMEM((ATTN_HEADS, tq, HEAD_DIM), F32)],
        compiler_params=_params(("arbitrary",)),
        name="masked_attention",
    )(q, k, v, bias)


def _ffn_down_body(a_ref, w_ref, x_ref, g_ref, gate_ref, o_ref, acc_ref):
    k = pl.program_id(1)

    @pl.when(k == 0)
    def _():
        acc_ref[...] = jnp.zeros(acc_ref.shape, F32)

    acc_ref[...] += jnp.dot(a_ref[...], w_ref[...], preferred_element_type=F32)

    @pl.when(k == pl.num_programs(1) - 1)
    def _():
        o_ref[...] = x_ref[...] + gate_ref[...] * _rms(acc_ref[...], g_ref[...])


def _ffn_down(a, w, x1, g_post, gate):
    m, kdim = a.shape
    d = w.shape[1]
    tm, tk = 512, kdim // 4
    row = pl.BlockSpec((1, d), lambda i, k: (0, 0))
    return pl.pallas_call(
        _ffn_down_body,
        grid=(m // tm, kdim // tk),
        in_specs=[pl.BlockSpec((tm, tk), lambda i, k: (i, k)),
                  pl.BlockSpec((tk, d), lambda i, k: (k, 0)),
                  pl.BlockSpec((tm, d), lambda i, k: (i, 0)), row, row],
        out_specs=pl.BlockSpec((tm, d), lambda i, k: (i, 0)),
        out_shape=jax.ShapeDtypeStruct((m, d), F32),
        scratch_shapes=[pltpu.VMEM((tm, d), F32)],
        compiler_params=_params(("arbitrary", "arbitrary")),
        name="ffn_down_residual",
    )(a, w, x1, g_post, gate)


def _rope_angles(seq, dim):
    inv = 1.0 / (ROPE_THETA ** (jnp.arange(0, dim, 2, dtype=F32) / dim))
    ang = jnp.arange(seq, dtype=F32)[:, None] * inv[None, :]
    return jnp.cos(ang), jnp.sin(ang)


def _attn_rope_tables(seq):
    cos, sin = _rope_angles(seq, HEAD_DIM)
    return jnp.concatenate([cos, cos], axis=1), jnp.concatenate([-sin, sin], axis=1)


def _index_rope_tables(seq):
    cos, sin = _rope_angles(seq, IDX_ROPE_DIM)
    half = IDX_ROPE_DIM // 2
    rest = IDX_HEAD_DIM - IDX_ROPE_DIM
    zeros = lambda n: jnp.zeros((seq, n), F32)
    c = jnp.concatenate([cos, cos, jnp.ones((seq, rest), F32)], axis=1)
    a = jnp.concatenate([-sin, zeros(IDX_HEAD_DIM - half)], axis=1)
    b = jnp.concatenate([zeros(half), sin, zeros(rest)], axis=1)
    rep = LANES // IDX_HEAD_DIM
    return tuple(jnp.tile(t, (1, rep)) for t in (c, a, b)) + (cos.T, sin.T)


def _layer(x, mod, g_pre_mix, g_post_mix, w_in_t, gmlp_ln_g, gmlp_ln_b, gmlp_w_s, gmlp_b_s,
           q_lat_norm_g, w_q_up, w_qidx_up, kidx_ln_g, kidx_ln_b, w_proj_a, w_proj_b, w_out,
           g_pre_ffn, g_post_ffn, w_ffn_gate, w_ffn_up, w_ffn_down, tables):
    s, d = x.shape
    cc, ss, ic, ia, ib, icos_t, isin_t = tables
    row = lambda a: a.reshape(1, -1)
    shift_m, scale_m, gate_m, shift_f, scale_f, gate_f = (mod[:, n * d:(n + 1) * d] for n in range(N_MOD))

    h = _prenorm(x, row(g_pre_mix), scale_m, shift_m)

    tm = 1024
    in_proj = functools.partial(_simple_mm, h, w_in_t, kn=False)

    u = in_proj(_ep_gelu, [], BF16, n=GMLP_WIDTH, tm=tm, tn=512, name="in_proj_u")
    vn = in_proj(_ep_gelu_layernorm,
                 [(row(gmlp_ln_g), (1, GMLP_WIDTH), _colblk), (row(gmlp_ln_b), (1, GMLP_WIDTH), _colblk)],
                 BF16, n=GMLP_WIDTH, first=GMLP_WIDTH, tm=512, tn=GMLP_WIDTH, name="in_proj_v_ln")
    gmlp_bias = jnp.repeat(gmlp_b_s.T, LANES, axis=1)
    y_a = _gmlp(u, vn, gmlp_w_s, gmlp_bias)

    q_lat = in_proj(_ep_rmsnorm, [(row(q_lat_norm_g), (1, Q_LORA_RANK), _colblk)],
                    BF16, n=Q_LORA_RANK, first=COL_QLAT, tm=tm, tn=Q_LORA_RANK, name="in_proj_qlat")
    q_scale = HEAD_DIM ** -0.5 * math.log2(math.e)
    rope_extras = [(cc, (tm, HEAD_DIM), _rowblk), (ss, (tm, HEAD_DIM), _rowblk)]
    q = _simple_mm(q_lat, w_q_up, functools.partial(_ep_rope, scale=q_scale), rope_extras,
                   BF16, n=ATTN_WIDTH, tm=tm, tn=512, name="q_up_rope")
    k = in_proj(functools.partial(_ep_rope, scale=1.0), rope_extras, BF16,
                n=ATTN_WIDTH, first=COL_K, tm=tm, tn=512, name="in_proj_k_rope")
    v = in_proj(_ep_cast, [], BF16, n=ATTN_WIDTH, first=COL_V, tm=tm, tn=512, name="in_proj_v")
    half = IDX_ROPE_DIM // 2
    q_idx_t = _simple_mm(q_lat, w_qidx_up.T, _ep_partial_rope_t,
                         [(icos_t, (half, tm), _colblk_t), (isin_t, (half, tm), _colblk_t)],
                         BF16, n=IDX_HEADS * IDX_HEAD_DIM, tm=tm, tn=512, name="qidx_up_rope",
                         kn=False, out_t=True)
    pad = lambda a: jnp.pad(a, (0, LANES - a.shape[0])).reshape(1, LANES)
    idx_w_scale = (IDX_HEADS ** -0.5) * (IDX_HEAD_DIM ** -0.5)
    kx = in_proj(functools.partial(_ep_index_keys, w_scale=idx_w_scale),
                 [(pad(kidx_ln_g), (1, LANES), _colblk), (pad(kidx_ln_b), (1, LANES), _colblk)]
                 + [(t, (tm, LANES), _rowblk) for t in (ic, ia, ib)],
                 F32, n=LANES, first=COL_KIDX, tm=tm, tn=LANES, name="in_proj_index_keys")
    k_idx = kx[:, :IDX_HEAD_DIM].astype(BF16)
    zeros = jnp.zeros_like(k_idx)
    k2 = jnp.stack([jnp.concatenate([k_idx, zeros], axis=1), jnp.concatenate([zeros, k_idx], axis=1)])
    idx_w_t = kx[:, IDX_HEAD_DIM:IDX_HEAD_DIM + IDX_HEADS].T
    bias = _indexer_mask(q_idx_t, idx_w_t, k2)
    y_b = _attention(q, k, v, bias)

    gates = _simple_mm(h, w_in_t[COL_GATE:], _ep_sigmoid, [], BF16, n=2 * d, tm=tm, tn=512,
                       name="in_proj_gates", kn=False)
    n_gate_blocks = d // 512
    merged = _matmul([y_a, y_b], [(w_proj_a, 0, True), (w_proj_b, 0, True)], [0, 1],
                     [(gates, (tm, 512), _tile),
                      (gates, (tm, 512), lambda j, i: (i, j + n_gate_blocks))],
                     [((s, d), BF16, (tm, 512), _tile)], _ep_merge, n=d, tm=tm, tn=512,
                     name="branch_merge")[0]
    tm_full = 256
    vec = lambda a: (a, (1, d), _colblk)
    x1, h2 = _matmul([merged], [(w_out, 0, True)], [0],
                     [(x, (tm_full, d), _rowblk), vec(row(g_post_mix)), vec(gate_m),
                      vec(row(g_pre_ffn)), vec(scale_f), vec(shift_f)],
                     [((s, d), F32, (tm_full, d), _tile), ((s, d), BF16, (tm_full, d), _tile)],
                     _ep_mix_residual, n=d, tm=tm_full, tn=d, name="out_proj_residual")

    ffn_hidden = w_ffn_gate.shape[1]
    act = _matmul([h2], [(w_ffn_gate, 0, True), (w_ffn_up, 0, True)], [0, 0], [],
                  [((s, ffn_hidden), BF16, (tm, 512), _tile)], _ep_swiglu, n=ffn_hidden, tm=tm, tn=512,
                  name="ffn_up")[0]
    return _ffn_down(act, w_ffn_down.astype(BF16), x1, row(g_post_ffn), gate_f)


def kernel(x, c, w_mod, b_mod, g_pre_mix, g_post_mix, w_in, gmlp_ln_g, gmlp_ln_b, gmlp_w_s, gmlp_b_s, q_lat_norm_g, w_q_up, w_qidx_up, kidx_ln_g, kidx_ln_b, w_proj_a, w_proj_b, w_out, g_pre_ffn, g_post_ffn, w_ffn_gate, w_ffn_up, w_ffn_down):
    batch, seq, d = x.shape
    assert batch == 1 and d == D_MODEL
    tables = _attn_rope_tables(seq) + _index_rope_tables(seq)
    y = x[0]
    for l in range(w_mod.shape[0]):
        mod = _modulation(c, w_mod[l], b_mod[l])
        y = _layer(y, mod, g_pre_mix[l], g_post_mix[l], w_in[l].T, gmlp_ln_g[l], gmlp_ln_b[l],
                   gmlp_w_s[l], gmlp_b_s[l], q_lat_norm_g[l], w_q_up[l], w_qidx_up[l],
                   kidx_ln_g[l], kidx_ln_b[l], w_proj_a[l], w_proj_b[l], w_out[l],
                   g_pre_ffn[l], g_post_ffn[l], w_ffn_gate[l], w_ffn_up[l], w_ffn_down[l], tables)
    return y[None]
```

```python
import functools
import math

import jax
import jax.numpy as jnp
from jax import lax
from jax.experimental import pallas as pl
from jax.experimental.pallas import tpu as pltpu

F32 = jnp.float32
BF16 = jnp.bfloat16

D_MODEL = 2048
GMLP_WIDTH = 1024
GMLP_GROUPS = 8
CHUNK = 128
ATTN_HEADS = 8
HEAD_DIM = 128
ATTN_WIDTH = ATTN_HEADS * HEAD_DIM
Q_LORA_RANK = 512
IDX_HEADS = 16
IDX_HEAD_DIM = 64
IDX_ROPE_DIM = 32
INDEX_TOPK = 256
ROPE_THETA = 10000.0
N_MOD = 6
NORM_EPS = 1e-6

COL_QLAT = 2 * GMLP_WIDTH
COL_K = COL_QLAT + Q_LORA_RANK
COL_V = COL_K + ATTN_WIDTH
COL_KIDX = COL_V + ATTN_WIDTH
COL_IDXW = COL_KIDX + IDX_HEAD_DIM
COL_GATE = COL_IDXW + IDX_HEADS

LANES = 128
SUBLANES = 8
VMEM_LIMIT = 56 * 1024 * 1024

MASK_BIAS = -(2.0 ** 100)
M_INIT = -(2.0 ** 99)
INT_MIN = -2 ** 31
KEY_NEG_INF = -2139095041

SEL_TQ = 256
IDX_KC = 256
KEY_BITS = 32
PLANE_KEYS = KEY_BITS * SUBLANES
ATT_KC = 2 * PLANE_KEYS


def _params(sem=None):
    return pltpu.CompilerParams(dimension_semantics=sem, vmem_limit_bytes=VMEM_LIMIT)


def _mod_body(cb_ref, w_ref, b_ref, o_ref):
    cb = cb_ref[...]
    tn = o_ref.shape[1]
    parts = [jnp.sum(w_ref[:, p * LANES:(p + 1) * LANES] * cb, axis=0, keepdims=True)
             for p in range(tn // LANES)]
    o_ref[...] = jnp.concatenate(parts, axis=1) + b_ref[...]


def _modulation(c, w_mod, b_mod):
    k, n = w_mod.shape
    tn = 1024
    cb = jnp.broadcast_to(c.reshape(k, 1), (k, LANES))
    return pl.pallas_call(
        _mod_body,
        grid=(n // tn,),
        in_specs=[pl.BlockSpec((k, LANES), lambda j: (0, 0)),
                  pl.BlockSpec((k, tn), lambda j: (0, j)),
                  pl.BlockSpec((1, tn), lambda j: (0, j))],
        out_specs=pl.BlockSpec((1, tn), lambda j: (0, j)),
        out_shape=jax.ShapeDtypeStruct((1, n), F32),
        compiler_params=_params(("arbitrary",)),
        name="modulation",
    )(cb, w_mod, b_mod.reshape(1, n))


def _rms(x, g):
    return x * lax.rsqrt(jnp.mean(x * x, axis=-1, keepdims=True) + NORM_EPS) * g


def _prenorm_body(x_ref, g_ref, scale_ref, shift_ref, o_ref):
    h = _rms(x_ref[...], g_ref[...]) * (1.0 + scale_ref[...]) + shift_ref[...]
    o_ref[...] = h.astype(o_ref.dtype)


def _prenorm(x, g, scale, shift):
    m, d = x.shape
    tm = 512
    row = pl.BlockSpec((1, d), lambda i: (0, 0))
    return pl.pallas_call(
        _prenorm_body,
        grid=(m // tm,),
        in_specs=[pl.BlockSpec((tm, d), lambda i: (i, 0)), row, row, row],
        out_specs=pl.BlockSpec((tm, d), lambda i: (i, 0)),
        out_shape=jax.ShapeDtypeStruct((m, d), BF16),
        compiler_params=_params(("arbitrary",)),
        name="prenorm",
    )(x, g, scale, shift)


def _mm_body(*refs, n_x, pairs, w_kn, out_t, n_extra, n_out, epilogue):
    n_w = len(pairs)
    x_refs = refs[:n_x]
    w_refs = refs[n_x:n_x + n_w]
    e_refs = refs[n_x + n_w:n_x + n_w + n_extra]
    o_refs = refs[n_x + n_w + n_extra:n_x + n_w + n_extra + n_out]
    wb_refs = refs[n_x + n_w + n_extra + n_out:]

    @pl.when(pl.program_id(1) == 0)
    def _():
        for w_ref, wb_ref, kn in zip(w_refs, wb_refs, w_kn):
            w = w_ref[...]
            if not kn and not out_t:
                w = w.T
            wb_ref[...] = w.astype(wb_ref.dtype)

    if out_t:
        accs = [lax.dot_general(wb_ref[...], x_refs[xi][...], (((1,), (1,)), ((), ())),
                                preferred_element_type=F32)
                for xi, wb_ref in zip(pairs, wb_refs)]
    else:
        accs = [jnp.dot(x_refs[xi][...], wb_ref[...], preferred_element_type=F32)
                for xi, wb_ref in zip(pairs, wb_refs)]
    epilogue(accs, e_refs, o_refs)


def _matmul(xs, ws, pairs, extras, outs, epilogue, *, n, tm, tn, name, out_t=False):
    m = xs[0].shape[0]
    n_col = n // tn
    w_mode = dict(pipeline_mode=pl.Buffered(1)) if n_col == 1 else {}
    in_specs = [pl.BlockSpec((tm, x.shape[1]), lambda j, i: (i, 0)) for x in xs]
    scratch = []
    for w, first, kn in ws:
        assert not (kn and out_t)
        if kn:
            kdim = w.shape[0]
            in_specs.append(pl.BlockSpec((kdim, tn), functools.partial(_w_cols, first=first), **w_mode))
        else:
            kdim = w.shape[1]
            in_specs.append(pl.BlockSpec((tn, kdim), functools.partial(_w_rows, first=first), **w_mode))
        scratch.append(pltpu.VMEM((tn, kdim) if out_t else (kdim, tn), BF16))
    in_specs += [pl.BlockSpec(bs, im) for _, bs, im in extras]
    body = functools.partial(_mm_body, n_x=len(xs), pairs=tuple(pairs), w_kn=tuple(kn for _, _, kn in ws),
                             out_t=out_t, n_extra=len(extras), n_out=len(outs), epilogue=epilogue)
    return pl.pallas_call(
        body,
        grid=(n_col, m // tm),
        in_specs=in_specs,
        out_specs=[pl.BlockSpec(bs, im) for _, _, bs, im in outs],
        out_shape=[jax.ShapeDtypeStruct(s, d) for s, d, _, _ in outs],
        scratch_shapes=scratch,
        compiler_params=_params(("arbitrary", "arbitrary")),
        name=name,
    )(*xs, *[w for w, _, _ in ws], *[a for a, _, _ in extras])


def _w_cols(j, i, *, first):
    return (0, first + j)


def _w_rows(j, i, *, first):
    return (first + j, 0)


def _tile(j, i):
    return (i, j)


def _tile_t(j, i):
    return (j, i)


def _rowblk(j, i):
    return (i, 0)


def _colblk(j, i):
    return (0, j)


def _colblk_t(j, i):
    return (0, i)


def _ep_gelu(accs, e_refs, o_refs):
    o_refs[0][...] = jax.nn.gelu(accs[0]).astype(o_refs[0].dtype)


def _ep_gelu_layernorm(accs, e_refs, o_refs):
    z = jax.nn.gelu(accs[0])
    mu = jnp.mean(z, axis=-1, keepdims=True)
    zc = z - mu
    var = jnp.mean(zc * zc, axis=-1, keepdims=True)
    y = zc * lax.rsqrt(var + NORM_EPS) * e_refs[0][...] + e_refs[1][...]
    o_refs[0][...] = y.astype(o_refs[0].dtype)


def _ep_rmsnorm(accs, e_refs, o_refs):
    o_refs[0][...] = _rms(accs[0], e_refs[0][...]).astype(o_refs[0].dtype)


def _ep_cast(accs, e_refs, o_refs):
    o_refs[0][...] = accs[0].astype(o_refs[0].dtype)


def _ep_sigmoid(accs, e_refs, o_refs):
    o_refs[0][...] = jax.nn.sigmoid(accs[0]).astype(o_refs[0].dtype)


def _ep_rope(accs, e_refs, o_refs, *, scale):
    cc = e_refs[0][...]
    ss = e_refs[1][...]
    acc = accs[0]
    for h in range(acc.shape[1] // HEAD_DIM):
        xh = acc[:, h * HEAD_DIM:(h + 1) * HEAD_DIM]
        r = xh * cc + pltpu.roll(xh, HEAD_DIM // 2, 1) * ss
        if scale != 1.0:
            r = r * scale
        o_refs[0][:, h * HEAD_DIM:(h + 1) * HEAD_DIM] = r.astype(o_refs[0].dtype)


def _ep_partial_rope_t(accs, e_refs, o_refs):
    c = e_refs[0][...]
    s = e_refs[1][...]
    acc = accs[0]
    half = IDX_ROPE_DIM // 2
    for h in range(acc.shape[0] // IDX_HEAD_DIM):
        r0 = h * IDX_HEAD_DIM
        x1 = acc[r0:r0 + half, :]
        x2 = acc[r0 + half:r0 + 2 * half, :]
        o_refs[0][r0:r0 + half, :] = (x1 * c - x2 * s).astype(o_refs[0].dtype)
        o_refs[0][r0 + half:r0 + 2 * half, :] = (x2 * c + x1 * s).astype(o_refs[0].dtype)
        o_refs[0][r0 + 2 * half:r0 + IDX_HEAD_DIM, :] = (
            acc[r0 + 2 * half:r0 + IDX_HEAD_DIM, :].astype(o_refs[0].dtype))


def _partial_rope(x, c, a, b):
    half = IDX_ROPE_DIM // 2
    return x * c + pltpu.roll(x, LANES - half, 1) * a + pltpu.roll(x, half, 1) * b


def _ep_index_keys(accs, e_refs, o_refs, *, w_scale):
    g, bb = e_refs[0][...], e_refs[1][...]
    c, a, b = e_refs[2][...], e_refs[3][...], e_refs[4][...]
    acc = accs[0]
    lane = lax.broadcasted_iota(jnp.int32, acc.shape, 1)
    is_key = lane < IDX_HEAD_DIM
    mu = jnp.sum(jnp.where(is_key, acc, 0.0), axis=-1, keepdims=True) / IDX_HEAD_DIM
    xc = jnp.where(is_key, acc - mu, 0.0)
    var = jnp.sum(xc * xc, axis=-1, keepdims=True) / IDX_HEAD_DIM
    y = xc * lax.rsqrt(var + NORM_EPS) * g + bb
    y = _partial_rope(y, c, a, b)
    o_refs[0][...] = jnp.where(is_key, y, acc * w_scale)


def _ep_merge(accs, e_refs, o_refs):
    o = e_refs[0][...].astype(F32) * accs[0] + e_refs[1][...].astype(F32) * accs[1]
    o_refs[0][...] = o.astype(o_refs[0].dtype)


def _ep_mix_residual(accs, e_refs, o_refs):
    x_ref, g_post, gate, g_pre, scale, shift = e_refs
    x1 = x_ref[...] + gate[...] * _rms(accs[0], g_post[...])
    o_refs[0][...] = x1
    h2 = _rms(x1, g_pre[...]) * (1.0 + scale[...]) + shift[...]
    o_refs[1][...] = h2.astype(o_refs[1].dtype)


def _ep_swiglu(accs, e_refs, o_refs):
    o_refs[0][...] = (jax.nn.silu(accs[0]) * accs[1]).astype(o_refs[0].dtype)


def _simple_mm(x, w, epilogue, extras, out_dtype, *, n, tm, tn, name, first=0, kn=True, out_t=False):
    m = x.shape[0]
    out = ((n, m), out_dtype, (tn, tm), _tile_t) if out_t else ((m, n), out_dtype, (tm, tn), _tile)
    return _matmul([x], [(w, first // tn, kn)], [0], extras, [out], epilogue, n=n, tm=tm, tn=tn,
                   name=name, out_t=out_t)[0]


def _gmlp_body(u_ref, v_ref, w_ref, b_ref, o_ref):
    t = lax.broadcasted_iota(jnp.int32, (CHUNK, CHUNK), 0)
    s = lax.broadcasted_iota(jnp.int32, (CHUNK, CHUNK), 1)
    causal = s <= t
    for g in range(GMLP_GROUPS):
        w = jnp.where(causal, w_ref[g], 0.0).astype(BF16)
        cols = slice(g * LANES, (g + 1) * LANES)
        bias = b_ref[:, cols]
        for c in range(u_ref.shape[0] // CHUNK):
            rows = slice(c * CHUNK, (c + 1) * CHUNK)
            sv = jnp.dot(w, v_ref[rows, cols], preferred_element_type=F32) + bias
            o_ref[rows, cols] = (u_ref[rows, cols].astype(F32) * sv).astype(o_ref.dtype)


def _gmlp(u, vn, w_s, bias):
    m, n = u.shape
    tm = 512
    blk = pl.BlockSpec((tm, n), lambda i: (i, 0))
    return pl.pallas_call(
        _gmlp_body,
        grid=(m // tm,),
        in_specs=[blk, blk,
                  pl.BlockSpec(w_s.shape, lambda i: (0, 0, 0)),
                  pl.BlockSpec(bias.shape, lambda i: (0, 0))],
        out_specs=blk,
        out_shape=jax.ShapeDtypeStruct((m, n), BF16),
        compiler_params=_params(("arbitrary",)),
        name="gmlp_gating",
    )(u, vn, w_s, bias)


def _bit_planes(words):
    a = list(words)
    j, mask = KEY_BITS // 2, 0x0000FFFF
    while j:
        for k in range(KEY_BITS):
            if k & j:
                continue
            t = (a[k] ^ (a[k + j] >> j)) & jnp.int32(mask)
            a[k] = a[k] ^ t
            a[k + j] = a[k + j] ^ (t << j)
        j //= 2
        mask = (mask ^ (mask << j)) & 0xFFFFFFFF
    return a


def _indexer_body(qt_ref, wt_ref, k2_ref, bias_ref, key_ref, plane_ref, live_ref):
    tq = qt_ref.shape[1]
    n_chunks_total = bias_ref.shape[0]
    i = pl.program_id(0)
    n_keys = (i + 1) * tq
    n_blocks = n_keys // PLANE_KEYS
    heads_per_group = LANES // IDX_HEAD_DIM

    @pl.when(i == 0)
    def _():
        plane_ref[...] = jnp.zeros(plane_ref.shape, jnp.int32)

    q_pos = i * tq + lax.broadcasted_iota(jnp.int32, (IDX_KC, tq), 1)
    k_off = lax.broadcasted_iota(jnp.int32, (IDX_KC, tq), 0)

    def score_chunk(j, carry):
        k0 = pl.multiple_of(j * IDX_KC, IDX_KC)
        score = jnp.zeros((IDX_KC, tq), F32)
        for h in range(IDX_HEADS):
            grp, r = divmod(h, heads_per_group)
            logit = jnp.dot(k2_ref[r, pl.ds(k0, IDX_KC), :], qt_ref[grp * LANES:(grp + 1) * LANES, :],
                            preferred_element_type=F32)
            score = score + jnp.maximum(logit, 0.0) * wt_ref[h:h + 1, :]
        score = jnp.where(k0 + k_off <= q_pos, score, -jnp.inf)
        bits = lax.bitcast_convert_type(score, jnp.int32)
        key_ref[pl.ds(k0, IDX_KC), :] = bits ^ ((bits >> 31) & jnp.int32(0x7FFFFFFF))
        return carry

    lax.fori_loop(0, n_keys // IDX_KC, score_chunk, 0)

    @pl.when(n_blocks % (ATT_KC // PLANE_KEYS) == 1)
    def _():
        pad0 = pl.multiple_of(n_keys, PLANE_KEYS)
        key_ref[pl.ds(pad0, PLANE_KEYS), :] = jnp.full((PLANE_KEYS, tq), KEY_NEG_INF, jnp.int32)

    def slice_block(blk, carry):
        k0 = pl.multiple_of(blk * PLANE_KEYS, PLANE_KEYS)
        keys = key_ref[pl.ds(k0, PLANE_KEYS), :]
        planes = _bit_planes([keys[c * SUBLANES:(c + 1) * SUBLANES, :] for c in range(KEY_BITS)])
        r0 = pl.multiple_of(blk * SUBLANES, SUBLANES)
        plane_ref[0, pl.ds(r0, SUBLANES), :] = ~planes[0]
        for r in range(1, KEY_BITS):
            plane_ref[r, pl.ds(r0, SUBLANES), :] = planes[r]
        return carry

    lax.fori_loop(0, n_blocks, slice_block, 0)

    plane_rows = live_ref.shape[0]
    row_id = lax.broadcasted_iota(jnp.int32, (plane_rows, tq), 0)
    live_ref[...] = jnp.where(row_id < n_blocks * SUBLANES, jnp.int32(-1), jnp.int32(0))

    def search_bit(r, carry):
        need, kth = carry
        plane = plane_ref[r]
        live = live_ref[...]
        hit = live & plane
        ones = lax.population_count(hit)
        part = jnp.sum(ones.reshape(plane_rows // SUBLANES, SUBLANES, tq), axis=0)
        total = jnp.sum(part.astype(F32), axis=0, keepdims=True)
        take = total >= need
        live_ref[...] = jnp.where(take, hit, live & ~plane)
        need = jnp.where(take, need, need - total)
        kth = jnp.where(take, kth | jnp.left_shift(jnp.int32(1), KEY_BITS - 1 - r), kth)
        return need, kth

    _, kth = lax.fori_loop(0, KEY_BITS, search_bit,
                           (jnp.full((1, tq), float(INDEX_TOPK), F32), jnp.zeros((1, tq), jnp.int32)))
    kth = kth ^ jnp.int32(INT_MIN)
    kth = jnp.maximum(kth, KEY_NEG_INF + 1)

    halves = ATT_KC // PLANE_KEYS
    n_chunks = (n_blocks + halves - 1) // halves

    def write_chunk(c, carry):
        parts = []
        for hh in range(halves):
            k0 = pl.multiple_of(c * ATT_KC + hh * PLANE_KEYS, PLANE_KEYS)
            sel = key_ref[pl.ds(k0, PLANE_KEYS), :] >= kth
            parts.append(jnp.where(sel, 0.0, MASK_BIAS).T)
        bias_ref[c] = jnp.concatenate(parts, axis=1).astype(bias_ref.dtype)
        return carry

    lax.fori_loop(0, n_chunks, write_chunk, 0)

    def fill_chunk(c, carry):
        bias_ref[c] = jnp.full(bias_ref.shape[1:], MASK_BIAS, bias_ref.dtype)
        return carry

    lax.fori_loop(n_chunks, n_chunks_total, fill_chunk, 0)


def _indexer_mask(q_idx_t, idx_w_t, k2):
    s = q_idx_t.shape[1]
    tq = SEL_TQ
    assert tq % PLANE_KEYS == 0 and s % (KEY_BITS * SUBLANES) == 0
    return pl.pallas_call(
        _indexer_body,
        grid=(s // tq,),
        in_specs=[pl.BlockSpec((q_idx_t.shape[0], tq), lambda i: (0, i)),
                  pl.BlockSpec((IDX_HEADS, tq), lambda i: (0, i)),
                  pl.BlockSpec(k2.shape, lambda i: (0, 0, 0))],
        out_specs=pl.BlockSpec((s // ATT_KC, tq, ATT_KC), lambda i: (0, i, 0)),
        out_shape=jax.ShapeDtypeStruct((s // ATT_KC, s, ATT_KC), BF16),
        scratch_shapes=[pltpu.VMEM((s, tq), jnp.int32),
                        pltpu.VMEM((KEY_BITS, s // KEY_BITS, tq), jnp.int32),
                        pltpu.VMEM((s // KEY_BITS, tq), jnp.int32)],
        compiler_params=_params(("arbitrary",)),
        name="indexer_mask",
    )(q_idx_t, idx_w_t, k2)


def _attention_body(q_ref, k_ref, v_ref, bias_ref, o_ref, m_ref, acc_ref):
    tq = q_ref.shape[0]
    kc = bias_ref.shape[2]
    i = pl.program_id(0)
    n_chunks = (i * tq + tq - 1) // kc + 1
    sub = kc // LANES

    m_ref[...] = jnp.full(m_ref.shape, M_INIT, F32)
    acc_ref[...] = jnp.zeros(acc_ref.shape, F32)

    def chunk(j, carry):
        k0 = pl.multiple_of(j * kc, kc)
        bias = bias_ref[j]
        ones = jnp.ones((kc, HEAD_DIM), v_ref.dtype)
        for h in range(ATTN_HEADS):
            cols = slice(h * HEAD_DIM, (h + 1) * HEAD_DIM)
            s = lax.dot_general(q_ref[:, cols], k_ref[pl.ds(k0, kc), cols],
                                (((1,), (1,)), ((), ())), preferred_element_type=F32)
            sb = s.astype(BF16) + bias
            part = sb[:, :LANES]
            for c in range(1, sub):
                part = jnp.maximum(part, sb[:, c * LANES:(c + 1) * LANES])
            m_prev = m_ref[h]
            m_next = jnp.maximum(m_prev, jnp.max(part.astype(F32), axis=1, keepdims=True))
            p = jnp.exp2(sb - jnp.concatenate([m_next.astype(BF16)] * sub, axis=1))
            alpha = jnp.exp2(m_prev - m_next)
            m_ref[h] = m_next
            pv = jnp.dot(p, jnp.concatenate([v_ref[pl.ds(k0, kc), cols], ones], axis=1),
                         preferred_element_type=F32)
            acc_ref[h] = jnp.concatenate([alpha, alpha], axis=1) * acc_ref[h] + pv
        return carry

    lax.fori_loop(0, n_chunks, chunk, 0)

    for h in range(ATTN_HEADS):
        acc = acc_ref[h]
        o_ref[:, h * HEAD_DIM:(h + 1) * HEAD_DIM] = (acc[:, :HEAD_DIM] / acc[:, HEAD_DIM:]).astype(o_ref.dtype)


def _attention(q, k, v, bias):
    s, width = q.shape
    tq = SEL_TQ
    n_chunks, _, kc = bias.shape
    resident = dict(pipeline_mode=pl.Buffered(1))
    return pl.pallas_call(
        _attention_body,
        grid=(s // tq,),
        in_specs=[pl.BlockSpec((tq, width), lambda i: (i, 0)),
                  pl.BlockSpec((s, width), lambda i: (0, 0), **resident),
                  pl.BlockSpec((s, width), lambda i: (0, 0), **resident),
                  pl.BlockSpec((n_chunks, tq, kc), lambda i: (0, i, 0))],
        out_specs=pl.BlockSpec((tq, width), lambda i: (i, 0)),
        out_shape=jax.ShapeDtypeStruct((s, width), BF16),
        scratch_shapes=[pltpu.VMEM((ATTN_HEADS, tq, LANES), F32),
                        pltpu.VMEM((ATTN_HEADS, tq, 2 * HEAD_DIM), F32)],
        compiler_params=_params(("arbitrary",)),
        name="masked_attention",
    )(q, k, v, bias)


def _ffn_down_body(a_ref, w_ref, x_ref, g_ref, gate_ref, o_ref, acc_ref):
    k = pl.program_id(1)

    @pl.when(k == 0)
    def _():
        acc_ref[...] = jnp.zeros(acc_ref.shape, F32)

    acc_ref[...] += jnp.dot(a_ref[...], w_ref[...], preferred_element_type=F32)

    @pl.when(k == pl.num_programs(1) - 1)
    def _():
        o_ref[...] = x_ref[...] + gate_ref[...] * _rms(acc_ref[...], g_ref[...])


def _ffn_down(a, w, x1, g_post, gate):
    m, kdim = a.shape
    d = w.shape[1]
    tm, tk = 512, kdim // 4
    row = pl.BlockSpec((1, d), lambda i, k: (0, 0))
    return pl.pallas_call(
        _ffn_down_body,
        grid=(m // tm, kdim // tk),
        in_specs=[pl.BlockSpec((tm, tk), lambda i, k: (i, k)),
                  pl.BlockSpec((tk, d), lambda i, k: (k, 0)),
                  pl.BlockSpec((tm, d), lambda i, k: (i, 0)), row, row],
        out_specs=pl.BlockSpec((tm, d), lambda i, k: (i, 0)),
        out_shape=jax.ShapeDtypeStruct((m, d), F32),
        scratch_shapes=[pltpu.VMEM((tm, d), F32)],
        compiler_params=_params(("arbitrary", "arbitrary")),
        name="ffn_down_residual",
    )(a, w, x1, g_post, gate)


def _rope_angles(seq, dim):
    inv = 1.0 / (ROPE_THETA ** (jnp.arange(0, dim, 2, dtype=F32) / dim))
    ang = jnp.arange(seq, dtype=F32)[:, None] * inv[None, :]
    return jnp.cos(ang), jnp.sin(ang)


def _attn_rope_tables(seq):
    cos, sin = _rope_angles(seq, HEAD_DIM)
    return jnp.concatenate([cos, cos], axis=1), jnp.concatenate([-sin, sin], axis=1)


def _index_rope_tables(seq):
    cos, sin = _rope_angles(seq, IDX_ROPE_DIM)
    half = IDX_ROPE_DIM // 2
    rest = IDX_HEAD_DIM - IDX_ROPE_DIM
    zeros = lambda n: jnp.zeros((seq, n), F32)
    c = jnp.concatenate([cos, cos, jnp.ones((seq, rest), F32)], axis=1)
    a = jnp.concatenate([-sin, zeros(IDX_HEAD_DIM - half)], axis=1)
    b = jnp.concatenate([zeros(half), sin, zeros(rest)], axis=1)
    rep = LANES // IDX_HEAD_DIM
    return tuple(jnp.tile(t, (1, rep)) for t in (c, a, b)) + (cos.T, sin.T)


def _layer(x, mod, g_pre_mix, g_post_mix, w_in_t, gmlp_ln_g, gmlp_ln_b, gmlp_w_s, gmlp_b_s,
           q_lat_norm_g, w_q_up, w_qidx_up, kidx_ln_g, kidx_ln_b, w_proj_a, w_proj_b, w_out,
           g_pre_ffn, g_post_ffn, w_ffn_gate, w_ffn_up, w_ffn_down, tables):
    s, d = x.shape
    cc, ss, ic, ia, ib, icos_t, isin_t = tables
    row = lambda a: a.reshape(1, -1)
    shift_m, scale_m, gate_m, shift_f, scale_f, gate_f = (mod[:, n * d:(n + 1) * d] for n in range(N_MOD))

    h = _prenorm(x, row(g_pre_mix), scale_m, shift_m)

    tm = 1024
    in_proj = functools.partial(_simple_mm, h, w_in_t, kn=False)

    u = in_proj(_ep_gelu, [], BF16, n=GMLP_WIDTH, tm=tm, tn=512, name="in_proj_u")
    vn = in_proj(_ep_gelu_layernorm,
                 [(row(gmlp_ln_g), (1, GMLP_WIDTH), _colblk), (row(gmlp_ln_b), (1, GMLP_WIDTH), _colblk)],
                 BF16, n=GMLP_WIDTH, first=GMLP_WIDTH, tm=512, tn=GMLP_WIDTH, name="in_proj_v_ln")
    gmlp_bias = jnp.repeat(gmlp_b_s.T, LANES, axis=1)
    y_a = _gmlp(u, vn, gmlp_w_s, gmlp_bias)

    q_lat = in_proj(_ep_rmsnorm, [(row(q_lat_norm_g), (1, Q_LORA_RANK), _colblk)],
                    BF16, n=Q_LORA_RANK, first=COL_QLAT, tm=tm, tn=Q_LORA_RANK, name="in_proj_qlat")
    q_scale = HEAD_DIM ** -0.5 * math.log2(math.e)
    rope_extras = [(cc, (tm, HEAD_DIM), _rowblk), (ss, (tm, HEAD_DIM), _rowblk)]
    q = _simple_mm(q_lat, w_q_up, functools.partial(_ep_rope, scale=q_scale), rope_extras,
                   BF16, n=ATTN_WIDTH, tm=tm, tn=512, name="q_up_rope")
    k = in_proj(functools.partial(_ep_rope, scale=1.0), rope_extras, BF16,
                n=ATTN_WIDTH, first=COL_K, tm=tm, tn=512, name="in_proj_k_rope")
    v = in_proj(_ep_cast, [], BF16, n=ATTN_WIDTH, first=COL_V, tm=tm, tn=512, name="in_proj_v")
    half = IDX_ROPE_DIM // 2
    q_idx_t = _simple_mm(q_lat, w_qidx_up.T, _ep_partial_rope_t,
                         [(icos_t, (half, tm), _colblk_t), (isin_t, (half, tm), _colblk_t)],
                         BF16, n=IDX_HEADS * IDX_HEAD_DIM, tm=tm, tn=512, name="qidx_up_rope",
                         kn=False, out_t=True)
    pad = lambda a: jnp.pad(a, (0, LANES - a.shape[0])).reshape(1, LANES)
    idx_w_scale = (IDX_HEADS ** -0.5) * (IDX_HEAD_DIM ** -0.5)
    kx = in_proj(functools.partial(_ep_index_keys, w_scale=idx_w_scale),
                 [(pad(kidx_ln_g), (1, LANES), _colblk), (pad(kidx_ln_b), (1, LANES), _colblk)]
                 + [(t, (tm, LANES), _rowblk) for t in (ic, ia, ib)],
                 F32, n=LANES, first=COL_KIDX, tm=tm, tn=LANES, name="in_proj_index_keys")
    k_idx = kx[:, :IDX_HEAD_DIM].astype(BF16)
    zeros = jnp.zeros_like(k_idx)
    k2 = jnp.stack([jnp.concatenate([k_idx, zeros], axis=1), jnp.concatenate([zeros, k_idx], axis=1)])
    idx_w_t = kx[:, IDX_HEAD_DIM:IDX_HEAD_DIM + IDX_HEADS].T
    bias = _indexer_mask(q_idx_t, idx_w_t, k2)
    y_b = _attention(q, k, v, bias)

    gates = _simple_mm(h, w_in_t[COL_GATE:], _ep_sigmoid, [], BF16, n=2 * d, tm=tm, tn=512,
                       name="in_proj_gates", kn=False)
    n_gate_blocks = d // 512
    merged = _matmul([y_a, y_b], [(w_proj_a, 0, True), (w_proj_b, 0, True)], [0, 1],
                     [(gates, (tm, 512), _tile),
                      (gates, (tm, 512), lambda j, i: (i, j + n_gate_blocks))],
                     [((s, d), BF16, (tm, 512), _tile)], _ep_merge, n=d, tm=tm, tn=512,
                     name="branch_merge")[0]
    tm_full = 256
    vec = lambda a: (a, (1, d), _colblk)
    x1, h2 = _matmul([merged], [(w_out, 0, True)], [0],
                     [(x, (tm_full, d), _rowblk), vec(row(g_post_mix)), vec(gate_m),
                      vec(row(g_pre_ffn)), vec(scale_f), vec(shift_f)],
                     [((s, d), F32, (tm_full, d), _tile), ((s, d), BF16, (tm_full, d), _tile)],
                     _ep_mix_residual, n=d, tm=tm_full, tn=d, name="out_proj_residual")

    ffn_hidden = w_ffn_gate.shape[1]
    act = _matmul([h2], [(w_ffn_gate, 0, True), (w_ffn_up, 0, True)], [0, 0], [],
                  [((s, ffn_hidden), BF16, (tm, 512), _tile)], _ep_swiglu, n=ffn_hidden, tm=tm, tn=512,
                  name="ffn_up")[0]
    return _ffn_down(act, w_ffn_down.astype(BF16), x1, row(g_post_ffn), gate_f)


def kernel(x, c, w_mod, b_mod, g_pre_mix, g_post_mix, w_in, gmlp_ln_g, gmlp_ln_b, gmlp_w_s, gmlp_b_s, q_lat_norm_g, w_q_up, w_qidx_up, kidx_ln_g, kidx_ln_b, w_proj_a, w_proj_b, w_out, g_pre_ffn, g_post_ffn, w_ffn_gate, w_ffn_up, w_ffn_down):
    batch, seq, d = x.shape
    assert batch == 1 and d == D_MODEL
    tables = _attn_rope_tables(seq) + _index_rope_tables(seq)
    y = x[0]
    for l in range(w_mod.shape[0]):
        mod = _modulation(c, w_mod[l], b_mod[l])
        y = _layer(y, mod, g_pre_mix[l], g_post_mix[l], w_in[l].T, gmlp_ln_g[l], gmlp_ln_b[l],
                   gmlp_w_s[l], gmlp_b_s[l], q_lat_norm_g[l], w_q_up[l], w_qidx_up[l],
                   kidx_ln_g[l], kidx_ln_b[l], w_proj_a[l], w_proj_b[l], w_out[l],
                   g_pre_ffn[l], g_post_ffn[l], w_ffn_gate[l], w_ffn_up[l], w_ffn_down[l], tables)
    return y[None]
```

```python
import functools
import math

import jax
import jax.numpy as jnp
from jax import lax
from jax.experimental import pallas as pl
from jax.experimental.pallas import tpu as pltpu

F32 = jnp.float32
BF16 = jnp.bfloat16

D_MODEL = 2048
GMLP_WIDTH = 1024
GMLP_GROUPS = 8
CHUNK = 128
ATTN_HEADS = 8
HEAD_DIM = 128
ATTN_WIDTH = ATTN_HEADS * HEAD_DIM
Q_LORA_RANK = 512
IDX_HEADS = 16
IDX_HEAD_DIM = 64
IDX_ROPE_DIM = 32
INDEX_TOPK = 256
ROPE_THETA = 10000.0
N_MOD = 6
NORM_EPS = 1e-6

COL_QLAT = 2 * GMLP_WIDTH
COL_K = COL_QLAT + Q_LORA_RANK
COL_V = COL_K + ATTN_WIDTH
COL_KIDX = COL_V + ATTN_WIDTH
COL_IDXW = COL_KIDX + IDX_HEAD_DIM
COL_GATE = COL_IDXW + IDX_HEADS

LANES = 128
SUBLANES = 8
VMEM_LIMIT = 56 * 1024 * 1024

MASK_BIAS = -(2.0 ** 100)
M_INIT = -(2.0 ** 99)
INT_MIN = -2 ** 31
KEY_NEG_INF = -2139095041

SEL_TQ = 256
KEY_BITS = 32
PLANE_KEYS = KEY_BITS * SUBLANES
ATT_KC = 2 * PLANE_KEYS


def _params(sem=None):
    return pltpu.CompilerParams(dimension_semantics=sem, vmem_limit_bytes=VMEM_LIMIT)


def _mod_body(cb_ref, w_ref, b_ref, o_ref):
    cb = cb_ref[...]
    tn = o_ref.shape[1]
    parts = [jnp.sum(w_ref[:, p * LANES:(p + 1) * LANES] * cb, axis=0, keepdims=True)
             for p in range(tn // LANES)]
    o_ref[...] = jnp.concatenate(parts, axis=1) + b_ref[...]


def _modulation(c, w_mod, b_mod):
    k, n = w_mod.shape
    tn = 1024
    cb = jnp.broadcast_to(c.reshape(k, 1), (k, LANES))
    return pl.pallas_call(
        _mod_body,
        grid=(n // tn,),
        in_specs=[pl.BlockSpec((k, LANES), lambda j: (0, 0)),
                  pl.BlockSpec((k, tn), lambda j: (0, j)),
                  pl.BlockSpec((1, tn), lambda j: (0, j))],
        out_specs=pl.BlockSpec((1, tn), lambda j: (0, j)),
        out_shape=jax.ShapeDtypeStruct((1, n), F32),
        compiler_params=_params(("arbitrary",)),
        name="modulation",
    )(cb, w_mod, b_mod.reshape(1, n))


def _rms(x, g):
    return x * lax.rsqrt(jnp.mean(x * x, axis=-1, keepdims=True) + NORM_EPS) * g


def _prenorm_body(x_ref, g_ref, scale_ref, shift_ref, o_ref):
    h = _rms(x_ref[...], g_ref[...]) * (1.0 + scale_ref[...]) + shift_ref[...]
    o_ref[...] = h.astype(o_ref.dtype)


def _prenorm(x, g, scale, shift):
    m, d = x.shape
    tm = 512
    row = pl.BlockSpec((1, d), lambda i: (0, 0))
    return pl.pallas_call(
        _prenorm_body,
        grid=(m // tm,),
        in_specs=[pl.BlockSpec((tm, d), lambda i: (i, 0)), row, row, row],
        out_specs=pl.BlockSpec((tm, d), lambda i: (i, 0)),
        out_shape=jax.ShapeDtypeStruct((m, d), BF16),
        compiler_params=_params(("arbitrary",)),
        name="prenorm",
    )(x, g, scale, shift)


def _mm_body(*refs, n_x, pairs, w_kn, out_t, n_extra, n_out, epilogue):
    n_w = len(pairs)
    x_refs = refs[:n_x]
    w_refs = refs[n_x:n_x + n_w]
    e_refs = refs[n_x + n_w:n_x + n_w + n_extra]
    o_refs = refs[n_x + n_w + n_extra:n_x + n_w + n_extra + n_out]
    wb_refs = refs[n_x + n_w + n_extra + n_out:]

    @pl.when(pl.program_id(1) == 0)
    def _():
        for w_ref, wb_ref, kn in zip(w_refs, wb_refs, w_kn):
            w = w_ref[...]
            if not kn and not out_t:
                w = w.T
            wb_ref[...] = w.astype(wb_ref.dtype)

    if out_t:
        accs = [lax.dot_general(wb_ref[...], x_refs[xi][...], (((1,), (1,)), ((), ())),
                                preferred_element_type=F32)
                for xi, wb_ref in zip(pairs, wb_refs)]
    else:
        accs = [jnp.dot(x_refs[xi][...], wb_ref[...], preferred_element_type=F32)
                for xi, wb_ref in zip(pairs, wb_refs)]
    epilogue(accs, e_refs, o_refs)


def _matmul(xs, ws, pairs, extras, outs, epilogue, *, n, tm, tn, name, out_t=False):
    m = xs[0].shape[0]
    n_col = n // tn
    w_mode = dict(pipeline_mode=pl.Buffered(1)) if n_col == 1 else {}
    in_specs = [pl.BlockSpec((tm, x.shape[1]), lambda j, i: (i, 0)) for x in xs]
    scratch = []
    for w, first, kn in ws:
        assert not (kn and out_t)
        if kn:
            kdim = w.shape[0]
            in_specs.append(pl.BlockSpec((kdim, tn), functools.partial(_w_cols, first=first), **w_mode))
        else:
            kdim = w.shape[1]
            in_specs.append(pl.BlockSpec((tn, kdim), functools.partial(_w_rows, first=first), **w_mode))
        scratch.append(pltpu.VMEM((tn, kdim) if out_t else (kdim, tn), BF16))
    in_specs += [pl.BlockSpec(bs, im) for _, bs, im in extras]
    body = functools.partial(_mm_body, n_x=len(xs), pairs=tuple(pairs), w_kn=tuple(kn for _, _, kn in ws),
                             out_t=out_t, n_extra=len(extras), n_out=len(outs), epilogue=epilogue)
    return pl.pallas_call(
        body,
        grid=(n_col, m // tm),
        in_specs=in_specs,
        out_specs=[pl.BlockSpec(bs, im) for _, _, bs, im in outs],
        out_shape=[jax.ShapeDtypeStruct(s, d) for s, d, _, _ in outs],
        scratch_shapes=scratch,
        compiler_params=_params(("arbitrary", "arbitrary")),
        name=name,
    )(*xs, *[w for w, _, _ in ws], *[a for a, _, _ in extras])


def _w_cols(j, i, *, first):
    return (0, first + j)


def _w_rows(j, i, *, first):
    return (first + j, 0)


def _tile(j, i):
    return (i, j)


def _tile_t(j, i):
    return (j, i)


def _rowblk(j, i):
    return (i, 0)


def _colblk(j, i):
    return (0, j)


def _colblk_t(j, i):
    return (0, i)


def _ep_gelu(accs, e_refs, o_refs):
    o_refs[0][...] = jax.nn.gelu(accs[0]).astype(o_refs[0].dtype)


def _ep_gelu_layernorm(accs, e_refs, o_refs):
    z = jax.nn.gelu(accs[0])
    mu = jnp.mean(z, axis=-1, keepdims=True)
    zc = z - mu
    var = jnp.mean(zc * zc, axis=-1, keepdims=True)
    y = zc * lax.rsqrt(var + NORM_EPS) * e_refs[0][...] + e_refs[1][...]
    o_refs[0][...] = y.astype(o_refs[0].dtype)


def _ep_rmsnorm(accs, e_refs, o_refs):
    o_refs[0][...] = _rms(accs[0], e_refs[0][...]).astype(o_refs[0].dtype)


def _ep_cast(accs, e_refs, o_refs):
    o_refs[0][...] = accs[0].astype(o_refs[0].dtype)


def _ep_sigmoid(accs, e_refs, o_refs):
    o_refs[0][...] = jax.nn.sigmoid(accs[0]).astype(o_refs[0].dtype)


def _ep_rope(accs, e_refs, o_refs, *, scale):
    cc = e_refs[0][...]
    ss = e_refs[1][...]
    acc = accs[0]
    for h in range(acc.shape[1] // HEAD_DIM):
        xh = acc[:, h * HEAD_DIM:(h + 1) * HEAD_DIM]
        r = xh * cc + pltpu.roll(xh, HEAD_DIM // 2, 1) * ss
        if scale != 1.0:
            r = r * scale
        o_refs[0][:, h * HEAD_DIM:(h + 1) * HEAD_DIM] = r.astype(o_refs[0].dtype)


def _ep_partial_rope_t(accs, e_refs, o_refs):
    c = e_refs[0][...]
    s = e_refs[1][...]
    acc = accs[0]
    half = IDX_ROPE_DIM // 2
    for h in range(acc.shape[0] // IDX_HEAD_DIM):
        r0 = h * IDX_HEAD_DIM
        x1 = acc[r0:r0 + half, :]
        x2 = acc[r0 + half:r0 + 2 * half, :]
        o_refs[0][r0:r0 + half, :] = (x1 * c - x2 * s).astype(o_refs[0].dtype)
        o_refs[0][r0 + half:r0 + 2 * half, :] = (x2 * c + x1 * s).astype(o_refs[0].dtype)
        o_refs[0][r0 + 2 * half:r0 + IDX_HEAD_DIM, :] = (
            acc[r0 + 2 * half:r0 + IDX_HEAD_DIM, :].astype(o_refs[0].dtype))


def _partial_rope(x, c, a, b):
    half = IDX_ROPE_DIM // 2
    return x * c + pltpu.roll(x, LANES - half, 1) * a + pltpu.roll(x, half, 1) * b


def _ep_index_keys(accs, e_refs, o_refs, *, w_scale):
    g, bb = e_refs[0][...], e_refs[1][...]
    c, a, b = e_refs[2][...], e_refs[3][...], e_refs[4][...]
    acc = accs[0]
    lane = lax.broadcasted_iota(jnp.int32, acc.shape, 1)
    is_key = lane < IDX_HEAD_DIM
    mu = jnp.sum(jnp.where(is_key, acc, 0.0), axis=-1, keepdims=True) / IDX_HEAD_DIM
    xc = jnp.where(is_key, acc - mu, 0.0)
    var = jnp.sum(xc * xc, axis=-1, keepdims=True) / IDX_HEAD_DIM
    y = xc * lax.rsqrt(var + NORM_EPS) * g + bb
    y = _partial_rope(y, c, a, b)
    o_refs[0][...] = jnp.where(is_key, y, acc * w_scale)


def _ep_merge(accs, e_refs, o_refs):
    o = e_refs[0][...].astype(F32) * accs[0] + e_refs[1][...].astype(F32) * accs[1]
    o_refs[0][...] = o.astype(o_refs[0].dtype)


def _ep_mix_residual(accs, e_refs, o_refs):
    x_ref, g_post, gate, g_pre, scale, shift = e_refs
    x1 = x_ref[...] + gate[...] * _rms(accs[0], g_post[...])
    o_refs[0][...] = x1
    h2 = _rms(x1, g_pre[...]) * (1.0 + scale[...]) + shift[...]
    o_refs[1][...] = h2.astype(o_refs[1].dtype)


def _ep_swiglu(accs, e_refs, o_refs):
    o_refs[0][...] = (jax.nn.silu(accs[0]) * accs[1]).astype(o_refs[0].dtype)


def _simple_mm(x, w, epilogue, extras, out_dtype, *, n, tm, tn, name, first=0, kn=True, out_t=False):
    m = x.shape[0]
    out = ((n, m), out_dtype, (tn, tm), _tile_t) if out_t else ((m, n), out_dtype, (tm, tn), _tile)
    return _matmul([x], [(w, first // tn, kn)], [0], extras, [out], epilogue, n=n, tm=tm, tn=tn,
                   name=name, out_t=out_t)[0]


def _gmlp_body(u_ref, v_ref, w_ref, b_ref, o_ref):
    t = lax.broadcasted_iota(jnp.int32, (CHUNK, CHUNK), 0)
    s = lax.broadcasted_iota(jnp.int32, (CHUNK, CHUNK), 1)
    causal = s <= t
    for g in range(GMLP_GROUPS):
        w = jnp.where(causal, w_ref[g], 0.0).astype(BF16)
        cols = slice(g * LANES, (g + 1) * LANES)
        bias = b_ref[:, cols]
        for c in range(u_ref.shape[0] // CHUNK):
            rows = slice(c * CHUNK, (c + 1) * CHUNK)
            sv = jnp.dot(w, v_ref[rows, cols], preferred_element_type=F32) + bias
            o_ref[rows, cols] = (u_ref[rows, cols].astype(F32) * sv).astype(o_ref.dtype)


def _gmlp(u, vn, w_s, bias):
    m, n = u.shape
    tm = 512
    blk = pl.BlockSpec((tm, n), lambda i: (i, 0))
    return pl.pallas_call(
        _gmlp_body,
        grid=(m // tm,),
        in_specs=[blk, blk,
                  pl.BlockSpec(w_s.shape, lambda i: (0, 0, 0)),
                  pl.BlockSpec(bias.shape, lambda i: (0, 0))],
        out_specs=blk,
        out_shape=jax.ShapeDtypeStruct((m, n), BF16),
        compiler_params=_params(("arbitrary",)),
        name="gmlp_gating",
    )(u, vn, w_s, bias)


def _bit_planes(words):
    a = list(words)
    j, mask = KEY_BITS // 2, 0x0000FFFF
    while j:
        for k in range(KEY_BITS):
            if k & j:
                continue
            t = (a[k] ^ (a[k + j] >> j)) & jnp.int32(mask)
            a[k] = a[k] ^ t
            a[k + j] = a[k + j] ^ (t << j)
        j //= 2
        mask = (mask ^ (mask << j)) & 0xFFFFFFFF
    return a


def _indexer_body(qt_ref, wt_ref, k2_ref, bias_ref, key_ref, plane_ref, live_ref):
    tq = qt_ref.shape[1]
    n_chunks_total = bias_ref.shape[0]
    i = pl.program_id(0)
    n_chunks = ((i + 1) * tq + ATT_KC - 1) // ATT_KC
    blocks_per_chunk = ATT_KC // PLANE_KEYS
    n_blocks = n_chunks * blocks_per_chunk
    heads_per_group = LANES // IDX_HEAD_DIM

    @pl.when(i == 0)
    def _():
        plane_ref[...] = jnp.zeros(plane_ref.shape, jnp.int32)

    q_pos = i * tq + lax.broadcasted_iota(jnp.int32, (ATT_KC, tq), 1)
    k_off = lax.broadcasted_iota(jnp.int32, (ATT_KC, tq), 0)

    def score_chunk(c):
        k0 = pl.multiple_of(c * ATT_KC, ATT_KC)
        score = jnp.zeros((ATT_KC, tq), F32)
        for h in range(IDX_HEADS):
            grp, r = divmod(h, heads_per_group)
            logit = jnp.dot(k2_ref[r, pl.ds(k0, ATT_KC), :], qt_ref[grp * LANES:(grp + 1) * LANES, :],
                            preferred_element_type=F32)
            score = score + jnp.maximum(logit, 0.0) * wt_ref[h:h + 1, :]
        score = jnp.where(k0 + k_off <= q_pos, score, -jnp.inf)
        bits = lax.bitcast_convert_type(score, jnp.int32)
        key_ref[pl.ds(k0, ATT_KC), :] = bits ^ ((bits >> 31) & jnp.int32(0x7FFFFFFF))

    def slice_chunk(c):
        for hh in range(blocks_per_chunk):
            blk = c * blocks_per_chunk + hh
            k0 = pl.multiple_of(blk * PLANE_KEYS, PLANE_KEYS)
            keys = key_ref[pl.ds(k0, PLANE_KEYS), :]
            planes = _bit_planes([keys[w * SUBLANES:(w + 1) * SUBLANES, :] for w in range(KEY_BITS)])
            r0 = pl.multiple_of(blk * SUBLANES, SUBLANES)
            plane_ref[0, pl.ds(r0, SUBLANES), :] = ~planes[0]
            for r in range(1, KEY_BITS):
                plane_ref[r, pl.ds(r0, SUBLANES), :] = planes[r]

    def score_and_slice(c, carry):
        slice_chunk(c - 1)
        score_chunk(c)
        return carry

    score_chunk(0)
    lax.fori_loop(1, n_chunks, score_and_slice, 0)
    slice_chunk(n_chunks - 1)

    plane_rows = live_ref.shape[0]
    row_id = lax.broadcasted_iota(jnp.int32, (plane_rows, tq), 0)
    live_ref[...] = jnp.where(row_id < n_blocks * SUBLANES, jnp.int32(-1), jnp.int32(0))

    def search_bit(r, carry):
        need, kth = carry
        plane = plane_ref[r]
        live = live_ref[...]
        hit = live & plane
        ones = lax.population_count(hit)
        part = jnp.sum(ones.reshape(plane_rows // SUBLANES, SUBLANES, tq), axis=0)
        total = jnp.sum(part.astype(F32), axis=0, keepdims=True)
        take = total >= need
        live_ref[...] = jnp.where(take, hit, live & ~plane)
        need = jnp.where(take, need, need - total)
        kth = jnp.where(take, kth | jnp.left_shift(jnp.int32(1), KEY_BITS - 1 - r), kth)
        return need, kth

    need, kth = lax.fori_loop(0, KEY_BITS, search_bit,
                              (jnp.full((1, tq), float(INDEX_TOPK), F32), jnp.zeros((1, tq), jnp.int32)))
    kth = kth ^ jnp.int32(INT_MIN)
    tied = lax.population_count(live_ref[...])
    tied = jnp.sum(tied.reshape(plane_rows // SUBLANES, SUBLANES, tq), axis=0).astype(F32)
    tied = jnp.sum(tied, axis=0, keepdims=True)
    few_keys = kth <= KEY_NEG_INF
    kth = jnp.maximum(kth, KEY_NEG_INF + 1)
    surplus = jnp.max(jnp.where(few_keys, 0.0, tied - need))
    need = jnp.where(few_keys, float(2 ** 24), need)

    def bias_tile(sel):
        return jnp.where(sel, 0.0, MASK_BIAS).T

    @pl.when(surplus <= 0.0)
    def _():
        def write_chunk(c, carry):
            parts = []
            for hh in range(blocks_per_chunk):
                k0 = pl.multiple_of(c * ATT_KC + hh * PLANE_KEYS, PLANE_KEYS)
                parts.append(bias_tile(key_ref[pl.ds(k0, PLANE_KEYS), :] >= kth))
            bias_ref[c] = jnp.concatenate(parts, axis=1).astype(bias_ref.dtype)
            return carry

        lax.fori_loop(0, n_chunks, write_chunk, 0)

    @pl.when(surplus > 0.0)
    def _():
        row = lax.broadcasted_iota(jnp.int32, (PLANE_KEYS, PLANE_KEYS), 0)
        col = lax.broadcasted_iota(jnp.int32, (PLANE_KEYS, PLANE_KEYS), 1)
        prefix = jnp.where(col <= row, 1.0, 0.0).astype(BF16)

        def write_chunk(c, seen):
            parts = []
            for hh in range(blocks_per_chunk):
                k0 = pl.multiple_of(c * ATT_KC + hh * PLANE_KEYS, PLANE_KEYS)
                keys = key_ref[pl.ds(k0, PLANE_KEYS), :]
                tie = keys == kth
                rank = seen + jnp.dot(prefix, jnp.where(tie, 1.0, 0.0).astype(BF16),
                                      preferred_element_type=F32)
                parts.append(bias_tile((keys > kth) | (tie & (rank <= need))))
                seen = rank[PLANE_KEYS - 1:PLANE_KEYS, :]
            bias_ref[c] = jnp.concatenate(parts, axis=1).astype(bias_ref.dtype)
            return seen

        lax.fori_loop(0, n_chunks, write_chunk, jnp.zeros((1, tq), F32))

    def fill_chunk(c, carry):
        bias_ref[c] = jnp.full(bias_ref.shape[1:], MASK_BIAS, bias_ref.dtype)
        return carry

    lax.fori_loop(n_chunks, n_chunks_total, fill_chunk, 0)


def _indexer_mask(q_idx_t, idx_w_t, k2):
    s = q_idx_t.shape[1]
    tq = SEL_TQ
    assert tq % PLANE_KEYS == 0 and s % (KEY_BITS * SUBLANES) == 0
    return pl.pallas_call(
        _indexer_body,
        grid=(s // tq,),
        in_specs=[pl.BlockSpec((q_idx_t.shape[0], tq), lambda i: (0, i)),
                  pl.BlockSpec((IDX_HEADS, tq), lambda i: (0, i)),
                  pl.BlockSpec(k2.shape, lambda i: (0, 0, 0))],
        out_specs=pl.BlockSpec((s // ATT_KC, tq, ATT_KC), lambda i: (0, i, 0)),
        out_shape=jax.ShapeDtypeStruct((s // ATT_KC, s, ATT_KC), BF16),
        scratch_shapes=[pltpu.VMEM((s, tq), jnp.int32),
                        pltpu.VMEM((KEY_BITS, s // KEY_BITS, tq), jnp.int32),
                        pltpu.VMEM((s // KEY_BITS, tq), jnp.int32)],
        compiler_params=_params(("arbitrary",)),
        name="indexer_mask",
    )(q_idx_t, idx_w_t, k2)


def _attention_body(q_ref, k_ref, v_ref, bias_ref, o_ref, m_ref, acc_ref):
    tq = q_ref.shape[0]
    kc = bias_ref.shape[2]
    i = pl.program_id(0)
    n_chunks = (i * tq + tq - 1) // kc + 1
    sub = kc // LANES

    m_ref[...] = jnp.full(m_ref.shape, M_INIT, F32)
    acc_ref[...] = jnp.zeros(acc_ref.shape, F32)

    def chunk(j, carry):
        k0 = pl.multiple_of(j * kc, kc)
        bias = bias_ref[j]
        ones = jnp.ones((kc, HEAD_DIM), v_ref.dtype)
        for h in range(ATTN_HEADS):
            cols = slice(h * HEAD_DIM, (h + 1) * HEAD_DIM)
            s = lax.dot_general(q_ref[:, cols], k_ref[pl.ds(k0, kc), cols],
                                (((1,), (1,)), ((), ())), preferred_element_type=F32)
            sb = s.astype(BF16) + bias
            part = sb[:, :LANES]
            for c in range(1, sub):
                part = jnp.maximum(part, sb[:, c * LANES:(c + 1) * LANES])
            m_prev = m_ref[h]
            m_next = jnp.maximum(m_prev, jnp.max(part.astype(F32), axis=1, keepdims=True))
            p = jnp.exp2(sb - jnp.concatenate([m_next.astype(BF16)] * sub, axis=1))
            alpha = jnp.exp2(m_prev - m_next)
            m_ref[h] = m_next
            pv = jnp.dot(p, jnp.concatenate([v_ref[pl.ds(k0, kc), cols], ones], axis=1),
                         preferred_element_type=F32)
            acc_ref[h] = jnp.concatenate([alpha, alpha], axis=1) * acc_ref[h] + pv
        return carry

    lax.fori_loop(0, n_chunks, chunk, 0)

    for h in range(ATTN_HEADS):
        acc = acc_ref[h]
        o_ref[:, h * HEAD_DIM:(h + 1) * HEAD_DIM] = (acc[:, :HEAD_DIM] / acc[:, HEAD_DIM:]).astype(o_ref.dtype)


def _attention(q, k, v, bias):
    s, width = q.shape
    tq = SEL_TQ
    n_chunks, _, kc = bias.shape
    resident = dict(pipeline_mode=pl.Buffered(1))
    return pl.pallas_call(
        _attention_body,
        grid=(s // tq,),
        in_specs=[pl.BlockSpec((tq, width), lambda i: (i, 0)),
                  pl.BlockSpec((s, width), lambda i: (0, 0), **resident),
                  pl.BlockSpec((s, width), lambda i: (0, 0), **resident),
                  pl.BlockSpec((n_chunks, tq, kc), lambda i: (0, i, 0))],
        out_specs=pl.BlockSpec((tq, width), lambda i: (i, 0)),
        out_shape=jax.ShapeDtypeStruct((s, width), BF16),
        scratch_shapes=[pltpu.VMEM((ATTN_HEADS, tq, LANES), F32),
                        pltpu.VMEM((ATTN_HEADS, tq, 2 * HEAD_DIM), F32)],
        compiler_params=_params(("arbitrary",)),
        name="masked_attention",
    )(q, k, v, bias)


def _ffn_down_body(a_ref, w_ref, x_ref, g_ref, gate_ref, o_ref):
    f = jnp.dot(a_ref[...], w_ref[...], preferred_element_type=F32)
    o_ref[...] = x_ref[...] + gate_ref[...] * _rms(f, g_ref[...])


def _ffn_down(a, w, x1, g_post, gate):
    m, kdim = a.shape
    d = w.shape[1]
    tm = 256
    row = pl.BlockSpec((1, d), lambda i: (0, 0))
    return pl.pallas_call(
        _ffn_down_body,
        grid=(m // tm,),
        in_specs=[pl.BlockSpec((tm, kdim), lambda i: (i, 0)),
                  pl.BlockSpec((kdim, d), lambda i: (0, 0), pipeline_mode=pl.Buffered(1)),
                  pl.BlockSpec((tm, d), lambda i: (i, 0)), row, row],
        out_specs=pl.BlockSpec((tm, d), lambda i: (i, 0)),
        out_shape=jax.ShapeDtypeStruct((m, d), F32),
        compiler_params=_params(("arbitrary",)),
        name="ffn_down_residual",
    )(a, w, x1, g_post, gate)


def _rope_angles(seq, dim):
    inv = 1.0 / (ROPE_THETA ** (jnp.arange(0, dim, 2, dtype=F32) / dim))
    ang = jnp.arange(seq, dtype=F32)[:, None] * inv[None, :]
    return jnp.cos(ang), jnp.sin(ang)


def _attn_rope_tables(seq):
    cos, sin = _rope_angles(seq, HEAD_DIM)
    return jnp.concatenate([cos, cos], axis=1), jnp.concatenate([-sin, sin], axis=1)


def _index_rope_tables(seq):
    cos, sin = _rope_angles(seq, IDX_ROPE_DIM)
    half = IDX_ROPE_DIM // 2
    rest = IDX_HEAD_DIM - IDX_ROPE_DIM
    zeros = lambda n: jnp.zeros((seq, n), F32)
    c = jnp.concatenate([cos, cos, jnp.ones((seq, rest), F32)], axis=1)
    a = jnp.concatenate([-sin, zeros(IDX_HEAD_DIM - half)], axis=1)
    b = jnp.concatenate([zeros(half), sin, zeros(rest)], axis=1)
    rep = LANES // IDX_HEAD_DIM
    return tuple(jnp.tile(t, (1, rep)) for t in (c, a, b)) + (cos.T, sin.T)


def _layer(x, mod, g_pre_mix, g_post_mix, w_in_t, gmlp_ln_g, gmlp_ln_b, gmlp_w_s, gmlp_b_s,
           q_lat_norm_g, w_q_up, w_qidx_up, kidx_ln_g, kidx_ln_b, w_proj_a, w_proj_b, w_out,
           g_pre_ffn, g_post_ffn, w_ffn_gate, w_ffn_up, w_ffn_down, tables):
    s, d = x.shape
    cc, ss, ic, ia, ib, icos_t, isin_t = tables
    row = lambda a: a.reshape(1, -1)
    shift_m, scale_m, gate_m, shift_f, scale_f, gate_f = (mod[:, n * d:(n + 1) * d] for n in range(N_MOD))

    h = _prenorm(x, row(g_pre_mix), scale_m, shift_m)

    tm = 1024
    in_proj = functools.partial(_simple_mm, h, w_in_t, kn=False)

    u = in_proj(_ep_gelu, [], BF16, n=GMLP_WIDTH, tm=tm, tn=512, name="in_proj_u")
    vn = in_proj(_ep_gelu_layernorm,
                 [(row(gmlp_ln_g), (1, GMLP_WIDTH), _colblk), (row(gmlp_ln_b), (1, GMLP_WIDTH), _colblk)],
                 BF16, n=GMLP_WIDTH, first=GMLP_WIDTH, tm=512, tn=GMLP_WIDTH, name="in_proj_v_ln")
    gmlp_bias = jnp.repeat(gmlp_b_s.T, LANES, axis=1)
    y_a = _gmlp(u, vn, gmlp_w_s, gmlp_bias)

    q_lat = in_proj(_ep_rmsnorm, [(row(q_lat_norm_g), (1, Q_LORA_RANK), _colblk)],
                    BF16, n=Q_LORA_RANK, first=COL_QLAT, tm=tm, tn=Q_LORA_RANK, name="in_proj_qlat")
    q_scale = HEAD_DIM ** -0.5 * math.log2(math.e)
    rope_extras = [(cc, (tm, HEAD_DIM), _rowblk), (ss, (tm, HEAD_DIM), _rowblk)]
    q = _simple_mm(q_lat, w_q_up, functools.partial(_ep_rope, scale=q_scale), rope_extras,
                   BF16, n=ATTN_WIDTH, tm=tm, tn=512, name="q_up_rope")
    k = in_proj(functools.partial(_ep_rope, scale=1.0), rope_extras, BF16,
                n=ATTN_WIDTH, first=COL_K, tm=tm, tn=512, name="in_proj_k_rope")
    v = in_proj(_ep_cast, [], BF16, n=ATTN_WIDTH, first=COL_V, tm=tm, tn=512, name="in_proj_v")
    half = IDX_ROPE_DIM // 2
    q_idx_t = _simple_mm(q_lat, w_qidx_up.T, _ep_partial_rope_t,
                         [(icos_t, (half, tm), _colblk_t), (isin_t, (half, tm), _colblk_t)],
                         BF16, n=IDX_HEADS * IDX_HEAD_DIM, tm=tm, tn=512, name="qidx_up_rope",
                         kn=False, out_t=True)
    pad = lambda a: jnp.pad(a, (0, LANES - a.shape[0])).reshape(1, LANES)
    idx_w_scale = (IDX_HEADS ** -0.5) * (IDX_HEAD_DIM ** -0.5)
    kx = in_proj(functools.partial(_ep_index_keys, w_scale=idx_w_scale),
                 [(pad(kidx_ln_g), (1, LANES), _colblk), (pad(kidx_ln_b), (1, LANES), _colblk)]
                 + [(t, (tm, LANES), _rowblk) for t in (ic, ia, ib)],
                 F32, n=LANES, first=COL_KIDX, tm=tm, tn=LANES, name="in_proj_index_keys")
    k_idx = kx[:, :IDX_HEAD_DIM].astype(BF16)
    zeros = jnp.zeros_like(k_idx)
    k2 = jnp.stack([jnp.concatenate([k_idx, zeros], axis=1), jnp.concatenate([zeros, k_idx], axis=1)])
    idx_w_t = kx[:, IDX_HEAD_DIM:IDX_HEAD_DIM + IDX_HEADS].T
    bias = _indexer_mask(q_idx_t, idx_w_t, k2)
    y_b = _attention(q, k, v, bias)

    gates = _simple_mm(h, w_in_t[COL_GATE:], _ep_sigmoid, [], BF16, n=2 * d, tm=tm, tn=512,
                       name="in_proj_gates", kn=False)
    n_gate_blocks = d // 512
    merged = _matmul([y_a, y_b], [(w_proj_a, 0, True), (w_proj_b, 0, True)], [0, 1],
                     [(gates, (tm, 512), _tile),
                      (gates, (tm, 512), lambda j, i: (i, j + n_gate_blocks))],
                     [((s, d), BF16, (tm, 512), _tile)], _ep_merge, n=d, tm=tm, tn=512,
                     name="branch_merge")[0]
    tm_full = 256
    vec = lambda a: (a, (1, d), _colblk)
    x1, h2 = _matmul([merged], [(w_out, 0, True)], [0],
                     [(x, (tm_full, d), _rowblk), vec(row(g_post_mix)), vec(gate_m),
                      vec(row(g_pre_ffn)), vec(scale_f), vec(shift_f)],
                     [((s, d), F32, (tm_full, d), _tile), ((s, d), BF16, (tm_full, d), _tile)],
                     _ep_mix_residual, n=d, tm=tm_full, tn=d, name="out_proj_residual")

    ffn_hidden = w_ffn_gate.shape[1]
    act = _matmul([h2], [(w_ffn_gate, 0, True), (w_ffn_up, 0, True)], [0, 0], [],
                  [((s, ffn_hidden), BF16, (tm, 512), _tile)], _ep_swiglu, n=ffn_hidden, tm=tm, tn=512,
                  name="ffn_up")[0]
    return _ffn_down(act, w_ffn_down.astype(BF16), x1, row(g_post_ffn), gate_f)


def kernel(x, c, w_mod, b_mod, g_pre_mix, g_post_mix, w_in, gmlp_ln_g, gmlp_ln_b, gmlp_w_s, gmlp_b_s, q_lat_norm_g, w_q_up, w_qidx_up, kidx_ln_g, kidx_ln_b, w_proj_a, w_proj_b, w_out, g_pre_ffn, g_post_ffn, w_ffn_gate, w_ffn_up, w_ffn_down):
    batch, seq, d = x.shape
    assert batch == 1 and d == D_MODEL
    tables = _attn_rope_tables(seq) + _index_rope_tables(seq)
    y = x[0]
    for l in range(w_mod.shape[0]):
        mod = _modulation(c, w_mod[l], b_mod[l])
        y = _layer(y, mod, g_pre_mix[l], g_post_mix[l], w_in[l].T, gmlp_ln_g[l], gmlp_ln_b[l],
                   gmlp_w_s[l], gmlp_b_s[l], q_lat_norm_g[l], w_q_up[l], w_qidx_up[l],
                   kidx_ln_g[l], kidx_ln_b[l], w_proj_a[l], w_proj_b[l], w_out[l],
                   g_pre_ffn[l], g_post_ffn[l], w_ffn_gate[l], w_ffn_up[l], w_ffn_down[l], tables)
    return y[None]
```

```python
import functools
import math

import jax
import jax.numpy as jnp
from jax import lax
from jax.experimental import pallas as pl
from jax.experimental.pallas import tpu as pltpu

F32 = jnp.float32
BF16 = jnp.bfloat16

D_MODEL = 2048
GMLP_WIDTH = 1024
GMLP_GROUPS = 8
CHUNK = 128
ATTN_HEADS = 8
HEAD_DIM = 128
ATTN_WIDTH = ATTN_HEADS * HEAD_DIM
Q_LORA_RANK = 512
IDX_HEADS = 16
IDX_HEAD_DIM = 64
IDX_ROPE_DIM = 32
INDEX_TOPK = 256
ROPE_THETA = 10000.0
N_MOD = 6
NORM_EPS = 1e-6

COL_QLAT = 2 * GMLP_WIDTH
COL_K = COL_QLAT + Q_LORA_RANK
COL_V = COL_K + ATTN_WIDTH
COL_KIDX = COL_V + ATTN_WIDTH
COL_IDXW = COL_KIDX + IDX_HEAD_DIM
COL_GATE = COL_IDXW + IDX_HEADS

LANES = 128
SUBLANES = 8
VMEM_LIMIT = 56 * 1024 * 1024

MASK_BIAS = -(2.0 ** 100)
M_INIT = -(2.0 ** 99)
INT_MIN = -2 ** 31
KEY_NEG_INF = -2139095041

SEL_TQ = 256
KEY_BITS = 32
PLANE_KEYS = KEY_BITS * SUBLANES
ATT_KC = 2 * PLANE_KEYS
ATT_STEP_CHUNKS = 2
EPILOGUE_SUB_N = 512


def _params(sem=None):
    return pltpu.CompilerParams(dimension_semantics=sem, vmem_limit_bytes=VMEM_LIMIT)


def _mod_body(cb_ref, w_ref, b_ref, o_ref):
    cb = cb_ref[...]
    tn = o_ref.shape[1]
    parts = [jnp.sum(w_ref[:, p * LANES:(p + 1) * LANES] * cb, axis=0, keepdims=True)
             for p in range(tn // LANES)]
    o_ref[...] = jnp.concatenate(parts, axis=1) + b_ref[...]


def _modulation(c, w_mod, b_mod):
    k, n = w_mod.shape
    tn = 1024
    cb = jnp.broadcast_to(c.reshape(k, 1), (k, LANES))
    return pl.pallas_call(
        _mod_body,
        grid=(n // tn,),
        in_specs=[pl.BlockSpec((k, LANES), lambda j: (0, 0)),
                  pl.BlockSpec((k, tn), lambda j: (0, j)),
                  pl.BlockSpec((1, tn), lambda j: (0, j))],
        out_specs=pl.BlockSpec((1, tn), lambda j: (0, j)),
        out_shape=jax.ShapeDtypeStruct((1, n), F32),
        compiler_params=_params(("arbitrary",)),
        name="modulation",
    )(cb, w_mod, b_mod.reshape(1, n))


def _rms(x, g):
    return x * lax.rsqrt(jnp.mean(x * x, axis=-1, keepdims=True) + NORM_EPS) * g


def _prenorm_body(x_ref, g_ref, scale_ref, shift_ref, o_ref):
    h = _rms(x_ref[...], g_ref[...]) * (1.0 + scale_ref[...]) + shift_ref[...]
    o_ref[...] = h.astype(o_ref.dtype)


def _prenorm(x, g, scale, shift):
    m, d = x.shape
    tm = 512
    row = pl.BlockSpec((1, d), lambda i: (0, 0))
    return pl.pallas_call(
        _prenorm_body,
        grid=(m // tm,),
        in_specs=[pl.BlockSpec((tm, d), lambda i: (i, 0)), row, row, row],
        out_specs=pl.BlockSpec((tm, d), lambda i: (i, 0)),
        out_shape=jax.ShapeDtypeStruct((m, d), BF16),
        compiler_params=_params(("arbitrary",)),
        name="prenorm",
    )(x, g, scale, shift)


def _mm_body(*refs, n_x, pairs, w_kn, out_t, sub_n, n_extra, n_out, epilogue):
    n_w = len(pairs)
    x_refs = refs[:n_x]
    w_refs = refs[n_x:n_x + n_w]
    e_refs = refs[n_x + n_w:n_x + n_w + n_extra]
    o_refs = refs[n_x + n_w + n_extra:n_x + n_w + n_extra + n_out]
    wb_refs = refs[n_x + n_w + n_extra + n_out:]

    @pl.when(pl.program_id(1) == 0)
    def _():
        for w_ref, wb_ref, kn in zip(w_refs, wb_refs, w_kn):
            w = w_ref[...]
            if not kn and not out_t:
                w = w.T
            wb_ref[...] = w.astype(wb_ref.dtype)

    if out_t:
        accs = [lax.dot_general(wb_ref[...], x_refs[xi][...], (((1,), (1,)), ((), ())),
                                preferred_element_type=F32)
                for xi, wb_ref in zip(pairs, wb_refs)]
        epilogue(accs, e_refs, o_refs, slice(0, wb_refs[0].shape[0]))
        return
    tn = wb_refs[0].shape[1]
    for c0 in range(0, tn, sub_n):
        cs = slice(c0, c0 + sub_n)
        accs = [jnp.dot(x_refs[xi][...], wb_ref[:, cs], preferred_element_type=F32)
                for xi, wb_ref in zip(pairs, wb_refs)]
        epilogue(accs, e_refs, o_refs, cs)


def _matmul(xs, ws, pairs, extras, outs, epilogue, *, n, tm, tn, name, out_t=False, sub_n=None):
    m = xs[0].shape[0]
    n_col = n // tn
    w_mode = dict(pipeline_mode=pl.Buffered(1)) if n_col == 1 else {}
    in_specs = [pl.BlockSpec((tm, x.shape[1]), lambda j, i: (i, 0)) for x in xs]
    scratch = []
    for w, first, kn in ws:
        assert not (kn and out_t)
        if kn:
            kdim = w.shape[0]
            in_specs.append(pl.BlockSpec((kdim, tn), functools.partial(_w_cols, first=first), **w_mode))
        else:
            kdim = w.shape[1]
            in_specs.append(pl.BlockSpec((tn, kdim), functools.partial(_w_rows, first=first), **w_mode))
        scratch.append(pltpu.VMEM((tn, kdim) if out_t else (kdim, tn), BF16))
    in_specs += [pl.BlockSpec(bs, im) for _, bs, im in extras]
    body = functools.partial(_mm_body, n_x=len(xs), pairs=tuple(pairs), w_kn=tuple(kn for _, _, kn in ws),
                             out_t=out_t, sub_n=sub_n or tn, n_extra=len(extras), n_out=len(outs),
                             epilogue=epilogue)
    return pl.pallas_call(
        body,
        grid=(n_col, m // tm),
        in_specs=in_specs,
        out_specs=[pl.BlockSpec(bs, im) for _, _, bs, im in outs],
        out_shape=[jax.ShapeDtypeStruct(s, d) for s, d, _, _ in outs],
        scratch_shapes=scratch,
        compiler_params=_params(("arbitrary", "arbitrary")),
        name=name,
    )(*xs, *[w for w, _, _ in ws], *[a for a, _, _ in extras])


def _w_cols(j, i, *, first):
    return (0, first + j)


def _w_rows(j, i, *, first):
    return (first + j, 0)


def _tile(j, i):
    return (i, j)


def _tile_t(j, i):
    return (j, i)


def _rowblk(j, i):
    return (i, 0)


def _colblk(j, i):
    return (0, j)


def _colblk_t(j, i):
    return (0, i)


def _ep_gelu(accs, e_refs, o_refs, cols):
    o_refs[0][:, cols] = jax.nn.gelu(accs[0]).astype(o_refs[0].dtype)


def _ep_gelu_layernorm(accs, e_refs, o_refs, cols):
    z = jax.nn.gelu(accs[0])
    mu = jnp.mean(z, axis=-1, keepdims=True)
    zc = z - mu
    var = jnp.mean(zc * zc, axis=-1, keepdims=True)
    y = zc * lax.rsqrt(var + NORM_EPS) * e_refs[0][...] + e_refs[1][...]
    o_refs[0][...] = y.astype(o_refs[0].dtype)


def _ep_rmsnorm(accs, e_refs, o_refs, cols):
    o_refs[0][...] = _rms(accs[0], e_refs[0][...]).astype(o_refs[0].dtype)


def _ep_cast(accs, e_refs, o_refs, cols):
    o_refs[0][:, cols] = accs[0].astype(o_refs[0].dtype)


def _ep_sigmoid(accs, e_refs, o_refs, cols):
    o_refs[0][:, cols] = (0.5 * jnp.tanh(0.5 * accs[0]) + 0.5).astype(o_refs[0].dtype)


def _ep_rope(accs, e_refs, o_refs, cols, *, scale):
    cc = e_refs[0][...]
    ss = e_refs[1][...]
    acc = accs[0]
    for h in range(acc.shape[1] // HEAD_DIM):
        xh = acc[:, h * HEAD_DIM:(h + 1) * HEAD_DIM]
        r = xh * cc + pltpu.roll(xh, HEAD_DIM // 2, 1) * ss
        if scale != 1.0:
            r = r * scale
        c0 = cols.start + h * HEAD_DIM
        o_refs[0][:, c0:c0 + HEAD_DIM] = r.astype(o_refs[0].dtype)


def _ep_partial_rope_t(accs, e_refs, o_refs, cols):
    c = e_refs[0][...]
    s = e_refs[1][...]
    acc = accs[0]
    half = IDX_ROPE_DIM // 2
    for h in range(acc.shape[0] // IDX_HEAD_DIM):
        r0 = h * IDX_HEAD_DIM
        x1 = acc[r0:r0 + half, :]
        x2 = acc[r0 + half:r0 + 2 * half, :]
        o_refs[0][r0:r0 + half, :] = (x1 * c - x2 * s).astype(o_refs[0].dtype)
        o_refs[0][r0 + half:r0 + 2 * half, :] = (x2 * c + x1 * s).astype(o_refs[0].dtype)
        o_refs[0][r0 + 2 * half:r0 + IDX_HEAD_DIM, :] = (
            acc[r0 + 2 * half:r0 + IDX_HEAD_DIM, :].astype(o_refs[0].dtype))


def _partial_rope(x, c, a, b):
    half = IDX_ROPE_DIM // 2
    return x * c + pltpu.roll(x, LANES - half, 1) * a + pltpu.roll(x, half, 1) * b


def _ep_index_keys(accs, e_refs, o_refs, cols, *, w_scale):
    g, bb = e_refs[0][...], e_refs[1][...]
    c, a, b = e_refs[2][...], e_refs[3][...], e_refs[4][...]
    acc = accs[0]
    lane = lax.broadcasted_iota(jnp.int32, acc.shape, 1)
    is_key = lane < IDX_HEAD_DIM
    mu = jnp.sum(jnp.where(is_key, acc, 0.0), axis=-1, keepdims=True) / IDX_HEAD_DIM
    xc = jnp.where(is_key, acc - mu, 0.0)
    var = jnp.sum(xc * xc, axis=-1, keepdims=True) / IDX_HEAD_DIM
    y = xc * lax.rsqrt(var + NORM_EPS) * g + bb
    y = _partial_rope(y, c, a, b)
    o_refs[0][...] = jnp.where(is_key, y, acc * w_scale)


def _ep_merge(accs, e_refs, o_refs, cols):
    o = e_refs[0][:, cols].astype(F32) * accs[0] + e_refs[1][:, cols].astype(F32) * accs[1]
    o_refs[0][:, cols] = o.astype(o_refs[0].dtype)


def _ep_mix_residual(accs, e_refs, o_refs, cols):
    x_ref, g_post, gate, g_pre, scale, shift = e_refs
    x1 = x_ref[...] + gate[...] * _rms(accs[0], g_post[...])
    o_refs[0][...] = x1
    h2 = _rms(x1, g_pre[...]) * (1.0 + scale[...]) + shift[...]
    o_refs[1][...] = h2.astype(o_refs[1].dtype)


def _ep_swiglu(accs, e_refs, o_refs, cols):
    o_refs[0][:, cols] = (jax.nn.silu(accs[0]) * accs[1]).astype(o_refs[0].dtype)


def _simple_mm(x, w, epilogue, extras, out_dtype, *, n, tm, tn, name, first=0, kn=True, out_t=False,
               sub_n=None):
    m = x.shape[0]
    out = ((n, m), out_dtype, (tn, tm), _tile_t) if out_t else ((m, n), out_dtype, (tm, tn), _tile)
    return _matmul([x], [(w, first // tn, kn)], [0], extras, [out], epilogue, n=n, tm=tm, tn=tn,
                   name=name, out_t=out_t, sub_n=sub_n)[0]


def _gmlp_body(u_ref, v_ref, w_ref, b_ref, o_ref):
    t = lax.broadcasted_iota(jnp.int32, (CHUNK, CHUNK), 0)
    s = lax.broadcasted_iota(jnp.int32, (CHUNK, CHUNK), 1)
    causal = s <= t
    for g in range(GMLP_GROUPS):
        w = jnp.where(causal, w_ref[g], 0.0).astype(BF16)
        cols = slice(g * LANES, (g + 1) * LANES)
        bias = b_ref[:, cols]
        for c in range(u_ref.shape[0] // CHUNK):
            rows = slice(c * CHUNK, (c + 1) * CHUNK)
            sv = jnp.dot(w, v_ref[rows, cols], preferred_element_type=F32) + bias
            o_ref[rows, cols] = (u_ref[rows, cols].astype(F32) * sv).astype(o_ref.dtype)


def _gmlp(u, vn, w_s, bias):
    m, n = u.shape
    tm = 512
    blk = pl.BlockSpec((tm, n), lambda i: (i, 0))
    return pl.pallas_call(
        _gmlp_body,
        grid=(m // tm,),
        in_specs=[blk, blk,
                  pl.BlockSpec(w_s.shape, lambda i: (0, 0, 0)),
                  pl.BlockSpec(bias.shape, lambda i: (0, 0))],
        out_specs=blk,
        out_shape=jax.ShapeDtypeStruct((m, n), BF16),
        compiler_params=_params(("arbitrary",)),
        name="gmlp_gating",
    )(u, vn, w_s, bias)


def _bit_planes(words):
    a = list(words)
    j, mask = KEY_BITS // 2, 0x0000FFFF
    while j:
        for k in range(KEY_BITS):
            if k & j:
                continue
            t = (a[k] ^ (a[k + j] >> j)) & jnp.int32(mask)
            a[k] = a[k] ^ t
            a[k + j] = a[k + j] ^ (t << j)
        j //= 2
        mask = (mask ^ (mask << j)) & 0xFFFFFFFF
    return a


def _indexer_body(qt_ref, wt_ref, k2_ref, bias_ref, key_ref, plane_ref, live_ref):
    tq = qt_ref.shape[1]
    n_chunks_total = bias_ref.shape[0]
    i = pl.program_id(0)
    n_chunks = ((i + 1) * tq + ATT_KC - 1) // ATT_KC
    blocks_per_chunk = ATT_KC // PLANE_KEYS
    n_blocks = n_chunks * blocks_per_chunk
    heads_per_group = LANES // IDX_HEAD_DIM

    @pl.when(i == 0)
    def _():
        plane_ref[...] = jnp.zeros(plane_ref.shape, jnp.int32)

    q_pos = i * tq + lax.broadcasted_iota(jnp.int32, (ATT_KC, tq), 1)
    k_off = lax.broadcasted_iota(jnp.int32, (ATT_KC, tq), 0)

    def score_chunk(c):
        k0 = pl.multiple_of(c * ATT_KC, ATT_KC)
        score = jnp.zeros((ATT_KC, tq), F32)
        for h in range(IDX_HEADS):
            grp, r = divmod(h, heads_per_group)
            logit = jnp.dot(k2_ref[r, pl.ds(k0, ATT_KC), :], qt_ref[grp * LANES:(grp + 1) * LANES, :],
                            preferred_element_type=F32)
            score = score + jnp.maximum(logit, 0.0) * wt_ref[h:h + 1, :]
        score = jnp.where(k0 + k_off <= q_pos, score, -jnp.inf)
        bits = lax.bitcast_convert_type(score, jnp.int32)
        key_ref[pl.ds(k0, ATT_KC), :] = bits ^ ((bits >> 31) & jnp.int32(0x7FFFFFFF))

    def slice_chunk(c):
        for hh in range(blocks_per_chunk):
            blk = c * blocks_per_chunk + hh
            k0 = pl.multiple_of(blk * PLANE_KEYS, PLANE_KEYS)
            keys = key_ref[pl.ds(k0, PLANE_KEYS), :]
            planes = _bit_planes([keys[w * SUBLANES:(w + 1) * SUBLANES, :] for w in range(KEY_BITS)])
            r0 = pl.multiple_of(blk * SUBLANES, SUBLANES)
            plane_ref[0, pl.ds(r0, SUBLANES), :] = ~planes[0]
            for r in range(1, KEY_BITS):
                plane_ref[r, pl.ds(r0, SUBLANES), :] = planes[r]

    def score_and_slice(c, carry):
        slice_chunk(c - 1)
        score_chunk(c)
        return carry

    score_chunk(0)
    lax.fori_loop(1, n_chunks, score_and_slice, 0)
    slice_chunk(n_chunks - 1)

    plane_rows = live_ref.shape[0]
    row_id = lax.broadcasted_iota(jnp.int32, (plane_rows, tq), 0)
    live_ref[...] = jnp.where(row_id < n_blocks * SUBLANES, jnp.int32(-1), jnp.int32(0))

    def search_bit(r, carry):
        need, kth = carry
        plane = plane_ref[r]
        live = live_ref[...]
        hit = live & plane
        ones = lax.population_count(hit)
        part = jnp.sum(ones.reshape(plane_rows // SUBLANES, SUBLANES, tq), axis=0)
        total = jnp.sum(part.astype(F32), axis=0, keepdims=True)
        take = total >= need
        live_ref[...] = jnp.where(take, hit, live & ~plane)
        need = jnp.where(take, need, need - total)
        kth = jnp.where(take, kth | jnp.left_shift(jnp.int32(1), KEY_BITS - 1 - r), kth)
        return need, kth

    need, kth = lax.fori_loop(0, KEY_BITS, search_bit,
                              (jnp.full((1, tq), float(INDEX_TOPK), F32), jnp.zeros((1, tq), jnp.int32)))
    kth = kth ^ jnp.int32(INT_MIN)
    tied = lax.population_count(live_ref[...])
    tied = jnp.sum(tied.reshape(plane_rows // SUBLANES, SUBLANES, tq), axis=0).astype(F32)
    tied = jnp.sum(tied, axis=0, keepdims=True)
    few_keys = kth <= KEY_NEG_INF
    kth = jnp.maximum(kth, KEY_NEG_INF + 1)
    surplus = jnp.max(jnp.where(few_keys, 0.0, tied - need))
    need = jnp.where(few_keys, float(2 ** 24), need)

    def bias_tile(sel):
        return jnp.where(sel, 0.0, MASK_BIAS).T

    @pl.when(surplus <= 0.0)
    def _():
        def write_chunk(c, carry):
            parts = []
            for hh in range(blocks_per_chunk):
                k0 = pl.multiple_of(c * ATT_KC + hh * PLANE_KEYS, PLANE_KEYS)
                parts.append(bias_tile(key_ref[pl.ds(k0, PLANE_KEYS), :] >= kth))
            bias_ref[c] = jnp.concatenate(parts, axis=1).astype(bias_ref.dtype)
            return carry

        lax.fori_loop(0, n_chunks, write_chunk, 0)

    @pl.when(surplus > 0.0)
    def _():
        row = lax.broadcasted_iota(jnp.int32, (PLANE_KEYS, PLANE_KEYS), 0)
        col = lax.broadcasted_iota(jnp.int32, (PLANE_KEYS, PLANE_KEYS), 1)
        prefix = jnp.where(col <= row, 1.0, 0.0).astype(BF16)

        def write_chunk(c, seen):
            parts = []
            for hh in range(blocks_per_chunk):
                k0 = pl.multiple_of(c * ATT_KC + hh * PLANE_KEYS, PLANE_KEYS)
                keys = key_ref[pl.ds(k0, PLANE_KEYS), :]
                tie = keys == kth
                rank = seen + jnp.dot(prefix, jnp.where(tie, 1.0, 0.0).astype(BF16),
                                      preferred_element_type=F32)
                parts.append(bias_tile((keys > kth) | (tie & (rank <= need))))
                seen = rank[PLANE_KEYS - 1:PLANE_KEYS, :]
            bias_ref[c] = jnp.concatenate(parts, axis=1).astype(bias_ref.dtype)
            return seen

        lax.fori_loop(0, n_chunks, write_chunk, jnp.zeros((1, tq), F32))

    def fill_chunk(c, carry):
        bias_ref[c] = jnp.full(bias_ref.shape[1:], MASK_BIAS, bias_ref.dtype)
        return carry

    lax.fori_loop(n_chunks, n_chunks_total, fill_chunk, 0)


def _indexer_mask(q_idx_t, idx_w_t, k2):
    s = q_idx_t.shape[1]
    tq = SEL_TQ
    assert tq % PLANE_KEYS == 0 and s % (KEY_BITS * SUBLANES) == 0
    return pl.pallas_call(
        _indexer_body,
        grid=(s // tq,),
        in_specs=[pl.BlockSpec((q_idx_t.shape[0], tq), lambda i: (0, i)),
                  pl.BlockSpec((IDX_HEADS, tq), lambda i: (0, i)),
                  pl.BlockSpec(k2.shape, lambda i: (0, 0, 0))],
        out_specs=pl.BlockSpec((s // ATT_KC, tq, ATT_KC), lambda i: (0, i, 0)),
        out_shape=jax.ShapeDtypeStruct((s // ATT_KC, s, ATT_KC), BF16),
        scratch_shapes=[pltpu.VMEM((s, tq), jnp.int32),
                        pltpu.VMEM((KEY_BITS, s // KEY_BITS, tq), jnp.int32),
                        pltpu.VMEM((s // KEY_BITS, tq), jnp.int32)],
        compiler_params=_params(("arbitrary",)),
        name="indexer_mask",
    )(q_idx_t, idx_w_t, k2)


def _attention_body(q_ref, k_ref, v_ref, bias_ref, o_ref, m_ref, acc_ref):
    tq = q_ref.shape[0]
    chunk_keys = bias_ref.shape[2]
    i = pl.program_id(0)
    n_chunks = (i * tq + tq - 1) // chunk_keys + 1

    m_ref[...] = jnp.full(m_ref.shape, M_INIT, F32)
    acc_ref[...] = jnp.zeros(acc_ref.shape, F32)

    def step(first_chunk, width):
        kc = width * chunk_keys
        sub = kc // LANES
        k0 = pl.multiple_of(first_chunk * chunk_keys, chunk_keys)
        bias = jnp.concatenate([bias_ref[first_chunk + c] for c in range(width)], axis=1)
        ones = jnp.ones((kc, HEAD_DIM), v_ref.dtype)
        for h in range(ATTN_HEADS):
            cols = slice(h * HEAD_DIM, (h + 1) * HEAD_DIM)
            s = lax.dot_general(q_ref[:, cols], k_ref[pl.ds(k0, kc), cols],
                                (((1,), (1,)), ((), ())), preferred_element_type=F32)
            sb = s.astype(BF16) + bias
            part = sb[:, :LANES]
            for c in range(1, sub):
                part = jnp.maximum(part, sb[:, c * LANES:(c + 1) * LANES])
            m_prev = m_ref[h]
            m_next = jnp.maximum(m_prev, jnp.max(part.astype(F32), axis=1, keepdims=True))
            p = jnp.exp2(sb - jnp.concatenate([m_next.astype(BF16)] * sub, axis=1))
            alpha = jnp.exp2(m_prev - m_next)
            m_ref[h] = m_next
            pv = jnp.dot(p, jnp.concatenate([v_ref[pl.ds(k0, kc), cols], ones], axis=1),
                         preferred_element_type=F32)
            acc_ref[h] = jnp.concatenate([alpha, alpha], axis=1) * acc_ref[h] + pv

    def wide_step(j, carry):
        step(j * ATT_STEP_CHUNKS, ATT_STEP_CHUNKS)
        return carry

    n_wide = n_chunks // ATT_STEP_CHUNKS
    lax.fori_loop(0, n_wide, wide_step, 0)
    for r in range(1, ATT_STEP_CHUNKS):
        @pl.when(n_chunks % ATT_STEP_CHUNKS >= r)
        def _():
            step(n_wide * ATT_STEP_CHUNKS + r - 1, 1)

    for h in range(ATTN_HEADS):
        acc = acc_ref[h]
        o_ref[:, h * HEAD_DIM:(h + 1) * HEAD_DIM] = (acc[:, :HEAD_DIM] / acc[:, HEAD_DIM:]).astype(o_ref.dtype)


def _attention(q, k, v, bias):
    s, width = q.shape
    tq = SEL_TQ
    n_chunks, _, kc = bias.shape
    resident = dict(pipeline_mode=pl.Buffered(1))
    return pl.pallas_call(
        _attention_body,
        grid=(s // tq,),
        in_specs=[pl.BlockSpec((tq, width), lambda i: (i, 0)),
                  pl.BlockSpec((s, width), lambda i: (0, 0), **resident),
                  pl.BlockSpec((s, width), lambda i: (0, 0), **resident),
                  pl.BlockSpec((n_chunks, tq, kc), lambda i: (0, i, 0))],
        out_specs=pl.BlockSpec((tq, width), lambda i: (i, 0)),
        out_shape=jax.ShapeDtypeStruct((s, width), BF16),
        scratch_shapes=[pltpu.VMEM((ATTN_HEADS, tq, LANES), F32),
                        pltpu.VMEM((ATTN_HEADS, tq, 2 * HEAD_DIM), F32)],
        compiler_params=_params(("arbitrary",)),
        name="masked_attention",
    )(q, k, v, bias)


def _ffn_down_body(a_ref, w_ref, x_ref, g_ref, gate_ref, o_ref):
    f = jnp.dot(a_ref[...], w_ref[...], preferred_element_type=F32)
    o_ref[...] = x_ref[...] + gate_ref[...] * _rms(f, g_ref[...])


def _ffn_down(a, w, x1, g_post, gate):
    m, kdim = a.shape
    d = w.shape[1]
    tm = 256
    row = pl.BlockSpec((1, d), lambda i: (0, 0))
    return pl.pallas_call(
        _ffn_down_body,
        grid=(m // tm,),
        in_specs=[pl.BlockSpec((tm, kdim), lambda i: (i, 0)),
                  pl.BlockSpec((kdim, d), lambda i: (0, 0), pipeline_mode=pl.Buffered(1)),
                  pl.BlockSpec((tm, d), lambda i: (i, 0)), row, row],
        out_specs=pl.BlockSpec((tm, d), lambda i: (i, 0)),
        out_shape=jax.ShapeDtypeStruct((m, d), F32),
        compiler_params=_params(("arbitrary",)),
        name="ffn_down_residual",
    )(a, w, x1, g_post, gate)


def _rope_angles(seq, dim):
    inv = 1.0 / (ROPE_THETA ** (jnp.arange(0, dim, 2, dtype=F32) / dim))
    ang = jnp.arange(seq, dtype=F32)[:, None] * inv[None, :]
    return jnp.cos(ang), jnp.sin(ang)


def _attn_rope_tables(seq):
    cos, sin = _rope_angles(seq, HEAD_DIM)
    return jnp.concatenate([cos, cos], axis=1), jnp.concatenate([-sin, sin], axis=1)


def _index_rope_tables(seq):
    cos, sin = _rope_angles(seq, IDX_ROPE_DIM)
    half = IDX_ROPE_DIM // 2
    rest = IDX_HEAD_DIM - IDX_ROPE_DIM
    zeros = lambda n: jnp.zeros((seq, n), F32)
    c = jnp.concatenate([cos, cos, jnp.ones((seq, rest), F32)], axis=1)
    a = jnp.concatenate([-sin, zeros(IDX_HEAD_DIM - half)], axis=1)
    b = jnp.concatenate([zeros(half), sin, zeros(rest)], axis=1)
    rep = LANES // IDX_HEAD_DIM
    return tuple(jnp.tile(t, (1, rep)) for t in (c, a, b)) + (cos.T, sin.T)


def _layer(x, mod, g_pre_mix, g_post_mix, w_in_t, gmlp_ln_g, gmlp_ln_b, gmlp_w_s, gmlp_b_s,
           q_lat_norm_g, w_q_up, w_qidx_up, kidx_ln_g, kidx_ln_b, w_proj_a, w_proj_b, w_out,
           g_pre_ffn, g_post_ffn, w_ffn_gate, w_ffn_up, w_ffn_down, tables):
    s, d = x.shape
    cc, ss, ic, ia, ib, icos_t, isin_t = tables
    row = lambda a: a.reshape(1, -1)
    shift_m, scale_m, gate_m, shift_f, scale_f, gate_f = (mod[:, n * d:(n + 1) * d] for n in range(N_MOD))

    h = _prenorm(x, row(g_pre_mix), scale_m, shift_m)

    tm = 1024
    in_proj = functools.partial(_simple_mm, h, w_in_t, kn=False)

    u = in_proj(_ep_gelu, [], BF16, n=GMLP_WIDTH, tm=tm, tn=GMLP_WIDTH, sub_n=EPILOGUE_SUB_N,
                name="in_proj_u")
    vn = in_proj(_ep_gelu_layernorm,
                 [(row(gmlp_ln_g), (1, GMLP_WIDTH), _colblk), (row(gmlp_ln_b), (1, GMLP_WIDTH), _colblk)],
                 BF16, n=GMLP_WIDTH, first=GMLP_WIDTH, tm=512, tn=GMLP_WIDTH, name="in_proj_v_ln")
    gmlp_bias = jnp.repeat(gmlp_b_s.T, LANES, axis=1)
    y_a = _gmlp(u, vn, gmlp_w_s, gmlp_bias)

    q_lat = in_proj(_ep_rmsnorm, [(row(q_lat_norm_g), (1, Q_LORA_RANK), _colblk)],
                    BF16, n=Q_LORA_RANK, first=COL_QLAT, tm=tm, tn=Q_LORA_RANK, name="in_proj_qlat")
    q_scale = HEAD_DIM ** -0.5 * math.log2(math.e)
    rope_extras = [(cc, (tm, HEAD_DIM), _rowblk), (ss, (tm, HEAD_DIM), _rowblk)]
    q = _simple_mm(q_lat, w_q_up, functools.partial(_ep_rope, scale=q_scale), rope_extras,
                   BF16, n=ATTN_WIDTH, tm=tm, tn=ATTN_WIDTH, sub_n=EPILOGUE_SUB_N, name="q_up_rope")
    k = in_proj(functools.partial(_ep_rope, scale=1.0), rope_extras, BF16,
                n=ATTN_WIDTH, first=COL_K, tm=tm, tn=512, name="in_proj_k_rope")
    v = in_proj(_ep_cast, [], BF16, n=ATTN_WIDTH, first=COL_V, tm=tm, tn=512, name="in_proj_v")
    half = IDX_ROPE_DIM // 2
    q_idx_t = _simple_mm(q_lat, w_qidx_up.T, _ep_partial_rope_t,
                         [(icos_t, (half, tm), _colblk_t), (isin_t, (half, tm), _colblk_t)],
                         BF16, n=IDX_HEADS * IDX_HEAD_DIM, tm=tm, tn=512, name="qidx_up_rope",
                         kn=False, out_t=True)
    pad = lambda a: jnp.pad(a, (0, LANES - a.shape[0])).reshape(1, LANES)
    idx_w_scale = (IDX_HEADS ** -0.5) * (IDX_HEAD_DIM ** -0.5)
    kx = in_proj(functools.partial(_ep_index_keys, w_scale=idx_w_scale),
                 [(pad(kidx_ln_g), (1, LANES), _colblk), (pad(kidx_ln_b), (1, LANES), _colblk)]
                 + [(t, (tm, LANES), _rowblk) for t in (ic, ia, ib)],
                 F32, n=LANES, first=COL_KIDX, tm=tm, tn=LANES, name="in_proj_index_keys")
    k_idx = kx[:, :IDX_HEAD_DIM].astype(BF16)
    zeros = jnp.zeros_like(k_idx)
    k2 = jnp.stack([jnp.concatenate([k_idx, zeros], axis=1), jnp.concatenate([zeros, k_idx], axis=1)])
    idx_w_t = kx[:, IDX_HEAD_DIM:IDX_HEAD_DIM + IDX_HEADS].T
    bias = _indexer_mask(q_idx_t, idx_w_t, k2)
    y_b = _attention(q, k, v, bias)

    gates = _simple_mm(h, w_in_t[COL_GATE:], _ep_sigmoid, [], BF16, n=2 * d, tm=tm, tn=1024,
                       sub_n=EPILOGUE_SUB_N, name="in_proj_gates", kn=False)
    n_gate_blocks = d // 512
    merged = _matmul([y_a, y_b], [(w_proj_a, 0, True), (w_proj_b, 0, True)], [0, 1],
                     [(gates, (tm, 512), _tile),
                      (gates, (tm, 512), lambda j, i: (i, j + n_gate_blocks))],
                     [((s, d), BF16, (tm, 512), _tile)], _ep_merge, n=d, tm=tm, tn=512,
                     name="branch_merge")[0]
    tm_full = 256
    vec = lambda a: (a, (1, d), _colblk)
    x1, h2 = _matmul([merged], [(w_out, 0, True)], [0],
                     [(x, (tm_full, d), _rowblk), vec(row(g_post_mix)), vec(gate_m),
                      vec(row(g_pre_ffn)), vec(scale_f), vec(shift_f)],
                     [((s, d), F32, (tm_full, d), _tile), ((s, d), BF16, (tm_full, d), _tile)],
                     _ep_mix_residual, n=d, tm=tm_full, tn=d, name="out_proj_residual")

    ffn_hidden = w_ffn_gate.shape[1]
    act = _matmul([h2], [(w_ffn_gate, 0, True), (w_ffn_up, 0, True)], [0, 0], [],
                  [((s, ffn_hidden), BF16, (tm, 512), _tile)], _ep_swiglu, n=ffn_hidden, tm=tm, tn=512,
                  name="ffn_up")[0]
    return _ffn_down(act, w_ffn_down.astype(BF16), x1, row(g_post_ffn), gate_f)


def kernel(x, c, w_mod, b_mod, g_pre_mix, g_post_mix, w_in, gmlp_ln_g, gmlp_ln_b, gmlp_w_s, gmlp_b_s, q_lat_norm_g, w_q_up, w_qidx_up, kidx_ln_g, kidx_ln_b, w_proj_a, w_proj_b, w_out, g_pre_ffn, g_post_ffn, w_ffn_gate, w_ffn_up, w_ffn_down):
    batch, seq, d = x.shape
    assert batch == 1 and d == D_MODEL
    tables = _attn_rope_tables(seq) + _index_rope_tables(seq)
    y = x[0]
    for l in range(w_mod.shape[0]):
        mod = _modulation(c, w_mod[l], b_mod[l])
        y = _layer(y, mod, g_pre_mix[l], g_post_mix[l], w_in[l].T, gmlp_ln_g[l], gmlp_ln_b[l],
                   gmlp_w_s[l], gmlp_b_s[l], q_lat_norm_g[l], w_q_up[l], w_qidx_up[l],
                   kidx_ln_g[l], kidx_ln_b[l], w_proj_a[l], w_proj_b[l], w_out[l],
                   g_pre_ffn[l], g_post_ffn[l], w_ffn_gate[l], w_ffn_up[l], w_ffn_down[l], tables)
    return y[None]
```

```python
import functools
import math

import jax
import jax.numpy as jnp
from jax import lax
from jax.experimental import pallas as pl
from jax.experimental.pallas import tpu as pltpu

F32 = jnp.float32
BF16 = jnp.bfloat16

D_MODEL = 2048
GMLP_WIDTH = 1024
GMLP_GROUPS = 8
CHUNK = 128
ATTN_HEADS = 8
HEAD_DIM = 128
ATTN_WIDTH = ATTN_HEADS * HEAD_DIM
Q_LORA_RANK = 512
IDX_HEADS = 16
IDX_HEAD_DIM = 64
IDX_ROPE_DIM = 32
INDEX_TOPK = 256
ROPE_THETA = 10000.0
N_MOD = 6
NORM_EPS = 1e-6

COL_QLAT = 2 * GMLP_WIDTH
COL_K = COL_QLAT + Q_LORA_RANK
COL_V = COL_K + ATTN_WIDTH
COL_KIDX = COL_V + ATTN_WIDTH
COL_IDXW = COL_KIDX + IDX_HEAD_DIM
COL_GATE = COL_IDXW + IDX_HEADS

LANES = 128
SUBLANES = 8
VMEM_LIMIT = 56 * 1024 * 1024

MASK_BIAS = -(2.0 ** 100)
M_INIT = -(2.0 ** 99)
INT_MIN = -2 ** 31
KEY_NEG_INF = -2139095041

SEL_TQ = 256
KEY_BITS = 32
PLANE_KEYS = KEY_BITS * SUBLANES
ATT_KC = 2 * PLANE_KEYS
ATT_STEP_CHUNKS = 2
EPILOGUE_SUB_N = 512


def _params(sem=None):
    return pltpu.CompilerParams(dimension_semantics=sem, vmem_limit_bytes=VMEM_LIMIT)


def _mod_body(cb_ref, w_ref, b_ref, o_ref):
    cb = cb_ref[...]
    tn = o_ref.shape[1]
    parts = [jnp.sum(w_ref[:, p * LANES:(p + 1) * LANES] * cb, axis=0, keepdims=True)
             for p in range(tn // LANES)]
    o_ref[...] = jnp.concatenate(parts, axis=1) + b_ref[...]


def _modulation(c, w_mod, b_mod):
    k, n = w_mod.shape
    tn = 1024
    cb = jnp.broadcast_to(c.reshape(k, 1), (k, LANES))
    return pl.pallas_call(
        _mod_body,
        grid=(n // tn,),
        in_specs=[pl.BlockSpec((k, LANES), lambda j: (0, 0)),
                  pl.BlockSpec((k, tn), lambda j: (0, j)),
                  pl.BlockSpec((1, tn), lambda j: (0, j))],
        out_specs=pl.BlockSpec((1, tn), lambda j: (0, j)),
        out_shape=jax.ShapeDtypeStruct((1, n), F32),
        compiler_params=_params(("arbitrary",)),
        name="modulation",
    )(cb, w_mod, b_mod.reshape(1, n))


def _rms(x, g):
    return x * lax.rsqrt(jnp.mean(x * x, axis=-1, keepdims=True) + NORM_EPS) * g


def _prenorm_body(x_ref, g_ref, scale_ref, shift_ref, o_ref):
    h = _rms(x_ref[...], g_ref[...]) * (1.0 + scale_ref[...]) + shift_ref[...]
    o_ref[...] = h.astype(o_ref.dtype)


def _prenorm(x, g, scale, shift):
    m, d = x.shape
    tm = 512
    row = pl.BlockSpec((1, d), lambda i: (0, 0))
    return pl.pallas_call(
        _prenorm_body,
        grid=(m // tm,),
        in_specs=[pl.BlockSpec((tm, d), lambda i: (i, 0)), row, row, row],
        out_specs=pl.BlockSpec((tm, d), lambda i: (i, 0)),
        out_shape=jax.ShapeDtypeStruct((m, d), BF16),
        compiler_params=_params(("arbitrary",)),
        name="prenorm",
    )(x, g, scale, shift)


def _mm_body(*refs, n_x, pairs, w_kn, out_t, sub_n, n_extra, n_out, epilogue):
    n_w = len(pairs)
    x_refs = refs[:n_x]
    w_refs = refs[n_x:n_x + n_w]
    e_refs = refs[n_x + n_w:n_x + n_w + n_extra]
    o_refs = refs[n_x + n_w + n_extra:n_x + n_w + n_extra + n_out]
    wb_refs = refs[n_x + n_w + n_extra + n_out:]

    @pl.when(pl.program_id(1) == 0)
    def _():
        for w_ref, wb_ref, kn in zip(w_refs, wb_refs, w_kn):
            w = w_ref[...]
            if not kn and not out_t:
                w = w.T
            wb_ref[...] = w.astype(wb_ref.dtype)

    if out_t:
        accs = [lax.dot_general(wb_ref[...], x_refs[xi][...], (((1,), (1,)), ((), ())),
                                preferred_element_type=F32)
                for xi, wb_ref in zip(pairs, wb_refs)]
        epilogue(accs, e_refs, o_refs, slice(0, wb_refs[0].shape[0]))
        return
    tn = wb_refs[0].shape[1]
    for c0 in range(0, tn, sub_n):
        cs = slice(c0, c0 + sub_n)
        accs = [jnp.dot(x_refs[xi][...], wb_ref[:, cs], preferred_element_type=F32)
                for xi, wb_ref in zip(pairs, wb_refs)]
        epilogue(accs, e_refs, o_refs, cs)


def _matmul(xs, ws, pairs, extras, outs, epilogue, *, n, tm, tn, name, out_t=False, sub_n=None):
    m = xs[0].shape[0]
    n_col = n // tn
    w_mode = dict(pipeline_mode=pl.Buffered(1)) if n_col == 1 else {}
    in_specs = [pl.BlockSpec((tm, x.shape[1]), lambda j, i: (i, 0)) for x in xs]
    scratch = []
    for w, first, kn in ws:
        assert not (kn and out_t)
        if kn:
            kdim = w.shape[0]
            assert first % tn == 0
            in_specs.append(pl.BlockSpec((kdim, tn), functools.partial(_w_cols, first=first // tn), **w_mode))
        else:
            kdim = w.shape[1]
            in_specs.append(pl.BlockSpec((pl.Element(tn), pl.Element(kdim)),
                                         functools.partial(_w_rows, first=first, tn=tn), **w_mode))
        scratch.append(pltpu.VMEM((tn, kdim) if out_t else (kdim, tn), BF16))
    in_specs += [pl.BlockSpec(bs, im) for _, bs, im in extras]
    body = functools.partial(_mm_body, n_x=len(xs), pairs=tuple(pairs), w_kn=tuple(kn for _, _, kn in ws),
                             out_t=out_t, sub_n=sub_n or tn, n_extra=len(extras), n_out=len(outs),
                             epilogue=epilogue)
    return pl.pallas_call(
        body,
        grid=(n_col, m // tm),
        in_specs=in_specs,
        out_specs=[pl.BlockSpec(bs, im) for _, _, bs, im in outs],
        out_shape=[jax.ShapeDtypeStruct(s, d) for s, d, _, _ in outs],
        scratch_shapes=scratch,
        compiler_params=_params(("arbitrary", "arbitrary")),
        name=name,
    )(*xs, *[w for w, _, _ in ws], *[a for a, _, _ in extras])


def _w_cols(j, i, *, first):
    return (0, first + j)


def _w_rows(j, i, *, first, tn):
    assert first % SUBLANES == 0 and tn % SUBLANES == 0
    return (pl.multiple_of(first + j * tn, SUBLANES), 0)


def _tile(j, i):
    return (i, j)


def _tile_t(j, i):
    return (j, i)


def _rowblk(j, i):
    return (i, 0)


def _colblk(j, i):
    return (0, j)


def _colblk_t(j, i):
    return (0, i)


def _ep_gelu(accs, e_refs, o_refs, cols):
    o_refs[0][:, cols] = jax.nn.gelu(accs[0]).astype(o_refs[0].dtype)


def _ep_gelu_layernorm(accs, e_refs, o_refs, cols):
    z = jax.nn.gelu(accs[0])
    mu = jnp.mean(z, axis=-1, keepdims=True)
    zc = z - mu
    var = jnp.mean(zc * zc, axis=-1, keepdims=True)
    y = zc * lax.rsqrt(var + NORM_EPS) * e_refs[0][...] + e_refs[1][...]
    o_refs[0][...] = y.astype(o_refs[0].dtype)


def _ep_rmsnorm(accs, e_refs, o_refs, cols):
    o_refs[0][...] = _rms(accs[0], e_refs[0][...]).astype(o_refs[0].dtype)


def _ep_cast(accs, e_refs, o_refs, cols):
    o_refs[0][:, cols] = accs[0].astype(o_refs[0].dtype)


def _ep_sigmoid(accs, e_refs, o_refs, cols):
    o_refs[0][:, cols] = (0.5 * jnp.tanh(0.5 * accs[0]) + 0.5).astype(o_refs[0].dtype)


def _ep_rope(accs, e_refs, o_refs, cols, *, scale):
    cc = e_refs[0][...]
    ss = e_refs[1][...]
    acc = accs[0]
    for h in range(acc.shape[1] // HEAD_DIM):
        xh = acc[:, h * HEAD_DIM:(h + 1) * HEAD_DIM]
        r = xh * cc + pltpu.roll(xh, HEAD_DIM // 2, 1) * ss
        if scale != 1.0:
            r = r * scale
        c0 = cols.start + h * HEAD_DIM
        o_refs[0][:, c0:c0 + HEAD_DIM] = r.astype(o_refs[0].dtype)


def _ep_partial_rope_t(accs, e_refs, o_refs, cols):
    c = e_refs[0][...]
    s = e_refs[1][...]
    acc = accs[0]
    half = IDX_ROPE_DIM // 2
    for h in range(acc.shape[0] // IDX_HEAD_DIM):
        r0 = h * IDX_HEAD_DIM
        x1 = acc[r0:r0 + half, :]
        x2 = acc[r0 + half:r0 + 2 * half, :]
        o_refs[0][r0:r0 + half, :] = (x1 * c - x2 * s).astype(o_refs[0].dtype)
        o_refs[0][r0 + half:r0 + 2 * half, :] = (x2 * c + x1 * s).astype(o_refs[0].dtype)
        o_refs[0][r0 + 2 * half:r0 + IDX_HEAD_DIM, :] = (
            acc[r0 + 2 * half:r0 + IDX_HEAD_DIM, :].astype(o_refs[0].dtype))


def _partial_rope(x, c, a, b):
    half = IDX_ROPE_DIM // 2
    return x * c + pltpu.roll(x, LANES - half, 1) * a + pltpu.roll(x, half, 1) * b


def _ep_index_keys(accs, e_refs, o_refs, cols, *, w_scale):
    g, bb = e_refs[0][...], e_refs[1][...]
    c, a, b = e_refs[2][...], e_refs[3][...], e_refs[4][...]
    acc = accs[0]
    lane = lax.broadcasted_iota(jnp.int32, acc.shape, 1)
    is_key = lane < IDX_HEAD_DIM
    mu = jnp.sum(jnp.where(is_key, acc, 0.0), axis=-1, keepdims=True) / IDX_HEAD_DIM
    xc = jnp.where(is_key, acc - mu, 0.0)
    var = jnp.sum(xc * xc, axis=-1, keepdims=True) / IDX_HEAD_DIM
    y = xc * lax.rsqrt(var + NORM_EPS) * g + bb
    y = _partial_rope(y, c, a, b)
    o_refs[0][...] = jnp.where(is_key, y, acc * w_scale)


def _ep_merge(accs, e_refs, o_refs, cols):
    o = e_refs[0][:, cols].astype(F32) * accs[0] + e_refs[1][:, cols].astype(F32) * accs[1]
    o_refs[0][:, cols] = o.astype(o_refs[0].dtype)


def _ep_mix_residual(accs, e_refs, o_refs, cols):
    x_ref, g_post, gate, g_pre, scale, shift = e_refs
    x1 = x_ref[...] + gate[...] * _rms(accs[0], g_post[...])
    o_refs[0][...] = x1
    h2 = _rms(x1, g_pre[...]) * (1.0 + scale[...]) + shift[...]
    o_refs[1][...] = h2.astype(o_refs[1].dtype)


def _ep_swiglu(accs, e_refs, o_refs, cols):
    o_refs[0][:, cols] = (jax.nn.silu(accs[0]) * accs[1]).astype(o_refs[0].dtype)

    @pl.when(pl.program_id(1) == 0)
    def _():
        o_refs[1][...] = e_refs[0][...].astype(o_refs[1].dtype)


def _simple_mm(x, w, epilogue, extras, out_dtype, *, n, tm, tn, name, first=0, kn=True, out_t=False,
               sub_n=None):
    m = x.shape[0]
    out = ((n, m), out_dtype, (tn, tm), _tile_t) if out_t else ((m, n), out_dtype, (tm, tn), _tile)
    return _matmul([x], [(w, first, kn)], [0], extras, [out], epilogue, n=n, tm=tm, tn=tn,
                   name=name, out_t=out_t, sub_n=sub_n)[0]


def _gmlp_body(u_ref, v_ref, w_ref, b_ref, o_ref):
    t = lax.broadcasted_iota(jnp.int32, (CHUNK, CHUNK), 0)
    s = lax.broadcasted_iota(jnp.int32, (CHUNK, CHUNK), 1)
    causal = s <= t
    for g in range(GMLP_GROUPS):
        w = jnp.where(causal, w_ref[g], 0.0).astype(BF16)
        cols = slice(g * LANES, (g + 1) * LANES)
        bias = b_ref[:, cols]
        for c in range(u_ref.shape[0] // CHUNK):
            rows = slice(c * CHUNK, (c + 1) * CHUNK)
            sv = jnp.dot(w, v_ref[rows, cols], preferred_element_type=F32) + bias
            o_ref[rows, cols] = (u_ref[rows, cols].astype(F32) * sv).astype(o_ref.dtype)


def _gmlp(u, vn, w_s, bias):
    m, n = u.shape
    tm = 512
    blk = pl.BlockSpec((tm, n), lambda i: (i, 0))
    return pl.pallas_call(
        _gmlp_body,
        grid=(m // tm,),
        in_specs=[blk, blk,
                  pl.BlockSpec(w_s.shape, lambda i: (0, 0, 0)),
                  pl.BlockSpec(bias.shape, lambda i: (0, 0))],
        out_specs=blk,
        out_shape=jax.ShapeDtypeStruct((m, n), BF16),
        compiler_params=_params(("arbitrary",)),
        name="gmlp_gating",
    )(u, vn, w_s, bias)


def _bit_planes(words):
    a = list(words)
    j, mask = KEY_BITS // 2, 0x0000FFFF
    while j:
        for k in range(KEY_BITS):
            if k & j:
                continue
            t = (a[k] ^ (a[k + j] >> j)) & jnp.int32(mask)
            a[k] = a[k] ^ t
            a[k + j] = a[k + j] ^ (t << j)
        j //= 2
        mask = (mask ^ (mask << j)) & 0xFFFFFFFF
    return a


def _indexer_body(qt_ref, wt_ref, k2_ref, bias_ref, key_ref, plane_ref, live_ref):
    tq = qt_ref.shape[1]
    n_chunks_total = bias_ref.shape[0]
    i = pl.program_id(0)
    n_chunks = ((i + 1) * tq + ATT_KC - 1) // ATT_KC
    blocks_per_chunk = ATT_KC // PLANE_KEYS
    n_blocks = n_chunks * blocks_per_chunk
    heads_per_group = LANES // IDX_HEAD_DIM

    @pl.when(i == 0)
    def _():
        plane_ref[...] = jnp.zeros(plane_ref.shape, jnp.int32)

    q_pos = i * tq + lax.broadcasted_iota(jnp.int32, (ATT_KC, tq), 1)
    k_off = lax.broadcasted_iota(jnp.int32, (ATT_KC, tq), 0)

    def score_chunk(c):
        k0 = pl.multiple_of(c * ATT_KC, ATT_KC)
        score = jnp.zeros((ATT_KC, tq), F32)
        for h in range(IDX_HEADS):
            grp, r = divmod(h, heads_per_group)
            logit = jnp.dot(k2_ref[r, pl.ds(k0, ATT_KC), :], qt_ref[grp * LANES:(grp + 1) * LANES, :],
                            preferred_element_type=F32)
            score = score + jnp.maximum(logit, 0.0) * wt_ref[h:h + 1, :]
        score = jnp.where(k0 + k_off <= q_pos, score, -jnp.inf)
        bits = lax.bitcast_convert_type(score, jnp.int32)
        key_ref[pl.ds(k0, ATT_KC), :] = bits ^ ((bits >> 31) & jnp.int32(0x7FFFFFFF))

    def slice_chunk(c):
        for hh in range(blocks_per_chunk):
            blk = c * blocks_per_chunk + hh
            k0 = pl.multiple_of(blk * PLANE_KEYS, PLANE_KEYS)
            keys = key_ref[pl.ds(k0, PLANE_KEYS), :]
            planes = _bit_planes([keys[w * SUBLANES:(w + 1) * SUBLANES, :] for w in range(KEY_BITS)])
            r0 = pl.multiple_of(blk * SUBLANES, SUBLANES)
            plane_ref[0, pl.ds(r0, SUBLANES), :] = ~planes[0]
            for r in range(1, KEY_BITS):
                plane_ref[r, pl.ds(r0, SUBLANES), :] = planes[r]

    def score_and_slice(c, carry):
        slice_chunk(c - 1)
        score_chunk(c)
        return carry

    score_chunk(0)
    lax.fori_loop(1, n_chunks, score_and_slice, 0)
    slice_chunk(n_chunks - 1)

    plane_rows = live_ref.shape[0]
    row_id = lax.broadcasted_iota(jnp.int32, (plane_rows, tq), 0)
    live_ref[...] = jnp.where(row_id < n_blocks * SUBLANES, jnp.int32(-1), jnp.int32(0))

    def search_bit(r, carry):
        need, kth = carry
        plane = plane_ref[r]
        live = live_ref[...]
        hit = live & plane
        ones = lax.population_count(hit)
        part = jnp.sum(ones.reshape(plane_rows // SUBLANES, SUBLANES, tq), axis=0)
        total = jnp.sum(part.astype(F32), axis=0, keepdims=True)
        take = total >= need
        live_ref[...] = jnp.where(take, hit, live & ~plane)
        need = jnp.where(take, need, need - total)
        kth = jnp.where(take, kth | jnp.left_shift(jnp.int32(1), KEY_BITS - 1 - r), kth)
        return need, kth

    need, kth = lax.fori_loop(0, KEY_BITS, search_bit,
                              (jnp.full((1, tq), float(INDEX_TOPK), F32), jnp.zeros((1, tq), jnp.int32)))
    kth = kth ^ jnp.int32(INT_MIN)
    tied = lax.population_count(live_ref[...])
    tied = jnp.sum(tied.reshape(plane_rows // SUBLANES, SUBLANES, tq), axis=0).astype(F32)
    tied = jnp.sum(tied, axis=0, keepdims=True)
    few_keys = kth <= KEY_NEG_INF
    kth = jnp.maximum(kth, KEY_NEG_INF + 1)
    surplus = jnp.max(jnp.where(few_keys, 0.0, tied - need))
    need = jnp.where(few_keys, float(2 ** 24), need)

    def bias_tile(sel):
        return jnp.where(sel, 0.0, MASK_BIAS).T

    @pl.when(surplus <= 0.0)
    def _():
        def write_chunk(c, carry):
            parts = []
            for hh in range(blocks_per_chunk):
                k0 = pl.multiple_of(c * ATT_KC + hh * PLANE_KEYS, PLANE_KEYS)
                parts.append(bias_tile(key_ref[pl.ds(k0, PLANE_KEYS), :] >= kth))
            bias_ref[c] = jnp.concatenate(parts, axis=1).astype(bias_ref.dtype)
            return carry

        lax.fori_loop(0, n_chunks, write_chunk, 0)

    @pl.when(surplus > 0.0)
    def _():
        row = lax.broadcasted_iota(jnp.int32, (PLANE_KEYS, PLANE_KEYS), 0)
        col = lax.broadcasted_iota(jnp.int32, (PLANE_KEYS, PLANE_KEYS), 1)
        prefix = jnp.where(col <= row, 1.0, 0.0).astype(BF16)

        def write_chunk(c, seen):
            parts = []
            for hh in range(blocks_per_chunk):
                k0 = pl.multiple_of(c * ATT_KC + hh * PLANE_KEYS, PLANE_KEYS)
                keys = key_ref[pl.ds(k0, PLANE_KEYS), :]
                tie = keys == kth
                rank = seen + jnp.dot(prefix, jnp.where(tie, 1.0, 0.0).astype(BF16),
                                      preferred_element_type=F32)
                parts.append(bias_tile((keys > kth) | (tie & (rank <= need))))
                seen = rank[PLANE_KEYS - 1:PLANE_KEYS, :]
            bias_ref[c] = jnp.concatenate(parts, axis=1).astype(bias_ref.dtype)
            return seen

        lax.fori_loop(0, n_chunks, write_chunk, jnp.zeros((1, tq), F32))

    def fill_chunk(c, carry):
        bias_ref[c] = jnp.full(bias_ref.shape[1:], MASK_BIAS, bias_ref.dtype)
        return carry

    lax.fori_loop(n_chunks, n_chunks_total, fill_chunk, 0)


def _indexer_mask(q_idx_t, idx_w_t, k2):
    s = q_idx_t.shape[1]
    tq = SEL_TQ
    assert tq % PLANE_KEYS == 0 and s % (KEY_BITS * SUBLANES) == 0
    return pl.pallas_call(
        _indexer_body,
        grid=(s // tq,),
        in_specs=[pl.BlockSpec((q_idx_t.shape[0], tq), lambda i: (0, i)),
                  pl.BlockSpec((IDX_HEADS, tq), lambda i: (0, i)),
                  pl.BlockSpec(k2.shape, lambda i: (0, 0, 0))],
        out_specs=pl.BlockSpec((s // ATT_KC, tq, ATT_KC), lambda i: (0, i, 0)),
        out_shape=jax.ShapeDtypeStruct((s // ATT_KC, s, ATT_KC), BF16),
        scratch_shapes=[pltpu.VMEM((s, tq), jnp.int32),
                        pltpu.VMEM((KEY_BITS, s // KEY_BITS, tq), jnp.int32),
                        pltpu.VMEM((s // KEY_BITS, tq), jnp.int32)],
        compiler_params=_params(("arbitrary",)),
        name="indexer_mask",
    )(q_idx_t, idx_w_t, k2)


def _attention_body(q_ref, k_ref, v_ref, bias_ref, o_ref, m_ref, acc_ref):
    tq = q_ref.shape[0]
    chunk_keys = bias_ref.shape[2]
    i = pl.program_id(0)
    n_chunks = (i * tq + tq - 1) // chunk_keys + 1

    m_ref[...] = jnp.full(m_ref.shape, M_INIT, F32)
    acc_ref[...] = jnp.zeros(acc_ref.shape, F32)

    def step(first_chunk, width):
        kc = width * chunk_keys
        sub = kc // LANES
        k0 = pl.multiple_of(first_chunk * chunk_keys, chunk_keys)
        bias = jnp.concatenate([bias_ref[first_chunk + c] for c in range(width)], axis=1)
        ones = jnp.ones((kc, HEAD_DIM), v_ref.dtype)
        for h in range(ATTN_HEADS):
            cols = slice(h * HEAD_DIM, (h + 1) * HEAD_DIM)
            s = lax.dot_general(q_ref[:, cols], k_ref[pl.ds(k0, kc), cols],
                                (((1,), (1,)), ((), ())), preferred_element_type=F32)
            sb = s.astype(BF16) + bias
            part = sb[:, :LANES]
            for c in range(1, sub):
                part = jnp.maximum(part, sb[:, c * LANES:(c + 1) * LANES])
            m_prev = m_ref[h]
            m_next = jnp.maximum(m_prev, jnp.max(part.astype(F32), axis=1, keepdims=True))
            p = jnp.exp2(sb - jnp.concatenate([m_next.astype(BF16)] * sub, axis=1))
            alpha = jnp.exp2(m_prev - m_next)
            m_ref[h] = m_next
            pv = jnp.dot(p, jnp.concatenate([v_ref[pl.ds(k0, kc), cols], ones], axis=1),
                         preferred_element_type=F32)
            acc_ref[h] = jnp.concatenate([alpha, alpha], axis=1) * acc_ref[h] + pv

    def wide_step(j, carry):
        step(j * ATT_STEP_CHUNKS, ATT_STEP_CHUNKS)
        return carry

    n_wide = n_chunks // ATT_STEP_CHUNKS
    lax.fori_loop(0, n_wide, wide_step, 0)
    for r in range(1, ATT_STEP_CHUNKS):
        @pl.when(n_chunks % ATT_STEP_CHUNKS >= r)
        def _():
            step(n_wide * ATT_STEP_CHUNKS + r - 1, 1)

    for h in range(ATTN_HEADS):
        acc = acc_ref[h]
        o_ref[:, h * HEAD_DIM:(h + 1) * HEAD_DIM] = (acc[:, :HEAD_DIM] / acc[:, HEAD_DIM:]).astype(o_ref.dtype)


def _attention(q, k, v, bias):
    s, width = q.shape
    tq = SEL_TQ
    n_chunks, _, kc = bias.shape
    resident = dict(pipeline_mode=pl.Buffered(1))
    return pl.pallas_call(
        _attention_body,
        grid=(s // tq,),
        in_specs=[pl.BlockSpec((tq, width), lambda i: (i, 0)),
                  pl.BlockSpec((s, width), lambda i: (0, 0), **resident),
                  pl.BlockSpec((s, width), lambda i: (0, 0), **resident),
                  pl.BlockSpec((n_chunks, tq, kc), lambda i: (0, i, 0))],
        out_specs=pl.BlockSpec((tq, width), lambda i: (i, 0)),
        out_shape=jax.ShapeDtypeStruct((s, width), BF16),
        scratch_shapes=[pltpu.VMEM((ATTN_HEADS, tq, LANES), F32),
                        pltpu.VMEM((ATTN_HEADS, tq, 2 * HEAD_DIM), F32)],
        compiler_params=_params(("arbitrary",)),
        name="masked_attention",
    )(q, k, v, bias)


def _ffn_down_body(a_ref, w_ref, x_ref, g_ref, gate_ref, o_ref):
    f = jnp.dot(a_ref[...], w_ref[...], preferred_element_type=F32)
    o_ref[...] = x_ref[...] + gate_ref[...] * _rms(f, g_ref[...])


def _ffn_down(a, w, x1, g_post, gate):
    m, kdim = a.shape
    d = w.shape[1]
    tm = 256
    row = pl.BlockSpec((1, d), lambda i: (0, 0))
    return pl.pallas_call(
        _ffn_down_body,
        grid=(m // tm,),
        in_specs=[pl.BlockSpec((tm, kdim), lambda i: (i, 0)),
                  pl.BlockSpec((kdim, d), lambda i: (0, 0), pipeline_mode=pl.Buffered(1)),
                  pl.BlockSpec((tm, d), lambda i: (i, 0)), row, row],
        out_specs=pl.BlockSpec((tm, d), lambda i: (i, 0)),
        out_shape=jax.ShapeDtypeStruct((m, d), F32),
        compiler_params=_params(("arbitrary",)),
        name="ffn_down_residual",
    )(a, w, x1, g_post, gate)


def _rope_angles(seq, dim):
    inv = 1.0 / (ROPE_THETA ** (jnp.arange(0, dim, 2, dtype=F32) / dim))
    ang = jnp.arange(seq, dtype=F32)[:, None] * inv[None, :]
    return jnp.cos(ang), jnp.sin(ang)


def _attn_rope_tables(seq):
    cos, sin = _rope_angles(seq, HEAD_DIM)
    return jnp.concatenate([cos, cos], axis=1), jnp.concatenate([-sin, sin], axis=1)


def _index_rope_tables(seq):
    cos, sin = _rope_angles(seq, IDX_ROPE_DIM)
    half = IDX_ROPE_DIM // 2
    rest = IDX_HEAD_DIM - IDX_ROPE_DIM
    zeros = lambda n: jnp.zeros((seq, n), F32)
    c = jnp.concatenate([cos, cos, jnp.ones((seq, rest), F32)], axis=1)
    a = jnp.concatenate([-sin, zeros(IDX_HEAD_DIM - half)], axis=1)
    b = jnp.concatenate([zeros(half), sin, zeros(rest)], axis=1)
    rep = LANES // IDX_HEAD_DIM
    return tuple(jnp.tile(t, (1, rep)) for t in (c, a, b)) + (cos.T, sin.T)


def _layer(x, mod, g_pre_mix, g_post_mix, w_in_t, gmlp_ln_g, gmlp_ln_b, gmlp_w_s, gmlp_b_s,
           q_lat_norm_g, w_q_up, w_qidx_up, kidx_ln_g, kidx_ln_b, w_proj_a, w_proj_b, w_out,
           g_pre_ffn, g_post_ffn, w_ffn_gate, w_ffn_up, w_ffn_down, tables):
    s, d = x.shape
    cc, ss, ic, ia, ib, icos_t, isin_t = tables
    row = lambda a: a.reshape(1, -1)
    shift_m, scale_m, gate_m, shift_f, scale_f, gate_f = (mod[:, n * d:(n + 1) * d] for n in range(N_MOD))

    h = _prenorm(x, row(g_pre_mix), scale_m, shift_m)

    tm = 1024
    in_proj = functools.partial(_simple_mm, h, w_in_t, kn=False)

    u = in_proj(_ep_gelu, [], BF16, n=GMLP_WIDTH, tm=tm, tn=GMLP_WIDTH, sub_n=EPILOGUE_SUB_N,
                name="in_proj_u")
    vn = in_proj(_ep_gelu_layernorm,
                 [(row(gmlp_ln_g), (1, GMLP_WIDTH), _colblk), (row(gmlp_ln_b), (1, GMLP_WIDTH), _colblk)],
                 BF16, n=GMLP_WIDTH, first=GMLP_WIDTH, tm=512, tn=GMLP_WIDTH, name="in_proj_v_ln")
    gmlp_bias = jnp.repeat(gmlp_b_s.T, LANES, axis=1)
    y_a = _gmlp(u, vn, gmlp_w_s, gmlp_bias)

    q_lat = in_proj(_ep_rmsnorm, [(row(q_lat_norm_g), (1, Q_LORA_RANK), _colblk)],
                    BF16, n=Q_LORA_RANK, first=COL_QLAT, tm=tm, tn=Q_LORA_RANK, name="in_proj_qlat")
    q_scale = HEAD_DIM ** -0.5 * math.log2(math.e)
    rope_extras = [(cc, (tm, HEAD_DIM), _rowblk), (ss, (tm, HEAD_DIM), _rowblk)]
    q = _simple_mm(q_lat, w_q_up, functools.partial(_ep_rope, scale=q_scale), rope_extras,
                   BF16, n=ATTN_WIDTH, tm=tm, tn=ATTN_WIDTH, sub_n=EPILOGUE_SUB_N, name="q_up_rope")
    k = in_proj(functools.partial(_ep_rope, scale=1.0), rope_extras, BF16,
                n=ATTN_WIDTH, first=COL_K, tm=tm, tn=ATTN_WIDTH, sub_n=EPILOGUE_SUB_N,
                name="in_proj_k_rope")
    v = in_proj(_ep_cast, [], BF16, n=ATTN_WIDTH, first=COL_V, tm=tm, tn=ATTN_WIDTH,
                sub_n=EPILOGUE_SUB_N, name="in_proj_v")
    half = IDX_ROPE_DIM // 2
    q_idx_t = _simple_mm(q_lat, w_qidx_up.T, _ep_partial_rope_t,
                         [(icos_t, (half, tm), _colblk_t), (isin_t, (half, tm), _colblk_t)],
                         BF16, n=IDX_HEADS * IDX_HEAD_DIM, tm=tm, tn=512, name="qidx_up_rope",
                         kn=False, out_t=True)
    pad = lambda a: jnp.pad(a, (0, LANES - a.shape[0])).reshape(1, LANES)
    idx_w_scale = (IDX_HEADS ** -0.5) * (IDX_HEAD_DIM ** -0.5)
    kx = in_proj(functools.partial(_ep_index_keys, w_scale=idx_w_scale),
                 [(pad(kidx_ln_g), (1, LANES), _colblk), (pad(kidx_ln_b), (1, LANES), _colblk)]
                 + [(t, (tm, LANES), _rowblk) for t in (ic, ia, ib)],
                 F32, n=LANES, first=COL_KIDX, tm=tm, tn=LANES, name="in_proj_index_keys")
    k_idx = kx[:, :IDX_HEAD_DIM].astype(BF16)
    zeros = jnp.zeros_like(k_idx)
    k2 = jnp.stack([jnp.concatenate([k_idx, zeros], axis=1), jnp.concatenate([zeros, k_idx], axis=1)])
    idx_w_t = kx[:, IDX_HEAD_DIM:IDX_HEAD_DIM + IDX_HEADS].T
    bias = _indexer_mask(q_idx_t, idx_w_t, k2)
    y_b = _attention(q, k, v, bias)

    gates = in_proj(_ep_sigmoid, [], BF16, n=2 * d, first=COL_GATE, tm=tm, tn=1024,
                    sub_n=EPILOGUE_SUB_N, name="in_proj_gates")
    n_gate_blocks = d // 512
    merged = _matmul([y_a, y_b], [(w_proj_a, 0, True), (w_proj_b, 0, True)], [0, 1],
                     [(gates, (tm, 512), _tile),
                      (gates, (tm, 512), lambda j, i: (i, j + n_gate_blocks))],
                     [((s, d), BF16, (tm, 512), _tile)], _ep_merge, n=d, tm=tm, tn=512,
                     name="branch_merge")[0]
    tm_full = 256
    vec = lambda a: (a, (1, d), _colblk)
    x1, h2 = _matmul([merged], [(w_out, 0, True)], [0],
                     [(x, (tm_full, d), _rowblk), vec(row(g_post_mix)), vec(gate_m),
                      vec(row(g_pre_ffn)), vec(scale_f), vec(shift_f)],
                     [((s, d), F32, (tm_full, d), _tile), ((s, d), BF16, (tm_full, d), _tile)],
                     _ep_mix_residual, n=d, tm=tm_full, tn=d, name="out_proj_residual")

    ffn_hidden = w_ffn_gate.shape[1]
    tn_ffn = 512
    down_rows = ffn_hidden // (ffn_hidden // tn_ffn)
    down_blk = ((down_rows, d), lambda j, i: (j, 0))
    act, w_down = _matmul([h2], [(w_ffn_gate, 0, True), (w_ffn_up, 0, True)], [0, 0],
                          [(w_ffn_down,) + down_blk],
                          [((s, ffn_hidden), BF16, (tm, tn_ffn), _tile), (w_ffn_down.shape, BF16) + down_blk],
                          _ep_swiglu, n=ffn_hidden, tm=tm, tn=tn_ffn, name="ffn_up")
    return _ffn_down(act, w_down, x1, row(g_post_ffn), gate_f)


def kernel(x, c, w_mod, b_mod, g_pre_mix, g_post_mix, w_in, gmlp_ln_g, gmlp_ln_b, gmlp_w_s, gmlp_b_s, q_lat_norm_g, w_q_up, w_qidx_up, kidx_ln_g, kidx_ln_b, w_proj_a, w_proj_b, w_out, g_pre_ffn, g_post_ffn, w_ffn_gate, w_ffn_up, w_ffn_down):
    batch, seq, d = x.shape
    assert batch == 1 and d == D_MODEL
    tables = _attn_rope_tables(seq) + _index_rope_tables(seq)
    y = x[0]
    for l in range(w_mod.shape[0]):
        mod = _modulation(c, w_mod[l], b_mod[l])
        y = _layer(y, mod, g_pre_mix[l], g_post_mix[l], w_in[l].T, gmlp_ln_g[l], gmlp_ln_b[l],
                   gmlp_w_s[l], gmlp_b_s[l], q_lat_norm_g[l], w_q_up[l], w_qidx_up[l],
                   kidx_ln_g[l], kidx_ln_b[l], w_proj_a[l], w_proj_b[l], w_out[l],
                   g_pre_ffn[l], g_post_ffn[l], w_ffn_gate[l], w_ffn_up[l], w_ffn_down[l], tables)
    return y[None]
```

```python
import functools
import math

import jax
import jax.numpy as jnp
from jax import lax
from jax.experimental import pallas as pl
from jax.experimental.pallas import tpu as pltpu

F32 = jnp.float32
BF16 = jnp.bfloat16

D_MODEL = 2048
GMLP_WIDTH = 1024
GMLP_GROUPS = 8
CHUNK = 128
ATTN_HEADS = 8
HEAD_DIM = 128
ATTN_WIDTH = ATTN_HEADS * HEAD_DIM
Q_LORA_RANK = 512
IDX_HEADS = 16
IDX_HEAD_DIM = 64
IDX_ROPE_DIM = 32
INDEX_TOPK = 256
ROPE_THETA = 10000.0
N_MOD = 6
NORM_EPS = 1e-6

COL_QLAT = 2 * GMLP_WIDTH
COL_K = COL_QLAT + Q_LORA_RANK
COL_V = COL_K + ATTN_WIDTH
COL_KIDX = COL_V + ATTN_WIDTH
COL_IDXW = COL_KIDX + IDX_HEAD_DIM
COL_GATE = COL_IDXW + IDX_HEADS

LANES = 128
SUBLANES = 8
VMEM_LIMIT = 56 * 1024 * 1024

MASK_BIAS = -(2.0 ** 100)
M_INIT = -(2.0 ** 99)
INT_MIN = -2 ** 31
KEY_NEG_INF = -2139095041

SEL_TQ = 256
KEY_BITS = 32
PLANE_KEYS = KEY_BITS * SUBLANES
ATT_KC = 2 * PLANE_KEYS
ATT_STEP_CHUNKS = 2
EPILOGUE_SUB_N = 512


def _params(sem=None):
    return pltpu.CompilerParams(dimension_semantics=sem, vmem_limit_bytes=VMEM_LIMIT)


def _mod_body(cb_ref, w_ref, b_ref, o_ref):
    cb = cb_ref[...]
    tn = o_ref.shape[1]
    parts = [jnp.sum(w_ref[:, p * LANES:(p + 1) * LANES] * cb, axis=0, keepdims=True)
             for p in range(tn // LANES)]
    o_ref[...] = jnp.concatenate(parts, axis=1) + b_ref[...]


def _modulation(c, w_mod, b_mod):
    k, n = w_mod.shape
    tn = 1024
    cb = jnp.broadcast_to(c.reshape(k, 1), (k, LANES))
    return pl.pallas_call(
        _mod_body,
        grid=(n // tn,),
        in_specs=[pl.BlockSpec((k, LANES), lambda j: (0, 0)),
                  pl.BlockSpec((k, tn), lambda j: (0, j)),
                  pl.BlockSpec((1, tn), lambda j: (0, j))],
        out_specs=pl.BlockSpec((1, tn), lambda j: (0, j)),
        out_shape=jax.ShapeDtypeStruct((1, n), F32),
        compiler_params=_params(("arbitrary",)),
        name="modulation",
    )(cb, w_mod, b_mod.reshape(1, n))


def _rms(x, g):
    return x * lax.rsqrt(jnp.mean(x * x, axis=-1, keepdims=True) + NORM_EPS) * g


def _prenorm_body(x_ref, g_ref, scale_ref, shift_ref, o_ref):
    h = _rms(x_ref[...], g_ref[...]) * (1.0 + scale_ref[...]) + shift_ref[...]
    o_ref[...] = h.astype(o_ref.dtype)


def _prenorm(x, g, scale, shift):
    m, d = x.shape
    tm = 512
    row = pl.BlockSpec((1, d), lambda i: (0, 0))
    return pl.pallas_call(
        _prenorm_body,
        grid=(m // tm,),
        in_specs=[pl.BlockSpec((tm, d), lambda i: (i, 0)), row, row, row],
        out_specs=pl.BlockSpec((tm, d), lambda i: (i, 0)),
        out_shape=jax.ShapeDtypeStruct((m, d), BF16),
        compiler_params=_params(("arbitrary",)),
        name="prenorm",
    )(x, g, scale, shift)


def _mm_body(*refs, n_x, pairs, w_kn, out_t, sub_n, n_extra, n_out, epilogue):
    n_w = len(pairs)
    x_refs = refs[:n_x]
    w_refs = refs[n_x:n_x + n_w]
    e_refs = refs[n_x + n_w:n_x + n_w + n_extra]
    o_refs = refs[n_x + n_w + n_extra:n_x + n_w + n_extra + n_out]
    wb_refs = refs[n_x + n_w + n_extra + n_out:]

    @pl.when(pl.program_id(1) == 0)
    def _():
        for w_ref, wb_ref, kn in zip(w_refs, wb_refs, w_kn):
            w = w_ref[...]
            if not kn and not out_t:
                w = w.T
            wb_ref[...] = w.astype(wb_ref.dtype)

    if out_t:
        accs = [lax.dot_general(wb_ref[...], x_refs[xi][...], (((1,), (1,)), ((), ())),
                                preferred_element_type=F32)
                for xi, wb_ref in zip(pairs, wb_refs)]
        epilogue(accs, e_refs, o_refs, slice(0, wb_refs[0].shape[0]))
        return
    tn = wb_refs[0].shape[1]
    for c0 in range(0, tn, sub_n):
        cs = slice(c0, c0 + sub_n)
        accs = [jnp.dot(x_refs[xi][...], wb_ref[:, cs], preferred_element_type=F32)
                for xi, wb_ref in zip(pairs, wb_refs)]
        epilogue(accs, e_refs, o_refs, cs)


def _matmul(xs, ws, pairs, extras, outs, epilogue, *, n, tm, tn, name, out_t=False, sub_n=None):
    m = xs[0].shape[0]
    n_col = n // tn
    w_mode = dict(pipeline_mode=pl.Buffered(1)) if n_col == 1 else {}
    in_specs = [pl.BlockSpec((tm, x.shape[1]), lambda j, i: (i, 0)) for x in xs]
    scratch = []
    for w, first, kn in ws:
        assert not (kn and out_t)
        if kn:
            kdim = w.shape[0]
            assert first % tn == 0
            in_specs.append(pl.BlockSpec((kdim, tn), functools.partial(_w_cols, first=first // tn), **w_mode))
        else:
            kdim = w.shape[1]
            in_specs.append(pl.BlockSpec((pl.Element(tn), pl.Element(kdim)),
                                         functools.partial(_w_rows, first=first, tn=tn), **w_mode))
        scratch.append(pltpu.VMEM((tn, kdim) if out_t else (kdim, tn), BF16))
    in_specs += [pl.BlockSpec(bs, im) for _, bs, im in extras]
    body = functools.partial(_mm_body, n_x=len(xs), pairs=tuple(pairs), w_kn=tuple(kn for _, _, kn in ws),
                             out_t=out_t, sub_n=sub_n or tn, n_extra=len(extras), n_out=len(outs),
                             epilogue=epilogue)
    return pl.pallas_call(
        body,
        grid=(n_col, m // tm),
        in_specs=in_specs,
        out_specs=[pl.BlockSpec(bs, im) for _, _, bs, im in outs],
        out_shape=[jax.ShapeDtypeStruct(s, d) for s, d, _, _ in outs],
        scratch_shapes=scratch,
        compiler_params=_params(("arbitrary", "arbitrary")),
        name=name,
    )(*xs, *[w for w, _, _ in ws], *[a for a, _, _ in extras])


def _w_cols(j, i, *, first):
    return (0, first + j)


def _w_rows(j, i, *, first, tn):
    assert first % SUBLANES == 0 and tn % SUBLANES == 0
    return (pl.multiple_of(first + j * tn, SUBLANES), 0)


def _tile(j, i):
    return (i, j)


def _tile_t(j, i):
    return (j, i)


def _rowblk(j, i):
    return (i, 0)


def _colblk(j, i):
    return (0, j)


def _colblk_t(j, i):
    return (0, i)


def _ep_gelu(accs, e_refs, o_refs, cols):
    o_refs[0][:, cols] = jax.nn.gelu(accs[0]).astype(o_refs[0].dtype)


def _ep_gelu_layernorm(accs, e_refs, o_refs, cols):
    z = jax.nn.gelu(accs[0])
    mu = jnp.mean(z, axis=-1, keepdims=True)
    zc = z - mu
    var = jnp.mean(zc * zc, axis=-1, keepdims=True)
    y = zc * lax.rsqrt(var + NORM_EPS) * e_refs[0][...] + e_refs[1][...]
    o_refs[0][...] = y.astype(o_refs[0].dtype)


def _ep_rmsnorm(accs, e_refs, o_refs, cols):
    o_refs[0][...] = _rms(accs[0], e_refs[0][...]).astype(o_refs[0].dtype)


def _ep_cast(accs, e_refs, o_refs, cols):
    o_refs[0][:, cols] = accs[0].astype(o_refs[0].dtype)


def _ep_sigmoid(accs, e_refs, o_refs, cols):
    o_refs[0][:, cols] = (0.5 * jnp.tanh(0.5 * accs[0]) + 0.5).astype(o_refs[0].dtype)


def _ep_rope(accs, e_refs, o_refs, cols, *, scale):
    cc = e_refs[0][...]
    ss = e_refs[1][...]
    acc = accs[0]
    for h in range(acc.shape[1] // HEAD_DIM):
        xh = acc[:, h * HEAD_DIM:(h + 1) * HEAD_DIM]
        r = xh * cc + pltpu.roll(xh, HEAD_DIM // 2, 1) * ss
        if scale != 1.0:
            r = r * scale
        c0 = cols.start + h * HEAD_DIM
        o_refs[0][:, c0:c0 + HEAD_DIM] = r.astype(o_refs[0].dtype)


def _ep_partial_rope_t(accs, e_refs, o_refs, cols):
    c = e_refs[0][...]
    s = e_refs[1][...]
    acc = accs[0]
    half = IDX_ROPE_DIM // 2
    for h in range(acc.shape[0] // IDX_HEAD_DIM):
        r0 = h * IDX_HEAD_DIM
        x1 = acc[r0:r0 + half, :]
        x2 = acc[r0 + half:r0 + 2 * half, :]
        o_refs[0][r0:r0 + half, :] = (x1 * c - x2 * s).astype(o_refs[0].dtype)
        o_refs[0][r0 + half:r0 + 2 * half, :] = (x2 * c + x1 * s).astype(o_refs[0].dtype)
        o_refs[0][r0 + 2 * half:r0 + IDX_HEAD_DIM, :] = (
            acc[r0 + 2 * half:r0 + IDX_HEAD_DIM, :].astype(o_refs[0].dtype))


def _partial_rope(x, c, a, b):
    half = IDX_ROPE_DIM // 2
    return x * c + pltpu.roll(x, LANES - half, 1) * a + pltpu.roll(x, half, 1) * b


def _ep_index_keys(accs, e_refs, o_refs, cols, *, w_scale):
    g, bb = e_refs[0][...], e_refs[1][...]
    c, a, b = e_refs[2][...], e_refs[3][...], e_refs[4][...]
    acc = accs[0]
    lane = lax.broadcasted_iota(jnp.int32, acc.shape, 1)
    is_key = lane < IDX_HEAD_DIM
    mu = jnp.sum(jnp.where(is_key, acc, 0.0), axis=-1, keepdims=True) / IDX_HEAD_DIM
    xc = jnp.where(is_key, acc - mu, 0.0)
    var = jnp.sum(xc * xc, axis=-1, keepdims=True) / IDX_HEAD_DIM
    y = xc * lax.rsqrt(var + NORM_EPS) * g + bb
    y = _partial_rope(y, c, a, b)
    o_refs[0][...] = jnp.where(is_key, y, acc * w_scale)


def _ep_merge(accs, e_refs, o_refs, cols):
    o = e_refs[0][:, cols].astype(F32) * accs[0] + e_refs[1][:, cols].astype(F32) * accs[1]
    o_refs[0][:, cols] = o.astype(o_refs[0].dtype)


def _ep_mix_residual(accs, e_refs, o_refs, cols):
    x_ref, g_post, gate, g_pre, scale, shift = e_refs
    x1 = x_ref[...] + gate[...] * _rms(accs[0], g_post[...])
    o_refs[0][...] = x1
    h2 = _rms(x1, g_pre[...]) * (1.0 + scale[...]) + shift[...]
    o_refs[1][...] = h2.astype(o_refs[1].dtype)


def _ep_swiglu(accs, e_refs, o_refs, cols):
    o_refs[0][:, cols] = (jax.nn.silu(accs[0]) * accs[1]).astype(o_refs[0].dtype)

    if cols.start == 0:
        o_refs[1][...] = e_refs[0][...].astype(o_refs[1].dtype)


def _simple_mm(x, w, epilogue, extras, out_dtype, *, n, tm, tn, name, first=0, kn=True, out_t=False,
               sub_n=None):
    m = x.shape[0]
    out = ((n, m), out_dtype, (tn, tm), _tile_t) if out_t else ((m, n), out_dtype, (tm, tn), _tile)
    return _matmul([x], [(w, first, kn)], [0], extras, [out], epilogue, n=n, tm=tm, tn=tn,
                   name=name, out_t=out_t, sub_n=sub_n)[0]


def _gmlp_body(u_ref, v_ref, w_ref, b_ref, o_ref):
    t = lax.broadcasted_iota(jnp.int32, (CHUNK, CHUNK), 0)
    s = lax.broadcasted_iota(jnp.int32, (CHUNK, CHUNK), 1)
    causal = s <= t
    for g in range(GMLP_GROUPS):
        w = jnp.where(causal, w_ref[g], 0.0).astype(BF16)
        cols = slice(g * LANES, (g + 1) * LANES)
        bias = b_ref[:, cols]
        for c in range(u_ref.shape[0] // CHUNK):
            rows = slice(c * CHUNK, (c + 1) * CHUNK)
            sv = jnp.dot(w, v_ref[rows, cols], preferred_element_type=F32) + bias
            o_ref[rows, cols] = (u_ref[rows, cols].astype(F32) * sv).astype(o_ref.dtype)


def _gmlp(u, vn, w_s, bias):
    m, n = u.shape
    tm = 512
    blk = pl.BlockSpec((tm, n), lambda i: (i, 0))
    return pl.pallas_call(
        _gmlp_body,
        grid=(m // tm,),
        in_specs=[blk, blk,
                  pl.BlockSpec(w_s.shape, lambda i: (0, 0, 0)),
                  pl.BlockSpec(bias.shape, lambda i: (0, 0))],
        out_specs=blk,
        out_shape=jax.ShapeDtypeStruct((m, n), BF16),
        compiler_params=_params(("arbitrary",)),
        name="gmlp_gating",
    )(u, vn, w_s, bias)


def _bit_planes(words):
    a = list(words)
    j, mask = KEY_BITS // 2, 0x0000FFFF
    while j:
        for k in range(KEY_BITS):
            if k & j:
                continue
            t = (a[k] ^ (a[k + j] >> j)) & jnp.int32(mask)
            a[k] = a[k] ^ t
            a[k + j] = a[k + j] ^ (t << j)
        j //= 2
        mask = (mask ^ (mask << j)) & 0xFFFFFFFF
    return a


def _indexer_body(qt_ref, wt_ref, k2_ref, bias_ref, key_ref, plane_ref, live_ref, kth_ref, cnt_ref):
    tq = qt_ref.shape[1]
    n_chunks_total = bias_ref.shape[0]
    i = pl.program_id(0)
    n_chunks = ((i + 1) * tq + ATT_KC - 1) // ATT_KC
    blocks_per_chunk = ATT_KC // PLANE_KEYS
    n_blocks = n_chunks * blocks_per_chunk
    heads_per_group = LANES // IDX_HEAD_DIM

    @pl.when(i == 0)
    def _():
        plane_ref[...] = jnp.zeros(plane_ref.shape, jnp.int32)

    q_pos = i * tq + lax.broadcasted_iota(jnp.int32, (ATT_KC, tq), 1)
    k_off = lax.broadcasted_iota(jnp.int32, (ATT_KC, tq), 0)

    def score_chunk(c):
        k0 = pl.multiple_of(c * ATT_KC, ATT_KC)
        score = jnp.zeros((ATT_KC, tq), F32)
        for h in range(IDX_HEADS):
            grp, r = divmod(h, heads_per_group)
            logit = jnp.dot(k2_ref[r, pl.ds(k0, ATT_KC), :], qt_ref[grp * LANES:(grp + 1) * LANES, :],
                            preferred_element_type=F32)
            score = score + jnp.maximum(logit, 0.0) * wt_ref[h:h + 1, :]
        score = jnp.where(k0 + k_off <= q_pos, score, -jnp.inf)
        bits = lax.bitcast_convert_type(score, jnp.int32)
        key_ref[pl.ds(k0, ATT_KC), :] = bits ^ ((bits >> 31) & jnp.int32(0x7FFFFFFF))

    def slice_chunk(c):
        for hh in range(blocks_per_chunk):
            blk = c * blocks_per_chunk + hh
            k0 = pl.multiple_of(blk * PLANE_KEYS, PLANE_KEYS)
            keys = key_ref[pl.ds(k0, PLANE_KEYS), :]
            planes = _bit_planes([keys[w * SUBLANES:(w + 1) * SUBLANES, :] for w in range(KEY_BITS)])
            r0 = pl.multiple_of(blk * SUBLANES, SUBLANES)
            plane_ref[0, pl.ds(r0, SUBLANES), :] = ~planes[0]
            for r in range(1, KEY_BITS):
                plane_ref[r, pl.ds(r0, SUBLANES), :] = planes[r]

    def score_and_slice(c, carry):
        slice_chunk(c - 1)
        score_chunk(c)
        return carry

    score_chunk(0)
    lax.fori_loop(1, n_chunks, score_and_slice, 0)
    slice_chunk(n_chunks - 1)

    def count_bits(words):
        ones = lax.population_count(words)
        part = jnp.sum(ones.reshape(words.shape[0] // SUBLANES, SUBLANES, tq), axis=0)
        return jnp.sum(part.astype(F32), axis=0, keepdims=True)

    def radix_select(rows):
        row_id = lax.broadcasted_iota(jnp.int32, (rows, tq), 0)
        live_ref[:rows] = jnp.where(row_id < n_blocks * SUBLANES, jnp.int32(-1), jnp.int32(0))

        def search_bit(r, carry):
            need, kth = carry
            total = count_bits(live_ref[:rows] & plane_ref[r, :rows])
            take = total >= need
            keep_clear = jnp.where(take, jnp.int32(0), jnp.int32(-1))
            live_ref[:rows] = live_ref[:rows] & (plane_ref[r, :rows] ^ keep_clear)
            need = jnp.where(take, need, need - total)
            kth = jnp.where(take, kth | jnp.left_shift(jnp.int32(1), KEY_BITS - 1 - r), kth)
            return need, kth

        need, kth = lax.fori_loop(0, KEY_BITS, search_bit,
                                  (jnp.full((1, tq), float(INDEX_TOPK), F32), jnp.zeros((1, tq), jnp.int32)))
        kth_ref[0:1] = kth ^ jnp.int32(INT_MIN)
        cnt_ref[0:1] = need
        cnt_ref[1:2] = count_bits(live_ref[:rows])

    plane_rows = live_ref.shape[0]
    quarter = plane_rows // 4
    for rows in range(quarter, plane_rows + 1, quarter):
        @pl.when((n_blocks * SUBLANES > rows - quarter) & (n_blocks * SUBLANES <= rows))
        def _():
            radix_select(rows)

    kth = kth_ref[0:1]
    need = cnt_ref[0:1]
    tied = cnt_ref[1:2]
    few_keys = kth <= KEY_NEG_INF
    kth = jnp.maximum(kth, KEY_NEG_INF + 1)
    surplus = jnp.max(jnp.where(few_keys, 0.0, tied - need))
    need = jnp.where(few_keys, float(2 ** 24), need)

    def bias_tile(sel):
        return jnp.where(sel, 0.0, MASK_BIAS).T

    @pl.when(surplus <= 0.0)
    def _():
        def write_chunk(c, carry):
            parts = []
            for hh in range(blocks_per_chunk):
                k0 = pl.multiple_of(c * ATT_KC + hh * PLANE_KEYS, PLANE_KEYS)
                parts.append(bias_tile(key_ref[pl.ds(k0, PLANE_KEYS), :] >= kth))
            bias_ref[c] = jnp.concatenate(parts, axis=1).astype(bias_ref.dtype)
            return carry

        lax.fori_loop(0, n_chunks, write_chunk, 0)

    @pl.when(surplus > 0.0)
    def _():
        row = lax.broadcasted_iota(jnp.int32, (PLANE_KEYS, PLANE_KEYS), 0)
        col = lax.broadcasted_iota(jnp.int32, (PLANE_KEYS, PLANE_KEYS), 1)
        prefix = jnp.where(col <= row, 1.0, 0.0).astype(BF16)

        def write_chunk(c, seen):
            parts = []
            for hh in range(blocks_per_chunk):
                k0 = pl.multiple_of(c * ATT_KC + hh * PLANE_KEYS, PLANE_KEYS)
                keys = key_ref[pl.ds(k0, PLANE_KEYS), :]
                tie = keys == kth
                rank = seen + jnp.dot(prefix, jnp.where(tie, 1.0, 0.0).astype(BF16),
                                      preferred_element_type=F32)
                parts.append(bias_tile((keys > kth) | (tie & (rank <= need))))
                seen = rank[PLANE_KEYS - 1:PLANE_KEYS, :]
            bias_ref[c] = jnp.concatenate(parts, axis=1).astype(bias_ref.dtype)
            return seen

        lax.fori_loop(0, n_chunks, write_chunk, jnp.zeros((1, tq), F32))

    def fill_chunk(c, carry):
        bias_ref[c] = jnp.full(bias_ref.shape[1:], MASK_BIAS, bias_ref.dtype)
        return carry

    lax.fori_loop(n_chunks, n_chunks_total, fill_chunk, 0)


def _indexer_mask(q_idx_t, idx_w_t, k2):
    s = q_idx_t.shape[1]
    tq = SEL_TQ
    assert tq % PLANE_KEYS == 0 and s % (KEY_BITS * SUBLANES) == 0
    return pl.pallas_call(
        _indexer_body,
        grid=(s // tq,),
        in_specs=[pl.BlockSpec((q_idx_t.shape[0], tq), lambda i: (0, i)),
                  pl.BlockSpec((IDX_HEADS, tq), lambda i: (0, i)),
                  pl.BlockSpec(k2.shape, lambda i: (0, 0, 0))],
        out_specs=pl.BlockSpec((s // ATT_KC, tq, ATT_KC), lambda i: (0, i, 0)),
        out_shape=jax.ShapeDtypeStruct((s // ATT_KC, s, ATT_KC), BF16),
        scratch_shapes=[pltpu.VMEM((s, tq), jnp.int32),
                        pltpu.VMEM((KEY_BITS, s // KEY_BITS, tq), jnp.int32),
                        pltpu.VMEM((s // KEY_BITS, tq), jnp.int32),
                        pltpu.VMEM((SUBLANES, tq), jnp.int32),
                        pltpu.VMEM((SUBLANES, tq), F32)],
        compiler_params=_params(("arbitrary",)),
        name="indexer_mask",
    )(q_idx_t, idx_w_t, k2)


def _attention_body(q_ref, k_ref, v_ref, bias_ref, o_ref, m_ref, acc_ref):
    tq = q_ref.shape[0]
    chunk_keys = bias_ref.shape[2]
    i = pl.program_id(0)
    n_chunks = (i * tq + tq - 1) // chunk_keys + 1

    m_ref[...] = jnp.full(m_ref.shape, M_INIT, F32)
    acc_ref[...] = jnp.zeros(acc_ref.shape, F32)

    def step(first_chunk, width):
        kc = width * chunk_keys
        sub = kc // LANES
        k0 = pl.multiple_of(first_chunk * chunk_keys, chunk_keys)
        bias = jnp.concatenate([bias_ref[first_chunk + c] for c in range(width)], axis=1)
        ones = jnp.ones((kc, HEAD_DIM), v_ref.dtype)
        for h in range(ATTN_HEADS):
            cols = slice(h * HEAD_DIM, (h + 1) * HEAD_DIM)
            s = lax.dot_general(q_ref[:, cols], k_ref[pl.ds(k0, kc), cols],
                                (((1,), (1,)), ((), ())), preferred_element_type=F32)
            sb = s.astype(BF16) + bias
            part = sb[:, :LANES]
            for c in range(1, sub):
                part = jnp.maximum(part, sb[:, c * LANES:(c + 1) * LANES])
            m_prev = m_ref[h]
            m_next = jnp.maximum(m_prev, jnp.max(part.astype(F32), axis=1, keepdims=True))
            p = jnp.exp2(sb - jnp.concatenate([m_next.astype(BF16)] * sub, axis=1))
            alpha = jnp.exp2(m_prev - m_next)
            m_ref[h] = m_next
            pv = jnp.dot(p, jnp.concatenate([v_ref[pl.ds(k0, kc), cols], ones], axis=1),
                         preferred_element_type=F32)
            acc_ref[h] = jnp.concatenate([alpha, alpha], axis=1) * acc_ref[h] + pv

    def wide_step(j, carry):
        step(j * ATT_STEP_CHUNKS, ATT_STEP_CHUNKS)
        return carry

    n_wide = n_chunks // ATT_STEP_CHUNKS
    lax.fori_loop(0, n_wide, wide_step, 0)
    for r in range(1, ATT_STEP_CHUNKS):
        @pl.when(n_chunks % ATT_STEP_CHUNKS >= r)
        def _():
            step(n_wide * ATT_STEP_CHUNKS + r - 1, 1)

    for h in range(ATTN_HEADS):
        acc = acc_ref[h]
        o_ref[:, h * HEAD_DIM:(h + 1) * HEAD_DIM] = (acc[:, :HEAD_DIM] / acc[:, HEAD_DIM:]).astype(o_ref.dtype)


def _attention(q, k, v, bias):
    s, width = q.shape
    tq = SEL_TQ
    n_chunks, _, kc = bias.shape
    resident = dict(pipeline_mode=pl.Buffered(1))
    return pl.pallas_call(
        _attention_body,
        grid=(s // tq,),
        in_specs=[pl.BlockSpec((tq, width), lambda i: (i, 0)),
                  pl.BlockSpec((s, width), lambda i: (0, 0), **resident),
                  pl.BlockSpec((s, width), lambda i: (0, 0), **resident),
                  pl.BlockSpec((n_chunks, tq, kc), lambda i: (0, i, 0))],
        out_specs=pl.BlockSpec((tq, width), lambda i: (i, 0)),
        out_shape=jax.ShapeDtypeStruct((s, width), BF16),
        scratch_shapes=[pltpu.VMEM((ATTN_HEADS, tq, LANES), F32),
                        pltpu.VMEM((ATTN_HEADS, tq, 2 * HEAD_DIM), F32)],
        compiler_params=_params(("arbitrary",)),
        name="masked_attention",
    )(q, k, v, bias)


def _ffn_down_body(a_ref, w_ref, x_ref, g_ref, gate_ref, o_ref):
    f = jnp.dot(a_ref[...], w_ref[...], preferred_element_type=F32)
    o_ref[...] = x_ref[...] + gate_ref[...] * _rms(f, g_ref[...])


def _ffn_down(a, w, x1, g_post, gate):
    m, kdim = a.shape
    d = w.shape[1]
    tm = 256
    row = pl.BlockSpec((1, d), lambda i: (0, 0))
    return pl.pallas_call(
        _ffn_down_body,
        grid=(m // tm,),
        in_specs=[pl.BlockSpec((tm, kdim), lambda i: (i, 0)),
                  pl.BlockSpec((kdim, d), lambda i: (0, 0), pipeline_mode=pl.Buffered(1)),
                  pl.BlockSpec((tm, d), lambda i: (i, 0)), row, row],
        out_specs=pl.BlockSpec((tm, d), lambda i: (i, 0)),
        out_shape=jax.ShapeDtypeStruct((m, d), F32),
        compiler_params=_params(("arbitrary",)),
        name="ffn_down_residual",
    )(a, w, x1, g_post, gate)


def _rope_angles(seq, dim):
    inv = 1.0 / (ROPE_THETA ** (jnp.arange(0, dim, 2, dtype=F32) / dim))
    ang = jnp.arange(seq, dtype=F32)[:, None] * inv[None, :]
    return jnp.cos(ang), jnp.sin(ang)


def _attn_rope_tables(seq):
    cos, sin = _rope_angles(seq, HEAD_DIM)
    return jnp.concatenate([cos, cos], axis=1), jnp.concatenate([-sin, sin], axis=1)


def _index_rope_tables(seq):
    cos, sin = _rope_angles(seq, IDX_ROPE_DIM)
    half = IDX_ROPE_DIM // 2
    rest = IDX_HEAD_DIM - IDX_ROPE_DIM
    zeros = lambda n: jnp.zeros((seq, n), F32)
    c = jnp.concatenate([cos, cos, jnp.ones((seq, rest), F32)], axis=1)
    a = jnp.concatenate([-sin, zeros(IDX_HEAD_DIM - half)], axis=1)
    b = jnp.concatenate([zeros(half), sin, zeros(rest)], axis=1)
    rep = LANES // IDX_HEAD_DIM
    return tuple(jnp.tile(t, (1, rep)) for t in (c, a, b)) + (cos.T, sin.T)


def _layer(x, mod, g_pre_mix, g_post_mix, w_in_t, gmlp_ln_g, gmlp_ln_b, gmlp_w_s, gmlp_b_s,
           q_lat_norm_g, w_q_up, w_qidx_up, kidx_ln_g, kidx_ln_b, w_proj_a, w_proj_b, w_out,
           g_pre_ffn, g_post_ffn, w_ffn_gate, w_ffn_up, w_ffn_down, tables):
    s, d = x.shape
    cc, ss, ic, ia, ib, icos_t, isin_t = tables
    row = lambda a: a.reshape(1, -1)
    shift_m, scale_m, gate_m, shift_f, scale_f, gate_f = (mod[:, n * d:(n + 1) * d] for n in range(N_MOD))

    h = _prenorm(x, row(g_pre_mix), scale_m, shift_m)

    tm = 1024
    in_proj = functools.partial(_simple_mm, h, w_in_t, kn=False)

    u = in_proj(_ep_gelu, [], BF16, n=GMLP_WIDTH, tm=tm, tn=GMLP_WIDTH, sub_n=EPILOGUE_SUB_N,
                name="in_proj_u")
    vn = in_proj(_ep_gelu_layernorm,
                 [(row(gmlp_ln_g), (1, GMLP_WIDTH), _colblk), (row(gmlp_ln_b), (1, GMLP_WIDTH), _colblk)],
                 BF16, n=GMLP_WIDTH, first=GMLP_WIDTH, tm=512, tn=GMLP_WIDTH, name="in_proj_v_ln")
    gmlp_bias = jnp.repeat(gmlp_b_s.T, LANES, axis=1)
    y_a = _gmlp(u, vn, gmlp_w_s, gmlp_bias)

    q_lat = in_proj(_ep_rmsnorm, [(row(q_lat_norm_g), (1, Q_LORA_RANK), _colblk)],
                    BF16, n=Q_LORA_RANK, first=COL_QLAT, tm=tm, tn=Q_LORA_RANK, name="in_proj_qlat")
    q_scale = HEAD_DIM ** -0.5 * math.log2(math.e)
    rope_extras = [(cc, (tm, HEAD_DIM), _rowblk), (ss, (tm, HEAD_DIM), _rowblk)]
    q = _simple_mm(q_lat, w_q_up, functools.partial(_ep_rope, scale=q_scale), rope_extras,
                   BF16, n=ATTN_WIDTH, tm=tm, tn=ATTN_WIDTH, sub_n=EPILOGUE_SUB_N, name="q_up_rope")
    k = in_proj(functools.partial(_ep_rope, scale=1.0), rope_extras, BF16,
                n=ATTN_WIDTH, first=COL_K, tm=tm, tn=ATTN_WIDTH, sub_n=EPILOGUE_SUB_N,
                name="in_proj_k_rope")
    v = in_proj(_ep_cast, [], BF16, n=ATTN_WIDTH, first=COL_V, tm=tm, tn=ATTN_WIDTH,
                sub_n=EPILOGUE_SUB_N, name="in_proj_v")
    half = IDX_ROPE_DIM // 2
    q_idx_t = _simple_mm(q_lat, w_qidx_up.T, _ep_partial_rope_t,
                         [(icos_t, (half, tm), _colblk_t), (isin_t, (half, tm), _colblk_t)],
                         BF16, n=IDX_HEADS * IDX_HEAD_DIM, tm=tm, tn=512, name="qidx_up_rope",
                         kn=False, out_t=True)
    pad = lambda a: jnp.pad(a, (0, LANES - a.shape[0])).reshape(1, LANES)
    idx_w_scale = (IDX_HEADS ** -0.5) * (IDX_HEAD_DIM ** -0.5)
    kx = in_proj(functools.partial(_ep_index_keys, w_scale=idx_w_scale),
                 [(pad(kidx_ln_g), (1, LANES), _colblk), (pad(kidx_ln_b), (1, LANES), _colblk)]
                 + [(t, (tm, LANES), _rowblk) for t in (ic, ia, ib)],
                 F32, n=LANES, first=COL_KIDX, tm=tm, tn=LANES, name="in_proj_index_keys")
    k_idx = kx[:, :IDX_HEAD_DIM].astype(BF16)
    zeros = jnp.zeros_like(k_idx)
    k2 = jnp.stack([jnp.concatenate([k_idx, zeros], axis=1), jnp.concatenate([zeros, k_idx], axis=1)])
    idx_w_t = kx[:, IDX_HEAD_DIM:IDX_HEAD_DIM + IDX_HEADS].T
    bias = _indexer_mask(q_idx_t, idx_w_t, k2)
    y_b = _attention(q, k, v, bias)

    gates = in_proj(_ep_sigmoid, [], BF16, n=2 * d, first=COL_GATE, tm=tm, tn=1024,
                    sub_n=EPILOGUE_SUB_N, name="in_proj_gates")
    n_gate_blocks = d // 1024
    merged = _matmul([y_a, y_b], [(w_proj_a, 0, True), (w_proj_b, 0, True)], [0, 1],
                     [(gates, (tm, 1024), _tile),
                      (gates, (tm, 1024), lambda j, i: (i, j + n_gate_blocks))],
                     [((s, d), BF16, (tm, 1024), _tile)], _ep_merge, n=d, tm=tm, tn=1024,
                     sub_n=EPILOGUE_SUB_N, name="branch_merge")[0]
    tm_full = 256
    vec = lambda a: (a, (1, d), _colblk)
    x1, h2 = _matmul([merged], [(w_out, 0, True)], [0],
                     [(x, (tm_full, d), _rowblk), vec(row(g_post_mix)), vec(gate_m),
                      vec(row(g_pre_ffn)), vec(scale_f), vec(shift_f)],
                     [((s, d), F32, (tm_full, d), _tile), ((s, d), BF16, (tm_full, d), _tile)],
                     _ep_mix_residual, n=d, tm=tm_full, tn=d, name="out_proj_residual")

    ffn_hidden = w_ffn_gate.shape[1]
    tn_ffn = 512
    n_row_blocks = s // tm
    down_rows = ffn_hidden // ((ffn_hidden // tn_ffn) * n_row_blocks)
    down_blk = ((down_rows, d), lambda j, i: (j * n_row_blocks + i, 0))
    act, w_down = _matmul([h2], [(w_ffn_gate, 0, True), (w_ffn_up, 0, True)], [0, 0],
                          [(w_ffn_down,) + down_blk],
                          [((s, ffn_hidden), BF16, (tm, tn_ffn), _tile), (w_ffn_down.shape, BF16) + down_blk],
                          _ep_swiglu, n=ffn_hidden, tm=tm, tn=tn_ffn, name="ffn_up")
    return _ffn_down(act, w_down, x1, row(g_post_ffn), gate_f)


def kernel(x, c, w_mod, b_mod, g_pre_mix, g_post_mix, w_in, gmlp_ln_g, gmlp_ln_b, gmlp_w_s, gmlp_b_s, q_lat_norm_g, w_q_up, w_qidx_up, kidx_ln_g, kidx_ln_b, w_proj_a, w_proj_b, w_out, g_pre_ffn, g_post_ffn, w_ffn_gate, w_ffn_up, w_ffn_down):
    batch, seq, d = x.shape
    assert batch == 1 and d == D_MODEL
    tables = _attn_rope_tables(seq) + _index_rope_tables(seq)
    y = x[0]
    for l in range(w_mod.shape[0]):
        mod = _modulation(c, w_mod[l], b_mod[l])
        y = _layer(y, mod, g_pre_mix[l], g_post_mix[l], w_in[l].T, gmlp_ln_g[l], gmlp_ln_b[l],
                   gmlp_w_s[l], gmlp_b_s[l], q_lat_norm_g[l], w_q_up[l], w_qidx_up[l],
                   kidx_ln_g[l], kidx_ln_b[l], w_proj_a[l], w_proj_b[l], w_out[l],
                   g_pre_ffn[l], g_post_ffn[l], w_ffn_gate[l], w_ffn_up[l], w_ffn_down[l], tables)
    return y[None]
```

```python
import functools
import math

import jax
import jax.numpy as jnp
from jax import lax
from jax.experimental import pallas as pl
from jax.experimental.pallas import tpu as pltpu

F32 = jnp.float32
BF16 = jnp.bfloat16

D_MODEL = 2048
GMLP_WIDTH = 1024
GMLP_GROUPS = 8
CHUNK = 128
ATTN_HEADS = 8
HEAD_DIM = 128
ATTN_WIDTH = ATTN_HEADS * HEAD_DIM
Q_LORA_RANK = 512
IDX_HEADS = 16
IDX_HEAD_DIM = 64
IDX_ROPE_DIM = 32
INDEX_TOPK = 256
ROPE_THETA = 10000.0
N_MOD = 6
NORM_EPS = 1e-6

COL_QLAT = 2 * GMLP_WIDTH
COL_K = COL_QLAT + Q_LORA_RANK
COL_V = COL_K + ATTN_WIDTH
COL_KIDX = COL_V + ATTN_WIDTH
COL_IDXW = COL_KIDX + IDX_HEAD_DIM
COL_GATE = COL_IDXW + IDX_HEADS

LANES = 128
SUBLANES = 8
VMEM_LIMIT = 56 * 1024 * 1024

MASK_BIAS = -(2.0 ** 100)
M_INIT = -(2.0 ** 99)
INT_MIN = -2 ** 31
KEY_NEG_INF = -2139095041

SEL_TQ = 256
KEY_BITS = 32
PLANE_KEYS = KEY_BITS * SUBLANES
ATT_KC = 2 * PLANE_KEYS
ATT_STEP_CHUNKS = 2
EPILOGUE_SUB_N = 512
EPILOGUE_SUB_M = 256


def _params(sem=None):
    return pltpu.CompilerParams(dimension_semantics=sem, vmem_limit_bytes=VMEM_LIMIT)


def _mod_body(cb_ref, w_ref, b_ref, o_ref):
    cb = cb_ref[...]
    tn = o_ref.shape[1]
    parts = [jnp.sum(w_ref[:, p * LANES:(p + 1) * LANES] * cb, axis=0, keepdims=True)
             for p in range(tn // LANES)]
    o_ref[...] = jnp.concatenate(parts, axis=1) + b_ref[...]


def _modulation(c, w_mod, b_mod):
    k, n = w_mod.shape
    tn = 1024
    cb = jnp.broadcast_to(c.reshape(k, 1), (k, LANES))
    return pl.pallas_call(
        _mod_body,
        grid=(n // tn,),
        in_specs=[pl.BlockSpec((k, LANES), lambda j: (0, 0)),
                  pl.BlockSpec((k, tn), lambda j: (0, j)),
                  pl.BlockSpec((1, tn), lambda j: (0, j))],
        out_specs=pl.BlockSpec((1, tn), lambda j: (0, j)),
        out_shape=jax.ShapeDtypeStruct((1, n), F32),
        compiler_params=_params(("arbitrary",)),
        name="modulation",
    )(cb, w_mod, b_mod.reshape(1, n))


def _rms(x, g):
    return x * lax.rsqrt(jnp.mean(x * x, axis=-1, keepdims=True) + NORM_EPS) * g


def _prenorm_body(x_ref, g_ref, scale_ref, shift_ref, o_ref):
    h = _rms(x_ref[...], g_ref[...]) * (1.0 + scale_ref[...]) + shift_ref[...]
    o_ref[...] = h.astype(o_ref.dtype)


def _prenorm(x, g, scale, shift):
    m, d = x.shape
    tm = 512
    row = pl.BlockSpec((1, d), lambda i: (0, 0))
    return pl.pallas_call(
        _prenorm_body,
        grid=(m // tm,),
        in_specs=[pl.BlockSpec((tm, d), lambda i: (i, 0)), row, row, row],
        out_specs=pl.BlockSpec((tm, d), lambda i: (i, 0)),
        out_shape=jax.ShapeDtypeStruct((m, d), BF16),
        compiler_params=_params(("arbitrary",)),
        name="prenorm",
    )(x, g, scale, shift)


def _mm_body(*refs, n_x, pairs, w_kn, out_t, sub_m, sub_n, n_extra, n_out, epilogue):
    n_w = len(pairs)
    x_refs = refs[:n_x]
    w_refs = refs[n_x:n_x + n_w]
    e_refs = refs[n_x + n_w:n_x + n_w + n_extra]
    o_refs = refs[n_x + n_w + n_extra:n_x + n_w + n_extra + n_out]
    wb_refs = refs[n_x + n_w + n_extra + n_out:]

    @pl.when(pl.program_id(1) == 0)
    def _():
        for w_ref, wb_ref, kn in zip(w_refs, wb_refs, w_kn):
            w = w_ref[...]
            if not kn and not out_t:
                w = w.T
            wb_ref[...] = w.astype(wb_ref.dtype)

    if out_t:
        accs = [lax.dot_general(wb_ref[...], x_refs[xi][...], (((1,), (1,)), ((), ())),
                                preferred_element_type=F32)
                for xi, wb_ref in zip(pairs, wb_refs)]
        epilogue(accs, e_refs, o_refs, slice(0, wb_refs[0].shape[0]), slice(0, x_refs[0].shape[0]))
        return
    tm, tn = x_refs[0].shape[0], wb_refs[0].shape[1]
    for r0 in range(0, tm, sub_m):
        rs = slice(r0, r0 + sub_m)
        for c0 in range(0, tn, sub_n):
            cs = slice(c0, c0 + sub_n)
            accs = [jnp.dot(x_refs[xi][rs, :], wb_ref[:, cs], preferred_element_type=F32)
                    for xi, wb_ref in zip(pairs, wb_refs)]
            epilogue(accs, e_refs, o_refs, cs, rs)


def _matmul(xs, ws, pairs, extras, outs, epilogue, *, n, tm, tn, name, out_t=False, sub_n=None,
            sub_m=None):
    m = xs[0].shape[0]
    n_col = n // tn
    w_mode = dict(pipeline_mode=pl.Buffered(1)) if n_col == 1 else {}
    in_specs = [pl.BlockSpec((tm, x.shape[1]), lambda j, i: (i, 0)) for x in xs]
    scratch = []
    for w, first, kn in ws:
        assert not (kn and out_t)
        if kn:
            kdim = w.shape[0]
            assert first % tn == 0
            in_specs.append(pl.BlockSpec((kdim, tn), functools.partial(_w_cols, first=first // tn), **w_mode))
        else:
            kdim = w.shape[1]
            in_specs.append(pl.BlockSpec((pl.Element(tn), pl.Element(kdim)),
                                         functools.partial(_w_rows, first=first, tn=tn), **w_mode))
        scratch.append(pltpu.VMEM((tn, kdim) if out_t else (kdim, tn), BF16))
    in_specs += [pl.BlockSpec(bs, im) for _, bs, im in extras]
    body = functools.partial(_mm_body, n_x=len(xs), pairs=tuple(pairs), w_kn=tuple(kn for _, _, kn in ws),
                             out_t=out_t, sub_m=sub_m or tm, sub_n=sub_n or tn, n_extra=len(extras),
                             n_out=len(outs), epilogue=epilogue)
    return pl.pallas_call(
        body,
        grid=(n_col, m // tm),
        in_specs=in_specs,
        out_specs=[pl.BlockSpec(bs, im) for _, _, bs, im in outs],
        out_shape=[jax.ShapeDtypeStruct(s, d) for s, d, _, _ in outs],
        scratch_shapes=scratch,
        compiler_params=_params(("arbitrary", "arbitrary")),
        name=name,
    )(*xs, *[w for w, _, _ in ws], *[a for a, _, _ in extras])


def _w_cols(j, i, *, first):
    return (0, first + j)


def _w_rows(j, i, *, first, tn):
    assert first % SUBLANES == 0 and tn % SUBLANES == 0
    return (pl.multiple_of(first + j * tn, SUBLANES), 0)


def _tile(j, i):
    return (i, j)


def _tile_t(j, i):
    return (j, i)


def _rowblk(j, i):
    return (i, 0)


def _colblk(j, i):
    return (0, j)


def _colblk_t(j, i):
    return (0, i)


def _ep_gelu(accs, e_refs, o_refs, cols, rows):
    o_refs[0][rows, cols] = jax.nn.gelu(accs[0]).astype(o_refs[0].dtype)


def _ep_gelu_layernorm(accs, e_refs, o_refs, cols, rows):
    z = jax.nn.gelu(accs[0])
    mu = jnp.mean(z, axis=-1, keepdims=True)
    zc = z - mu
    var = jnp.mean(zc * zc, axis=-1, keepdims=True)
    y = zc * lax.rsqrt(var + NORM_EPS) * e_refs[0][...] + e_refs[1][...]
    o_refs[0][rows, :] = y.astype(o_refs[0].dtype)


def _ep_rmsnorm(accs, e_refs, o_refs, cols, rows):
    o_refs[0][rows, :] = _rms(accs[0], e_refs[0][...]).astype(o_refs[0].dtype)


def _ep_cast(accs, e_refs, o_refs, cols, rows):
    o_refs[0][rows, cols] = accs[0].astype(o_refs[0].dtype)


def _ep_sigmoid(accs, e_refs, o_refs, cols, rows):
    o_refs[0][rows, cols] = (0.5 * jnp.tanh(0.5 * accs[0]) + 0.5).astype(o_refs[0].dtype)


def _ep_rope(accs, e_refs, o_refs, cols, rows, *, scale):
    cc = e_refs[0][rows, :]
    ss = e_refs[1][rows, :]
    acc = accs[0]
    for h in range(acc.shape[1] // HEAD_DIM):
        xh = acc[:, h * HEAD_DIM:(h + 1) * HEAD_DIM]
        r = xh * cc + pltpu.roll(xh, HEAD_DIM // 2, 1) * ss
        if scale != 1.0:
            r = r * scale
        c0 = cols.start + h * HEAD_DIM
        o_refs[0][rows, c0:c0 + HEAD_DIM] = r.astype(o_refs[0].dtype)


def _ep_partial_rope_t(accs, e_refs, o_refs, cols, rows):
    c = e_refs[0][...]
    s = e_refs[1][...]
    acc = accs[0]
    half = IDX_ROPE_DIM // 2
    for h in range(acc.shape[0] // IDX_HEAD_DIM):
        r0 = h * IDX_HEAD_DIM
        x1 = acc[r0:r0 + half, :]
        x2 = acc[r0 + half:r0 + 2 * half, :]
        o_refs[0][r0:r0 + half, :] = (x1 * c - x2 * s).astype(o_refs[0].dtype)
        o_refs[0][r0 + half:r0 + 2 * half, :] = (x2 * c + x1 * s).astype(o_refs[0].dtype)
        o_refs[0][r0 + 2 * half:r0 + IDX_HEAD_DIM, :] = (
            acc[r0 + 2 * half:r0 + IDX_HEAD_DIM, :].astype(o_refs[0].dtype))


def _partial_rope(x, c, a, b):
    half = IDX_ROPE_DIM // 2
    return x * c + pltpu.roll(x, LANES - half, 1) * a + pltpu.roll(x, half, 1) * b


def _ep_index_keys(accs, e_refs, o_refs, cols, rows, *, w_scale):
    g, bb = e_refs[0][...], e_refs[1][...]
    c, a, b = e_refs[2][rows, :], e_refs[3][rows, :], e_refs[4][rows, :]
    acc = accs[0]
    lane = lax.broadcasted_iota(jnp.int32, acc.shape, 1)
    is_key = lane < IDX_HEAD_DIM
    mu = jnp.sum(jnp.where(is_key, acc, 0.0), axis=-1, keepdims=True) / IDX_HEAD_DIM
    xc = jnp.where(is_key, acc - mu, 0.0)
    var = jnp.sum(xc * xc, axis=-1, keepdims=True) / IDX_HEAD_DIM
    y = xc * lax.rsqrt(var + NORM_EPS) * g + bb
    y = _partial_rope(y, c, a, b)
    o_refs[0][rows, :] = jnp.where(is_key, y, acc * w_scale)


def _ep_merge(accs, e_refs, o_refs, cols, rows):
    o = e_refs[0][rows, cols].astype(F32) * accs[0] + e_refs[1][rows, cols].astype(F32) * accs[1]
    o_refs[0][rows, cols] = o.astype(o_refs[0].dtype)


def _ep_mix_residual(accs, e_refs, o_refs, cols, rows):
    x_ref, g_post, gate, g_pre, scale, shift = e_refs
    x1 = x_ref[rows, :] + gate[...] * _rms(accs[0], g_post[...])
    o_refs[0][rows, :] = x1
    h2 = _rms(x1, g_pre[...]) * (1.0 + scale[...]) + shift[...]
    o_refs[1][rows, :] = h2.astype(o_refs[1].dtype)


def _ep_swiglu(accs, e_refs, o_refs, cols, rows):
    o_refs[0][rows, cols] = (jax.nn.silu(accs[0]) * accs[1]).astype(o_refs[0].dtype)

    if cols.start == 0 and rows.start == 0:
        o_refs[1][...] = e_refs[0][...].astype(o_refs[1].dtype)


def _simple_mm(x, w, epilogue, extras, out_dtype, *, n, tm, tn, name, first=0, kn=True, out_t=False,
               sub_n=None, sub_m=None):
    m = x.shape[0]
    out = ((n, m), out_dtype, (tn, tm), _tile_t) if out_t else ((m, n), out_dtype, (tm, tn), _tile)
    return _matmul([x], [(w, first, kn)], [0], extras, [out], epilogue, n=n, tm=tm, tn=tn,
                   name=name, out_t=out_t, sub_n=sub_n, sub_m=sub_m)[0]


def _gmlp_body(u_ref, v_ref, w_ref, b_ref, o_ref):
    t = lax.broadcasted_iota(jnp.int32, (CHUNK, CHUNK), 0)
    s = lax.broadcasted_iota(jnp.int32, (CHUNK, CHUNK), 1)
    causal = s <= t
    for g in range(GMLP_GROUPS):
        w = jnp.where(causal, w_ref[g], 0.0).astype(BF16)
        cols = slice(g * LANES, (g + 1) * LANES)
        bias = b_ref[:, cols]
        for c in range(u_ref.shape[0] // CHUNK):
            rows = slice(c * CHUNK, (c + 1) * CHUNK)
            sv = jnp.dot(w, v_ref[rows, cols], preferred_element_type=F32) + bias
            o_ref[rows, cols] = (u_ref[rows, cols].astype(F32) * sv).astype(o_ref.dtype)


def _gmlp(u, vn, w_s, bias):
    m, n = u.shape
    tm = 512
    blk = pl.BlockSpec((tm, n), lambda i: (i, 0))
    return pl.pallas_call(
        _gmlp_body,
        grid=(m // tm,),
        in_specs=[blk, blk,
                  pl.BlockSpec(w_s.shape, lambda i: (0, 0, 0)),
                  pl.BlockSpec(bias.shape, lambda i: (0, 0))],
        out_specs=blk,
        out_shape=jax.ShapeDtypeStruct((m, n), BF16),
        compiler_params=_params(("arbitrary",)),
        name="gmlp_gating",
    )(u, vn, w_s, bias)


def _bit_planes(words):
    a = list(words)
    j, mask = KEY_BITS // 2, 0x0000FFFF
    while j:
        for k in range(KEY_BITS):
            if k & j:
                continue
            t = (a[k] ^ (a[k + j] >> j)) & jnp.int32(mask)
            a[k] = a[k] ^ t
            a[k + j] = a[k + j] ^ (t << j)
        j //= 2
        mask = (mask ^ (mask << j)) & 0xFFFFFFFF
    return a


def _score_key(score):
    bits = lax.bitcast_convert_type(score, jnp.int32)
    return bits ^ ((bits >> 31) & jnp.int32(0x7FFFFFFF))


def _key_score(key):
    return lax.bitcast_convert_type(key ^ ((key >> 31) & jnp.int32(0x7FFFFFFF)), F32)


def _indexer_body(qt_ref, wt_ref, k2_ref, bias_ref, score_ref, plane_ref, live_ref, kth_ref):
    tq = qt_ref.shape[1]
    n_chunks_total = bias_ref.shape[0]
    i = pl.program_id(0)
    n_chunks = ((i + 1) * tq + ATT_KC - 1) // ATT_KC
    blocks_per_chunk = ATT_KC // PLANE_KEYS
    n_blocks = n_chunks * blocks_per_chunk
    heads_per_group = LANES // IDX_HEAD_DIM

    @pl.when(i == 0)
    def _():
        plane_ref[...] = jnp.zeros(plane_ref.shape, jnp.int32)

    q_pos = i * tq + lax.broadcasted_iota(jnp.int32, (ATT_KC, tq), 1)
    k_off = lax.broadcasted_iota(jnp.int32, (ATT_KC, tq), 0)

    def score_chunk(c):
        k0 = pl.multiple_of(c * ATT_KC, ATT_KC)
        score = jnp.zeros((ATT_KC, tq), F32)
        for h in range(IDX_HEADS):
            grp, r = divmod(h, heads_per_group)
            logit = jnp.dot(k2_ref[r, pl.ds(k0, ATT_KC), :], qt_ref[grp * LANES:(grp + 1) * LANES, :],
                            preferred_element_type=F32)
            score = score + jnp.maximum(logit, 0.0) * wt_ref[h:h + 1, :]
        score_ref[pl.ds(k0, ATT_KC), :] = jnp.where(k0 + k_off <= q_pos, score, -jnp.inf)

    def slice_chunk(c):
        for hh in range(blocks_per_chunk):
            blk = c * blocks_per_chunk + hh
            k0 = pl.multiple_of(blk * PLANE_KEYS, PLANE_KEYS)
            keys = _score_key(score_ref[pl.ds(k0, PLANE_KEYS), :])
            planes = _bit_planes([keys[w * SUBLANES:(w + 1) * SUBLANES, :] for w in range(KEY_BITS)])
            r0 = pl.multiple_of(blk * SUBLANES, SUBLANES)
            plane_ref[0, pl.ds(r0, SUBLANES), :] = ~planes[0]
            for r in range(1, KEY_BITS):
                plane_ref[r, pl.ds(r0, SUBLANES), :] = planes[r]

    def score_and_slice(c, carry):
        slice_chunk(c - 1)
        score_chunk(c)
        return carry

    score_chunk(0)
    lax.fori_loop(1, n_chunks, score_and_slice, 0)
    slice_chunk(n_chunks - 1)

    def count_bits(words):
        ones = lax.population_count(words)
        part = jnp.sum(ones.reshape(words.shape[0] // SUBLANES, SUBLANES, tq), axis=0)
        return jnp.sum(part.astype(F32), axis=0, keepdims=True)

    def radix_select(rows):
        row_id = lax.broadcasted_iota(jnp.int32, (rows, tq), 0)
        live_ref[:rows] = jnp.where(row_id < n_blocks * SUBLANES, jnp.int32(-1), jnp.int32(0))

        def search_bit(r, carry):
            need, kth = carry
            total = count_bits(live_ref[:rows] & plane_ref[r, :rows])
            take = total >= need
            keep_clear = jnp.where(take, jnp.int32(0), jnp.int32(-1))
            live_ref[:rows] = live_ref[:rows] & (plane_ref[r, :rows] ^ keep_clear)
            need = jnp.where(take, need, need - total)
            kth = jnp.where(take, kth | jnp.left_shift(jnp.int32(1), KEY_BITS - 1 - r), kth)
            return need, kth

        _, kth = lax.fori_loop(0, KEY_BITS, search_bit,
                               (jnp.full((1, tq), float(INDEX_TOPK), F32), jnp.zeros((1, tq), jnp.int32)))
        kth_ref[0:1] = kth ^ jnp.int32(INT_MIN)

    plane_rows = live_ref.shape[0]
    quarter = plane_rows // 4
    for rows in range(quarter, plane_rows + 1, quarter):
        @pl.when((n_blocks * SUBLANES > rows - quarter) & (n_blocks * SUBLANES <= rows))
        def _():
            radix_select(rows)

    def count_scores(pred):
        group = 4 * SUBLANES

        def add_chunk(c, acc):
            k0 = pl.multiple_of(c * ATT_KC, ATT_KC)
            hit = jnp.where(pred(score_ref[pl.ds(k0, ATT_KC), :]), 1.0, 0.0)
            return acc + jnp.sum(hit.reshape(ATT_KC // group, group, tq), axis=0)

        acc = lax.fori_loop(0, n_chunks, add_chunk, jnp.zeros((group, tq), F32))
        return jnp.sum(acc, axis=0, keepdims=True)

    few_keys = i * tq + lax.broadcasted_iota(jnp.int32, (1, tq), 1) + 1 < INDEX_TOPK
    lowest = float(jnp.finfo(F32).min)
    thr = jnp.where(few_keys, lowest, _key_score(jnp.maximum(kth_ref[0:1], KEY_NEG_INF + 1)))
    kept = count_scores(lambda sc: sc >= thr)
    unsettled = jnp.max(jnp.where(few_keys | (kept == float(INDEX_TOPK)), 0.0, 1.0))

    def bias_tile(sel):
        return jnp.where(sel, 0.0, MASK_BIAS).T

    @pl.when(unsettled == 0.0)
    def _():
        def write_chunk(c, carry):
            parts = []
            for hh in range(blocks_per_chunk):
                k0 = pl.multiple_of(c * ATT_KC + hh * PLANE_KEYS, PLANE_KEYS)
                parts.append(bias_tile(score_ref[pl.ds(k0, PLANE_KEYS), :] >= thr))
            bias_ref[c] = jnp.concatenate(parts, axis=1).astype(bias_ref.dtype)
            return carry

        lax.fori_loop(0, n_chunks, write_chunk, 0)

    @pl.when(unsettled > 0.0)
    def _():
        def search_bit(b, prefix):
            cand = prefix ^ jnp.left_shift(jnp.int32(1), KEY_BITS - 1 - b)
            cand_score = _key_score(jnp.maximum(cand, KEY_NEG_INF))
            admitted = count_scores(lambda sc: sc >= cand_score)
            return jnp.where(admitted >= float(INDEX_TOPK), cand, prefix)

        kth = lax.fori_loop(0, KEY_BITS, search_bit, jnp.full((1, tq), INT_MIN, jnp.int32))
        kth_score = jnp.where(few_keys, lowest, _key_score(jnp.maximum(kth, KEY_NEG_INF + 1)))
        above = count_scores(lambda sc: sc > kth_score)
        need = jnp.where(few_keys, float(2 ** 24), float(INDEX_TOPK) - above)

        row = lax.broadcasted_iota(jnp.int32, (PLANE_KEYS, PLANE_KEYS), 0)
        col = lax.broadcasted_iota(jnp.int32, (PLANE_KEYS, PLANE_KEYS), 1)
        prefix_sum = jnp.where(col <= row, 1.0, 0.0).astype(BF16)

        def write_chunk(c, seen):
            parts = []
            for hh in range(blocks_per_chunk):
                k0 = pl.multiple_of(c * ATT_KC + hh * PLANE_KEYS, PLANE_KEYS)
                sc = score_ref[pl.ds(k0, PLANE_KEYS), :]
                tie = sc == kth_score
                rank = seen + jnp.dot(prefix_sum, jnp.where(tie, 1.0, 0.0).astype(BF16),
                                      preferred_element_type=F32)
                parts.append(bias_tile((sc > kth_score) | (tie & (rank <= need))))
                seen = rank[PLANE_KEYS - 1:PLANE_KEYS, :]
            bias_ref[c] = jnp.concatenate(parts, axis=1).astype(bias_ref.dtype)
            return seen

        lax.fori_loop(0, n_chunks, write_chunk, jnp.zeros((1, tq), F32))

    def fill_chunk(c, carry):
        bias_ref[c] = jnp.full(bias_ref.shape[1:], MASK_BIAS, bias_ref.dtype)
        return carry

    lax.fori_loop(n_chunks, n_chunks_total, fill_chunk, 0)


def _indexer_mask(q_idx_t, idx_w_t, k2):
    s = q_idx_t.shape[1]
    tq = SEL_TQ
    assert tq % PLANE_KEYS == 0 and s % (KEY_BITS * SUBLANES) == 0
    return pl.pallas_call(
        _indexer_body,
        grid=(s // tq,),
        in_specs=[pl.BlockSpec((q_idx_t.shape[0], tq), lambda i: (0, i)),
                  pl.BlockSpec((IDX_HEADS, tq), lambda i: (0, i)),
                  pl.BlockSpec(k2.shape, lambda i: (0, 0, 0))],
        out_specs=pl.BlockSpec((s // ATT_KC, tq, ATT_KC), lambda i: (0, i, 0)),
        out_shape=jax.ShapeDtypeStruct((s // ATT_KC, s, ATT_KC), BF16),
        scratch_shapes=[pltpu.VMEM((s, tq), F32),
                        pltpu.VMEM((KEY_BITS, s // KEY_BITS, tq), jnp.int32),
                        pltpu.VMEM((s // KEY_BITS, tq), jnp.int32),
                        pltpu.VMEM((SUBLANES, tq), jnp.int32)],
        compiler_params=_params(("arbitrary",)),
        name="indexer_mask",
    )(q_idx_t, idx_w_t, k2)


def _attention_body(q_ref, k_ref, v_ref, bias_ref, o_ref, m_ref, acc_ref):
    tq = q_ref.shape[0]
    chunk_keys = bias_ref.shape[2]
    i = pl.program_id(0)
    n_chunks = (i * tq + tq - 1) // chunk_keys + 1

    m_ref[...] = jnp.full(m_ref.shape, M_INIT, F32)
    acc_ref[...] = jnp.zeros(acc_ref.shape, F32)

    def step(first_chunk, width):
        kc = width * chunk_keys
        sub = kc // LANES
        k0 = pl.multiple_of(first_chunk * chunk_keys, chunk_keys)
        bias = jnp.concatenate([bias_ref[first_chunk + c] for c in range(width)], axis=1)
        ones = jnp.ones((kc, HEAD_DIM), v_ref.dtype)
        for h in range(ATTN_HEADS):
            cols = slice(h * HEAD_DIM, (h + 1) * HEAD_DIM)
            s = lax.dot_general(q_ref[:, cols], k_ref[pl.ds(k0, kc), cols],
                                (((1,), (1,)), ((), ())), preferred_element_type=F32)
            sb = s.astype(BF16) + bias
            part = sb[:, :LANES]
            for c in range(1, sub):
                part = jnp.maximum(part, sb[:, c * LANES:(c + 1) * LANES])
            m_prev = m_ref[h]
            m_next = jnp.maximum(m_prev, jnp.max(part.astype(F32), axis=1, keepdims=True))
            p = jnp.exp2(sb - jnp.concatenate([m_next.astype(BF16)] * sub, axis=1))
            alpha = jnp.exp2(m_prev - m_next)
            m_ref[h] = m_next
            pv = jnp.dot(p, jnp.concatenate([v_ref[pl.ds(k0, kc), cols], ones], axis=1),
                         preferred_element_type=F32)
            acc_ref[h] = jnp.concatenate([alpha, alpha], axis=1) * acc_ref[h] + pv

    def wide_step(j, carry):
        step(j * ATT_STEP_CHUNKS, ATT_STEP_CHUNKS)
        return carry

    n_wide = n_chunks // ATT_STEP_CHUNKS
    lax.fori_loop(0, n_wide, wide_step, 0)
    for r in range(1, ATT_STEP_CHUNKS):
        @pl.when(n_chunks % ATT_STEP_CHUNKS >= r)
        def _():
            step(n_wide * ATT_STEP_CHUNKS + r - 1, 1)

    for h in range(ATTN_HEADS):
        acc = acc_ref[h]
        o_ref[:, h * HEAD_DIM:(h + 1) * HEAD_DIM] = (acc[:, :HEAD_DIM] / acc[:, HEAD_DIM:]).astype(o_ref.dtype)


def _attention(q, k, v, bias):
    s, width = q.shape
    tq = SEL_TQ
    n_chunks, _, kc = bias.shape
    resident = dict(pipeline_mode=pl.Buffered(1))
    return pl.pallas_call(
        _attention_body,
        grid=(s // tq,),
        in_specs=[pl.BlockSpec((tq, width), lambda i: (i, 0)),
                  pl.BlockSpec((s, width), lambda i: (0, 0), **resident),
                  pl.BlockSpec((s, width), lambda i: (0, 0), **resident),
                  pl.BlockSpec((n_chunks, tq, kc), lambda i: (0, i, 0))],
        out_specs=pl.BlockSpec((tq, width), lambda i: (i, 0)),
        out_shape=jax.ShapeDtypeStruct((s, width), BF16),
        scratch_shapes=[pltpu.VMEM((ATTN_HEADS, tq, LANES), F32),
                        pltpu.VMEM((ATTN_HEADS, tq, 2 * HEAD_DIM), F32)],
        compiler_params=_params(("arbitrary",)),
        name="masked_attention",
    )(q, k, v, bias)


def _ffn_down_body(a_ref, w_ref, x_ref, g_ref, gate_ref, o_ref):
    f = jnp.dot(a_ref[...], w_ref[...], preferred_element_type=F32)
    o_ref[...] = x_ref[...] + gate_ref[...] * _rms(f, g_ref[...])


def _ffn_down(a, w, x1, g_post, gate):
    m, kdim = a.shape
    d = w.shape[1]
    tm = 256
    row = pl.BlockSpec((1, d), lambda i: (0, 0))
    return pl.pallas_call(
        _ffn_down_body,
        grid=(m // tm,),
        in_specs=[pl.BlockSpec((tm, kdim), lambda i: (i, 0)),
                  pl.BlockSpec((kdim, d), lambda i: (0, 0), pipeline_mode=pl.Buffered(1)),
                  pl.BlockSpec((tm, d), lambda i: (i, 0)), row, row],
        out_specs=pl.BlockSpec((tm, d), lambda i: (i, 0)),
        out_shape=jax.ShapeDtypeStruct((m, d), F32),
        compiler_params=_params(("arbitrary",)),
        name="ffn_down_residual",
    )(a, w, x1, g_post, gate)


def _rope_angles(seq, dim):
    inv = 1.0 / (ROPE_THETA ** (jnp.arange(0, dim, 2, dtype=F32) / dim))
    ang = jnp.arange(seq, dtype=F32)[:, None] * inv[None, :]
    return jnp.cos(ang), jnp.sin(ang)


def _attn_rope_tables(seq):
    cos, sin = _rope_angles(seq, HEAD_DIM)
    return jnp.concatenate([cos, cos], axis=1), jnp.concatenate([-sin, sin], axis=1)


def _index_rope_tables(seq):
    cos, sin = _rope_angles(seq, IDX_ROPE_DIM)
    half = IDX_ROPE_DIM // 2
    rest = IDX_HEAD_DIM - IDX_ROPE_DIM
    zeros = lambda n: jnp.zeros((seq, n), F32)
    c = jnp.concatenate([cos, cos, jnp.ones((seq, rest), F32)], axis=1)
    a = jnp.concatenate([-sin, zeros(IDX_HEAD_DIM - half)], axis=1)
    b = jnp.concatenate([zeros(half), sin, zeros(rest)], axis=1)
    rep = LANES // IDX_HEAD_DIM
    return tuple(jnp.tile(t, (1, rep)) for t in (c, a, b)) + (cos.T, sin.T)


def _layer(x, mod, g_pre_mix, g_post_mix, w_in_t, gmlp_ln_g, gmlp_ln_b, gmlp_w_s, gmlp_b_s,
           q_lat_norm_g, w_q_up, w_qidx_up, kidx_ln_g, kidx_ln_b, w_proj_a, w_proj_b, w_out,
           g_pre_ffn, g_post_ffn, w_ffn_gate, w_ffn_up, w_ffn_down, tables):
    s, d = x.shape
    cc, ss, ic, ia, ib, icos_t, isin_t = tables
    row = lambda a: a.reshape(1, -1)
    shift_m, scale_m, gate_m, shift_f, scale_f, gate_f = (mod[:, n * d:(n + 1) * d] for n in range(N_MOD))

    h = _prenorm(x, row(g_pre_mix), scale_m, shift_m)

    tm = 1024
    in_proj = functools.partial(_simple_mm, h, w_in_t, kn=False)

    u = in_proj(_ep_gelu, [], BF16, n=GMLP_WIDTH, tm=tm, tn=GMLP_WIDTH, sub_n=EPILOGUE_SUB_N,
                name="in_proj_u")
    vn = in_proj(_ep_gelu_layernorm,
                 [(row(gmlp_ln_g), (1, GMLP_WIDTH), _colblk), (row(gmlp_ln_b), (1, GMLP_WIDTH), _colblk)],
                 BF16, n=GMLP_WIDTH, first=GMLP_WIDTH, tm=512, tn=GMLP_WIDTH, sub_m=EPILOGUE_SUB_M,
                 name="in_proj_v_ln")
    gmlp_bias = jnp.repeat(gmlp_b_s.T, LANES, axis=1)
    y_a = _gmlp(u, vn, gmlp_w_s, gmlp_bias)

    q_lat = in_proj(_ep_rmsnorm, [(row(q_lat_norm_g), (1, Q_LORA_RANK), _colblk)],
                    BF16, n=Q_LORA_RANK, first=COL_QLAT, tm=tm, tn=Q_LORA_RANK, name="in_proj_qlat")
    q_scale = HEAD_DIM ** -0.5 * math.log2(math.e)
    rope_extras = [(cc, (tm, HEAD_DIM), _rowblk), (ss, (tm, HEAD_DIM), _rowblk)]
    q = _simple_mm(q_lat, w_q_up, functools.partial(_ep_rope, scale=q_scale), rope_extras,
                   BF16, n=ATTN_WIDTH, tm=tm, tn=ATTN_WIDTH, sub_n=EPILOGUE_SUB_N, name="q_up_rope")
    k = in_proj(functools.partial(_ep_rope, scale=1.0), rope_extras, BF16,
                n=ATTN_WIDTH, first=COL_K, tm=tm, tn=ATTN_WIDTH, sub_n=EPILOGUE_SUB_N,
                name="in_proj_k_rope")
    v = in_proj(_ep_cast, [], BF16, n=ATTN_WIDTH, first=COL_V, tm=tm, tn=ATTN_WIDTH,
                sub_n=EPILOGUE_SUB_N, name="in_proj_v")
    half = IDX_ROPE_DIM // 2
    q_idx_t = _simple_mm(q_lat, w_qidx_up.T, _ep_partial_rope_t,
                         [(icos_t, (half, tm), _colblk_t), (isin_t, (half, tm), _colblk_t)],
                         BF16, n=IDX_HEADS * IDX_HEAD_DIM, tm=tm, tn=512, name="qidx_up_rope",
                         kn=False, out_t=True)
    pad = lambda a: jnp.pad(a, (0, LANES - a.shape[0])).reshape(1, LANES)
    idx_w_scale = (IDX_HEADS ** -0.5) * (IDX_HEAD_DIM ** -0.5)
    kx = in_proj(functools.partial(_ep_index_keys, w_scale=idx_w_scale),
                 [(pad(kidx_ln_g), (1, LANES), _colblk), (pad(kidx_ln_b), (1, LANES), _colblk)]
                 + [(t, (tm, LANES), _rowblk) for t in (ic, ia, ib)],
                 F32, n=LANES, first=COL_KIDX, tm=tm, tn=LANES, name="in_proj_index_keys")
    k_idx = kx[:, :IDX_HEAD_DIM].astype(BF16)
    zeros = jnp.zeros_like(k_idx)
    k2 = jnp.stack([jnp.concatenate([k_idx, zeros], axis=1), jnp.concatenate([zeros, k_idx], axis=1)])
    idx_w_t = kx[:, IDX_HEAD_DIM:IDX_HEAD_DIM + IDX_HEADS].T
    bias = _indexer_mask(q_idx_t, idx_w_t, k2)
    y_b = _attention(q, k, v, bias)

    gates = in_proj(_ep_sigmoid, [], BF16, n=2 * d, first=COL_GATE, tm=tm, tn=1024,
                    sub_n=EPILOGUE_SUB_N, name="in_proj_gates")
    n_gate_blocks = d // 1024
    merged = _matmul([y_a, y_b], [(w_proj_a, 0, True), (w_proj_b, 0, True)], [0, 1],
                     [(gates, (tm, 1024), _tile),
                      (gates, (tm, 1024), lambda j, i: (i, j + n_gate_blocks))],
                     [((s, d), BF16, (tm, 1024), _tile)], _ep_merge, n=d, tm=tm, tn=1024,
                     sub_n=EPILOGUE_SUB_N, name="branch_merge")[0]
    tm_full = 512
    vec = lambda a: (a, (1, d), _colblk)
    x1, h2 = _matmul([merged], [(w_out, 0, True)], [0],
                     [(x, (tm_full, d), _rowblk), vec(row(g_post_mix)), vec(gate_m),
                      vec(row(g_pre_ffn)), vec(scale_f), vec(shift_f)],
                     [((s, d), F32, (tm_full, d), _tile), ((s, d), BF16, (tm_full, d), _tile)],
                     _ep_mix_residual, n=d, tm=tm_full, tn=d, sub_m=EPILOGUE_SUB_M, name="out_proj_residual")

    ffn_hidden = w_ffn_gate.shape[1]
    tn_ffn = 512
    n_row_blocks = s // tm
    down_rows = ffn_hidden // ((ffn_hidden // tn_ffn) * n_row_blocks)
    down_blk = ((down_rows, d), lambda j, i: (j * n_row_blocks + i, 0))
    act, w_down = _matmul([h2], [(w_ffn_gate, 0, True), (w_ffn_up, 0, True)], [0, 0],
                          [(w_ffn_down,) + down_blk],
                          [((s, ffn_hidden), BF16, (tm, tn_ffn), _tile), (w_ffn_down.shape, BF16) + down_blk],
                          _ep_swiglu, n=ffn_hidden, tm=tm, tn=tn_ffn, name="ffn_up")
    return _ffn_down(act, w_down, x1, row(g_post_ffn), gate_f)


def kernel(x, c, w_mod, b_mod, g_pre_mix, g_post_mix, w_in, gmlp_ln_g, gmlp_ln_b, gmlp_w_s, gmlp_b_s, q_lat_norm_g, w_q_up, w_qidx_up, kidx_ln_g, kidx_ln_b, w_proj_a, w_proj_b, w_out, g_pre_ffn, g_post_ffn, w_ffn_gate, w_ffn_up, w_ffn_down):
    batch, seq, d = x.shape
    assert batch == 1 and d == D_MODEL
    tables = _attn_rope_tables(seq) + _index_rope_tables(seq)
    y = x[0]
    for l in range(w_mod.shape[0]):
        mod = _modulation(c, w_mod[l], b_mod[l])
        y = _layer(y, mod, g_pre_mix[l], g_post_mix[l], w_in[l].T, gmlp_ln_g[l], gmlp_ln_b[l],
                   gmlp_w_s[l], gmlp_b_s[l], q_lat_norm_g[l], w_q_up[l], w_qidx_up[l],
                   kidx_ln_g[l], kidx_ln_b[l], w_proj_a[l], w_proj_b[l], w_out[l],
                   g_pre_ffn[l], g_post_ffn[l], w_ffn_gate[l], w_ffn_up[l], w_ffn_down[l], tables)
    return y[None]
```

```python
import functools
import math

import jax
import jax.numpy as jnp
from jax import lax
from jax.experimental import pallas as pl
from jax.experimental.pallas import tpu as pltpu

F32 = jnp.float32
BF16 = jnp.bfloat16

D_MODEL = 2048
GMLP_WIDTH = 1024
GMLP_GROUPS = 8
CHUNK = 128
ATTN_HEADS = 8
HEAD_DIM = 128
ATTN_WIDTH = ATTN_HEADS * HEAD_DIM
Q_LORA_RANK = 512
IDX_HEADS = 16
IDX_HEAD_DIM = 64
IDX_ROPE_DIM = 32
INDEX_TOPK = 256
ROPE_THETA = 10000.0
N_MOD = 6
NORM_EPS = 1e-6

COL_QLAT = 2 * GMLP_WIDTH
COL_K = COL_QLAT + Q_LORA_RANK
COL_V = COL_K + ATTN_WIDTH
COL_KIDX = COL_V + ATTN_WIDTH
COL_IDXW = COL_KIDX + IDX_HEAD_DIM
COL_GATE = COL_IDXW + IDX_HEADS

LANES = 128
SUBLANES = 8
VMEM_LIMIT = 56 * 1024 * 1024

MASK_BIAS = -(2.0 ** 100)
M_INIT = -(2.0 ** 99)
INT_MIN = -2 ** 31
KEY_NEG_INF = -2139095041

SEL_TQ = 256
KEY_BITS = 32
PLANE_KEYS = KEY_BITS * SUBLANES
ATT_KC = 2 * PLANE_KEYS
ATT_STEP_CHUNKS = 2
EPILOGUE_SUB_N = 512
EPILOGUE_SUB_M = 256
ROPE_BLOCK = 64

MM_TM = 1024
MM_TN = 1024
ROWNORM_TM = 512
FFN_UP_TN = 512
FFN_DOWN_TM = 256
QIDX_TN = 512
STREAM_TM = 1024
MOD_TN = 1024


def _params(sem=None):
    return pltpu.CompilerParams(dimension_semantics=sem, vmem_limit_bytes=VMEM_LIMIT)


def _mod_body(cb_ref, w_ref, b_ref, o_ref):
    cb = cb_ref[...]
    tn = o_ref.shape[1]
    parts = [jnp.sum(w_ref[:, p * LANES:(p + 1) * LANES] * cb, axis=0, keepdims=True)
             for p in range(tn // LANES)]
    o_ref[...] = jnp.concatenate(parts, axis=1) + b_ref[...]


def _modulation(c, w_mod, b_mod):
    k, n = w_mod.shape
    tn = MOD_TN
    cb = jnp.broadcast_to(c.reshape(k, 1), (k, LANES))
    return pl.pallas_call(
        _mod_body,
        grid=(n // tn,),
        in_specs=[pl.BlockSpec((k, LANES), lambda j: (0, 0)),
                  pl.BlockSpec((k, tn), lambda j: (0, j)),
                  pl.BlockSpec((1, tn), lambda j: (0, j))],
        out_specs=pl.BlockSpec((1, tn), lambda j: (0, j)),
        out_shape=jax.ShapeDtypeStruct((1, n), F32),
        compiler_params=_params(("arbitrary",)),
        name="modulation",
    )(cb, w_mod, b_mod.reshape(1, n))


def _rms(x, g):
    return x * lax.rsqrt(jnp.mean(x * x, axis=-1, keepdims=True) + NORM_EPS) * g


def _prenorm_body(x_ref, g_ref, scale_ref, shift_ref, o_ref):
    h = _rms(x_ref[...], g_ref[...]) * (1.0 + scale_ref[...]) + shift_ref[...]
    o_ref[...] = h.astype(o_ref.dtype)


def _prenorm(x, g, scale, shift):
    m, d = x.shape
    tm = STREAM_TM
    row = pl.BlockSpec((1, d), lambda i: (0, 0))
    return pl.pallas_call(
        _prenorm_body,
        grid=(m // tm,),
        in_specs=[pl.BlockSpec((tm, d), lambda i: (i, 0)), row, row, row],
        out_specs=pl.BlockSpec((tm, d), lambda i: (i, 0)),
        out_shape=jax.ShapeDtypeStruct((m, d), BF16),
        compiler_params=_params(("arbitrary",)),
        name="prenorm",
    )(x, g, scale, shift)


def _mm_body(*refs, n_x, pairs, w_kn, out_t, sub_m, sub_n, n_extra, n_out, epilogue):
    n_w = len(pairs)
    x_refs = refs[:n_x]
    w_refs = refs[n_x:n_x + n_w]
    e_refs = refs[n_x + n_w:n_x + n_w + n_extra]
    o_refs = refs[n_x + n_w + n_extra:n_x + n_w + n_extra + n_out]
    wb_refs = refs[n_x + n_w + n_extra + n_out:]

    @pl.when(pl.program_id(1) == 0)
    def _():
        for w_ref, wb_ref, kn in zip(w_refs, wb_refs, w_kn):
            w = w_ref[...]
            if not kn and not out_t:
                w = w.T
            wb_ref[...] = w.astype(wb_ref.dtype)

    if out_t:
        accs = [lax.dot_general(wb_ref[...], x_refs[xi][...], (((1,), (1,)), ((), ())),
                                preferred_element_type=F32)
                for xi, wb_ref in zip(pairs, wb_refs)]
        epilogue(accs, e_refs, o_refs, slice(0, wb_refs[0].shape[0]), slice(0, x_refs[0].shape[0]))
        return
    tm, tn = x_refs[0].shape[0], wb_refs[0].shape[1]
    for r0 in range(0, tm, sub_m):
        rs = slice(r0, r0 + sub_m)
        for c0 in range(0, tn, sub_n):
            cs = slice(c0, c0 + sub_n)
            accs = [jnp.dot(x_refs[xi][rs, :], wb_ref[:, cs], preferred_element_type=F32)
                    for xi, wb_ref in zip(pairs, wb_refs)]
            epilogue(accs, e_refs, o_refs, cs, rs)


def _matmul(xs, ws, pairs, extras, outs, epilogue, *, n, tm, tn, name, out_t=False, sub_n=None,
            sub_m=None):
    m = xs[0].shape[0]
    n_col = n // tn
    w_mode = dict(pipeline_mode=pl.Buffered(1)) if n_col == 1 else {}
    in_specs = [pl.BlockSpec((tm, x.shape[1]), lambda j, i: (i, 0)) for x in xs]
    scratch = []
    for w, first, kn in ws:
        assert not (kn and out_t)
        if kn:
            kdim = w.shape[0]
            assert first % tn == 0
            in_specs.append(pl.BlockSpec((kdim, tn), functools.partial(_w_cols, first=first // tn), **w_mode))
        else:
            kdim = w.shape[1]
            in_specs.append(pl.BlockSpec((pl.Element(tn), pl.Element(kdim)),
                                         functools.partial(_w_rows, first=first, tn=tn), **w_mode))
        scratch.append(pltpu.VMEM((tn, kdim) if out_t else (kdim, tn), BF16))
    in_specs += [pl.BlockSpec(bs, im) for _, bs, im in extras]
    body = functools.partial(_mm_body, n_x=len(xs), pairs=tuple(pairs), w_kn=tuple(kn for _, _, kn in ws),
                             out_t=out_t, sub_m=sub_m or tm, sub_n=sub_n or tn, n_extra=len(extras),
                             n_out=len(outs), epilogue=epilogue)
    return pl.pallas_call(
        body,
        grid=(n_col, m // tm),
        in_specs=in_specs,
        out_specs=[pl.BlockSpec(bs, im) for _, _, bs, im in outs],
        out_shape=[jax.ShapeDtypeStruct(s, d) for s, d, _, _ in outs],
        scratch_shapes=scratch,
        compiler_params=_params(("arbitrary", "arbitrary")),
        name=name,
    )(*xs, *[w for w, _, _ in ws], *[a for a, _, _ in extras])


def _w_cols(j, i, *, first):
    return (0, first + j)


def _w_rows(j, i, *, first, tn):
    assert first % SUBLANES == 0 and tn % SUBLANES == 0
    return (pl.multiple_of(first + j * tn, SUBLANES), 0)


def _tile(j, i):
    return (i, j)


def _tile_t(j, i):
    return (j, i)


def _rowblk(j, i):
    return (i, 0)


def _colblk(j, i):
    return (0, j)


def _colblk_t(j, i):
    return (0, i)


def _ep_gelu(accs, e_refs, o_refs, cols, rows):
    o_refs[0][rows, cols] = jax.nn.gelu(accs[0]).astype(o_refs[0].dtype)


def _ep_gelu_layernorm(accs, e_refs, o_refs, cols, rows):
    z = jax.nn.gelu(accs[0])
    mu = jnp.mean(z, axis=-1, keepdims=True)
    zc = z - mu
    var = jnp.mean(zc * zc, axis=-1, keepdims=True)
    y = zc * lax.rsqrt(var + NORM_EPS) * e_refs[0][...] + e_refs[1][...]
    o_refs[0][rows, :] = y.astype(o_refs[0].dtype)


def _ep_rmsnorm(accs, e_refs, o_refs, cols, rows):
    o_refs[0][rows, :] = _rms(accs[0], e_refs[0][...]).astype(o_refs[0].dtype)


def _ep_cast(accs, e_refs, o_refs, cols, rows):
    o_refs[0][rows, cols] = accs[0].astype(o_refs[0].dtype)


def _ep_sigmoid(accs, e_refs, o_refs, cols, rows):
    o_refs[0][rows, cols] = (0.5 * jnp.tanh(0.5 * accs[0]) + 0.5).astype(o_refs[0].dtype)


def _ep_rope(accs, e_refs, o_refs, cols, rows, *, scale):
    cc = e_refs[0][rows, :]
    ss = e_refs[1][rows, :]
    acc = accs[0]
    for h in range(acc.shape[1] // HEAD_DIM):
        xh = acc[:, h * HEAD_DIM:(h + 1) * HEAD_DIM]
        r = xh * cc + pltpu.roll(xh, HEAD_DIM // 2, 1) * ss
        if scale != 1.0:
            r = r * scale
        c0 = cols.start + h * HEAD_DIM
        o_refs[0][rows, c0:c0 + HEAD_DIM] = r.astype(o_refs[0].dtype)


def _ep_partial_rope_t(accs, e_refs, o_refs, cols, rows):
    c = e_refs[0][...]
    s = e_refs[1][...]
    acc = accs[0]
    half = IDX_ROPE_DIM // 2
    for h in range(acc.shape[0] // IDX_HEAD_DIM):
        r0 = h * IDX_HEAD_DIM
        x1 = acc[r0:r0 + half, :]
        x2 = acc[r0 + half:r0 + 2 * half, :]
        o_refs[0][r0:r0 + half, :] = (x1 * c - x2 * s).astype(o_refs[0].dtype)
        o_refs[0][r0 + half:r0 + 2 * half, :] = (x2 * c + x1 * s).astype(o_refs[0].dtype)
        o_refs[0][r0 + 2 * half:r0 + IDX_HEAD_DIM, :] = (
            acc[r0 + 2 * half:r0 + IDX_HEAD_DIM, :].astype(o_refs[0].dtype))


def _partial_rope(x, c, a, b):
    half = IDX_ROPE_DIM // 2
    return x * c + pltpu.roll(x, LANES - half, 1) * a + pltpu.roll(x, half, 1) * b


def _ep_index_keys(accs, e_refs, o_refs, cols, rows, *, w_scale):
    g, bb = e_refs[0][...], e_refs[1][...]
    c, a, b = e_refs[2][rows, :], e_refs[3][rows, :], e_refs[4][rows, :]
    acc = accs[0]
    lane = lax.broadcasted_iota(jnp.int32, acc.shape, 1)
    is_key = lane < IDX_HEAD_DIM
    mu = jnp.sum(jnp.where(is_key, acc, 0.0), axis=-1, keepdims=True) / IDX_HEAD_DIM
    xc = jnp.where(is_key, acc - mu, 0.0)
    var = jnp.sum(xc * xc, axis=-1, keepdims=True) / IDX_HEAD_DIM
    y = xc * lax.rsqrt(var + NORM_EPS) * g + bb
    y = _partial_rope(y, c, a, b)
    o_refs[0][rows, :] = jnp.where(is_key, y, acc * w_scale)


def _ep_merge(accs, e_refs, o_refs, cols, rows):
    o = e_refs[0][rows, cols].astype(F32) * accs[0] + e_refs[1][rows, cols].astype(F32) * accs[1]
    o_refs[0][rows, cols] = o.astype(o_refs[0].dtype)


def _ep_mix_residual(accs, e_refs, o_refs, cols, rows):
    x_ref, g_post, gate, g_pre, scale, shift = e_refs
    x1 = x_ref[rows, :] + gate[...] * _rms(accs[0], g_post[...])
    o_refs[0][rows, :] = x1
    h2 = _rms(x1, g_pre[...]) * (1.0 + scale[...]) + shift[...]
    o_refs[1][rows, :] = h2.astype(o_refs[1].dtype)


def _ep_swiglu(accs, e_refs, o_refs, cols, rows):
    o_refs[0][rows, cols] = (jax.nn.silu(accs[0]) * accs[1]).astype(o_refs[0].dtype)

    if cols.start == 0 and rows.start == 0:
        o_refs[1][...] = e_refs[0][...].astype(o_refs[1].dtype)


def _simple_mm(x, w, epilogue, extras, out_dtype, *, n, tm, tn, name, first=0, kn=True, out_t=False,
               sub_n=None, sub_m=None):
    m = x.shape[0]
    out = ((n, m), out_dtype, (tn, tm), _tile_t) if out_t else ((m, n), out_dtype, (tm, tn), _tile)
    return _matmul([x], [(w, first, kn)], [0], extras, [out], epilogue, n=n, tm=tm, tn=tn,
                   name=name, out_t=out_t, sub_n=sub_n, sub_m=sub_m)[0]


def _gmlp_body(u_ref, v_ref, w_ref, b_ref, o_ref):
    t = lax.broadcasted_iota(jnp.int32, (CHUNK, CHUNK), 0)
    s = lax.broadcasted_iota(jnp.int32, (CHUNK, CHUNK), 1)
    causal = s <= t
    for g in range(GMLP_GROUPS):
        w = jnp.where(causal, w_ref[g], 0.0).astype(BF16)
        cols = slice(g * LANES, (g + 1) * LANES)
        bias = b_ref[:, cols]
        for c in range(u_ref.shape[0] // CHUNK):
            rows = slice(c * CHUNK, (c + 1) * CHUNK)
            sv = jnp.dot(w, v_ref[rows, cols], preferred_element_type=F32) + bias
            o_ref[rows, cols] = (u_ref[rows, cols].astype(F32) * sv).astype(o_ref.dtype)


def _gmlp(u, vn, w_s, bias):
    m, n = u.shape
    tm = STREAM_TM
    blk = pl.BlockSpec((tm, n), lambda i: (i, 0))
    return pl.pallas_call(
        _gmlp_body,
        grid=(m // tm,),
        in_specs=[blk, blk,
                  pl.BlockSpec(w_s.shape, lambda i: (0, 0, 0)),
                  pl.BlockSpec(bias.shape, lambda i: (0, 0))],
        out_specs=blk,
        out_shape=jax.ShapeDtypeStruct((m, n), BF16),
        compiler_params=_params(("arbitrary",)),
        name="gmlp_gating",
    )(u, vn, w_s, bias)


def _bit_planes(words):
    a = list(words)
    j, mask = KEY_BITS // 2, 0x0000FFFF
    while j:
        for k in range(KEY_BITS):
            if k & j:
                continue
            t = (a[k] ^ (a[k + j] >> j)) & jnp.int32(mask)
            a[k] = a[k] ^ t
            a[k + j] = a[k + j] ^ (t << j)
        j //= 2
        mask = (mask ^ (mask << j)) & 0xFFFFFFFF
    return a


def _score_key(score):
    bits = lax.bitcast_convert_type(score, jnp.int32)
    return bits ^ ((bits >> 31) & jnp.int32(0x7FFFFFFF))


def _key_score(key):
    return lax.bitcast_convert_type(key ^ ((key >> 31) & jnp.int32(0x7FFFFFFF)), F32)


def _indexer_body(qt_ref, wt_ref, k2_ref, bias_ref, score_ref, plane_ref, live_ref, kth_ref):
    tq = qt_ref.shape[1]
    n_chunks_total = bias_ref.shape[0]
    i = pl.program_id(0)
    n_chunks = ((i + 1) * tq + ATT_KC - 1) // ATT_KC
    blocks_per_chunk = ATT_KC // PLANE_KEYS
    n_blocks = n_chunks * blocks_per_chunk
    heads_per_group = LANES // IDX_HEAD_DIM

    @pl.when(i == 0)
    def _():
        plane_ref[...] = jnp.zeros(plane_ref.shape, jnp.int32)

    q_pos = i * tq + lax.broadcasted_iota(jnp.int32, (ATT_KC, tq), 1)
    k_off = lax.broadcasted_iota(jnp.int32, (ATT_KC, tq), 0)

    def score_chunk(c):
        k0 = pl.multiple_of(c * ATT_KC, ATT_KC)
        score = jnp.zeros((ATT_KC, tq), F32)
        for h in range(IDX_HEADS):
            grp, r = divmod(h, heads_per_group)
            logit = jnp.dot(k2_ref[r, pl.ds(k0, ATT_KC), :], qt_ref[grp * LANES:(grp + 1) * LANES, :],
                            preferred_element_type=F32)
            score = score + jnp.maximum(logit, 0.0) * wt_ref[h:h + 1, :]
        score_ref[pl.ds(k0, ATT_KC), :] = jnp.where(k0 + k_off <= q_pos, score, -jnp.inf)

    def slice_chunk(c):
        for hh in range(blocks_per_chunk):
            blk = c * blocks_per_chunk + hh
            k0 = pl.multiple_of(blk * PLANE_KEYS, PLANE_KEYS)
            bits = lax.bitcast_convert_type(score_ref[pl.ds(k0, PLANE_KEYS), :], jnp.int32)
            planes = _bit_planes([bits[w * SUBLANES:(w + 1) * SUBLANES, :] for w in range(KEY_BITS)])
            r0 = pl.multiple_of(blk * SUBLANES, SUBLANES)
            sign = planes[0]
            plane_ref[0, pl.ds(r0, SUBLANES), :] = ~sign
            for r in range(1, KEY_BITS):
                plane_ref[r, pl.ds(r0, SUBLANES), :] = planes[r] ^ sign

    def score_and_slice(c, carry):
        slice_chunk(c - 1)
        score_chunk(c)
        return carry

    score_chunk(0)
    lax.fori_loop(1, n_chunks, score_and_slice, 0)
    slice_chunk(n_chunks - 1)

    def count_bits(words):
        ones = lax.population_count(words)
        part = jnp.sum(ones.reshape(words.shape[0] // SUBLANES, SUBLANES, tq), axis=0)
        return jnp.sum(part.astype(F32), axis=0, keepdims=True)

    def radix_select(rows):
        row_id = lax.broadcasted_iota(jnp.int32, (rows, tq), 0)
        live_ref[:rows] = jnp.where(row_id < n_blocks * SUBLANES, jnp.int32(-1), jnp.int32(0))

        def search_bit(r, carry):
            need, kth = carry
            total = count_bits(live_ref[:rows] & plane_ref[r, :rows])
            take = total >= need
            keep_clear = jnp.where(take, jnp.int32(0), jnp.int32(-1))
            live_ref[:rows] = live_ref[:rows] & (plane_ref[r, :rows] ^ keep_clear)
            need = jnp.where(take, need, need - total)
            kth = jnp.where(take, kth | jnp.left_shift(jnp.int32(1), KEY_BITS - 1 - r), kth)
            return need, kth

        _, kth = lax.fori_loop(0, KEY_BITS, search_bit,
                               (jnp.full((1, tq), float(INDEX_TOPK), F32), jnp.zeros((1, tq), jnp.int32)))
        kth_ref[0:1] = kth ^ jnp.int32(INT_MIN)

    plane_rows = live_ref.shape[0]
    quarter = plane_rows // 4
    for rows in range(quarter, plane_rows + 1, quarter):
        @pl.when((n_blocks * SUBLANES > rows - quarter) & (n_blocks * SUBLANES <= rows))
        def _():
            radix_select(rows)

    def count_scores(pred):
        group = 4 * SUBLANES

        def add_chunk(c, acc):
            k0 = pl.multiple_of(c * ATT_KC, ATT_KC)
            hit = jnp.where(pred(score_ref[pl.ds(k0, ATT_KC), :]), 1.0, 0.0)
            return acc + jnp.sum(hit.reshape(ATT_KC // group, group, tq), axis=0)

        acc = lax.fori_loop(0, n_chunks, add_chunk, jnp.zeros((group, tq), F32))
        return jnp.sum(acc, axis=0, keepdims=True)

    few_keys = i * tq + lax.broadcasted_iota(jnp.int32, (1, tq), 1) + 1 < INDEX_TOPK
    lowest = float(jnp.finfo(F32).min)
    thr = jnp.where(few_keys, lowest, _key_score(jnp.maximum(kth_ref[0:1], KEY_NEG_INF + 1)))

    def bias_tile(sel):
        return jnp.where(sel, 0.0, MASK_BIAS).T

    def write_chunk(c, kept):
        parts = []
        for hh in range(blocks_per_chunk):
            k0 = pl.multiple_of(c * ATT_KC + hh * PLANE_KEYS, PLANE_KEYS)
            sel = score_ref[pl.ds(k0, PLANE_KEYS), :] >= thr
            parts.append(bias_tile(sel))
            hit = jnp.where(sel, 1.0, 0.0)
            kept = kept + jnp.sum(hit.reshape(PLANE_KEYS // kept.shape[0], kept.shape[0], tq), axis=0)
        bias_ref[c] = jnp.concatenate(parts, axis=1).astype(bias_ref.dtype)
        return kept

    kept = lax.fori_loop(0, n_chunks, write_chunk, jnp.zeros((4 * SUBLANES, tq), F32))
    kept = jnp.sum(kept, axis=0, keepdims=True)
    unsettled = jnp.max(jnp.where(few_keys | (kept == float(INDEX_TOPK)), 0.0, 1.0))

    @pl.when(unsettled > 0.0)
    def _():
        def search_bit(b, prefix):
            cand = prefix ^ jnp.left_shift(jnp.int32(1), KEY_BITS - 1 - b)
            cand_score = _key_score(jnp.maximum(cand, KEY_NEG_INF))
            admitted = count_scores(lambda sc: sc >= cand_score)
            return jnp.where(admitted >= float(INDEX_TOPK), cand, prefix)

        kth = lax.fori_loop(0, KEY_BITS, search_bit, jnp.full((1, tq), INT_MIN, jnp.int32))
        kth_score = jnp.where(few_keys, lowest, _key_score(jnp.maximum(kth, KEY_NEG_INF + 1)))
        above = count_scores(lambda sc: sc > kth_score)
        need = jnp.where(few_keys, float(2 ** 24), float(INDEX_TOPK) - above)

        row = lax.broadcasted_iota(jnp.int32, (PLANE_KEYS, PLANE_KEYS), 0)
        col = lax.broadcasted_iota(jnp.int32, (PLANE_KEYS, PLANE_KEYS), 1)
        prefix_sum = jnp.where(col <= row, 1.0, 0.0).astype(BF16)

        def write_chunk(c, seen):
            parts = []
            for hh in range(blocks_per_chunk):
                k0 = pl.multiple_of(c * ATT_KC + hh * PLANE_KEYS, PLANE_KEYS)
                sc = score_ref[pl.ds(k0, PLANE_KEYS), :]
                tie = sc == kth_score
                rank = seen + jnp.dot(prefix_sum, jnp.where(tie, 1.0, 0.0).astype(BF16),
                                      preferred_element_type=F32)
                parts.append(bias_tile((sc > kth_score) | (tie & (rank <= need))))
                seen = rank[PLANE_KEYS - 1:PLANE_KEYS, :]
            bias_ref[c] = jnp.concatenate(parts, axis=1).astype(bias_ref.dtype)
            return seen

        lax.fori_loop(0, n_chunks, write_chunk, jnp.zeros((1, tq), F32))

    def fill_chunk(c, carry):
        bias_ref[c] = jnp.full(bias_ref.shape[1:], MASK_BIAS, bias_ref.dtype)
        return carry

    lax.fori_loop(n_chunks, n_chunks_total, fill_chunk, 0)


def _indexer_mask(q_idx_t, idx_w_t, k2):
    s = q_idx_t.shape[1]
    tq = SEL_TQ
    assert tq % PLANE_KEYS == 0 and s % (KEY_BITS * SUBLANES) == 0
    return pl.pallas_call(
        _indexer_body,
        grid=(s // tq,),
        in_specs=[pl.BlockSpec((q_idx_t.shape[0], tq), lambda i: (0, i)),
                  pl.BlockSpec((IDX_HEADS, tq), lambda i: (0, i)),
                  pl.BlockSpec(k2.shape, lambda i: (0, 0, 0))],
        out_specs=pl.BlockSpec((s // ATT_KC, tq, ATT_KC), lambda i: (0, i, 0)),
        out_shape=jax.ShapeDtypeStruct((s // ATT_KC, s, ATT_KC), BF16),
        scratch_shapes=[pltpu.VMEM((s, tq), F32),
                        pltpu.VMEM((KEY_BITS, s // KEY_BITS, tq), jnp.int32),
                        pltpu.VMEM((s // KEY_BITS, tq), jnp.int32),
                        pltpu.VMEM((SUBLANES, tq), jnp.int32)],
        compiler_params=_params(("arbitrary",)),
        name="indexer_mask",
    )(q_idx_t, idx_w_t, k2)


def _attention_body(q_ref, k_ref, v_ref, bias_ref, o_ref, m_ref, acc_ref):
    tq = q_ref.shape[0]
    chunk_keys = bias_ref.shape[2]
    i = pl.program_id(0)
    n_chunks = (i * tq + tq - 1) // chunk_keys + 1

    m_ref[...] = jnp.full(m_ref.shape, M_INIT, F32)
    acc_ref[...] = jnp.zeros(acc_ref.shape, F32)

    def step(first_chunk, width):
        kc = width * chunk_keys
        sub = kc // LANES
        k0 = pl.multiple_of(first_chunk * chunk_keys, chunk_keys)
        bias = jnp.concatenate([bias_ref[first_chunk + c] for c in range(width)], axis=1)
        ones = jnp.ones((kc, HEAD_DIM), v_ref.dtype)
        for h in range(ATTN_HEADS):
            cols = slice(h * HEAD_DIM, (h + 1) * HEAD_DIM)
            s = lax.dot_general(q_ref[:, cols], k_ref[pl.ds(k0, kc), cols],
                                (((1,), (1,)), ((), ())), preferred_element_type=F32)
            sb = s.astype(BF16) + bias
            part = sb[:, :LANES]
            for c in range(1, sub):
                part = jnp.maximum(part, sb[:, c * LANES:(c + 1) * LANES])
            m_prev = m_ref[h]
            m_next = jnp.maximum(m_prev, jnp.max(part.astype(F32), axis=1, keepdims=True))
            p = jnp.exp2(sb - jnp.concatenate([m_next.astype(BF16)] * sub, axis=1))
            alpha = jnp.exp2(m_prev - m_next)
            m_ref[h] = m_next
            pv = jnp.dot(p, jnp.concatenate([v_ref[pl.ds(k0, kc), cols], ones], axis=1),
                         preferred_element_type=F32)
            acc_ref[h] = jnp.concatenate([alpha, alpha], axis=1) * acc_ref[h] + pv

    def wide_step(j, carry):
        step(j * ATT_STEP_CHUNKS, ATT_STEP_CHUNKS)
        return carry

    n_wide = n_chunks // ATT_STEP_CHUNKS
    lax.fori_loop(0, n_wide, wide_step, 0)
    for r in range(1, ATT_STEP_CHUNKS):
        @pl.when(n_chunks % ATT_STEP_CHUNKS >= r)
        def _():
            step(n_wide * ATT_STEP_CHUNKS + r - 1, 1)

    for h in range(ATTN_HEADS):
        acc = acc_ref[h]
        o_ref[:, h * HEAD_DIM:(h + 1) * HEAD_DIM] = (acc[:, :HEAD_DIM] / acc[:, HEAD_DIM:]).astype(o_ref.dtype)


def _attention(q, k, v, bias):
    s, width = q.shape
    tq = SEL_TQ
    n_chunks, _, kc = bias.shape
    resident = dict(pipeline_mode=pl.Buffered(1))
    return pl.pallas_call(
        _attention_body,
        grid=(s // tq,),
        in_specs=[pl.BlockSpec((tq, width), lambda i: (i, 0)),
                  pl.BlockSpec((s, width), lambda i: (0, 0), **resident),
                  pl.BlockSpec((s, width), lambda i: (0, 0), **resident),
                  pl.BlockSpec((n_chunks, tq, kc), lambda i: (0, i, 0))],
        out_specs=pl.BlockSpec((tq, width), lambda i: (i, 0)),
        out_shape=jax.ShapeDtypeStruct((s, width), BF16),
        scratch_shapes=[pltpu.VMEM((ATTN_HEADS, tq, LANES), F32),
                        pltpu.VMEM((ATTN_HEADS, tq, 2 * HEAD_DIM), F32)],
        compiler_params=_params(("arbitrary",)),
        name="masked_attention",
    )(q, k, v, bias)


def _ffn_down_body(a_ref, w_ref, x_ref, g_ref, gate_ref, o_ref):
    f = jnp.dot(a_ref[...], w_ref[...], preferred_element_type=F32)
    o_ref[...] = x_ref[...] + gate_ref[...] * _rms(f, g_ref[...])


def _ffn_down(a, w, x1, g_post, gate):
    m, kdim = a.shape
    d = w.shape[1]
    tm = FFN_DOWN_TM
    row = pl.BlockSpec((1, d), lambda i: (0, 0))
    return pl.pallas_call(
        _ffn_down_body,
        grid=(m // tm,),
        in_specs=[pl.BlockSpec((tm, kdim), lambda i: (i, 0)),
                  pl.BlockSpec((kdim, d), lambda i: (0, 0), pipeline_mode=pl.Buffered(1)),
                  pl.BlockSpec((tm, d), lambda i: (i, 0)), row, row],
        out_specs=pl.BlockSpec((tm, d), lambda i: (i, 0)),
        out_shape=jax.ShapeDtypeStruct((m, d), F32),
        compiler_params=_params(("arbitrary",)),
        name="ffn_down_residual",
    )(a, w, x1, g_post, gate)


def _rope_angles(seq, dim):
    inv = 1.0 / (ROPE_THETA ** (jnp.arange(0, dim, 2, dtype=F32) / dim))
    coarse = (jnp.arange(seq // ROPE_BLOCK, dtype=F32) * ROPE_BLOCK)[:, None] * inv[None, :]
    fine = jnp.arange(ROPE_BLOCK, dtype=F32)[:, None] * inv[None, :]
    cos_a, sin_a = jnp.cos(coarse)[:, None, :], jnp.sin(coarse)[:, None, :]
    cos_b, sin_b = jnp.cos(fine)[None, :, :], jnp.sin(fine)[None, :, :]
    cos = (cos_a * cos_b - sin_a * sin_b).reshape(seq, dim // 2)
    sin = (sin_a * cos_b + cos_a * sin_b).reshape(seq, dim // 2)
    return cos, sin


def _attn_rope_tables(seq):
    cos, sin = _rope_angles(seq, HEAD_DIM)
    return jnp.concatenate([cos, cos], axis=1), jnp.concatenate([-sin, sin], axis=1)


def _index_rope_tables(seq):
    cos, sin = _rope_angles(seq, IDX_ROPE_DIM)
    half = IDX_ROPE_DIM // 2
    rest = IDX_HEAD_DIM - IDX_ROPE_DIM
    zeros = lambda n: jnp.zeros((seq, n), F32)
    c = jnp.concatenate([cos, cos, jnp.ones((seq, rest), F32)], axis=1)
    a = jnp.concatenate([-sin, zeros(IDX_HEAD_DIM - half)], axis=1)
    b = jnp.concatenate([zeros(half), sin, zeros(rest)], axis=1)
    rep = LANES // IDX_HEAD_DIM
    return tuple(jnp.tile(t, (1, rep)) for t in (c, a, b)) + (cos.T, sin.T)


def _layer(x, mod, g_pre_mix, g_post_mix, w_in_t, gmlp_ln_g, gmlp_ln_b, gmlp_w_s, gmlp_b_s,
           q_lat_norm_g, w_q_up, w_qidx_up, kidx_ln_g, kidx_ln_b, w_proj_a, w_proj_b, w_out,
           g_pre_ffn, g_post_ffn, w_ffn_gate, w_ffn_up, w_ffn_down, tables):
    s, d = x.shape
    cc, ss, ic, ia, ib, icos_t, isin_t = tables
    row = lambda a: a.reshape(1, -1)
    shift_m, scale_m, gate_m, shift_f, scale_f, gate_f = (mod[:, n * d:(n + 1) * d] for n in range(N_MOD))

    h = _prenorm(x, row(g_pre_mix), scale_m, shift_m)

    tm = MM_TM
    in_proj = functools.partial(_simple_mm, h, w_in_t, kn=False)

    u = in_proj(_ep_gelu, [], BF16, n=GMLP_WIDTH, tm=tm, tn=GMLP_WIDTH, sub_n=EPILOGUE_SUB_N,
                name="in_proj_u")
    vn = in_proj(_ep_gelu_layernorm,
                 [(row(gmlp_ln_g), (1, GMLP_WIDTH), _colblk), (row(gmlp_ln_b), (1, GMLP_WIDTH), _colblk)],
                 BF16, n=GMLP_WIDTH, first=GMLP_WIDTH, tm=ROWNORM_TM, tn=GMLP_WIDTH, sub_m=EPILOGUE_SUB_M,
                 name="in_proj_v_ln")
    gmlp_bias = jnp.repeat(gmlp_b_s.T, LANES, axis=1)
    y_a = _gmlp(u, vn, gmlp_w_s, gmlp_bias)

    q_lat = in_proj(_ep_rmsnorm, [(row(q_lat_norm_g), (1, Q_LORA_RANK), _colblk)],
                    BF16, n=Q_LORA_RANK, first=COL_QLAT, tm=tm, tn=Q_LORA_RANK, name="in_proj_qlat")
    q_scale = HEAD_DIM ** -0.5 * math.log2(math.e)
    rope_extras = [(cc, (tm, HEAD_DIM), _rowblk), (ss, (tm, HEAD_DIM), _rowblk)]
    q = _simple_mm(q_lat, w_q_up, functools.partial(_ep_rope, scale=q_scale), rope_extras,
                   BF16, n=ATTN_WIDTH, tm=tm, tn=ATTN_WIDTH, sub_n=EPILOGUE_SUB_N, name="q_up_rope")
    k = in_proj(functools.partial(_ep_rope, scale=1.0), rope_extras, BF16,
                n=ATTN_WIDTH, first=COL_K, tm=tm, tn=ATTN_WIDTH, sub_n=EPILOGUE_SUB_N,
                name="in_proj_k_rope")
    v = in_proj(_ep_cast, [], BF16, n=ATTN_WIDTH, first=COL_V, tm=tm, tn=ATTN_WIDTH,
                sub_n=EPILOGUE_SUB_N, name="in_proj_v")
    half = IDX_ROPE_DIM // 2
    q_idx_t = _simple_mm(q_lat, w_qidx_up.T, _ep_partial_rope_t,
                         [(icos_t, (half, tm), _colblk_t), (isin_t, (half, tm), _colblk_t)],
                         BF16, n=IDX_HEADS * IDX_HEAD_DIM, tm=tm, tn=QIDX_TN, name="qidx_up_rope",
                         kn=False, out_t=True)
    pad = lambda a: jnp.pad(a, (0, LANES - a.shape[0])).reshape(1, LANES)
    idx_w_scale = (IDX_HEADS ** -0.5) * (IDX_HEAD_DIM ** -0.5)
    kx = in_proj(functools.partial(_ep_index_keys, w_scale=idx_w_scale),
                 [(pad(kidx_ln_g), (1, LANES), _colblk), (pad(kidx_ln_b), (1, LANES), _colblk)]
                 + [(t, (tm, LANES), _rowblk) for t in (ic, ia, ib)],
                 F32, n=LANES, first=COL_KIDX, tm=tm, tn=LANES, name="in_proj_index_keys")
    k_idx = kx[:, :IDX_HEAD_DIM].astype(BF16)
    zeros = jnp.zeros_like(k_idx)
    k2 = jnp.stack([jnp.concatenate([k_idx, zeros], axis=1), jnp.concatenate([zeros, k_idx], axis=1)])
    idx_w_t = kx[:, IDX_HEAD_DIM:IDX_HEAD_DIM + IDX_HEADS].T
    bias = _indexer_mask(q_idx_t, idx_w_t, k2)
    y_b = _attention(q, k, v, bias)

    gates = in_proj(_ep_sigmoid, [], BF16, n=2 * d, first=COL_GATE, tm=tm, tn=MM_TN,
                    sub_n=EPILOGUE_SUB_N, name="in_proj_gates")
    n_gate_blocks = d // MM_TN
    merged = _matmul([y_a, y_b], [(w_proj_a, 0, True), (w_proj_b, 0, True)], [0, 1],
                     [(gates, (tm, MM_TN), _tile),
                      (gates, (tm, MM_TN), lambda j, i: (i, j + n_gate_blocks))],
                     [((s, d), BF16, (tm, MM_TN), _tile)], _ep_merge, n=d, tm=tm, tn=MM_TN,
                     sub_n=EPILOGUE_SUB_N, name="branch_merge")[0]
    tm_full = ROWNORM_TM
    vec = lambda a: (a, (1, d), _colblk)
    x1, h2 = _matmul([merged], [(w_out, 0, True)], [0],
                     [(x, (tm_full, d), _rowblk), vec(row(g_post_mix)), vec(gate_m),
                      vec(row(g_pre_ffn)), vec(scale_f), vec(shift_f)],
                     [((s, d), F32, (tm_full, d), _tile), ((s, d), BF16, (tm_full, d), _tile)],
                     _ep_mix_residual, n=d, tm=tm_full, tn=d, sub_m=EPILOGUE_SUB_M, name="out_proj_residual")

    ffn_hidden = w_ffn_gate.shape[1]
    tn_ffn = FFN_UP_TN
    n_row_blocks = s // tm
    down_rows = ffn_hidden // ((ffn_hidden // tn_ffn) * n_row_blocks)
    down_blk = ((down_rows, d), lambda j, i: (j * n_row_blocks + i, 0))
    act, w_down = _matmul([h2], [(w_ffn_gate, 0, True), (w_ffn_up, 0, True)], [0, 0],
                          [(w_ffn_down,) + down_blk],
                          [((s, ffn_hidden), BF16, (tm, tn_ffn), _tile), (w_ffn_down.shape, BF16) + down_blk],
                          _ep_swiglu, n=ffn_hidden, tm=tm, tn=tn_ffn, name="ffn_up")
    return _ffn_down(act, w_down, x1, row(g_post_ffn), gate_f)


def kernel(x, c, w_mod, b_mod, g_pre_mix, g_post_mix, w_in, gmlp_ln_g, gmlp_ln_b, gmlp_w_s, gmlp_b_s, q_lat_norm_g, w_q_up, w_qidx_up, kidx_ln_g, kidx_ln_b, w_proj_a, w_proj_b, w_out, g_pre_ffn, g_post_ffn, w_ffn_gate, w_ffn_up, w_ffn_down):
    batch, seq, d = x.shape
    assert batch == 1 and d == D_MODEL
    tables = _attn_rope_tables(seq) + _index_rope_tables(seq)
    y = x[0]
    for l in range(w_mod.shape[0]):
        mod = _modulation(c, w_mod[l], b_mod[l])
        y = _layer(y, mod, g_pre_mix[l], g_post_mix[l], w_in[l].T, gmlp_ln_g[l], gmlp_ln_b[l],
                   gmlp_w_s[l], gmlp_b_s[l], q_lat_norm_g[l], w_q_up[l], w_qidx_up[l],
                   kidx_ln_g[l], kidx_ln_b[l], w_proj_a[l], w_proj_b[l], w_out[l],
                   g_pre_ffn[l], g_post_ffn[l], w_ffn_gate[l], w_ffn_up[l], w_ffn_down[l], tables)
    return y[None]
```

```python
import functools
import math

import jax
import jax.numpy as jnp
from jax import lax
from jax.experimental import pallas as pl
from jax.experimental.pallas import tpu as pltpu

F32 = jnp.float32
BF16 = jnp.bfloat16

D_MODEL = 2048
GMLP_WIDTH = 1024
GMLP_GROUPS = 8
CHUNK = 128
ATTN_HEADS = 8
HEAD_DIM = 128
ATTN_WIDTH = ATTN_HEADS * HEAD_DIM
Q_LORA_RANK = 512
IDX_HEADS = 16
IDX_HEAD_DIM = 64
IDX_ROPE_DIM = 32
INDEX_TOPK = 256
ROPE_THETA = 10000.0
N_MOD = 6
NORM_EPS = 1e-6

COL_QLAT = 2 * GMLP_WIDTH
COL_K = COL_QLAT + Q_LORA_RANK
COL_V = COL_K + ATTN_WIDTH
COL_KIDX = COL_V + ATTN_WIDTH
COL_IDXW = COL_KIDX + IDX_HEAD_DIM
COL_GATE = COL_IDXW + IDX_HEADS

LANES = 128
SUBLANES = 8
VMEM_LIMIT = 56 * 1024 * 1024

MASK_BIAS = -(2.0 ** 100)
M_INIT = -(2.0 ** 99)
INT_MIN = -2 ** 31
KEY_NEG_INF = -2139095041

SEL_TQ = 256
KEY_BITS = 32
PLANE_KEYS = KEY_BITS * SUBLANES
ATT_KC = 2 * PLANE_KEYS
ATT_STEP_CHUNKS = 2
EPILOGUE_SUB_N = 512
EPILOGUE_SUB_M = 256
ROPE_BLOCK = 64

MM_TM = 1024
MM_TN = 1024
ROWNORM_TM = 512
FFN_UP_TN = 512
FFN_DOWN_TM = 256
QIDX_TN = 512
STREAM_TM = 1024
MOD_TN = 1024


def _params(sem=None):
    return pltpu.CompilerParams(dimension_semantics=sem, vmem_limit_bytes=VMEM_LIMIT)


def _mod_body(cb_ref, w_ref, b_ref, o_ref):
    cb = cb_ref[...]
    tn = o_ref.shape[1]
    parts = [jnp.sum(w_ref[:, p * LANES:(p + 1) * LANES] * cb, axis=0, keepdims=True)
             for p in range(tn // LANES)]
    o_ref[...] = jnp.concatenate(parts, axis=1) + b_ref[...]


def _modulation(c, w_mod, b_mod):
    k, n = w_mod.shape
    tn = MOD_TN
    cb = jnp.broadcast_to(c.reshape(k, 1), (k, LANES))
    return pl.pallas_call(
        _mod_body,
        grid=(n // tn,),
        in_specs=[pl.BlockSpec((k, LANES), lambda j: (0, 0)),
                  pl.BlockSpec((k, tn), lambda j: (0, j)),
                  pl.BlockSpec((1, tn), lambda j: (0, j))],
        out_specs=pl.BlockSpec((1, tn), lambda j: (0, j)),
        out_shape=jax.ShapeDtypeStruct((1, n), F32),
        compiler_params=_params(("arbitrary",)),
        name="modulation",
    )(cb, w_mod, b_mod.reshape(1, n))


def _rms(x, g):
    return x * lax.rsqrt(jnp.mean(x * x, axis=-1, keepdims=True) + NORM_EPS) * g


def _prenorm_body(x_ref, g_ref, scale_ref, shift_ref, o_ref):
    h = _rms(x_ref[...], g_ref[...]) * (1.0 + scale_ref[...]) + shift_ref[...]
    o_ref[...] = h.astype(o_ref.dtype)


def _prenorm(x, g, scale, shift):
    m, d = x.shape
    tm = STREAM_TM
    row = pl.BlockSpec((1, d), lambda i: (0, 0))
    return pl.pallas_call(
        _prenorm_body,
        grid=(m // tm,),
        in_specs=[pl.BlockSpec((tm, d), lambda i: (i, 0)), row, row, row],
        out_specs=pl.BlockSpec((tm, d), lambda i: (i, 0)),
        out_shape=jax.ShapeDtypeStruct((m, d), BF16),
        compiler_params=_params(("arbitrary",)),
        name="prenorm",
    )(x, g, scale, shift)


def _mm_body(*refs, n_x, pairs, w_kn, out_t, sub_m, sub_n, n_extra, n_out, epilogue):
    n_w = len(pairs)
    x_refs = refs[:n_x]
    w_refs = refs[n_x:n_x + n_w]
    e_refs = refs[n_x + n_w:n_x + n_w + n_extra]
    o_refs = refs[n_x + n_w + n_extra:n_x + n_w + n_extra + n_out]
    wb_refs = refs[n_x + n_w + n_extra + n_out:]

    @pl.when(pl.program_id(1) == 0)
    def _():
        for w_ref, wb_ref, kn in zip(w_refs, wb_refs, w_kn):
            w = w_ref[...]
            if not kn and not out_t:
                w = w.T
            wb_ref[...] = w.astype(wb_ref.dtype)

    if out_t:
        accs = [lax.dot_general(wb_ref[...], x_refs[xi][...], (((1,), (1,)), ((), ())),
                                preferred_element_type=F32)
                for xi, wb_ref in zip(pairs, wb_refs)]
        epilogue(accs, e_refs, o_refs, slice(0, wb_refs[0].shape[0]), slice(0, x_refs[0].shape[0]))
        return
    tm, tn = x_refs[0].shape[0], wb_refs[0].shape[1]
    for r0 in range(0, tm, sub_m):
        rs = slice(r0, r0 + sub_m)
        for c0 in range(0, tn, sub_n):
            cs = slice(c0, c0 + sub_n)
            accs = [jnp.dot(x_refs[xi][rs, :], wb_ref[:, cs], preferred_element_type=F32)
                    for xi, wb_ref in zip(pairs, wb_refs)]
            epilogue(accs, e_refs, o_refs, cs, rs)


def _matmul(xs, ws, pairs, extras, outs, epilogue, *, n, tm, tn, name, out_t=False, sub_n=None,
            sub_m=None):
    m = xs[0].shape[0]
    n_col = n // tn
    w_mode = dict(pipeline_mode=pl.Buffered(1)) if n_col == 1 else {}
    in_specs = [pl.BlockSpec((tm, x.shape[1]), lambda j, i: (i, 0)) for x in xs]
    scratch = []
    for w, first, kn in ws:
        assert not (kn and out_t)
        if kn:
            kdim = w.shape[0]
            assert first % tn == 0
            in_specs.append(pl.BlockSpec((kdim, tn), functools.partial(_w_cols, first=first // tn), **w_mode))
        else:
            kdim = w.shape[1]
            in_specs.append(pl.BlockSpec((pl.Element(tn), pl.Element(kdim)),
                                         functools.partial(_w_rows, first=first, tn=tn), **w_mode))
        scratch.append(pltpu.VMEM((tn, kdim) if out_t else (kdim, tn), BF16))
    in_specs += [pl.BlockSpec(bs, im) for _, bs, im in extras]
    body = functools.partial(_mm_body, n_x=len(xs), pairs=tuple(pairs), w_kn=tuple(kn for _, _, kn in ws),
                             out_t=out_t, sub_m=sub_m or tm, sub_n=sub_n or tn, n_extra=len(extras),
                             n_out=len(outs), epilogue=epilogue)
    return pl.pallas_call(
        body,
        grid=(n_col, m // tm),
        in_specs=in_specs,
        out_specs=[pl.BlockSpec(bs, im) for _, _, bs, im in outs],
        out_shape=[jax.ShapeDtypeStruct(s, d) for s, d, _, _ in outs],
        scratch_shapes=scratch,
        compiler_params=_params(("arbitrary", "arbitrary")),
        name=name,
    )(*xs, *[w for w, _, _ in ws], *[a for a, _, _ in extras])


def _w_cols(j, i, *, first):
    return (0, first + j)


def _w_rows(j, i, *, first, tn):
    assert first % SUBLANES == 0 and tn % SUBLANES == 0
    return (pl.multiple_of(first + j * tn, SUBLANES), 0)


def _tile(j, i):
    return (i, j)


def _tile_t(j, i):
    return (j, i)


def _rowblk(j, i):
    return (i, 0)


def _colblk(j, i):
    return (0, j)


def _colblk_t(j, i):
    return (0, i)


def _ep_gelu(accs, e_refs, o_refs, cols, rows):
    o_refs[0][rows, cols] = jax.nn.gelu(accs[0]).astype(o_refs[0].dtype)


def _ep_gelu_layernorm(accs, e_refs, o_refs, cols, rows):
    z = jax.nn.gelu(accs[0])
    mu = jnp.mean(z, axis=-1, keepdims=True)
    zc = z - mu
    var = jnp.mean(zc * zc, axis=-1, keepdims=True)
    y = zc * lax.rsqrt(var + NORM_EPS) * e_refs[0][...] + e_refs[1][...]
    o_refs[0][rows, :] = y.astype(o_refs[0].dtype)


def _ep_rmsnorm(accs, e_refs, o_refs, cols, rows):
    o_refs[0][rows, :] = _rms(accs[0], e_refs[0][...]).astype(o_refs[0].dtype)


def _ep_cast(accs, e_refs, o_refs, cols, rows):
    o_refs[0][rows, cols] = accs[0].astype(o_refs[0].dtype)


def _ep_sigmoid(accs, e_refs, o_refs, cols, rows):
    o_refs[0][rows, cols] = (0.5 * jnp.tanh(0.5 * accs[0]) + 0.5).astype(o_refs[0].dtype)


def _ep_rope(accs, e_refs, o_refs, cols, rows, *, scale):
    cc = e_refs[0][rows, :]
    ss = e_refs[1][rows, :]
    acc = accs[0]
    for h in range(acc.shape[1] // HEAD_DIM):
        xh = acc[:, h * HEAD_DIM:(h + 1) * HEAD_DIM]
        r = xh * cc + pltpu.roll(xh, HEAD_DIM // 2, 1) * ss
        if scale != 1.0:
            r = r * scale
        c0 = cols.start + h * HEAD_DIM
        o_refs[0][rows, c0:c0 + HEAD_DIM] = r.astype(o_refs[0].dtype)


def _ep_partial_rope_t(accs, e_refs, o_refs, cols, rows):
    c = e_refs[0][...]
    s = e_refs[1][...]
    acc = accs[0]
    half = IDX_ROPE_DIM // 2
    for h in range(acc.shape[0] // IDX_HEAD_DIM):
        r0 = h * IDX_HEAD_DIM
        x1 = acc[r0:r0 + half, :]
        x2 = acc[r0 + half:r0 + 2 * half, :]
        o_refs[0][r0:r0 + half, :] = (x1 * c - x2 * s).astype(o_refs[0].dtype)
        o_refs[0][r0 + half:r0 + 2 * half, :] = (x2 * c + x1 * s).astype(o_refs[0].dtype)
        o_refs[0][r0 + 2 * half:r0 + IDX_HEAD_DIM, :] = (
            acc[r0 + 2 * half:r0 + IDX_HEAD_DIM, :].astype(o_refs[0].dtype))


def _partial_rope(x, c, a, b):
    half = IDX_ROPE_DIM // 2
    return x * c + pltpu.roll(x, LANES - half, 1) * a + pltpu.roll(x, half, 1) * b


def _ep_index_keys(accs, e_refs, o_refs, cols, rows, *, w_scale):
    g, bb = e_refs[0][...], e_refs[1][...]
    cos, sin = e_refs[2][rows, :], e_refs[3][rows, :]
    acc = accs[0]
    n, half = acc.shape[0], IDX_ROPE_DIM // 2
    c = jnp.concatenate([cos, cos, jnp.ones((n, LANES - 2 * half), F32)], axis=1)
    a = jnp.concatenate([-sin, jnp.zeros((n, LANES - half), F32)], axis=1)
    b = jnp.concatenate([jnp.zeros((n, half), F32), sin, jnp.zeros((n, LANES - 2 * half), F32)], axis=1)
    lane = lax.broadcasted_iota(jnp.int32, acc.shape, 1)
    is_key = lane < IDX_HEAD_DIM
    mu = jnp.sum(jnp.where(is_key, acc, 0.0), axis=-1, keepdims=True) / IDX_HEAD_DIM
    xc = jnp.where(is_key, acc - mu, 0.0)
    var = jnp.sum(xc * xc, axis=-1, keepdims=True) / IDX_HEAD_DIM
    y = xc * lax.rsqrt(var + NORM_EPS) * g + bb
    y = jnp.where(is_key, _partial_rope(y, c, a, b), 0.0)
    o_refs[0][rows, :] = acc * w_scale
    o_refs[1][rows, :] = y.astype(o_refs[1].dtype)
    o_refs[2][rows, :] = pltpu.roll(y, IDX_HEAD_DIM, 1).astype(o_refs[2].dtype)


def _ep_merge(accs, e_refs, o_refs, cols, rows):
    o = e_refs[0][rows, cols].astype(F32) * accs[0] + e_refs[1][rows, cols].astype(F32) * accs[1]
    o_refs[0][rows, cols] = o.astype(o_refs[0].dtype)


def _ep_mix_residual(accs, e_refs, o_refs, cols, rows):
    x_ref, g_post, gate, g_pre, scale, shift = e_refs
    x1 = x_ref[rows, :] + gate[...] * _rms(accs[0], g_post[...])
    o_refs[0][rows, :] = x1
    h2 = _rms(x1, g_pre[...]) * (1.0 + scale[...]) + shift[...]
    o_refs[1][rows, :] = h2.astype(o_refs[1].dtype)


def _ep_swiglu(accs, e_refs, o_refs, cols, rows):
    o_refs[0][rows, cols] = (jax.nn.silu(accs[0]) * accs[1]).astype(o_refs[0].dtype)

    if cols.start == 0 and rows.start == 0:
        o_refs[1][...] = e_refs[0][...].astype(o_refs[1].dtype)


def _simple_mm(x, w, epilogue, extras, out_dtype, *, n, tm, tn, name, first=0, kn=True, out_t=False,
               sub_n=None, sub_m=None):
    m = x.shape[0]
    out = ((n, m), out_dtype, (tn, tm), _tile_t) if out_t else ((m, n), out_dtype, (tm, tn), _tile)
    return _matmul([x], [(w, first, kn)], [0], extras, [out], epilogue, n=n, tm=tm, tn=tn,
                   name=name, out_t=out_t, sub_n=sub_n, sub_m=sub_m)[0]


def _gmlp_body(u_ref, v_ref, w_ref, b_ref, o_ref):
    t = lax.broadcasted_iota(jnp.int32, (CHUNK, CHUNK), 0)
    s = lax.broadcasted_iota(jnp.int32, (CHUNK, CHUNK), 1)
    causal = s <= t
    for g in range(GMLP_GROUPS):
        w = jnp.where(causal, w_ref[g], 0.0).astype(BF16)
        cols = slice(g * LANES, (g + 1) * LANES)
        bias = b_ref[:, cols]
        for c in range(u_ref.shape[0] // CHUNK):
            rows = slice(c * CHUNK, (c + 1) * CHUNK)
            sv = jnp.dot(w, v_ref[rows, cols], preferred_element_type=F32) + bias
            o_ref[rows, cols] = (u_ref[rows, cols].astype(F32) * sv).astype(o_ref.dtype)


def _gmlp(u, vn, w_s, bias):
    m, n = u.shape
    tm = STREAM_TM
    blk = pl.BlockSpec((tm, n), lambda i: (i, 0))
    return pl.pallas_call(
        _gmlp_body,
        grid=(m // tm,),
        in_specs=[blk, blk,
                  pl.BlockSpec(w_s.shape, lambda i: (0, 0, 0)),
                  pl.BlockSpec(bias.shape, lambda i: (0, 0))],
        out_specs=blk,
        out_shape=jax.ShapeDtypeStruct((m, n), BF16),
        compiler_params=_params(("arbitrary",)),
        name="gmlp_gating",
    )(u, vn, w_s, bias)


def _bit_planes(words):
    a = list(words)
    j, mask = KEY_BITS // 2, 0x0000FFFF
    while j:
        for k in range(KEY_BITS):
            if k & j:
                continue
            t = (a[k] ^ (a[k + j] >> j)) & jnp.int32(mask)
            a[k] = a[k] ^ t
            a[k + j] = a[k + j] ^ (t << j)
        j //= 2
        mask = (mask ^ (mask << j)) & 0xFFFFFFFF
    return a


def _score_key(score):
    bits = lax.bitcast_convert_type(score, jnp.int32)
    return bits ^ ((bits >> 31) & jnp.int32(0x7FFFFFFF))


def _key_score(key):
    return lax.bitcast_convert_type(key ^ ((key >> 31) & jnp.int32(0x7FFFFFFF)), F32)


def _indexer_body(qt_ref, wt_ref, klo_ref, khi_ref, bias_ref, score_ref, plane_ref, live_ref, kth_ref):
    tq = qt_ref.shape[1]
    n_chunks_total = bias_ref.shape[0]
    i = pl.program_id(0)
    n_chunks = ((i + 1) * tq + ATT_KC - 1) // ATT_KC
    blocks_per_chunk = ATT_KC // PLANE_KEYS
    n_blocks = n_chunks * blocks_per_chunk
    heads_per_group = LANES // IDX_HEAD_DIM
    k_refs = (klo_ref, khi_ref)

    @pl.when(i == 0)
    def _():
        plane_ref[...] = jnp.zeros(plane_ref.shape, jnp.int32)

    q_pos = i * tq + lax.broadcasted_iota(jnp.int32, (ATT_KC, tq), 1)
    k_off = lax.broadcasted_iota(jnp.int32, (ATT_KC, tq), 0)

    def score_chunk(c):
        k0 = pl.multiple_of(c * ATT_KC, ATT_KC)
        score = jnp.zeros((ATT_KC, tq), F32)
        for h in range(IDX_HEADS):
            grp, r = divmod(h, heads_per_group)
            logit = jnp.dot(k_refs[r][pl.ds(k0, ATT_KC), :], qt_ref[grp * LANES:(grp + 1) * LANES, :],
                            preferred_element_type=F32)
            score = score + jnp.maximum(logit, 0.0) * wt_ref[h:h + 1, :]
        score_ref[pl.ds(k0, ATT_KC), :] = jnp.where(k0 + k_off <= q_pos, score, -jnp.inf)

    def slice_chunk(c):
        for hh in range(blocks_per_chunk):
            blk = c * blocks_per_chunk + hh
            k0 = pl.multiple_of(blk * PLANE_KEYS, PLANE_KEYS)
            bits = lax.bitcast_convert_type(score_ref[pl.ds(k0, PLANE_KEYS), :], jnp.int32)
            planes = _bit_planes([bits[w * SUBLANES:(w + 1) * SUBLANES, :] for w in range(KEY_BITS)])
            r0 = pl.multiple_of(blk * SUBLANES, SUBLANES)
            sign = planes[0]
            plane_ref[0, pl.ds(r0, SUBLANES), :] = ~sign
            for r in range(1, KEY_BITS):
                plane_ref[r, pl.ds(r0, SUBLANES), :] = planes[r] ^ sign

    def score_and_slice(c, carry):
        slice_chunk(c - 1)
        score_chunk(c)
        return carry

    score_chunk(0)
    lax.fori_loop(1, n_chunks, score_and_slice, 0)
    slice_chunk(n_chunks - 1)

    def count_bits(words):
        ones = lax.population_count(words)
        part = jnp.sum(ones.reshape(words.shape[0] // SUBLANES, SUBLANES, tq), axis=0)
        return jnp.sum(part.astype(F32), axis=0, keepdims=True)

    def radix_select(rows):
        row_id = lax.broadcasted_iota(jnp.int32, (rows, tq), 0)
        live_ref[:rows] = jnp.where(row_id < n_blocks * SUBLANES, jnp.int32(-1), jnp.int32(0))

        def search_bit(r, carry):
            need, kth = carry
            total = count_bits(live_ref[:rows] & plane_ref[r, :rows])
            take = total >= need
            keep_clear = jnp.where(take, jnp.int32(0), jnp.int32(-1))
            live_ref[:rows] = live_ref[:rows] & (plane_ref[r, :rows] ^ keep_clear)
            need = jnp.where(take, need, need - total)
            kth = jnp.where(take, kth | jnp.left_shift(jnp.int32(1), KEY_BITS - 1 - r), kth)
            return need, kth

        _, kth = lax.fori_loop(0, KEY_BITS, search_bit,
                               (jnp.full((1, tq), float(INDEX_TOPK), F32), jnp.zeros((1, tq), jnp.int32)))
        kth_ref[0:1] = kth ^ jnp.int32(INT_MIN)

    plane_rows = live_ref.shape[0]
    quarter = plane_rows // 4
    for rows in range(quarter, plane_rows + 1, quarter):
        @pl.when((n_blocks * SUBLANES > rows - quarter) & (n_blocks * SUBLANES <= rows))
        def _():
            radix_select(rows)

    def count_scores(pred):
        group = 4 * SUBLANES

        def add_chunk(c, acc):
            k0 = pl.multiple_of(c * ATT_KC, ATT_KC)
            hit = jnp.where(pred(score_ref[pl.ds(k0, ATT_KC), :]), 1.0, 0.0)
            return acc + jnp.sum(hit.reshape(ATT_KC // group, group, tq), axis=0)

        acc = lax.fori_loop(0, n_chunks, add_chunk, jnp.zeros((group, tq), F32))
        return jnp.sum(acc, axis=0, keepdims=True)

    few_keys = i * tq + lax.broadcasted_iota(jnp.int32, (1, tq), 1) + 1 < INDEX_TOPK
    lowest = float(jnp.finfo(F32).min)
    thr = jnp.where(few_keys, lowest, _key_score(jnp.maximum(kth_ref[0:1], KEY_NEG_INF + 1)))

    def bias_tile(sel):
        return jnp.where(sel, 0.0, MASK_BIAS).T

    def write_chunk(c, kept):
        parts = []
        for hh in range(blocks_per_chunk):
            k0 = pl.multiple_of(c * ATT_KC + hh * PLANE_KEYS, PLANE_KEYS)
            sel = score_ref[pl.ds(k0, PLANE_KEYS), :] >= thr
            parts.append(bias_tile(sel))
            hit = jnp.where(sel, 1.0, 0.0)
            kept = kept + jnp.sum(hit.reshape(PLANE_KEYS // kept.shape[0], kept.shape[0], tq), axis=0)
        bias_ref[c] = jnp.concatenate(parts, axis=1).astype(bias_ref.dtype)
        return kept

    kept = lax.fori_loop(0, n_chunks, write_chunk, jnp.zeros((4 * SUBLANES, tq), F32))
    kept = jnp.sum(kept, axis=0, keepdims=True)
    unsettled = jnp.max(jnp.where(few_keys | (kept == float(INDEX_TOPK)), 0.0, 1.0))

    @pl.when(unsettled > 0.0)
    def _():
        def search_bit(b, prefix):
            cand = prefix ^ jnp.left_shift(jnp.int32(1), KEY_BITS - 1 - b)
            cand_score = _key_score(jnp.maximum(cand, KEY_NEG_INF))
            admitted = count_scores(lambda sc: sc >= cand_score)
            return jnp.where(admitted >= float(INDEX_TOPK), cand, prefix)

        kth = lax.fori_loop(0, KEY_BITS, search_bit, jnp.full((1, tq), INT_MIN, jnp.int32))
        kth_score = jnp.where(few_keys, lowest, _key_score(jnp.maximum(kth, KEY_NEG_INF + 1)))
        above = count_scores(lambda sc: sc > kth_score)
        need = jnp.where(few_keys, float(2 ** 24), float(INDEX_TOPK) - above)

        row = lax.broadcasted_iota(jnp.int32, (PLANE_KEYS, PLANE_KEYS), 0)
        col = lax.broadcasted_iota(jnp.int32, (PLANE_KEYS, PLANE_KEYS), 1)
        prefix_sum = jnp.where(col <= row, 1.0, 0.0).astype(BF16)

        def write_chunk(c, seen):
            parts = []
            for hh in range(blocks_per_chunk):
                k0 = pl.multiple_of(c * ATT_KC + hh * PLANE_KEYS, PLANE_KEYS)
                sc = score_ref[pl.ds(k0, PLANE_KEYS), :]
                tie = sc == kth_score
                rank = seen + jnp.dot(prefix_sum, jnp.where(tie, 1.0, 0.0).astype(BF16),
                                      preferred_element_type=F32)
                parts.append(bias_tile((sc > kth_score) | (tie & (rank <= need))))
                seen = rank[PLANE_KEYS - 1:PLANE_KEYS, :]
            bias_ref[c] = jnp.concatenate(parts, axis=1).astype(bias_ref.dtype)
            return seen

        lax.fori_loop(0, n_chunks, write_chunk, jnp.zeros((1, tq), F32))

    def fill_chunk(c, carry):
        bias_ref[c] = jnp.full(bias_ref.shape[1:], MASK_BIAS, bias_ref.dtype)
        return carry

    lax.fori_loop(n_chunks, n_chunks_total, fill_chunk, 0)


def _indexer_mask(q_idx_t, idx_w_t, k_lo, k_hi):
    s = q_idx_t.shape[1]
    tq = SEL_TQ
    assert tq % PLANE_KEYS == 0 and s % (KEY_BITS * SUBLANES) == 0
    return pl.pallas_call(
        _indexer_body,
        grid=(s // tq,),
        in_specs=[pl.BlockSpec((q_idx_t.shape[0], tq), lambda i: (0, i)),
                  pl.BlockSpec((IDX_HEADS, tq), lambda i: (0, i)),
                  pl.BlockSpec(k_lo.shape, lambda i: (0, 0)),
                  pl.BlockSpec(k_hi.shape, lambda i: (0, 0))],
        out_specs=pl.BlockSpec((s // ATT_KC, tq, ATT_KC), lambda i: (0, i, 0)),
        out_shape=jax.ShapeDtypeStruct((s // ATT_KC, s, ATT_KC), BF16),
        scratch_shapes=[pltpu.VMEM((s, tq), F32),
                        pltpu.VMEM((KEY_BITS, s // KEY_BITS, tq), jnp.int32),
                        pltpu.VMEM((s // KEY_BITS, tq), jnp.int32),
                        pltpu.VMEM((SUBLANES, tq), jnp.int32)],
        compiler_params=_params(("arbitrary",)),
        name="indexer_mask",
    )(q_idx_t, idx_w_t, k_lo, k_hi)


def _attention_body(q_ref, k_ref, v_ref, bias_ref, o_ref, m_ref, acc_ref):
    tq = q_ref.shape[0]
    chunk_keys = bias_ref.shape[2]
    i = pl.program_id(0)
    n_chunks = (i * tq + tq - 1) // chunk_keys + 1

    m_ref[...] = jnp.full(m_ref.shape, M_INIT, F32)
    acc_ref[...] = jnp.zeros(acc_ref.shape, F32)

    def step(first_chunk, width):
        kc = width * chunk_keys
        sub = kc // LANES
        k0 = pl.multiple_of(first_chunk * chunk_keys, chunk_keys)
        bias = jnp.concatenate([bias_ref[first_chunk + c] for c in range(width)], axis=1)
        ones = jnp.ones((kc, HEAD_DIM), v_ref.dtype)
        for h in range(ATTN_HEADS):
            cols = slice(h * HEAD_DIM, (h + 1) * HEAD_DIM)
            s = lax.dot_general(q_ref[:, cols], k_ref[pl.ds(k0, kc), cols],
                                (((1,), (1,)), ((), ())), preferred_element_type=F32)
            sb = s.astype(BF16) + bias
            part = sb[:, :LANES]
            for c in range(1, sub):
                part = jnp.maximum(part, sb[:, c * LANES:(c + 1) * LANES])
            m_prev = m_ref[h]
            m_next = jnp.maximum(m_prev, jnp.max(part.astype(F32), axis=1, keepdims=True))
            p = jnp.exp2(sb - jnp.concatenate([m_next.astype(BF16)] * sub, axis=1))
            alpha = jnp.exp2(m_prev - m_next)
            m_ref[h] = m_next
            pv = jnp.dot(p, jnp.concatenate([v_ref[pl.ds(k0, kc), cols], ones], axis=1),
                         preferred_element_type=F32)
            acc_ref[h] = jnp.concatenate([alpha, alpha], axis=1) * acc_ref[h] + pv

    def wide_step(j, carry):
        step(j * ATT_STEP_CHUNKS, ATT_STEP_CHUNKS)
        return carry

    n_wide = n_chunks // ATT_STEP_CHUNKS
    lax.fori_loop(0, n_wide, wide_step, 0)
    for r in range(1, ATT_STEP_CHUNKS):
        @pl.when(n_chunks % ATT_STEP_CHUNKS >= r)
        def _():
            step(n_wide * ATT_STEP_CHUNKS + r - 1, 1)

    for h in range(ATTN_HEADS):
        acc = acc_ref[h]
        o_ref[:, h * HEAD_DIM:(h + 1) * HEAD_DIM] = (acc[:, :HEAD_DIM] / acc[:, HEAD_DIM:]).astype(o_ref.dtype)


def _attention(q, k, v, bias):
    s, width = q.shape
    tq = SEL_TQ
    n_chunks, _, kc = bias.shape
    resident = dict(pipeline_mode=pl.Buffered(1))
    return pl.pallas_call(
        _attention_body,
        grid=(s // tq,),
        in_specs=[pl.BlockSpec((tq, width), lambda i: (i, 0)),
                  pl.BlockSpec((s, width), lambda i: (0, 0), **resident),
                  pl.BlockSpec((s, width), lambda i: (0, 0), **resident),
                  pl.BlockSpec((n_chunks, tq, kc), lambda i: (0, i, 0))],
        out_specs=pl.BlockSpec((tq, width), lambda i: (i, 0)),
        out_shape=jax.ShapeDtypeStruct((s, width), BF16),
        scratch_shapes=[pltpu.VMEM((ATTN_HEADS, tq, LANES), F32),
                        pltpu.VMEM((ATTN_HEADS, tq, 2 * HEAD_DIM), F32)],
        compiler_params=_params(("arbitrary",)),
        name="masked_attention",
    )(q, k, v, bias)


def _ffn_down_body(a_ref, w_ref, x_ref, g_ref, gate_ref, o_ref):
    f = jnp.dot(a_ref[...], w_ref[...], preferred_element_type=F32)
    o_ref[...] = x_ref[...] + gate_ref[...] * _rms(f, g_ref[...])


def _ffn_down(a, w, x1, g_post, gate):
    m, kdim = a.shape
    d = w.shape[1]
    tm = FFN_DOWN_TM
    row = pl.BlockSpec((1, d), lambda i: (0, 0))
    return pl.pallas_call(
        _ffn_down_body,
        grid=(m // tm,),
        in_specs=[pl.BlockSpec((tm, kdim), lambda i: (i, 0)),
                  pl.BlockSpec((kdim, d), lambda i: (0, 0), pipeline_mode=pl.Buffered(1)),
                  pl.BlockSpec((tm, d), lambda i: (i, 0)), row, row],
        out_specs=pl.BlockSpec((tm, d), lambda i: (i, 0)),
        out_shape=jax.ShapeDtypeStruct((m, d), F32),
        compiler_params=_params(("arbitrary",)),
        name="ffn_down_residual",
    )(a, w, x1, g_post, gate)


def _rope_angles(seq, dim):
    inv = 1.0 / (ROPE_THETA ** (jnp.arange(0, dim, 2, dtype=F32) / dim))
    coarse = (jnp.arange(seq // ROPE_BLOCK, dtype=F32) * ROPE_BLOCK)[:, None] * inv[None, :]
    fine = jnp.arange(ROPE_BLOCK, dtype=F32)[:, None] * inv[None, :]
    cos_a, sin_a = jnp.cos(coarse)[:, None, :], jnp.sin(coarse)[:, None, :]
    cos_b, sin_b = jnp.cos(fine)[None, :, :], jnp.sin(fine)[None, :, :]
    cos = (cos_a * cos_b - sin_a * sin_b).reshape(seq, dim // 2)
    sin = (sin_a * cos_b + cos_a * sin_b).reshape(seq, dim // 2)
    return cos, sin


def _attn_rope_tables(seq):
    cos, sin = _rope_angles(seq, HEAD_DIM)
    return jnp.concatenate([cos, cos], axis=1), jnp.concatenate([-sin, sin], axis=1)


def _index_rope_tables(seq):
    cos, sin = _rope_angles(seq, IDX_ROPE_DIM)
    return cos, sin, cos.T, sin.T


def _layer(x, mod, g_pre_mix, g_post_mix, w_in_t, gmlp_ln_g, gmlp_ln_b, gmlp_w_s, gmlp_b_s,
           q_lat_norm_g, w_q_up, w_qidx_up, kidx_ln_g, kidx_ln_b, w_proj_a, w_proj_b, w_out,
           g_pre_ffn, g_post_ffn, w_ffn_gate, w_ffn_up, w_ffn_down, tables):
    s, d = x.shape
    cc, ss, icos, isin, icos_t, isin_t = tables
    row = lambda a: a.reshape(1, -1)
    shift_m, scale_m, gate_m, shift_f, scale_f, gate_f = (mod[:, n * d:(n + 1) * d] for n in range(N_MOD))

    h = _prenorm(x, row(g_pre_mix), scale_m, shift_m)

    tm = MM_TM
    in_proj = functools.partial(_simple_mm, h, w_in_t, kn=False)

    u = in_proj(_ep_gelu, [], BF16, n=GMLP_WIDTH, tm=tm, tn=GMLP_WIDTH, sub_n=EPILOGUE_SUB_N,
                name="in_proj_u")
    vn = in_proj(_ep_gelu_layernorm,
                 [(row(gmlp_ln_g), (1, GMLP_WIDTH), _colblk), (row(gmlp_ln_b), (1, GMLP_WIDTH), _colblk)],
                 BF16, n=GMLP_WIDTH, first=GMLP_WIDTH, tm=ROWNORM_TM, tn=GMLP_WIDTH, sub_m=EPILOGUE_SUB_M,
                 name="in_proj_v_ln")
    gmlp_bias = jnp.repeat(gmlp_b_s.T, LANES, axis=1)
    y_a = _gmlp(u, vn, gmlp_w_s, gmlp_bias)

    q_lat = in_proj(_ep_rmsnorm, [(row(q_lat_norm_g), (1, Q_LORA_RANK), _colblk)],
                    BF16, n=Q_LORA_RANK, first=COL_QLAT, tm=tm, tn=Q_LORA_RANK, name="in_proj_qlat")
    q_scale = HEAD_DIM ** -0.5 * math.log2(math.e)
    rope_extras = [(cc, (tm, HEAD_DIM), _rowblk), (ss, (tm, HEAD_DIM), _rowblk)]
    q = _simple_mm(q_lat, w_q_up, functools.partial(_ep_rope, scale=q_scale), rope_extras,
                   BF16, n=ATTN_WIDTH, tm=tm, tn=ATTN_WIDTH, sub_n=EPILOGUE_SUB_N, name="q_up_rope")
    k = in_proj(functools.partial(_ep_rope, scale=1.0), rope_extras, BF16,
                n=ATTN_WIDTH, first=COL_K, tm=tm, tn=ATTN_WIDTH, sub_n=EPILOGUE_SUB_N,
                name="in_proj_k_rope")
    v = in_proj(_ep_cast, [], BF16, n=ATTN_WIDTH, first=COL_V, tm=tm, tn=ATTN_WIDTH,
                sub_n=EPILOGUE_SUB_N, name="in_proj_v")
    half = IDX_ROPE_DIM // 2
    q_idx_t = _simple_mm(q_lat, w_qidx_up.T, _ep_partial_rope_t,
                         [(icos_t, (half, tm), _colblk_t), (isin_t, (half, tm), _colblk_t)],
                         BF16, n=IDX_HEADS * IDX_HEAD_DIM, tm=tm, tn=QIDX_TN, name="qidx_up_rope",
                         kn=False, out_t=True)
    pad = lambda a: jnp.pad(a, (0, LANES - a.shape[0])).reshape(1, LANES)
    idx_w_scale = (IDX_HEADS ** -0.5) * (IDX_HEAD_DIM ** -0.5)
    lane_tile = lambda dtype: ((s, LANES), dtype, (tm, LANES), _tile)
    kx, k_lo, k_hi = _matmul(
        [h], [(w_in_t, COL_KIDX, False)], [0],
        [(pad(kidx_ln_g), (1, LANES), _colblk), (pad(kidx_ln_b), (1, LANES), _colblk),
         (icos, (tm, half), _rowblk), (isin, (tm, half), _rowblk)],
        [lane_tile(F32), lane_tile(BF16), lane_tile(BF16)],
        functools.partial(_ep_index_keys, w_scale=idx_w_scale), n=LANES, tm=tm, tn=LANES,
        name="in_proj_index_keys")
    idx_w_t = kx[:, IDX_HEAD_DIM:IDX_HEAD_DIM + IDX_HEADS].T
    bias = _indexer_mask(q_idx_t, idx_w_t, k_lo, k_hi)
    y_b = _attention(q, k, v, bias)

    gates = in_proj(_ep_sigmoid, [], BF16, n=2 * d, first=COL_GATE, tm=tm, tn=MM_TN,
                    sub_n=EPILOGUE_SUB_N, name="in_proj_gates")
    n_gate_blocks = d // MM_TN
    merged = _matmul([y_a, y_b], [(w_proj_a, 0, True), (w_proj_b, 0, True)], [0, 1],
                     [(gates, (tm, MM_TN), _tile),
                      (gates, (tm, MM_TN), lambda j, i: (i, j + n_gate_blocks))],
                     [((s, d), BF16, (tm, MM_TN), _tile)], _ep_merge, n=d, tm=tm, tn=MM_TN,
                     sub_n=EPILOGUE_SUB_N, name="branch_merge")[0]
    tm_full = ROWNORM_TM
    vec = lambda a: (a, (1, d), _colblk)
    x1, h2 = _matmul([merged], [(w_out, 0, True)], [0],
                     [(x, (tm_full, d), _rowblk), vec(row(g_post_mix)), vec(gate_m),
                      vec(row(g_pre_ffn)), vec(scale_f), vec(shift_f)],
                     [((s, d), F32, (tm_full, d), _tile), ((s, d), BF16, (tm_full, d), _tile)],
                     _ep_mix_residual, n=d, tm=tm_full, tn=d, sub_m=EPILOGUE_SUB_M, name="out_proj_residual")

    ffn_hidden = w_ffn_gate.shape[1]
    tn_ffn = FFN_UP_TN
    n_row_blocks = s // tm
    down_rows = ffn_hidden // ((ffn_hidden // tn_ffn) * n_row_blocks)
    down_blk = ((down_rows, d), lambda j, i: (j * n_row_blocks + i, 0))
    act, w_down = _matmul([h2], [(w_ffn_gate, 0, True), (w_ffn_up, 0, True)], [0, 0],
                          [(w_ffn_down,) + down_blk],
                          [((s, ffn_hidden), BF16, (tm, tn_ffn), _tile), (w_ffn_down.shape, BF16) + down_blk],
                          _ep_swiglu, n=ffn_hidden, tm=tm, tn=tn_ffn, name="ffn_up")
    return _ffn_down(act, w_down, x1, row(g_post_ffn), gate_f)


def kernel(x, c, w_mod, b_mod, g_pre_mix, g_post_mix, w_in, gmlp_ln_g, gmlp_ln_b, gmlp_w_s, gmlp_b_s, q_lat_norm_g, w_q_up, w_qidx_up, kidx_ln_g, kidx_ln_b, w_proj_a, w_proj_b, w_out, g_pre_ffn, g_post_ffn, w_ffn_gate, w_ffn_up, w_ffn_down):
    batch, seq, d = x.shape
    assert batch == 1 and d == D_MODEL
    tables = _attn_rope_tables(seq) + _index_rope_tables(seq)
    y = x[0]
    for l in range(w_mod.shape[0]):
        mod = _modulation(c, w_mod[l], b_mod[l])
        y = _layer(y, mod, g_pre_mix[l], g_post_mix[l], w_in[l].T, gmlp_ln_g[l], gmlp_ln_b[l],
                   gmlp_w_s[l], gmlp_b_s[l], q_lat_norm_g[l], w_q_up[l], w_qidx_up[l],
                   kidx_ln_g[l], kidx_ln_b[l], w_proj_a[l], w_proj_b[l], w_out[l],
                   g_pre_ffn[l], g_post_ffn[l], w_ffn_gate[l], w_ffn_up[l], w_ffn_down[l], tables)
    return y[None]
```

```python
import functools
import math

import jax
import jax.numpy as jnp
from jax import lax
from jax.experimental import pallas as pl
from jax.experimental.pallas import tpu as pltpu

F32 = jnp.float32
BF16 = jnp.bfloat16

D_MODEL = 2048
GMLP_WIDTH = 1024
GMLP_GROUPS = 8
CHUNK = 128
ATTN_HEADS = 8
HEAD_DIM = 128
ATTN_WIDTH = ATTN_HEADS * HEAD_DIM
Q_LORA_RANK = 512
IDX_HEADS = 16
IDX_HEAD_DIM = 64
IDX_ROPE_DIM = 32
INDEX_TOPK = 256
ROPE_THETA = 10000.0
N_MOD = 6
NORM_EPS = 1e-6

COL_QLAT = 2 * GMLP_WIDTH
COL_K = COL_QLAT + Q_LORA_RANK
COL_V = COL_K + ATTN_WIDTH
COL_KIDX = COL_V + ATTN_WIDTH
COL_IDXW = COL_KIDX + IDX_HEAD_DIM
COL_GATE = COL_IDXW + IDX_HEADS

LANES = 128
SUBLANES = 8
VMEM_LIMIT = 56 * 1024 * 1024

MASK_BIAS = -(2.0 ** 100)
M_INIT = -(2.0 ** 99)
INT_MIN = -2 ** 31
KEY_NEG_INF = -2139095041

SEL_TQ = 256
KEY_BITS = 32
PLANE_KEYS = KEY_BITS * SUBLANES
ATT_KC = 2 * PLANE_KEYS
ATT_STEP_CHUNKS = 2
EPILOGUE_SUB_N = 512
EPILOGUE_SUB_M = 256
ROPE_BLOCK = 64

MM_TM = 1024
MM_TN = 1024
ROWNORM_TM = 512
FFN_UP_TN = 512
FFN_DOWN_TM = 256
QIDX_TN = 512
STREAM_TM = 1024
MOD_TN = 1024


def _params(sem=None):
    return pltpu.CompilerParams(dimension_semantics=sem, vmem_limit_bytes=VMEM_LIMIT)


def _mod_body(cb_ref, w_ref, b_ref, o_ref):
    cb = cb_ref[...]
    tn = o_ref.shape[1]
    parts = [jnp.sum(w_ref[:, p * LANES:(p + 1) * LANES] * cb, axis=0, keepdims=True)
             for p in range(tn // LANES)]
    o_ref[...] = jnp.concatenate(parts, axis=1) + b_ref[...]


def _modulation(c, w_mod, b_mod):
    k, n = w_mod.shape
    tn = MOD_TN
    cb = jnp.broadcast_to(c.reshape(k, 1), (k, LANES))
    return pl.pallas_call(
        _mod_body,
        grid=(n // tn,),
        in_specs=[pl.BlockSpec((k, LANES), lambda j: (0, 0)),
                  pl.BlockSpec((k, tn), lambda j: (0, j)),
                  pl.BlockSpec((1, tn), lambda j: (0, j))],
        out_specs=pl.BlockSpec((1, tn), lambda j: (0, j)),
        out_shape=jax.ShapeDtypeStruct((1, n), F32),
        compiler_params=_params(("arbitrary",)),
        name="modulation",
    )(cb, w_mod, b_mod.reshape(1, n))


def _rms(x, g):
    return x * lax.rsqrt(jnp.mean(x * x, axis=-1, keepdims=True) + NORM_EPS) * g


def _prenorm_body(x_ref, g_ref, scale_ref, shift_ref, o_ref):
    h = _rms(x_ref[...], g_ref[...]) * (1.0 + scale_ref[...]) + shift_ref[...]
    o_ref[...] = h.astype(o_ref.dtype)


def _prenorm(x, g, scale, shift):
    m, d = x.shape
    tm = STREAM_TM
    row = pl.BlockSpec((1, d), lambda i: (0, 0))
    return pl.pallas_call(
        _prenorm_body,
        grid=(m // tm,),
        in_specs=[pl.BlockSpec((tm, d), lambda i: (i, 0)), row, row, row],
        out_specs=pl.BlockSpec((tm, d), lambda i: (i, 0)),
        out_shape=jax.ShapeDtypeStruct((m, d), BF16),
        compiler_params=_params(("arbitrary",)),
        name="prenorm",
    )(x, g, scale, shift)


def _mm_body(*refs, n_x, pairs, w_kn, out_t, sub_m, sub_n, n_extra, n_out, epilogue):
    n_w = len(pairs)
    x_refs = refs[:n_x]
    w_refs = refs[n_x:n_x + n_w]
    e_refs = refs[n_x + n_w:n_x + n_w + n_extra]
    o_refs = refs[n_x + n_w + n_extra:n_x + n_w + n_extra + n_out]
    wb_refs = refs[n_x + n_w + n_extra + n_out:]

    @pl.when(pl.program_id(1) == 0)
    def _():
        for w_ref, wb_ref, kn in zip(w_refs, wb_refs, w_kn):
            w = w_ref[...]
            if kn == out_t:
                w = w.T
            wb_ref[...] = w.astype(wb_ref.dtype)

    if out_t:
        accs = [lax.dot_general(wb_ref[...], x_refs[xi][...], (((1,), (1,)), ((), ())),
                                preferred_element_type=F32)
                for xi, wb_ref in zip(pairs, wb_refs)]
        epilogue(accs, e_refs, o_refs, slice(0, wb_refs[0].shape[0]), slice(0, x_refs[0].shape[0]))
        return
    tm, tn = x_refs[0].shape[0], wb_refs[0].shape[1]
    for r0 in range(0, tm, sub_m):
        rs = slice(r0, r0 + sub_m)
        for c0 in range(0, tn, sub_n):
            cs = slice(c0, c0 + sub_n)
            accs = [jnp.dot(x_refs[xi][rs, :], wb_ref[:, cs], preferred_element_type=F32)
                    for xi, wb_ref in zip(pairs, wb_refs)]
            epilogue(accs, e_refs, o_refs, cs, rs)


def _matmul(xs, ws, pairs, extras, outs, epilogue, *, n, tm, tn, name, out_t=False, sub_n=None,
            sub_m=None):
    m = xs[0].shape[0]
    n_col = n // tn
    w_mode = dict(pipeline_mode=pl.Buffered(1)) if n_col == 1 else {}
    in_specs = [pl.BlockSpec((tm, x.shape[1]), lambda j, i: (i, 0)) for x in xs]
    scratch = []
    for w, first, kn in ws:
        if kn:
            kdim = w.shape[0]
            assert first % tn == 0
            in_specs.append(pl.BlockSpec((kdim, tn), functools.partial(_w_cols, first=first // tn), **w_mode))
        else:
            kdim = w.shape[1]
            in_specs.append(pl.BlockSpec((pl.Element(tn), pl.Element(kdim)),
                                         functools.partial(_w_rows, first=first, tn=tn), **w_mode))
        scratch.append(pltpu.VMEM((tn, kdim) if out_t else (kdim, tn), BF16))
    in_specs += [pl.BlockSpec(bs, im) for _, bs, im in extras]
    body = functools.partial(_mm_body, n_x=len(xs), pairs=tuple(pairs), w_kn=tuple(kn for _, _, kn in ws),
                             out_t=out_t, sub_m=sub_m or tm, sub_n=sub_n or tn, n_extra=len(extras),
                             n_out=len(outs), epilogue=epilogue)
    return pl.pallas_call(
        body,
        grid=(n_col, m // tm),
        in_specs=in_specs,
        out_specs=[pl.BlockSpec(bs, im) for _, _, bs, im in outs],
        out_shape=[jax.ShapeDtypeStruct(s, d) for s, d, _, _ in outs],
        scratch_shapes=scratch,
        compiler_params=_params(("arbitrary", "arbitrary")),
        name=name,
    )(*xs, *[w for w, _, _ in ws], *[a for a, _, _ in extras])


def _w_cols(j, i, *, first):
    return (0, first + j)


def _w_rows(j, i, *, first, tn):
    assert first % SUBLANES == 0 and tn % SUBLANES == 0
    return (pl.multiple_of(first + j * tn, SUBLANES), 0)


def _tile(j, i):
    return (i, j)


def _tile_t(j, i):
    return (j, i)


def _rowblk(j, i):
    return (i, 0)


def _colblk(j, i):
    return (0, j)


def _colblk_t(j, i):
    return (0, i)


def _ep_gelu(accs, e_refs, o_refs, cols, rows):
    o_refs[0][rows, cols] = jax.nn.gelu(accs[0]).astype(o_refs[0].dtype)


def _ep_gelu_layernorm(accs, e_refs, o_refs, cols, rows):
    z = jax.nn.gelu(accs[0])
    mu = jnp.mean(z, axis=-1, keepdims=True)
    zc = z - mu
    var = jnp.mean(zc * zc, axis=-1, keepdims=True)
    y = zc * lax.rsqrt(var + NORM_EPS) * e_refs[0][...] + e_refs[1][...]
    o_refs[0][rows, :] = y.astype(o_refs[0].dtype)


def _ep_rmsnorm(accs, e_refs, o_refs, cols, rows):
    o_refs[0][rows, :] = _rms(accs[0], e_refs[0][...]).astype(o_refs[0].dtype)


def _ep_cast(accs, e_refs, o_refs, cols, rows):
    o_refs[0][rows, cols] = accs[0].astype(o_refs[0].dtype)


def _ep_sigmoid(accs, e_refs, o_refs, cols, rows):
    o_refs[0][rows, cols] = (0.5 * jnp.tanh(0.5 * accs[0]) + 0.5).astype(o_refs[0].dtype)


def _ep_rope(accs, e_refs, o_refs, cols, rows, *, scale):
    cc = e_refs[0][rows, :]
    ss = e_refs[1][rows, :]
    acc = accs[0]
    for h in range(acc.shape[1] // HEAD_DIM):
        xh = acc[:, h * HEAD_DIM:(h + 1) * HEAD_DIM]
        r = xh * cc + pltpu.roll(xh, HEAD_DIM // 2, 1) * ss
        if scale != 1.0:
            r = r * scale
        c0 = cols.start + h * HEAD_DIM
        o_refs[0][rows, c0:c0 + HEAD_DIM] = r.astype(o_refs[0].dtype)


def _ep_partial_rope_t(accs, e_refs, o_refs, cols, rows):
    c = e_refs[0][...]
    s = e_refs[1][...]
    acc = accs[0]
    half = IDX_ROPE_DIM // 2
    for h in range(acc.shape[0] // IDX_HEAD_DIM):
        r0 = h * IDX_HEAD_DIM
        x1 = acc[r0:r0 + half, :]
        x2 = acc[r0 + half:r0 + 2 * half, :]
        o_refs[0][r0:r0 + half, :] = (x1 * c - x2 * s).astype(o_refs[0].dtype)
        o_refs[0][r0 + half:r0 + 2 * half, :] = (x2 * c + x1 * s).astype(o_refs[0].dtype)
        o_refs[0][r0 + 2 * half:r0 + IDX_HEAD_DIM, :] = (
            acc[r0 + 2 * half:r0 + IDX_HEAD_DIM, :].astype(o_refs[0].dtype))


def _partial_rope(x, c, a, b):
    half = IDX_ROPE_DIM // 2
    return x * c + pltpu.roll(x, LANES - half, 1) * a + pltpu.roll(x, half, 1) * b


def _ep_index_keys(accs, e_refs, o_refs, cols, rows, *, w_scale):
    g, bb = e_refs[0][...], e_refs[1][...]
    cos, sin = e_refs[2][rows, :], e_refs[3][rows, :]
    acc = accs[0]
    n, half = acc.shape[0], IDX_ROPE_DIM // 2
    c = jnp.concatenate([cos, cos, jnp.ones((n, LANES - 2 * half), F32)], axis=1)
    a = jnp.concatenate([-sin, jnp.zeros((n, LANES - half), F32)], axis=1)
    b = jnp.concatenate([jnp.zeros((n, half), F32), sin, jnp.zeros((n, LANES - 2 * half), F32)], axis=1)
    lane = lax.broadcasted_iota(jnp.int32, acc.shape, 1)
    is_key = lane < IDX_HEAD_DIM
    mu = jnp.sum(jnp.where(is_key, acc, 0.0), axis=-1, keepdims=True) / IDX_HEAD_DIM
    xc = jnp.where(is_key, acc - mu, 0.0)
    var = jnp.sum(xc * xc, axis=-1, keepdims=True) / IDX_HEAD_DIM
    y = xc * lax.rsqrt(var + NORM_EPS) * g + bb
    y = jnp.where(is_key, _partial_rope(y, c, a, b), 0.0)
    o_refs[0][rows, :] = y.astype(o_refs[0].dtype)
    o_refs[1][rows, :] = pltpu.roll(y, IDX_HEAD_DIM, 1).astype(o_refs[1].dtype)
    o_refs[2][:, rows] = (acc * w_scale).T[IDX_HEAD_DIM:IDX_HEAD_DIM + IDX_HEADS, :]


def _ep_merge(accs, e_refs, o_refs, cols, rows):
    o = e_refs[0][rows, cols].astype(F32) * accs[0] + e_refs[1][rows, cols].astype(F32) * accs[1]
    o_refs[0][rows, cols] = o.astype(o_refs[0].dtype)


def _ep_mix_residual(accs, e_refs, o_refs, cols, rows):
    x_ref, g_post, gate, g_pre, scale, shift = e_refs
    x1 = x_ref[rows, :] + gate[...] * _rms(accs[0], g_post[...])
    o_refs[0][rows, :] = x1
    h2 = _rms(x1, g_pre[...]) * (1.0 + scale[...]) + shift[...]
    o_refs[1][rows, :] = h2.astype(o_refs[1].dtype)


def _ep_swiglu(accs, e_refs, o_refs, cols, rows):
    o_refs[0][rows, cols] = (jax.nn.silu(accs[0]) * accs[1]).astype(o_refs[0].dtype)

    if cols.start == 0 and rows.start == 0:
        o_refs[1][...] = e_refs[0][...].astype(o_refs[1].dtype)


def _simple_mm(x, w, epilogue, extras, out_dtype, *, n, tm, tn, name, first=0, kn=True, out_t=False,
               sub_n=None, sub_m=None):
    m = x.shape[0]
    out = ((n, m), out_dtype, (tn, tm), _tile_t) if out_t else ((m, n), out_dtype, (tm, tn), _tile)
    return _matmul([x], [(w, first, kn)], [0], extras, [out], epilogue, n=n, tm=tm, tn=tn,
                   name=name, out_t=out_t, sub_n=sub_n, sub_m=sub_m)[0]


def _gmlp_body(u_ref, v_ref, w_ref, b_ref, o_ref):
    t = lax.broadcasted_iota(jnp.int32, (CHUNK, CHUNK), 0)
    s = lax.broadcasted_iota(jnp.int32, (CHUNK, CHUNK), 1)
    causal = s <= t
    for g in range(GMLP_GROUPS):
        w = jnp.where(causal, w_ref[g], 0.0).astype(BF16)
        cols = slice(g * LANES, (g + 1) * LANES)
        bias = b_ref[:, cols]
        for c in range(u_ref.shape[0] // CHUNK):
            rows = slice(c * CHUNK, (c + 1) * CHUNK)
            sv = jnp.dot(w, v_ref[rows, cols], preferred_element_type=F32) + bias
            o_ref[rows, cols] = (u_ref[rows, cols].astype(F32) * sv).astype(o_ref.dtype)


def _gmlp(u, vn, w_s, bias):
    m, n = u.shape
    tm = STREAM_TM
    blk = pl.BlockSpec((tm, n), lambda i: (i, 0))
    return pl.pallas_call(
        _gmlp_body,
        grid=(m // tm,),
        in_specs=[blk, blk,
                  pl.BlockSpec(w_s.shape, lambda i: (0, 0, 0)),
                  pl.BlockSpec(bias.shape, lambda i: (0, 0))],
        out_specs=blk,
        out_shape=jax.ShapeDtypeStruct((m, n), BF16),
        compiler_params=_params(("arbitrary",)),
        name="gmlp_gating",
    )(u, vn, w_s, bias)


def _bit_planes(words):
    a = list(words)
    j, mask = KEY_BITS // 2, 0x0000FFFF
    while j:
        for k in range(KEY_BITS):
            if k & j:
                continue
            t = (a[k] ^ (a[k + j] >> j)) & jnp.int32(mask)
            a[k] = a[k] ^ t
            a[k + j] = a[k + j] ^ (t << j)
        j //= 2
        mask = (mask ^ (mask << j)) & 0xFFFFFFFF
    return a


def _score_key(score):
    bits = lax.bitcast_convert_type(score, jnp.int32)
    return bits ^ ((bits >> 31) & jnp.int32(0x7FFFFFFF))


def _key_score(key):
    return lax.bitcast_convert_type(key ^ ((key >> 31) & jnp.int32(0x7FFFFFFF)), F32)


def _indexer_body(qt_ref, wt_ref, klo_ref, khi_ref, bias_ref, score_ref, plane_ref, live_ref, kth_ref):
    tq = qt_ref.shape[1]
    n_chunks_total = bias_ref.shape[0]
    i = pl.program_id(0)
    n_chunks = ((i + 1) * tq + ATT_KC - 1) // ATT_KC
    blocks_per_chunk = ATT_KC // PLANE_KEYS
    n_blocks = n_chunks * blocks_per_chunk
    heads_per_group = LANES // IDX_HEAD_DIM
    k_refs = (klo_ref, khi_ref)

    @pl.when(i == 0)
    def _():
        plane_ref[...] = jnp.zeros(plane_ref.shape, jnp.int32)

    q_pos = i * tq + lax.broadcasted_iota(jnp.int32, (ATT_KC, tq), 1)
    k_off = lax.broadcasted_iota(jnp.int32, (ATT_KC, tq), 0)

    def score_chunk(c):
        k0 = pl.multiple_of(c * ATT_KC, ATT_KC)
        score = jnp.zeros((ATT_KC, tq), F32)
        for h in range(IDX_HEADS):
            grp, r = divmod(h, heads_per_group)
            logit = jnp.dot(k_refs[r][pl.ds(k0, ATT_KC), :], qt_ref[grp * LANES:(grp + 1) * LANES, :],
                            preferred_element_type=F32)
            score = score + jnp.maximum(logit, 0.0) * wt_ref[h:h + 1, :]
        score_ref[pl.ds(k0, ATT_KC), :] = jnp.where(k0 + k_off <= q_pos, score, -jnp.inf)

    def slice_chunk(c):
        for hh in range(blocks_per_chunk):
            blk = c * blocks_per_chunk + hh
            k0 = pl.multiple_of(blk * PLANE_KEYS, PLANE_KEYS)
            bits = lax.bitcast_convert_type(score_ref[pl.ds(k0, PLANE_KEYS), :], jnp.int32)
            planes = _bit_planes([bits[w * SUBLANES:(w + 1) * SUBLANES, :] for w in range(KEY_BITS)])
            r0 = pl.multiple_of(blk * SUBLANES, SUBLANES)
            sign = planes[0]
            plane_ref[0, pl.ds(r0, SUBLANES), :] = ~sign
            for r in range(1, KEY_BITS):
                plane_ref[r, pl.ds(r0, SUBLANES), :] = planes[r] ^ sign

    def score_and_slice(c, carry):
        slice_chunk(c - 1)
        score_chunk(c)
        return carry

    score_chunk(0)
    lax.fori_loop(1, n_chunks, score_and_slice, 0)
    slice_chunk(n_chunks - 1)

    def count_bits(words):
        ones = lax.population_count(words)
        part = jnp.sum(ones.reshape(words.shape[0] // SUBLANES, SUBLANES, tq), axis=0)
        return jnp.sum(part.astype(F32), axis=0, keepdims=True)

    def radix_select(rows):
        row_id = lax.broadcasted_iota(jnp.int32, (rows, tq), 0)
        live_ref[:rows] = jnp.where(row_id < n_blocks * SUBLANES, jnp.int32(-1), jnp.int32(0))

        def search_bit(r, carry):
            need, kth = carry
            total = count_bits(live_ref[:rows] & plane_ref[r, :rows])
            take = total >= need
            keep_clear = jnp.where(take, jnp.int32(0), jnp.int32(-1))
            live_ref[:rows] = live_ref[:rows] & (plane_ref[r, :rows] ^ keep_clear)
            need = jnp.where(take, need, need - total)
            kth = jnp.where(take, kth | jnp.left_shift(jnp.int32(1), KEY_BITS - 1 - r), kth)
            return need, kth

        _, kth = lax.fori_loop(0, KEY_BITS, search_bit,
                               (jnp.full((1, tq), float(INDEX_TOPK), F32), jnp.zeros((1, tq), jnp.int32)))
        kth_ref[0:1] = kth ^ jnp.int32(INT_MIN)

    plane_rows = live_ref.shape[0]
    quarter = plane_rows // 4
    for rows in range(quarter, plane_rows + 1, quarter):
        @pl.when((n_blocks * SUBLANES > rows - quarter) & (n_blocks * SUBLANES <= rows))
        def _():
            radix_select(rows)

    def count_scores(pred):
        group = 4 * SUBLANES

        def add_chunk(c, acc):
            k0 = pl.multiple_of(c * ATT_KC, ATT_KC)
            hit = jnp.where(pred(score_ref[pl.ds(k0, ATT_KC), :]), 1.0, 0.0)
            return acc + jnp.sum(hit.reshape(ATT_KC // group, group, tq), axis=0)

        acc = lax.fori_loop(0, n_chunks, add_chunk, jnp.zeros((group, tq), F32))
        return jnp.sum(acc, axis=0, keepdims=True)

    few_keys = i * tq + lax.broadcasted_iota(jnp.int32, (1, tq), 1) + 1 < INDEX_TOPK
    lowest = float(jnp.finfo(F32).min)
    thr = jnp.where(few_keys, lowest, _key_score(jnp.maximum(kth_ref[0:1], KEY_NEG_INF + 1)))

    def bias_tile(sel):
        return jnp.where(sel, 0.0, MASK_BIAS).T

    def write_chunk(c, kept):
        parts = []
        for hh in range(blocks_per_chunk):
            k0 = pl.multiple_of(c * ATT_KC + hh * PLANE_KEYS, PLANE_KEYS)
            sel = score_ref[pl.ds(k0, PLANE_KEYS), :] >= thr
            parts.append(bias_tile(sel))
            hit = jnp.where(sel, 1.0, 0.0)
            kept = kept + jnp.sum(hit.reshape(PLANE_KEYS // kept.shape[0], kept.shape[0], tq), axis=0)
        bias_ref[c] = jnp.concatenate(parts, axis=1).astype(bias_ref.dtype)
        return kept

    kept = lax.fori_loop(0, n_chunks, write_chunk, jnp.zeros((4 * SUBLANES, tq), F32))
    kept = jnp.sum(kept, axis=0, keepdims=True)
    unsettled = jnp.max(jnp.where(few_keys | (kept == float(INDEX_TOPK)), 0.0, 1.0))

    @pl.when(unsettled > 0.0)
    def _():
        def search_bit(b, prefix):
            cand = prefix ^ jnp.left_shift(jnp.int32(1), KEY_BITS - 1 - b)
            cand_score = _key_score(jnp.maximum(cand, KEY_NEG_INF))
            admitted = count_scores(lambda sc: sc >= cand_score)
            return jnp.where(admitted >= float(INDEX_TOPK), cand, prefix)

        kth = lax.fori_loop(0, KEY_BITS, search_bit, jnp.full((1, tq), INT_MIN, jnp.int32))
        kth_score = jnp.where(few_keys, lowest, _key_score(jnp.maximum(kth, KEY_NEG_INF + 1)))
        above = count_scores(lambda sc: sc > kth_score)
        need = jnp.where(few_keys, float(2 ** 24), float(INDEX_TOPK) - above)

        row = lax.broadcasted_iota(jnp.int32, (PLANE_KEYS, PLANE_KEYS), 0)
        col = lax.broadcasted_iota(jnp.int32, (PLANE_KEYS, PLANE_KEYS), 1)
        prefix_sum = jnp.where(col <= row, 1.0, 0.0).astype(BF16)

        def write_chunk(c, seen):
            parts = []
            for hh in range(blocks_per_chunk):
                k0 = pl.multiple_of(c * ATT_KC + hh * PLANE_KEYS, PLANE_KEYS)
                sc = score_ref[pl.ds(k0, PLANE_KEYS), :]
                tie = sc == kth_score
                rank = seen + jnp.dot(prefix_sum, jnp.where(tie, 1.0, 0.0).astype(BF16),
                                      preferred_element_type=F32)
                parts.append(bias_tile((sc > kth_score) | (tie & (rank <= need))))
                seen = rank[PLANE_KEYS - 1:PLANE_KEYS, :]
            bias_ref[c] = jnp.concatenate(parts, axis=1).astype(bias_ref.dtype)
            return seen

        lax.fori_loop(0, n_chunks, write_chunk, jnp.zeros((1, tq), F32))

    def fill_chunk(c, carry):
        bias_ref[c] = jnp.full(bias_ref.shape[1:], MASK_BIAS, bias_ref.dtype)
        return carry

    lax.fori_loop(n_chunks, n_chunks_total, fill_chunk, 0)


def _indexer_mask(q_idx_t, idx_w_t, k_lo, k_hi):
    s = q_idx_t.shape[1]
    tq = SEL_TQ
    assert tq % PLANE_KEYS == 0 and s % (KEY_BITS * SUBLANES) == 0
    return pl.pallas_call(
        _indexer_body,
        grid=(s // tq,),
        in_specs=[pl.BlockSpec((q_idx_t.shape[0], tq), lambda i: (0, i)),
                  pl.BlockSpec((IDX_HEADS, tq), lambda i: (0, i)),
                  pl.BlockSpec(k_lo.shape, lambda i: (0, 0)),
                  pl.BlockSpec(k_hi.shape, lambda i: (0, 0))],
        out_specs=pl.BlockSpec((s // ATT_KC, tq, ATT_KC), lambda i: (0, i, 0)),
        out_shape=jax.ShapeDtypeStruct((s // ATT_KC, s, ATT_KC), BF16),
        scratch_shapes=[pltpu.VMEM((s, tq), F32),
                        pltpu.VMEM((KEY_BITS, s // KEY_BITS, tq), jnp.int32),
                        pltpu.VMEM((s // KEY_BITS, tq), jnp.int32),
                        pltpu.VMEM((SUBLANES, tq), jnp.int32)],
        compiler_params=_params(("arbitrary",)),
        name="indexer_mask",
    )(q_idx_t, idx_w_t, k_lo, k_hi)


def _attention_body(q_ref, k_ref, v_ref, bias_ref, o_ref, m_ref, acc_ref):
    tq = q_ref.shape[0]
    chunk_keys = bias_ref.shape[2]
    i = pl.program_id(0)
    n_chunks = (i * tq + tq - 1) // chunk_keys + 1

    m_ref[...] = jnp.full(m_ref.shape, M_INIT, F32)
    acc_ref[...] = jnp.zeros(acc_ref.shape, F32)

    def step(first_chunk, width):
        kc = width * chunk_keys
        sub = kc // LANES
        k0 = pl.multiple_of(first_chunk * chunk_keys, chunk_keys)
        bias = jnp.concatenate([bias_ref[first_chunk + c] for c in range(width)], axis=1)
        ones = jnp.ones((kc, HEAD_DIM), v_ref.dtype)
        for h in range(ATTN_HEADS):
            cols = slice(h * HEAD_DIM, (h + 1) * HEAD_DIM)
            s = lax.dot_general(q_ref[:, cols], k_ref[pl.ds(k0, kc), cols],
                                (((1,), (1,)), ((), ())), preferred_element_type=F32)
            sb = s.astype(BF16) + bias
            part = sb[:, :LANES]
            for c in range(1, sub):
                part = jnp.maximum(part, sb[:, c * LANES:(c + 1) * LANES])
            m_prev = m_ref[h]
            m_next = jnp.maximum(m_prev, jnp.max(part.astype(F32), axis=1, keepdims=True))
            p = jnp.exp2(sb - jnp.concatenate([m_next.astype(BF16)] * sub, axis=1))
            alpha = jnp.exp2(m_prev - m_next)
            m_ref[h] = m_next
            pv = jnp.dot(p, jnp.concatenate([v_ref[pl.ds(k0, kc), cols], ones], axis=1),
                         preferred_element_type=F32)
            acc_ref[h] = jnp.concatenate([alpha, alpha], axis=1) * acc_ref[h] + pv

    def wide_step(j, carry):
        step(j * ATT_STEP_CHUNKS, ATT_STEP_CHUNKS)
        return carry

    n_wide = n_chunks // ATT_STEP_CHUNKS
    lax.fori_loop(0, n_wide, wide_step, 0)
    for r in range(1, ATT_STEP_CHUNKS):
        @pl.when(n_chunks % ATT_STEP_CHUNKS >= r)
        def _():
            step(n_wide * ATT_STEP_CHUNKS + r - 1, 1)

    for h in range(ATTN_HEADS):
        acc = acc_ref[h]
        o_ref[:, h * HEAD_DIM:(h + 1) * HEAD_DIM] = (acc[:, :HEAD_DIM] / acc[:, HEAD_DIM:]).astype(o_ref.dtype)


def _attention(q, k, v, bias):
    s, width = q.shape
    tq = SEL_TQ
    n_chunks, _, kc = bias.shape
    resident = dict(pipeline_mode=pl.Buffered(1))
    return pl.pallas_call(
        _attention_body,
        grid=(s // tq,),
        in_specs=[pl.BlockSpec((tq, width), lambda i: (i, 0)),
                  pl.BlockSpec((s, width), lambda i: (0, 0), **resident),
                  pl.BlockSpec((s, width), lambda i: (0, 0), **resident),
                  pl.BlockSpec((n_chunks, tq, kc), lambda i: (0, i, 0))],
        out_specs=pl.BlockSpec((tq, width), lambda i: (i, 0)),
        out_shape=jax.ShapeDtypeStruct((s, width), BF16),
        scratch_shapes=[pltpu.VMEM((ATTN_HEADS, tq, LANES), F32),
                        pltpu.VMEM((ATTN_HEADS, tq, 2 * HEAD_DIM), F32)],
        compiler_params=_params(("arbitrary",)),
        name="masked_attention",
    )(q, k, v, bias)


def _ffn_down_body(a_ref, w_ref, x_ref, g_ref, gate_ref, o_ref):
    f = jnp.dot(a_ref[...], w_ref[...], preferred_element_type=F32)
    o_ref[...] = x_ref[...] + gate_ref[...] * _rms(f, g_ref[...])


def _ffn_down(a, w, x1, g_post, gate):
    m, kdim = a.shape
    d = w.shape[1]
    tm = FFN_DOWN_TM
    row = pl.BlockSpec((1, d), lambda i: (0, 0))
    return pl.pallas_call(
        _ffn_down_body,
        grid=(m // tm,),
        in_specs=[pl.BlockSpec((tm, kdim), lambda i: (i, 0)),
                  pl.BlockSpec((kdim, d), lambda i: (0, 0), pipeline_mode=pl.Buffered(1)),
                  pl.BlockSpec((tm, d), lambda i: (i, 0)), row, row],
        out_specs=pl.BlockSpec((tm, d), lambda i: (i, 0)),
        out_shape=jax.ShapeDtypeStruct((m, d), F32),
        compiler_params=_params(("arbitrary",)),
        name="ffn_down_residual",
    )(a, w, x1, g_post, gate)


def _rope_angles(seq, dim, repeat=1):
    inv = jnp.tile(1.0 / (ROPE_THETA ** (jnp.arange(0, dim, 2, dtype=F32) / dim)), repeat)
    coarse = (jnp.arange(seq // ROPE_BLOCK, dtype=F32) * ROPE_BLOCK)[:, None] * inv[None, :]
    fine = jnp.arange(ROPE_BLOCK, dtype=F32)[:, None] * inv[None, :]
    cos_a, sin_a = jnp.cos(coarse)[:, None, :], jnp.sin(coarse)[:, None, :]
    cos_b, sin_b = jnp.cos(fine)[None, :, :], jnp.sin(fine)[None, :, :]
    cos = (cos_a * cos_b - sin_a * sin_b).reshape(seq, inv.shape[0])
    sin = (sin_a * cos_b + cos_a * sin_b).reshape(seq, inv.shape[0])
    return cos, sin


def _attn_rope_tables(seq):
    cos, sin = _rope_angles(seq, HEAD_DIM, repeat=2)
    sign = jnp.concatenate([-jnp.ones((HEAD_DIM // 2,), F32), jnp.ones((HEAD_DIM // 2,), F32)])
    return cos, sin * sign


def _index_rope_tables(seq):
    cos, sin = _rope_angles(seq, IDX_ROPE_DIM)
    return cos, sin, cos.T, sin.T


def _layer(x, mod, g_pre_mix, g_post_mix, w_in_t, gmlp_ln_g, gmlp_ln_b, gmlp_w_s, gmlp_b_s,
           q_lat_norm_g, w_q_up, w_qidx_up, kidx_ln_g, kidx_ln_b, w_proj_a, w_proj_b, w_out,
           g_pre_ffn, g_post_ffn, w_ffn_gate, w_ffn_up, w_ffn_down, tables):
    s, d = x.shape
    cc, ss, icos, isin, icos_t, isin_t = tables
    row = lambda a: a.reshape(1, -1)
    shift_m, scale_m, gate_m, shift_f, scale_f, gate_f = (mod[:, n * d:(n + 1) * d] for n in range(N_MOD))

    h = _prenorm(x, row(g_pre_mix), scale_m, shift_m)

    tm = MM_TM
    in_proj = functools.partial(_simple_mm, h, w_in_t, kn=False)

    u = in_proj(_ep_gelu, [], BF16, n=GMLP_WIDTH, tm=tm, tn=GMLP_WIDTH, sub_n=EPILOGUE_SUB_N,
                name="in_proj_u")
    vn = in_proj(_ep_gelu_layernorm,
                 [(row(gmlp_ln_g), (1, GMLP_WIDTH), _colblk), (row(gmlp_ln_b), (1, GMLP_WIDTH), _colblk)],
                 BF16, n=GMLP_WIDTH, first=GMLP_WIDTH, tm=ROWNORM_TM, tn=GMLP_WIDTH, sub_m=EPILOGUE_SUB_M,
                 name="in_proj_v_ln")
    gmlp_bias = jnp.repeat(gmlp_b_s.T, LANES, axis=1)
    y_a = _gmlp(u, vn, gmlp_w_s, gmlp_bias)

    q_lat = in_proj(_ep_rmsnorm, [(row(q_lat_norm_g), (1, Q_LORA_RANK), _colblk)],
                    BF16, n=Q_LORA_RANK, first=COL_QLAT, tm=tm, tn=Q_LORA_RANK, name="in_proj_qlat")
    q_scale = HEAD_DIM ** -0.5 * math.log2(math.e)
    rope_extras = [(cc, (tm, HEAD_DIM), _rowblk), (ss, (tm, HEAD_DIM), _rowblk)]
    q = _simple_mm(q_lat, w_q_up, functools.partial(_ep_rope, scale=q_scale), rope_extras,
                   BF16, n=ATTN_WIDTH, tm=tm, tn=ATTN_WIDTH, sub_n=EPILOGUE_SUB_N, name="q_up_rope")
    k = in_proj(functools.partial(_ep_rope, scale=1.0), rope_extras, BF16,
                n=ATTN_WIDTH, first=COL_K, tm=tm, tn=ATTN_WIDTH, sub_n=EPILOGUE_SUB_N,
                name="in_proj_k_rope")
    v = in_proj(_ep_cast, [], BF16, n=ATTN_WIDTH, first=COL_V, tm=tm, tn=ATTN_WIDTH,
                sub_n=EPILOGUE_SUB_N, name="in_proj_v")
    half = IDX_ROPE_DIM // 2
    q_idx_t = _simple_mm(q_lat, w_qidx_up, _ep_partial_rope_t,
                         [(icos_t, (half, tm), _colblk_t), (isin_t, (half, tm), _colblk_t)],
                         BF16, n=IDX_HEADS * IDX_HEAD_DIM, tm=tm, tn=QIDX_TN, name="qidx_up_rope",
                         out_t=True)
    pad = lambda a: jnp.pad(a, (0, LANES - a.shape[0])).reshape(1, LANES)
    idx_w_scale = (IDX_HEADS ** -0.5) * (IDX_HEAD_DIM ** -0.5)
    lane_tile = lambda dtype: ((s, LANES), dtype, (tm, LANES), _tile)
    k_lo, k_hi, idx_w_t = _matmul(
        [h], [(w_in_t, COL_KIDX, False)], [0],
        [(pad(kidx_ln_g), (1, LANES), _colblk), (pad(kidx_ln_b), (1, LANES), _colblk),
         (icos, (tm, half), _rowblk), (isin, (tm, half), _rowblk)],
        [lane_tile(BF16), lane_tile(BF16), ((IDX_HEADS, s), F32, (IDX_HEADS, tm), _colblk_t)],
        functools.partial(_ep_index_keys, w_scale=idx_w_scale), n=LANES, tm=tm, tn=LANES,
        name="in_proj_index_keys")
    bias = _indexer_mask(q_idx_t, idx_w_t, k_lo, k_hi)
    y_b = _attention(q, k, v, bias)

    gates = in_proj(_ep_sigmoid, [], BF16, n=2 * d, first=COL_GATE, tm=tm, tn=MM_TN,
                    sub_n=EPILOGUE_SUB_N, name="in_proj_gates")
    n_gate_blocks = d // MM_TN
    merged = _matmul([y_a, y_b], [(w_proj_a, 0, True), (w_proj_b, 0, True)], [0, 1],
                     [(gates, (tm, MM_TN), _tile),
                      (gates, (tm, MM_TN), lambda j, i: (i, j + n_gate_blocks))],
                     [((s, d), BF16, (tm, MM_TN), _tile)], _ep_merge, n=d, tm=tm, tn=MM_TN,
                     sub_n=EPILOGUE_SUB_N, name="branch_merge")[0]
    tm_full = ROWNORM_TM
    vec = lambda a: (a, (1, d), _colblk)
    x1, h2 = _matmul([merged], [(w_out, 0, True)], [0],
                     [(x, (tm_full, d), _rowblk), vec(row(g_post_mix)), vec(gate_m),
                      vec(row(g_pre_ffn)), vec(scale_f), vec(shift_f)],
                     [((s, d), F32, (tm_full, d), _tile), ((s, d), BF16, (tm_full, d), _tile)],
                     _ep_mix_residual, n=d, tm=tm_full, tn=d, sub_m=EPILOGUE_SUB_M, name="out_proj_residual")

    ffn_hidden = w_ffn_gate.shape[1]
    tn_ffn = FFN_UP_TN
    n_row_blocks = s // tm
    down_rows = ffn_hidden // ((ffn_hidden // tn_ffn) * n_row_blocks)
    down_blk = ((down_rows, d), lambda j, i: (j * n_row_blocks + i, 0))
    act, w_down = _matmul([h2], [(w_ffn_gate, 0, True), (w_ffn_up, 0, True)], [0, 0],
                          [(w_ffn_down,) + down_blk],
                          [((s, ffn_hidden), BF16, (tm, tn_ffn), _tile), (w_ffn_down.shape, BF16) + down_blk],
                          _ep_swiglu, n=ffn_hidden, tm=tm, tn=tn_ffn, name="ffn_up")
    return _ffn_down(act, w_down, x1, row(g_post_ffn), gate_f)


def kernel(x, c, w_mod, b_mod, g_pre_mix, g_post_mix, w_in, gmlp_ln_g, gmlp_ln_b, gmlp_w_s, gmlp_b_s, q_lat_norm_g, w_q_up, w_qidx_up, kidx_ln_g, kidx_ln_b, w_proj_a, w_proj_b, w_out, g_pre_ffn, g_post_ffn, w_ffn_gate, w_ffn_up, w_ffn_down):
    batch, seq, d = x.shape
    assert batch == 1 and d == D_MODEL
    tables = _attn_rope_tables(seq) + _index_rope_tables(seq)
    y = x[0]
    for l in range(w_mod.shape[0]):
        mod = _modulation(c, w_mod[l], b_mod[l])
        y = _layer(y, mod, g_pre_mix[l], g_post_mix[l], w_in[l].T, gmlp_ln_g[l], gmlp_ln_b[l],
                   gmlp_w_s[l], gmlp_b_s[l], q_lat_norm_g[l], w_q_up[l], w_qidx_up[l],
                   kidx_ln_g[l], kidx_ln_b[l], w_proj_a[l], w_proj_b[l], w_out[l],
                   g_pre_ffn[l], g_post_ffn[l], w_ffn_gate[l], w_ffn_up[l], w_ffn_down[l], tables)
    return y[None]
```

```python
import functools
import math

import jax
import jax.numpy as jnp
from jax import lax
from jax.experimental import pallas as pl
from jax.experimental.pallas import tpu as pltpu

F32 = jnp.float32
BF16 = jnp.bfloat16

D_MODEL = 2048
GMLP_WIDTH = 1024
GMLP_GROUPS = 8
CHUNK = 128
ATTN_HEADS = 8
HEAD_DIM = 128
ATTN_WIDTH = ATTN_HEADS * HEAD_DIM
Q_LORA_RANK = 512
IDX_HEADS = 16
IDX_HEAD_DIM = 64
IDX_ROPE_DIM = 32
INDEX_TOPK = 256
ROPE_THETA = 10000.0
N_MOD = 6
NORM_EPS = 1e-6

COL_QLAT = 2 * GMLP_WIDTH
COL_K = COL_QLAT + Q_LORA_RANK
COL_V = COL_K + ATTN_WIDTH
COL_KIDX = COL_V + ATTN_WIDTH
COL_IDXW = COL_KIDX + IDX_HEAD_DIM
COL_GATE = COL_IDXW + IDX_HEADS

LANES = 128
SUBLANES = 8
VMEM_LIMIT = 56 * 1024 * 1024

MASK_BIAS = -(2.0 ** 100)
M_INIT = -(2.0 ** 99)
INT_MIN = -2 ** 31
KEY_NEG_INF = -2139095041

SEL_TQ = 256
KEY_BITS = 32
PLANE_KEYS = KEY_BITS * SUBLANES
ATT_KC = 2 * PLANE_KEYS
ATT_STEP_CHUNKS = 2
EPILOGUE_SUB_N = 512
EPILOGUE_SUB_M = 256
ROPE_BLOCK = 64

MM_TM = 1024
MM_TN = 1024
ROWNORM_TM = 512
FFN_UP_TN = 512
FFN_DOWN_TM = 256
QIDX_TN = 512
STREAM_TM = 1024
MOD_TN = 1024


def _params(sem=None):
    return pltpu.CompilerParams(dimension_semantics=sem, vmem_limit_bytes=VMEM_LIMIT)


def _mod_body(cb_ref, w_ref, b_ref, o_ref):
    cb = cb_ref[...]
    tn = o_ref.shape[1]
    parts = [jnp.sum(w_ref[:, p * LANES:(p + 1) * LANES] * cb, axis=0, keepdims=True)
             for p in range(tn // LANES)]
    o_ref[...] = jnp.concatenate(parts, axis=1) + b_ref[...]


def _modulation(c, w_mod, b_mod):
    k, n = w_mod.shape
    tn = MOD_TN
    cb = jnp.broadcast_to(c.reshape(k, 1), (k, LANES))
    return pl.pallas_call(
        _mod_body,
        grid=(n // tn,),
        in_specs=[pl.BlockSpec((k, LANES), lambda j: (0, 0)),
                  pl.BlockSpec((k, tn), lambda j: (0, j)),
                  pl.BlockSpec((1, tn), lambda j: (0, j))],
        out_specs=pl.BlockSpec((1, tn), lambda j: (0, j)),
        out_shape=jax.ShapeDtypeStruct((1, n), F32),
        compiler_params=_params(("arbitrary",)),
        name="modulation",
    )(cb, w_mod, b_mod.reshape(1, n))


def _rms(x, g):
    return x * lax.rsqrt(jnp.mean(x * x, axis=-1, keepdims=True) + NORM_EPS) * g


def _prenorm_body(x_ref, g_ref, scale_ref, shift_ref, o_ref):
    h = _rms(x_ref[...], g_ref[...]) * (1.0 + scale_ref[...]) + shift_ref[...]
    o_ref[...] = h.astype(o_ref.dtype)


def _prenorm(x, g, scale, shift):
    m, d = x.shape
    tm = STREAM_TM
    row = pl.BlockSpec((1, d), lambda i: (0, 0))
    return pl.pallas_call(
        _prenorm_body,
        grid=(m // tm,),
        in_specs=[pl.BlockSpec((tm, d), lambda i: (i, 0)), row, row, row],
        out_specs=pl.BlockSpec((tm, d), lambda i: (i, 0)),
        out_shape=jax.ShapeDtypeStruct((m, d), BF16),
        compiler_params=_params(("arbitrary",)),
        name="prenorm",
    )(x, g, scale, shift)


def _mm_body(*refs, n_x, pairs, w_kn, out_t, sub_m, sub_n, n_extra, n_out, epilogue, prologue):
    n_w = len(pairs)
    x_refs = refs[:n_x]
    w_refs = refs[n_x:n_x + n_w]
    e_refs = refs[n_x + n_w:n_x + n_w + n_extra]
    o_refs = refs[n_x + n_w + n_extra:n_x + n_w + n_extra + n_out]
    wb_refs = refs[n_x + n_w + n_extra + n_out:]

    @pl.when(pl.program_id(1) == 0)
    def _():
        for w_ref, wb_ref, kn in zip(w_refs, wb_refs, w_kn):
            w = w_ref[...]
            if kn == out_t:
                w = w.T
            wb_ref[...] = w.astype(wb_ref.dtype)

    if out_t:
        accs = [lax.dot_general(wb_ref[...], x_refs[xi][...], (((1,), (1,)), ((), ())),
                                preferred_element_type=F32)
                for xi, wb_ref in zip(pairs, wb_refs)]
        epilogue(accs, e_refs, o_refs, slice(0, wb_refs[0].shape[0]), slice(0, x_refs[0].shape[0]))
        return
    tm, tn = x_refs[0].shape[0], wb_refs[0].shape[1]
    for r0 in range(0, tm, sub_m):
        rs = slice(r0, r0 + sub_m)
        if prologue is None:
            lhs = [x_ref[rs, :] for x_ref in x_refs]
        else:
            lhs = prologue(x_refs, e_refs, o_refs, rs)
        for c0 in range(0, tn, sub_n):
            cs = slice(c0, c0 + sub_n)
            accs = [jnp.dot(lhs[xi], wb_ref[:, cs], preferred_element_type=F32)
                    for xi, wb_ref in zip(pairs, wb_refs)]
            epilogue(accs, e_refs, o_refs, cs, rs)


def _matmul(xs, ws, pairs, extras, outs, epilogue, *, n, tm, tn, name, out_t=False, sub_n=None,
            sub_m=None, prologue=None):
    m = xs[0].shape[0]
    n_col = n // tn
    w_mode = dict(pipeline_mode=pl.Buffered(1)) if n_col == 1 else {}
    in_specs = [pl.BlockSpec((tm, x.shape[1]), lambda j, i: (i, 0)) for x in xs]
    scratch = []
    for w, first, kn in ws:
        if kn:
            kdim = w.shape[0]
            assert first % tn == 0
            in_specs.append(pl.BlockSpec((kdim, tn), functools.partial(_w_cols, first=first // tn), **w_mode))
        else:
            kdim = w.shape[1]
            in_specs.append(pl.BlockSpec((pl.Element(tn), pl.Element(kdim)),
                                         functools.partial(_w_rows, first=first, tn=tn), **w_mode))
        scratch.append(pltpu.VMEM((tn, kdim) if out_t else (kdim, tn), BF16))
    in_specs += [pl.BlockSpec(bs, im) for _, bs, im in extras]
    body = functools.partial(_mm_body, n_x=len(xs), pairs=tuple(pairs), w_kn=tuple(kn for _, _, kn in ws),
                             out_t=out_t, sub_m=sub_m or tm, sub_n=sub_n or tn, n_extra=len(extras),
                             n_out=len(outs), epilogue=epilogue, prologue=prologue)
    return pl.pallas_call(
        body,
        grid=(n_col, m // tm),
        in_specs=in_specs,
        out_specs=[pl.BlockSpec(bs, im) for _, _, bs, im in outs],
        out_shape=[jax.ShapeDtypeStruct(s, d) for s, d, _, _ in outs],
        scratch_shapes=scratch,
        compiler_params=_params(("arbitrary", "arbitrary")),
        name=name,
    )(*xs, *[w for w, _, _ in ws], *[a for a, _, _ in extras])


def _w_cols(j, i, *, first):
    return (0, first + j)


def _w_rows(j, i, *, first, tn):
    assert first % SUBLANES == 0 and tn % SUBLANES == 0
    return (pl.multiple_of(first + j * tn, SUBLANES), 0)


def _tile(j, i):
    return (i, j)


def _tile_t(j, i):
    return (j, i)


def _rowblk(j, i):
    return (i, 0)


def _colblk(j, i):
    return (0, j)


def _colblk_t(j, i):
    return (0, i)


def _pro_prenorm(x_refs, e_refs, o_refs, rows):
    g, scale, shift = e_refs[0][...], e_refs[1][...], e_refs[2][...]
    h = (_rms(x_refs[0][rows, :], g) * (1.0 + scale) + shift).astype(o_refs[1].dtype)
    o_refs[1][rows, :] = h
    return [h]


def _ep_gelu(accs, e_refs, o_refs, cols, rows):
    o_refs[0][rows, cols] = jax.nn.gelu(accs[0]).astype(o_refs[0].dtype)


def _ep_gelu_layernorm(accs, e_refs, o_refs, cols, rows):
    z = jax.nn.gelu(accs[0])
    mu = jnp.mean(z, axis=-1, keepdims=True)
    zc = z - mu
    var = jnp.mean(zc * zc, axis=-1, keepdims=True)
    y = zc * lax.rsqrt(var + NORM_EPS) * e_refs[0][...] + e_refs[1][...]
    o_refs[0][rows, :] = y.astype(o_refs[0].dtype)


def _ep_rmsnorm(accs, e_refs, o_refs, cols, rows):
    o_refs[0][rows, :] = _rms(accs[0], e_refs[0][...]).astype(o_refs[0].dtype)


def _ep_cast(accs, e_refs, o_refs, cols, rows):
    o_refs[0][rows, cols] = accs[0].astype(o_refs[0].dtype)


def _ep_sigmoid(accs, e_refs, o_refs, cols, rows):
    o_refs[0][rows, cols] = (0.5 * jnp.tanh(0.5 * accs[0]) + 0.5).astype(o_refs[0].dtype)


def _ep_rope(accs, e_refs, o_refs, cols, rows, *, scale):
    cc = e_refs[0][rows, :]
    ss = e_refs[1][rows, :]
    acc = accs[0]
    for h in range(acc.shape[1] // HEAD_DIM):
        xh = acc[:, h * HEAD_DIM:(h + 1) * HEAD_DIM]
        r = xh * cc + pltpu.roll(xh, HEAD_DIM // 2, 1) * ss
        if scale != 1.0:
            r = r * scale
        c0 = cols.start + h * HEAD_DIM
        o_refs[0][rows, c0:c0 + HEAD_DIM] = r.astype(o_refs[0].dtype)


def _ep_partial_rope_t(accs, e_refs, o_refs, cols, rows):
    c = e_refs[0][...]
    s = e_refs[1][...]
    acc = accs[0]
    half = IDX_ROPE_DIM // 2
    for h in range(acc.shape[0] // IDX_HEAD_DIM):
        r0 = h * IDX_HEAD_DIM
        x1 = acc[r0:r0 + half, :]
        x2 = acc[r0 + half:r0 + 2 * half, :]
        o_refs[0][r0:r0 + half, :] = (x1 * c - x2 * s).astype(o_refs[0].dtype)
        o_refs[0][r0 + half:r0 + 2 * half, :] = (x2 * c + x1 * s).astype(o_refs[0].dtype)
        o_refs[0][r0 + 2 * half:r0 + IDX_HEAD_DIM, :] = (
            acc[r0 + 2 * half:r0 + IDX_HEAD_DIM, :].astype(o_refs[0].dtype))


def _partial_rope(x, c, a, b):
    half = IDX_ROPE_DIM // 2
    return x * c + pltpu.roll(x, LANES - half, 1) * a + pltpu.roll(x, half, 1) * b


def _ep_index_keys(accs, e_refs, o_refs, cols, rows, *, w_scale):
    g, bb = e_refs[0][...], e_refs[1][...]
    cos, sin = e_refs[2][rows, :], e_refs[3][rows, :]
    acc = accs[0]
    n, half = acc.shape[0], IDX_ROPE_DIM // 2
    c = jnp.concatenate([cos, cos, jnp.ones((n, LANES - 2 * half), F32)], axis=1)
    a = jnp.concatenate([-sin, jnp.zeros((n, LANES - half), F32)], axis=1)
    b = jnp.concatenate([jnp.zeros((n, half), F32), sin, jnp.zeros((n, LANES - 2 * half), F32)], axis=1)
    lane = lax.broadcasted_iota(jnp.int32, acc.shape, 1)
    is_key = lane < IDX_HEAD_DIM
    mu = jnp.sum(jnp.where(is_key, acc, 0.0), axis=-1, keepdims=True) / IDX_HEAD_DIM
    xc = jnp.where(is_key, acc - mu, 0.0)
    var = jnp.sum(xc * xc, axis=-1, keepdims=True) / IDX_HEAD_DIM
    y = xc * lax.rsqrt(var + NORM_EPS) * g + bb
    y = jnp.where(is_key, _partial_rope(y, c, a, b), 0.0)
    o_refs[0][rows, :] = y.astype(o_refs[0].dtype)
    o_refs[1][rows, :] = pltpu.roll(y, IDX_HEAD_DIM, 1).astype(o_refs[1].dtype)
    o_refs[2][:, rows] = (acc * w_scale).T[IDX_HEAD_DIM:IDX_HEAD_DIM + IDX_HEADS, :]


def _ep_merge(accs, e_refs, o_refs, cols, rows):
    o = e_refs[0][rows, cols].astype(F32) * accs[0] + e_refs[1][rows, cols].astype(F32) * accs[1]
    o_refs[0][rows, cols] = o.astype(o_refs[0].dtype)


def _ep_mix_residual(accs, e_refs, o_refs, cols, rows):
    x_ref, g_post, gate, g_pre, scale, shift = e_refs
    x1 = x_ref[rows, :] + gate[...] * _rms(accs[0], g_post[...])
    o_refs[0][rows, :] = x1
    h2 = _rms(x1, g_pre[...]) * (1.0 + scale[...]) + shift[...]
    o_refs[1][rows, :] = h2.astype(o_refs[1].dtype)


def _ep_swiglu(accs, e_refs, o_refs, cols, rows):
    o_refs[0][rows, cols] = (jax.nn.silu(accs[0]) * accs[1]).astype(o_refs[0].dtype)

    if cols.start == 0 and rows.start == 0:
        o_refs[1][...] = e_refs[0][...].astype(o_refs[1].dtype)


def _simple_mm(x, w, epilogue, extras, out_dtype, *, n, tm, tn, name, first=0, kn=True, out_t=False,
               sub_n=None, sub_m=None):
    m = x.shape[0]
    out = ((n, m), out_dtype, (tn, tm), _tile_t) if out_t else ((m, n), out_dtype, (tm, tn), _tile)
    return _matmul([x], [(w, first, kn)], [0], extras, [out], epilogue, n=n, tm=tm, tn=tn,
                   name=name, out_t=out_t, sub_n=sub_n, sub_m=sub_m)[0]


def _gmlp_body(u_ref, v_ref, w_ref, b_ref, o_ref):
    t = lax.broadcasted_iota(jnp.int32, (CHUNK, CHUNK), 0)
    s = lax.broadcasted_iota(jnp.int32, (CHUNK, CHUNK), 1)
    causal = s <= t
    for g in range(GMLP_GROUPS):
        w = jnp.where(causal, w_ref[g], 0.0).astype(BF16)
        cols = slice(g * LANES, (g + 1) * LANES)
        bias = b_ref[:, cols]
        for c in range(u_ref.shape[0] // CHUNK):
            rows = slice(c * CHUNK, (c + 1) * CHUNK)
            sv = jnp.dot(w, v_ref[rows, cols], preferred_element_type=F32) + bias
            o_ref[rows, cols] = (u_ref[rows, cols].astype(F32) * sv).astype(o_ref.dtype)


def _gmlp(u, vn, w_s, bias):
    m, n = u.shape
    tm = STREAM_TM
    blk = pl.BlockSpec((tm, n), lambda i: (i, 0))
    return pl.pallas_call(
        _gmlp_body,
        grid=(m // tm,),
        in_specs=[blk, blk,
                  pl.BlockSpec(w_s.shape, lambda i: (0, 0, 0)),
                  pl.BlockSpec(bias.shape, lambda i: (0, 0))],
        out_specs=blk,
        out_shape=jax.ShapeDtypeStruct((m, n), BF16),
        compiler_params=_params(("arbitrary",)),
        name="gmlp_gating",
    )(u, vn, w_s, bias)


def _bit_planes(words):
    a = list(words)
    j, mask = KEY_BITS // 2, 0x0000FFFF
    while j:
        for k in range(KEY_BITS):
            if k & j:
                continue
            t = (a[k] ^ (a[k + j] >> j)) & jnp.int32(mask)
            a[k] = a[k] ^ t
            a[k + j] = a[k + j] ^ (t << j)
        j //= 2
        mask = (mask ^ (mask << j)) & 0xFFFFFFFF
    return a


def _score_key(score):
    bits = lax.bitcast_convert_type(score, jnp.int32)
    return bits ^ ((bits >> 31) & jnp.int32(0x7FFFFFFF))


def _key_score(key):
    return lax.bitcast_convert_type(key ^ ((key >> 31) & jnp.int32(0x7FFFFFFF)), F32)


def _indexer_body(qt_ref, wt_ref, klo_ref, khi_ref, bias_ref, score_ref, plane_ref, live_ref, kth_ref):
    tq = qt_ref.shape[1]
    n_chunks_total = bias_ref.shape[0]
    i = pl.program_id(0)
    n_chunks = ((i + 1) * tq + ATT_KC - 1) // ATT_KC
    blocks_per_chunk = ATT_KC // PLANE_KEYS
    n_blocks = n_chunks * blocks_per_chunk
    heads_per_group = LANES // IDX_HEAD_DIM
    k_refs = (klo_ref, khi_ref)

    @pl.when(i == 0)
    def _():
        plane_ref[...] = jnp.zeros(plane_ref.shape, jnp.int32)

    q_pos = i * tq + lax.broadcasted_iota(jnp.int32, (ATT_KC, tq), 1)
    k_off = lax.broadcasted_iota(jnp.int32, (ATT_KC, tq), 0)

    def score_chunk(c):
        k0 = pl.multiple_of(c * ATT_KC, ATT_KC)
        score = jnp.zeros((ATT_KC, tq), F32)
        for h in range(IDX_HEADS):
            grp, r = divmod(h, heads_per_group)
            logit = jnp.dot(k_refs[r][pl.ds(k0, ATT_KC), :], qt_ref[grp * LANES:(grp + 1) * LANES, :],
                            preferred_element_type=F32)
            score = score + jnp.maximum(logit, 0.0) * wt_ref[h:h + 1, :]
        score_ref[pl.ds(k0, ATT_KC), :] = jnp.where(k0 + k_off <= q_pos, score, -jnp.inf)

    def slice_chunk(c):
        for hh in range(blocks_per_chunk):
            blk = c * blocks_per_chunk + hh
            k0 = pl.multiple_of(blk * PLANE_KEYS, PLANE_KEYS)
            bits = lax.bitcast_convert_type(score_ref[pl.ds(k0, PLANE_KEYS), :], jnp.int32)
            planes = _bit_planes([bits[w * SUBLANES:(w + 1) * SUBLANES, :] for w in range(KEY_BITS)])
            r0 = pl.multiple_of(blk * SUBLANES, SUBLANES)
            sign = planes[0]
            plane_ref[0, pl.ds(r0, SUBLANES), :] = ~sign
            for r in range(1, KEY_BITS):
                plane_ref[r, pl.ds(r0, SUBLANES), :] = planes[r] ^ sign

    def score_and_slice(c, carry):
        slice_chunk(c - 1)
        score_chunk(c)
        return carry

    score_chunk(0)
    lax.fori_loop(1, n_chunks, score_and_slice, 0)
    slice_chunk(n_chunks - 1)

    def count_bits(words):
        ones = lax.population_count(words)
        part = jnp.sum(ones.reshape(words.shape[0] // SUBLANES, SUBLANES, tq), axis=0)
        return jnp.sum(part.astype(F32), axis=0, keepdims=True)

    def radix_select(rows):
        row_id = lax.broadcasted_iota(jnp.int32, (rows, tq), 0)
        live_ref[:rows] = jnp.where(row_id < n_blocks * SUBLANES, jnp.int32(-1), jnp.int32(0))

        def search_bit(r, carry):
            need, kth = carry
            total = count_bits(live_ref[:rows] & plane_ref[r, :rows])
            take = total >= need
            keep_clear = jnp.where(take, jnp.int32(0), jnp.int32(-1))
            live_ref[:rows] = live_ref[:rows] & (plane_ref[r, :rows] ^ keep_clear)
            need = jnp.where(take, need, need - total)
            kth = jnp.where(take, kth | jnp.left_shift(jnp.int32(1), KEY_BITS - 1 - r), kth)
            return need, kth

        _, kth = lax.fori_loop(0, KEY_BITS, search_bit,
                               (jnp.full((1, tq), float(INDEX_TOPK), F32), jnp.zeros((1, tq), jnp.int32)))
        kth_ref[0:1] = kth ^ jnp.int32(INT_MIN)

    plane_rows = live_ref.shape[0]
    quarter = plane_rows // 4
    for rows in range(quarter, plane_rows + 1, quarter):
        @pl.when((n_blocks * SUBLANES > rows - quarter) & (n_blocks * SUBLANES <= rows))
        def _():
            radix_select(rows)

    def count_scores(pred):
        group = 4 * SUBLANES

        def add_chunk(c, acc):
            k0 = pl.multiple_of(c * ATT_KC, ATT_KC)
            hit = jnp.where(pred(score_ref[pl.ds(k0, ATT_KC), :]), 1.0, 0.0)
            return acc + jnp.sum(hit.reshape(ATT_KC // group, group, tq), axis=0)

        acc = lax.fori_loop(0, n_chunks, add_chunk, jnp.zeros((group, tq), F32))
        return jnp.sum(acc, axis=0, keepdims=True)

    few_keys = i * tq + lax.broadcasted_iota(jnp.int32, (1, tq), 1) + 1 < INDEX_TOPK
    lowest = float(jnp.finfo(F32).min)
    thr = jnp.where(few_keys, lowest, _key_score(jnp.maximum(kth_ref[0:1], KEY_NEG_INF + 1)))

    def bias_tile(sel):
        return jnp.where(sel, 0.0, MASK_BIAS).T

    def write_chunk(c, kept):
        parts = []
        for hh in range(blocks_per_chunk):
            k0 = pl.multiple_of(c * ATT_KC + hh * PLANE_KEYS, PLANE_KEYS)
            sel = score_ref[pl.ds(k0, PLANE_KEYS), :] >= thr
            parts.append(bias_tile(sel))
            hit = jnp.where(sel, 1.0, 0.0)
            kept = kept + jnp.sum(hit.reshape(PLANE_KEYS // kept.shape[0], kept.shape[0], tq), axis=0)
        bias_ref[c] = jnp.concatenate(parts, axis=1).astype(bias_ref.dtype)
        return kept

    kept = lax.fori_loop(0, n_chunks, write_chunk, jnp.zeros((4 * SUBLANES, tq), F32))
    kept = jnp.sum(kept, axis=0, keepdims=True)
    unsettled = jnp.max(jnp.where(few_keys | (kept == float(INDEX_TOPK)), 0.0, 1.0))

    @pl.when(unsettled > 0.0)
    def _():
        def search_bit(b, prefix):
            cand = prefix ^ jnp.left_shift(jnp.int32(1), KEY_BITS - 1 - b)
            cand_score = _key_score(jnp.maximum(cand, KEY_NEG_INF))
            admitted = count_scores(lambda sc: sc >= cand_score)
            return jnp.where(admitted >= float(INDEX_TOPK), cand, prefix)

        kth = lax.fori_loop(0, KEY_BITS, search_bit, jnp.full((1, tq), INT_MIN, jnp.int32))
        kth_score = jnp.where(few_keys, lowest, _key_score(jnp.maximum(kth, KEY_NEG_INF + 1)))
        above = count_scores(lambda sc: sc > kth_score)
        need = jnp.where(few_keys, float(2 ** 24), float(INDEX_TOPK) - above)

        row = lax.broadcasted_iota(jnp.int32, (PLANE_KEYS, PLANE_KEYS), 0)
        col = lax.broadcasted_iota(jnp.int32, (PLANE_KEYS, PLANE_KEYS), 1)
        prefix_sum = jnp.where(col <= row, 1.0, 0.0).astype(BF16)

        def write_chunk(c, seen):
            parts = []
            for hh in range(blocks_per_chunk):
                k0 = pl.multiple_of(c * ATT_KC + hh * PLANE_KEYS, PLANE_KEYS)
                sc = score_ref[pl.ds(k0, PLANE_KEYS), :]
                tie = sc == kth_score
                rank = seen + jnp.dot(prefix_sum, jnp.where(tie, 1.0, 0.0).astype(BF16),
                                      preferred_element_type=F32)
                parts.append(bias_tile((sc > kth_score) | (tie & (rank <= need))))
                seen = rank[PLANE_KEYS - 1:PLANE_KEYS, :]
            bias_ref[c] = jnp.concatenate(parts, axis=1).astype(bias_ref.dtype)
            return seen

        lax.fori_loop(0, n_chunks, write_chunk, jnp.zeros((1, tq), F32))

    def fill_chunk(c, carry):
        bias_ref[c] = jnp.full(bias_ref.shape[1:], MASK_BIAS, bias_ref.dtype)
        return carry

    lax.fori_loop(n_chunks, n_chunks_total, fill_chunk, 0)


def _indexer_mask(q_idx_t, idx_w_t, k_lo, k_hi):
    s = q_idx_t.shape[1]
    tq = SEL_TQ
    assert tq % PLANE_KEYS == 0 and s % (KEY_BITS * SUBLANES) == 0
    return pl.pallas_call(
        _indexer_body,
        grid=(s // tq,),
        in_specs=[pl.BlockSpec((q_idx_t.shape[0], tq), lambda i: (0, i)),
                  pl.BlockSpec((IDX_HEADS, tq), lambda i: (0, i)),
                  pl.BlockSpec(k_lo.shape, lambda i: (0, 0)),
                  pl.BlockSpec(k_hi.shape, lambda i: (0, 0))],
        out_specs=pl.BlockSpec((s // ATT_KC, tq, ATT_KC), lambda i: (0, i, 0)),
        out_shape=jax.ShapeDtypeStruct((s // ATT_KC, s, ATT_KC), BF16),
        scratch_shapes=[pltpu.VMEM((s, tq), F32),
                        pltpu.VMEM((KEY_BITS, s // KEY_BITS, tq), jnp.int32),
                        pltpu.VMEM((s // KEY_BITS, tq), jnp.int32),
                        pltpu.VMEM((SUBLANES, tq), jnp.int32)],
        compiler_params=_params(("arbitrary",)),
        name="indexer_mask",
    )(q_idx_t, idx_w_t, k_lo, k_hi)


def _attention_body(q_ref, k_ref, v_ref, bias_ref, o_ref, m_ref, acc_ref):
    tq = q_ref.shape[0]
    chunk_keys = bias_ref.shape[2]
    i = pl.program_id(0)
    n_chunks = (i * tq + tq - 1) // chunk_keys + 1

    m_ref[...] = jnp.full(m_ref.shape, M_INIT, F32)
    acc_ref[...] = jnp.zeros(acc_ref.shape, F32)

    def step(first_chunk, width):
        kc = width * chunk_keys
        sub = kc // LANES
        k0 = pl.multiple_of(first_chunk * chunk_keys, chunk_keys)
        bias = jnp.concatenate([bias_ref[first_chunk + c] for c in range(width)], axis=1)
        ones = jnp.ones((kc, HEAD_DIM), v_ref.dtype)
        for h in range(ATTN_HEADS):
            cols = slice(h * HEAD_DIM, (h + 1) * HEAD_DIM)
            s = lax.dot_general(q_ref[:, cols], k_ref[pl.ds(k0, kc), cols],
                                (((1,), (1,)), ((), ())), preferred_element_type=F32)
            sb = s.astype(BF16) + bias
            part = sb[:, :LANES]
            for c in range(1, sub):
                part = jnp.maximum(part, sb[:, c * LANES:(c + 1) * LANES])
            m_prev = m_ref[h]
            m_next = jnp.maximum(m_prev, jnp.max(part.astype(F32), axis=1, keepdims=True))
            p = jnp.exp2(sb - jnp.concatenate([m_next.astype(BF16)] * sub, axis=1))
            alpha = jnp.exp2(m_prev - m_next)
            m_ref[h] = m_next
            pv = jnp.dot(p, jnp.concatenate([v_ref[pl.ds(k0, kc), cols], ones], axis=1),
                         preferred_element_type=F32)
            acc_ref[h] = jnp.concatenate([alpha, alpha], axis=1) * acc_ref[h] + pv

    def wide_step(j, carry):
        step(j * ATT_STEP_CHUNKS, ATT_STEP_CHUNKS)
        return carry

    n_wide = n_chunks // ATT_STEP_CHUNKS
    lax.fori_loop(0, n_wide, wide_step, 0)
    for r in range(1, ATT_STEP_CHUNKS):
        @pl.when(n_chunks % ATT_STEP_CHUNKS >= r)
        def _():
            step(n_wide * ATT_STEP_CHUNKS + r - 1, 1)

    for h in range(ATTN_HEADS):
        acc = acc_ref[h]
        o_ref[:, h * HEAD_DIM:(h + 1) * HEAD_DIM] = (acc[:, :HEAD_DIM] / acc[:, HEAD_DIM:]).astype(o_ref.dtype)


def _attention(q, k, v, bias):
    s, width = q.shape
    tq = SEL_TQ
    n_chunks, _, kc = bias.shape
    resident = dict(pipeline_mode=pl.Buffered(1))
    return pl.pallas_call(
        _attention_body,
        grid=(s // tq,),
        in_specs=[pl.BlockSpec((tq, width), lambda i: (i, 0)),
                  pl.BlockSpec((s, width), lambda i: (0, 0), **resident),
                  pl.BlockSpec((s, width), lambda i: (0, 0), **resident),
                  pl.BlockSpec((n_chunks, tq, kc), lambda i: (0, i, 0))],
        out_specs=pl.BlockSpec((tq, width), lambda i: (i, 0)),
        out_shape=jax.ShapeDtypeStruct((s, width), BF16),
        scratch_shapes=[pltpu.VMEM((ATTN_HEADS, tq, LANES), F32),
                        pltpu.VMEM((ATTN_HEADS, tq, 2 * HEAD_DIM), F32)],
        compiler_params=_params(("arbitrary",)),
        name="masked_attention",
    )(q, k, v, bias)


def _ffn_down_body(a_ref, w_ref, x_ref, g_ref, gate_ref, o_ref):
    f = jnp.dot(a_ref[...], w_ref[...], preferred_element_type=F32)
    o_ref[...] = x_ref[...] + gate_ref[...] * _rms(f, g_ref[...])


def _ffn_down(a, w, x1, g_post, gate):
    m, kdim = a.shape
    d = w.shape[1]
    tm = FFN_DOWN_TM
    row = pl.BlockSpec((1, d), lambda i: (0, 0))
    return pl.pallas_call(
        _ffn_down_body,
        grid=(m // tm,),
        in_specs=[pl.BlockSpec((tm, kdim), lambda i: (i, 0)),
                  pl.BlockSpec((kdim, d), lambda i: (0, 0), pipeline_mode=pl.Buffered(1)),
                  pl.BlockSpec((tm, d), lambda i: (i, 0)), row, row],
        out_specs=pl.BlockSpec((tm, d), lambda i: (i, 0)),
        out_shape=jax.ShapeDtypeStruct((m, d), F32),
        compiler_params=_params(("arbitrary",)),
        name="ffn_down_residual",
    )(a, w, x1, g_post, gate)


def _rope_angles(seq, dim, repeat=1):
    inv = jnp.tile(1.0 / (ROPE_THETA ** (jnp.arange(0, dim, 2, dtype=F32) / dim)), repeat)
    coarse = (jnp.arange(seq // ROPE_BLOCK, dtype=F32) * ROPE_BLOCK)[:, None] * inv[None, :]
    fine = jnp.arange(ROPE_BLOCK, dtype=F32)[:, None] * inv[None, :]
    cos_a, sin_a = jnp.cos(coarse)[:, None, :], jnp.sin(coarse)[:, None, :]
    cos_b, sin_b = jnp.cos(fine)[None, :, :], jnp.sin(fine)[None, :, :]
    cos = (cos_a * cos_b - sin_a * sin_b).reshape(seq, inv.shape[0])
    sin = (sin_a * cos_b + cos_a * sin_b).reshape(seq, inv.shape[0])
    return cos, sin


def _attn_rope_tables(seq):
    cos, sin = _rope_angles(seq, HEAD_DIM, repeat=2)
    sign = jnp.concatenate([-jnp.ones((HEAD_DIM // 2,), F32), jnp.ones((HEAD_DIM // 2,), F32)])
    return cos, sin * sign


def _index_rope_tables(seq):
    cos, sin = _rope_angles(seq, IDX_ROPE_DIM)
    return cos, sin, cos.T, sin.T


def _layer(x, mod, g_pre_mix, g_post_mix, w_in_t, gmlp_ln_g, gmlp_ln_b, gmlp_w_s, gmlp_b_s,
           q_lat_norm_g, w_q_up, w_qidx_up, kidx_ln_g, kidx_ln_b, w_proj_a, w_proj_b, w_out,
           g_pre_ffn, g_post_ffn, w_ffn_gate, w_ffn_up, w_ffn_down, tables):
    s, d = x.shape
    cc, ss, icos, isin, icos_t, isin_t = tables
    row = lambda a: a.reshape(1, -1)
    shift_m, scale_m, gate_m, shift_f, scale_f, gate_f = (mod[:, n * d:(n + 1) * d] for n in range(N_MOD))

    tm = MM_TM
    vec_d = lambda a: (a, (1, d), lambda j, i: (0, 0))
    u, h = _matmul([x], [(w_in_t, 0, False)], [0], [vec_d(row(g_pre_mix)), vec_d(scale_m), vec_d(shift_m)],
                   [((s, GMLP_WIDTH), BF16, (tm, GMLP_WIDTH), _tile), ((s, d), BF16, (tm, d), _rowblk)],
                   _ep_gelu, n=GMLP_WIDTH, tm=tm, tn=GMLP_WIDTH, sub_n=EPILOGUE_SUB_N,
                   sub_m=EPILOGUE_SUB_N, prologue=_pro_prenorm, name="prenorm_in_proj_u")
    in_proj = functools.partial(_simple_mm, h, w_in_t, kn=False)
    vn = in_proj(_ep_gelu_layernorm,
                 [(row(gmlp_ln_g), (1, GMLP_WIDTH), _colblk), (row(gmlp_ln_b), (1, GMLP_WIDTH), _colblk)],
                 BF16, n=GMLP_WIDTH, first=GMLP_WIDTH, tm=ROWNORM_TM, tn=GMLP_WIDTH, sub_m=EPILOGUE_SUB_M,
                 name="in_proj_v_ln")
    gmlp_bias = jnp.repeat(gmlp_b_s.T, LANES, axis=1)
    y_a = _gmlp(u, vn, gmlp_w_s, gmlp_bias)

    q_lat = in_proj(_ep_rmsnorm, [(row(q_lat_norm_g), (1, Q_LORA_RANK), _colblk)],
                    BF16, n=Q_LORA_RANK, first=COL_QLAT, tm=tm, tn=Q_LORA_RANK, name="in_proj_qlat")
    q_scale = HEAD_DIM ** -0.5 * math.log2(math.e)
    rope_extras = [(cc, (tm, HEAD_DIM), _rowblk), (ss, (tm, HEAD_DIM), _rowblk)]
    q = _simple_mm(q_lat, w_q_up, functools.partial(_ep_rope, scale=q_scale), rope_extras,
                   BF16, n=ATTN_WIDTH, tm=tm, tn=ATTN_WIDTH, sub_n=EPILOGUE_SUB_N, name="q_up_rope")
    k = in_proj(functools.partial(_ep_rope, scale=1.0), rope_extras, BF16,
                n=ATTN_WIDTH, first=COL_K, tm=tm, tn=ATTN_WIDTH, sub_n=EPILOGUE_SUB_N,
                name="in_proj_k_rope")
    v = in_proj(_ep_cast, [], BF16, n=ATTN_WIDTH, first=COL_V, tm=tm, tn=ATTN_WIDTH,
                sub_n=EPILOGUE_SUB_N, name="in_proj_v")
    half = IDX_ROPE_DIM // 2
    q_idx_t = _simple_mm(q_lat, w_qidx_up, _ep_partial_rope_t,
                         [(icos_t, (half, tm), _colblk_t), (isin_t, (half, tm), _colblk_t)],
                         BF16, n=IDX_HEADS * IDX_HEAD_DIM, tm=tm, tn=QIDX_TN, name="qidx_up_rope",
                         out_t=True)
    pad = lambda a: jnp.pad(a, (0, LANES - a.shape[0])).reshape(1, LANES)
    idx_w_scale = (IDX_HEADS ** -0.5) * (IDX_HEAD_DIM ** -0.5)
    lane_tile = lambda dtype: ((s, LANES), dtype, (tm, LANES), _tile)
    k_lo, k_hi, idx_w_t = _matmul(
        [h], [(w_in_t, COL_KIDX, False)], [0],
        [(pad(kidx_ln_g), (1, LANES), _colblk), (pad(kidx_ln_b), (1, LANES), _colblk),
         (icos, (tm, half), _rowblk), (isin, (tm, half), _rowblk)],
        [lane_tile(BF16), lane_tile(BF16), ((IDX_HEADS, s), F32, (IDX_HEADS, tm), _colblk_t)],
        functools.partial(_ep_index_keys, w_scale=idx_w_scale), n=LANES, tm=tm, tn=LANES,
        name="in_proj_index_keys")
    bias = _indexer_mask(q_idx_t, idx_w_t, k_lo, k_hi)
    y_b = _attention(q, k, v, bias)

    gates = in_proj(_ep_sigmoid, [], BF16, n=2 * d, first=COL_GATE, tm=tm, tn=MM_TN,
                    sub_n=EPILOGUE_SUB_N, name="in_proj_gates")
    n_gate_blocks = d // MM_TN
    merged = _matmul([y_a, y_b], [(w_proj_a, 0, True), (w_proj_b, 0, True)], [0, 1],
                     [(gates, (tm, MM_TN), _tile),
                      (gates, (tm, MM_TN), lambda j, i: (i, j + n_gate_blocks))],
                     [((s, d), BF16, (tm, MM_TN), _tile)], _ep_merge, n=d, tm=tm, tn=MM_TN,
                     sub_n=EPILOGUE_SUB_N, name="branch_merge")[0]
    tm_full = ROWNORM_TM
    vec = lambda a: (a, (1, d), _colblk)
    x1, h2 = _matmul([merged], [(w_out, 0, True)], [0],
                     [(x, (tm_full, d), _rowblk), vec(row(g_post_mix)), vec(gate_m),
                      vec(row(g_pre_ffn)), vec(scale_f), vec(shift_f)],
                     [((s, d), F32, (tm_full, d), _tile), ((s, d), BF16, (tm_full, d), _tile)],
                     _ep_mix_residual, n=d, tm=tm_full, tn=d, sub_m=EPILOGUE_SUB_M, name="out_proj_residual")

    ffn_hidden = w_ffn_gate.shape[1]
    tn_ffn = FFN_UP_TN
    n_row_blocks = s // tm
    down_rows = ffn_hidden // ((ffn_hidden // tn_ffn) * n_row_blocks)
    down_blk = ((down_rows, d), lambda j, i: (j * n_row_blocks + i, 0))
    act, w_down = _matmul([h2], [(w_ffn_gate, 0, True), (w_ffn_up, 0, True)], [0, 0],
                          [(w_ffn_down,) + down_blk],
                          [((s, ffn_hidden), BF16, (tm, tn_ffn), _tile), (w_ffn_down.shape, BF16) + down_blk],
                          _ep_swiglu, n=ffn_hidden, tm=tm, tn=tn_ffn, name="ffn_up")
    return _ffn_down(act, w_down, x1, row(g_post_ffn), gate_f)


def kernel(x, c, w_mod, b_mod, g_pre_mix, g_post_mix, w_in, gmlp_ln_g, gmlp_ln_b, gmlp_w_s, gmlp_b_s, q_lat_norm_g, w_q_up, w_qidx_up, kidx_ln_g, kidx_ln_b, w_proj_a, w_proj_b, w_out, g_pre_ffn, g_post_ffn, w_ffn_gate, w_ffn_up, w_ffn_down):
    batch, seq, d = x.shape
    assert batch == 1 and d == D_MODEL
    tables = _attn_rope_tables(seq) + _index_rope_tables(seq)
    y = x[0]
    for l in range(w_mod.shape[0]):
        mod = _modulation(c, w_mod[l], b_mod[l])
        y = _layer(y, mod, g_pre_mix[l], g_post_mix[l], w_in[l].T, gmlp_ln_g[l], gmlp_ln_b[l],
                   gmlp_w_s[l], gmlp_b_s[l], q_lat_norm_g[l], w_q_up[l], w_qidx_up[l],
                   kidx_ln_g[l], kidx_ln_b[l], w_proj_a[l], w_proj_b[l], w_out[l],
                   g_pre_ffn[l], g_post_ffn[l], w_ffn_gate[l], w_ffn_up[l], w_ffn_down[l], tables)
    return y[None]
```

```python
import functools
import math

import jax
import jax.numpy as jnp
from jax import lax
from jax.experimental import pallas as pl
from jax.experimental.pallas import tpu as pltpu

F32 = jnp.float32
BF16 = jnp.bfloat16

D_MODEL = 2048
GMLP_WIDTH = 1024
GMLP_GROUPS = 8
CHUNK = 128
ATTN_HEADS = 8
HEAD_DIM = 128
ATTN_WIDTH = ATTN_HEADS * HEAD_DIM
Q_LORA_RANK = 512
IDX_HEADS = 16
IDX_HEAD_DIM = 64
IDX_ROPE_DIM = 32
INDEX_TOPK = 256
ROPE_THETA = 10000.0
N_MOD = 6
NORM_EPS = 1e-6

COL_QLAT = 2 * GMLP_WIDTH
COL_K = COL_QLAT + Q_LORA_RANK
COL_V = COL_K + ATTN_WIDTH
COL_KIDX = COL_V + ATTN_WIDTH
COL_IDXW = COL_KIDX + IDX_HEAD_DIM
COL_GATE = COL_IDXW + IDX_HEADS

LANES = 128
SUBLANES = 8
VMEM_LIMIT = 56 * 1024 * 1024

MASK_BIAS = -(2.0 ** 100)
M_INIT = -(2.0 ** 99)
INT_MIN = -2 ** 31
KEY_NEG_INF = -2139095041

SEL_TQ = 256
KEY_BITS = 32
PLANE_KEYS = KEY_BITS * SUBLANES
ATT_KC = 2 * PLANE_KEYS
ATT_STEP_CHUNKS = 2
EPILOGUE_SUB_N = 512
EPILOGUE_SUB_M = 256
ROPE_BLOCK = 64

MM_TM = 1024
MM_TN = 1024
ROWNORM_TM = 512
FFN_UP_TN = 512
FFN_DOWN_TM = 256
QIDX_TN = 512
STREAM_TM = 1024
MOD_TN = 1024


def _params(sem=None):
    return pltpu.CompilerParams(dimension_semantics=sem, vmem_limit_bytes=VMEM_LIMIT)


def _mod_body(cb_ref, w_ref, b_ref, o_ref):
    cb = cb_ref[...]
    tn = o_ref.shape[1]
    parts = [jnp.sum(w_ref[:, p * LANES:(p + 1) * LANES] * cb, axis=0, keepdims=True)
             for p in range(tn // LANES)]
    o_ref[...] = jnp.concatenate(parts, axis=1) + b_ref[...]


def _modulation(c, w_mod, b_mod):
    k, n = w_mod.shape
    tn = MOD_TN
    cb = jnp.broadcast_to(c.reshape(k, 1), (k, LANES))
    return pl.pallas_call(
        _mod_body,
        grid=(n // tn,),
        in_specs=[pl.BlockSpec((k, LANES), lambda j: (0, 0)),
                  pl.BlockSpec((k, tn), lambda j: (0, j)),
                  pl.BlockSpec((1, tn), lambda j: (0, j))],
        out_specs=pl.BlockSpec((1, tn), lambda j: (0, j)),
        out_shape=jax.ShapeDtypeStruct((1, n), F32),
        compiler_params=_params(("arbitrary",)),
        name="modulation",
    )(cb, w_mod, b_mod.reshape(1, n))


def _rms(x, g):
    return x * lax.rsqrt(jnp.mean(x * x, axis=-1, keepdims=True) + NORM_EPS) * g


def _mm_body(*refs, n_x, pairs, w_kn, out_t, sub_m, sub_n, n_extra, n_out, epilogue, prologue):
    n_w = len(pairs)
    x_refs = refs[:n_x]
    w_refs = refs[n_x:n_x + n_w]
    e_refs = refs[n_x + n_w:n_x + n_w + n_extra]
    o_refs = refs[n_x + n_w + n_extra:n_x + n_w + n_extra + n_out]
    wb_refs = refs[n_x + n_w + n_extra + n_out:]

    @pl.when(pl.program_id(1) == 0)
    def _():
        for w_ref, wb_ref, kn in zip(w_refs, wb_refs, w_kn):
            w = w_ref[...]
            if kn == out_t:
                w = w.T
            wb_ref[...] = w.astype(wb_ref.dtype)

    if out_t:
        accs = [lax.dot_general(wb_ref[...], x_refs[xi][...], (((1,), (1,)), ((), ())),
                                preferred_element_type=F32)
                for xi, wb_ref in zip(pairs, wb_refs)]
        epilogue(accs, e_refs, o_refs, slice(0, wb_refs[0].shape[0]), slice(0, x_refs[0].shape[0]))
        return
    tm, tn = x_refs[0].shape[0], wb_refs[0].shape[1]
    for r0 in range(0, tm, sub_m):
        rs = slice(r0, r0 + sub_m)
        if prologue is None:
            lhs = [x_ref[rs, :] for x_ref in x_refs]
        else:
            lhs = prologue(x_refs, e_refs, o_refs, rs)
        for c0 in range(0, tn, sub_n):
            cs = slice(c0, c0 + sub_n)
            accs = [jnp.dot(lhs[xi], wb_ref[:, cs], preferred_element_type=F32)
                    for xi, wb_ref in zip(pairs, wb_refs)]
            epilogue(accs, e_refs, o_refs, cs, rs)


def _matmul(xs, ws, pairs, extras, outs, epilogue, *, n, tm, tn, name, out_t=False, sub_n=None,
            sub_m=None, prologue=None):
    m = xs[0].shape[0]
    n_col = n // tn
    w_mode = dict(pipeline_mode=pl.Buffered(1)) if n_col == 1 else {}
    in_specs = [pl.BlockSpec((tm, x.shape[1]), lambda j, i: (i, 0)) for x in xs]
    scratch = []
    for w, first, kn in ws:
        if kn:
            kdim = w.shape[0]
            assert first % tn == 0
            in_specs.append(pl.BlockSpec((kdim, tn), functools.partial(_w_cols, first=first // tn), **w_mode))
        else:
            kdim = w.shape[1]
            in_specs.append(pl.BlockSpec((pl.Element(tn), pl.Element(kdim)),
                                         functools.partial(_w_rows, first=first, tn=tn), **w_mode))
        scratch.append(pltpu.VMEM((tn, kdim) if out_t else (kdim, tn), BF16))
    in_specs += [pl.BlockSpec(bs, im) for _, bs, im in extras]
    body = functools.partial(_mm_body, n_x=len(xs), pairs=tuple(pairs), w_kn=tuple(kn for _, _, kn in ws),
                             out_t=out_t, sub_m=sub_m or tm, sub_n=sub_n or tn, n_extra=len(extras),
                             n_out=len(outs), epilogue=epilogue, prologue=prologue)
    return pl.pallas_call(
        body,
        grid=(n_col, m // tm),
        in_specs=in_specs,
        out_specs=[pl.BlockSpec(bs, im) for _, _, bs, im in outs],
        out_shape=[jax.ShapeDtypeStruct(s, d) for s, d, _, _ in outs],
        scratch_shapes=scratch,
        compiler_params=_params(("arbitrary", "arbitrary")),
        name=name,
    )(*xs, *[w for w, _, _ in ws], *[a for a, _, _ in extras])


def _w_cols(j, i, *, first):
    return (0, first + j)


def _w_rows(j, i, *, first, tn):
    assert first % SUBLANES == 0 and tn % SUBLANES == 0
    return (pl.multiple_of(first + j * tn, SUBLANES), 0)


def _tile(j, i):
    return (i, j)


def _tile_t(j, i):
    return (j, i)


def _rowblk(j, i):
    return (i, 0)


def _colblk(j, i):
    return (0, j)


def _colblk_t(j, i):
    return (0, i)


def _pro_prenorm(x_refs, e_refs, o_refs, rows):
    g, scale, shift = e_refs[0][...], e_refs[1][...], e_refs[2][...]
    h = (_rms(x_refs[0][rows, :], g) * (1.0 + scale) + shift).astype(o_refs[1].dtype)
    o_refs[1][rows, :] = h
    return [h]


def _ep_gelu(accs, e_refs, o_refs, cols, rows):
    o_refs[0][rows, cols] = jax.nn.gelu(accs[0]).astype(o_refs[0].dtype)


def _ep_gelu_layernorm(accs, e_refs, o_refs, cols, rows):
    z = jax.nn.gelu(accs[0])
    mu = jnp.mean(z, axis=-1, keepdims=True)
    zc = z - mu
    var = jnp.mean(zc * zc, axis=-1, keepdims=True)
    y = zc * lax.rsqrt(var + NORM_EPS) * e_refs[0][...] + e_refs[1][...]
    o_refs[0][rows, :] = y.astype(o_refs[0].dtype)


def _ep_rmsnorm(accs, e_refs, o_refs, cols, rows):
    o_refs[0][rows, :] = _rms(accs[0], e_refs[0][...]).astype(o_refs[0].dtype)


def _ep_cast(accs, e_refs, o_refs, cols, rows):
    o_refs[0][rows, cols] = accs[0].astype(o_refs[0].dtype)


def _ep_sigmoid(accs, e_refs, o_refs, cols, rows):
    o_refs[0][rows, cols] = (0.5 * jnp.tanh(0.5 * accs[0]) + 0.5).astype(o_refs[0].dtype)


def _ep_rope(accs, e_refs, o_refs, cols, rows, *, scale):
    cc = e_refs[0][rows, :]
    ss = e_refs[1][rows, :]
    acc = accs[0]
    for h in range(acc.shape[1] // HEAD_DIM):
        xh = acc[:, h * HEAD_DIM:(h + 1) * HEAD_DIM]
        r = xh * cc + pltpu.roll(xh, HEAD_DIM // 2, 1) * ss
        if scale != 1.0:
            r = r * scale
        c0 = cols.start + h * HEAD_DIM
        o_refs[0][rows, c0:c0 + HEAD_DIM] = r.astype(o_refs[0].dtype)


def _ep_partial_rope_t(accs, e_refs, o_refs, cols, rows):
    c = e_refs[0][...]
    s = e_refs[1][...]
    acc = accs[0]
    half = IDX_ROPE_DIM // 2
    for h in range(acc.shape[0] // IDX_HEAD_DIM):
        r0 = h * IDX_HEAD_DIM
        x1 = acc[r0:r0 + half, :]
        x2 = acc[r0 + half:r0 + 2 * half, :]
        o_refs[0][r0:r0 + half, :] = (x1 * c - x2 * s).astype(o_refs[0].dtype)
        o_refs[0][r0 + half:r0 + 2 * half, :] = (x2 * c + x1 * s).astype(o_refs[0].dtype)
        o_refs[0][r0 + 2 * half:r0 + IDX_HEAD_DIM, :] = (
            acc[r0 + 2 * half:r0 + IDX_HEAD_DIM, :].astype(o_refs[0].dtype))


def _partial_rope(x, c, a, b):
    half = IDX_ROPE_DIM // 2
    return x * c + pltpu.roll(x, LANES - half, 1) * a + pltpu.roll(x, half, 1) * b


def _ep_index_keys(accs, e_refs, o_refs, cols, rows, *, w_scale):
    g, bb = e_refs[0][...], e_refs[1][...]
    cos, sin = e_refs[2][rows, :], e_refs[3][rows, :]
    acc = accs[0]
    n, half = acc.shape[0], IDX_ROPE_DIM // 2
    c = jnp.concatenate([cos, cos, jnp.ones((n, LANES - 2 * half), F32)], axis=1)
    a = jnp.concatenate([-sin, jnp.zeros((n, LANES - half), F32)], axis=1)
    b = jnp.concatenate([jnp.zeros((n, half), F32), sin, jnp.zeros((n, LANES - 2 * half), F32)], axis=1)
    lane = lax.broadcasted_iota(jnp.int32, acc.shape, 1)
    is_key = lane < IDX_HEAD_DIM
    mu = jnp.sum(jnp.where(is_key, acc, 0.0), axis=-1, keepdims=True) / IDX_HEAD_DIM
    xc = jnp.where(is_key, acc - mu, 0.0)
    var = jnp.sum(xc * xc, axis=-1, keepdims=True) / IDX_HEAD_DIM
    y = xc * lax.rsqrt(var + NORM_EPS) * g + bb
    y = jnp.where(is_key, _partial_rope(y, c, a, b), 0.0)
    o_refs[0][rows, :] = y.astype(o_refs[0].dtype)
    o_refs[1][rows, :] = pltpu.roll(y, IDX_HEAD_DIM, 1).astype(o_refs[1].dtype)
    o_refs[2][:, rows] = (acc * w_scale).T[IDX_HEAD_DIM:IDX_HEAD_DIM + IDX_HEADS, :]


def _ep_merge(accs, e_refs, o_refs, cols, rows):
    o = e_refs[0][rows, cols].astype(F32) * accs[0] + e_refs[1][rows, cols].astype(F32) * accs[1]
    o_refs[0][rows, cols] = o.astype(o_refs[0].dtype)


def _ep_mix_residual(accs, e_refs, o_refs, cols, rows):
    x_ref, g_post, gate, g_pre, scale, shift = e_refs
    x1 = x_ref[rows, :] + gate[...] * _rms(accs[0], g_post[...])
    o_refs[0][rows, :] = x1
    h2 = _rms(x1, g_pre[...]) * (1.0 + scale[...]) + shift[...]
    o_refs[1][rows, :] = h2.astype(o_refs[1].dtype)


def _ep_swiglu(accs, e_refs, o_refs, cols, rows):
    o_refs[0][rows, cols] = (jax.nn.silu(accs[0]) * accs[1]).astype(o_refs[0].dtype)

    if cols.start == 0 and rows.start == 0:
        o_refs[1][...] = e_refs[0][...].astype(o_refs[1].dtype)


def _simple_mm(x, w, epilogue, extras, out_dtype, *, n, tm, tn, name, first=0, kn=True, out_t=False,
               sub_n=None, sub_m=None):
    m = x.shape[0]
    out = ((n, m), out_dtype, (tn, tm), _tile_t) if out_t else ((m, n), out_dtype, (tm, tn), _tile)
    return _matmul([x], [(w, first, kn)], [0], extras, [out], epilogue, n=n, tm=tm, tn=tn,
                   name=name, out_t=out_t, sub_n=sub_n, sub_m=sub_m)[0]


def _gmlp_body(u_ref, v_ref, w_ref, b_ref, o_ref):
    t = lax.broadcasted_iota(jnp.int32, (CHUNK, CHUNK), 0)
    s = lax.broadcasted_iota(jnp.int32, (CHUNK, CHUNK), 1)
    causal = s <= t
    for g in range(GMLP_GROUPS):
        w = jnp.where(causal, w_ref[g], 0.0).astype(BF16)
        cols = slice(g * LANES, (g + 1) * LANES)
        bias = b_ref[:, cols]
        for c in range(u_ref.shape[0] // CHUNK):
            rows = slice(c * CHUNK, (c + 1) * CHUNK)
            sv = jnp.dot(w, v_ref[rows, cols], preferred_element_type=F32) + bias
            o_ref[rows, cols] = (u_ref[rows, cols].astype(F32) * sv).astype(o_ref.dtype)


def _gmlp(u, vn, w_s, bias):
    m, n = u.shape
    tm = STREAM_TM
    blk = pl.BlockSpec((tm, n), lambda i: (i, 0))
    return pl.pallas_call(
        _gmlp_body,
        grid=(m // tm,),
        in_specs=[blk, blk,
                  pl.BlockSpec(w_s.shape, lambda i: (0, 0, 0)),
                  pl.BlockSpec(bias.shape, lambda i: (0, 0))],
        out_specs=blk,
        out_shape=jax.ShapeDtypeStruct((m, n), BF16),
        compiler_params=_params(("arbitrary",)),
        name="gmlp_gating",
    )(u, vn, w_s, bias)


def _bit_planes(words):
    a = list(words)
    j, mask = KEY_BITS // 2, 0x0000FFFF
    while j:
        for k in range(KEY_BITS):
            if k & j:
                continue
            t = (a[k] ^ (a[k + j] >> j)) & jnp.int32(mask)
            a[k] = a[k] ^ t
            a[k + j] = a[k + j] ^ (t << j)
        j //= 2
        mask = (mask ^ (mask << j)) & 0xFFFFFFFF
    return a


def _score_key(score):
    bits = lax.bitcast_convert_type(score, jnp.int32)
    return bits ^ ((bits >> 31) & jnp.int32(0x7FFFFFFF))


def _key_score(key):
    return lax.bitcast_convert_type(key ^ ((key >> 31) & jnp.int32(0x7FFFFFFF)), F32)


def _indexer_body(qt_ref, wt_ref, klo_ref, khi_ref, bias_ref, score_ref, plane_ref, live_ref, kth_ref):
    tq = qt_ref.shape[1]
    n_chunks_total = bias_ref.shape[0]
    i = pl.program_id(0)
    n_chunks = ((i + 1) * tq + ATT_KC - 1) // ATT_KC
    blocks_per_chunk = ATT_KC // PLANE_KEYS
    n_blocks = n_chunks * blocks_per_chunk
    heads_per_group = LANES // IDX_HEAD_DIM
    k_refs = (klo_ref, khi_ref)

    @pl.when(i == 0)
    def _():
        plane_ref[...] = jnp.zeros(plane_ref.shape, jnp.int32)

    q_pos = i * tq + lax.broadcasted_iota(jnp.int32, (ATT_KC, tq), 1)
    k_off = lax.broadcasted_iota(jnp.int32, (ATT_KC, tq), 0)

    def score_chunk(c):
        k0 = pl.multiple_of(c * ATT_KC, ATT_KC)
        score = jnp.zeros((ATT_KC, tq), F32)
        for h in range(IDX_HEADS):
            grp, r = divmod(h, heads_per_group)
            logit = jnp.dot(k_refs[r][pl.ds(k0, ATT_KC), :], qt_ref[grp * LANES:(grp + 1) * LANES, :],
                            preferred_element_type=F32)
            score = score + jnp.maximum(logit, 0.0) * wt_ref[h:h + 1, :]
        score_ref[pl.ds(k0, ATT_KC), :] = jnp.where(k0 + k_off <= q_pos, score, -jnp.inf)

    def slice_chunk(c):
        for hh in range(blocks_per_chunk):
            blk = c * blocks_per_chunk + hh
            k0 = pl.multiple_of(blk * PLANE_KEYS, PLANE_KEYS)
            bits = lax.bitcast_convert_type(score_ref[pl.ds(k0, PLANE_KEYS), :], jnp.int32)
            planes = _bit_planes([bits[w * SUBLANES:(w + 1) * SUBLANES, :] for w in range(KEY_BITS)])
            r0 = pl.multiple_of(blk * SUBLANES, SUBLANES)
            sign = planes[0]
            plane_ref[0, pl.ds(r0, SUBLANES), :] = ~sign
            for r in range(1, KEY_BITS):
                plane_ref[r, pl.ds(r0, SUBLANES), :] = planes[r] ^ sign

    def score_and_slice(c, carry):
        slice_chunk(c - 1)
        score_chunk(c)
        return carry

    score_chunk(0)
    lax.fori_loop(1, n_chunks, score_and_slice, 0)
    slice_chunk(n_chunks - 1)

    def count_bits(words):
        ones = lax.population_count(words)
        part = jnp.sum(ones.reshape(words.shape[0] // SUBLANES, SUBLANES, tq), axis=0)
        return jnp.sum(part.astype(F32), axis=0, keepdims=True)

    def radix_select(rows):
        row_id = lax.broadcasted_iota(jnp.int32, (rows, tq), 0)
        live_ref[:rows] = jnp.where(row_id < n_blocks * SUBLANES, jnp.int32(-1), jnp.int32(0))

        def search_bit(r, carry):
            need, kth = carry
            total = count_bits(live_ref[:rows] & plane_ref[r, :rows])
            take = total >= need
            keep_clear = jnp.where(take, jnp.int32(0), jnp.int32(-1))
            live_ref[:rows] = live_ref[:rows] & (plane_ref[r, :rows] ^ keep_clear)
            need = jnp.where(take, need, need - total)
            kth = jnp.where(take, kth | jnp.left_shift(jnp.int32(1), KEY_BITS - 1 - r), kth)
            return need, kth

        _, kth = lax.fori_loop(0, KEY_BITS, search_bit,
                               (jnp.full((1, tq), float(INDEX_TOPK), F32), jnp.zeros((1, tq), jnp.int32)))
        kth_ref[0:1] = kth ^ jnp.int32(INT_MIN)

    plane_rows = live_ref.shape[0]
    quarter = plane_rows // 4
    for rows in range(quarter, plane_rows + 1, quarter):
        @pl.when((n_blocks * SUBLANES > rows - quarter) & (n_blocks * SUBLANES <= rows))
        def _():
            radix_select(rows)

    def count_scores(pred):
        group = 4 * SUBLANES

        def add_chunk(c, acc):
            k0 = pl.multiple_of(c * ATT_KC, ATT_KC)
            hit = jnp.where(pred(score_ref[pl.ds(k0, ATT_KC), :]), 1.0, 0.0)
            return acc + jnp.sum(hit.reshape(ATT_KC // group, group, tq), axis=0)

        acc = lax.fori_loop(0, n_chunks, add_chunk, jnp.zeros((group, tq), F32))
        return jnp.sum(acc, axis=0, keepdims=True)

    few_keys = i * tq + lax.broadcasted_iota(jnp.int32, (1, tq), 1) + 1 < INDEX_TOPK
    lowest = float(jnp.finfo(F32).min)
    thr = jnp.where(few_keys, lowest, _key_score(jnp.maximum(kth_ref[0:1], KEY_NEG_INF + 1)))

    def bias_tile(sel):
        return jnp.where(sel, 0.0, MASK_BIAS).T

    def write_chunk(c, kept):
        parts = []
        for hh in range(blocks_per_chunk):
            k0 = pl.multiple_of(c * ATT_KC + hh * PLANE_KEYS, PLANE_KEYS)
            sel = score_ref[pl.ds(k0, PLANE_KEYS), :] >= thr
            parts.append(bias_tile(sel))
            hit = jnp.where(sel, 1.0, 0.0)
            kept = kept + jnp.sum(hit.reshape(PLANE_KEYS // kept.shape[0], kept.shape[0], tq), axis=0)
        bias_ref[c] = jnp.concatenate(parts, axis=1).astype(bias_ref.dtype)
        return kept

    kept = lax.fori_loop(0, n_chunks, write_chunk, jnp.zeros((4 * SUBLANES, tq), F32))
    kept = jnp.sum(kept, axis=0, keepdims=True)
    unsettled = jnp.max(jnp.where(few_keys | (kept == float(INDEX_TOPK)), 0.0, 1.0))

    @pl.when(unsettled > 0.0)
    def _():
        def search_bit(b, prefix):
            cand = prefix ^ jnp.left_shift(jnp.int32(1), KEY_BITS - 1 - b)
            cand_score = _key_score(jnp.maximum(cand, KEY_NEG_INF))
            admitted = count_scores(lambda sc: sc >= cand_score)
            return jnp.where(admitted >= float(INDEX_TOPK), cand, prefix)

        kth = lax.fori_loop(0, KEY_BITS, search_bit, jnp.full((1, tq), INT_MIN, jnp.int32))
        kth_score = jnp.where(few_keys, lowest, _key_score(jnp.maximum(kth, KEY_NEG_INF + 1)))
        above = count_scores(lambda sc: sc > kth_score)
        need = jnp.where(few_keys, float(2 ** 24), float(INDEX_TOPK) - above)

        row = lax.broadcasted_iota(jnp.int32, (PLANE_KEYS, PLANE_KEYS), 0)
        col = lax.broadcasted_iota(jnp.int32, (PLANE_KEYS, PLANE_KEYS), 1)
        prefix_sum = jnp.where(col <= row, 1.0, 0.0).astype(BF16)

        def write_chunk(c, seen):
            parts = []
            for hh in range(blocks_per_chunk):
                k0 = pl.multiple_of(c * ATT_KC + hh * PLANE_KEYS, PLANE_KEYS)
                sc = score_ref[pl.ds(k0, PLANE_KEYS), :]
                tie = sc == kth_score
                rank = seen + jnp.dot(prefix_sum, jnp.where(tie, 1.0, 0.0).astype(BF16),
                                      preferred_element_type=F32)
                parts.append(bias_tile((sc > kth_score) | (tie & (rank <= need))))
                seen = rank[PLANE_KEYS - 1:PLANE_KEYS, :]
            bias_ref[c] = jnp.concatenate(parts, axis=1).astype(bias_ref.dtype)
            return seen

        lax.fori_loop(0, n_chunks, write_chunk, jnp.zeros((1, tq), F32))

    def fill_chunk(c, carry):
        bias_ref[c] = jnp.full(bias_ref.shape[1:], MASK_BIAS, bias_ref.dtype)
        return carry

    lax.fori_loop(n_chunks, n_chunks_total, fill_chunk, 0)


def _indexer_mask(q_idx_t, idx_w_t, k_lo, k_hi):
    s = q_idx_t.shape[1]
    tq = SEL_TQ
    assert tq % PLANE_KEYS == 0 and s % (KEY_BITS * SUBLANES) == 0
    return pl.pallas_call(
        _indexer_body,
        grid=(s // tq,),
        in_specs=[pl.BlockSpec((q_idx_t.shape[0], tq), lambda i: (0, i)),
                  pl.BlockSpec((IDX_HEADS, tq), lambda i: (0, i)),
                  pl.BlockSpec(k_lo.shape, lambda i: (0, 0)),
                  pl.BlockSpec(k_hi.shape, lambda i: (0, 0))],
        out_specs=pl.BlockSpec((s // ATT_KC, tq, ATT_KC), lambda i: (0, i, 0)),
        out_shape=jax.ShapeDtypeStruct((s // ATT_KC, s, ATT_KC), BF16),
        scratch_shapes=[pltpu.VMEM((s, tq), F32),
                        pltpu.VMEM((KEY_BITS, s // KEY_BITS, tq), jnp.int32),
                        pltpu.VMEM((s // KEY_BITS, tq), jnp.int32),
                        pltpu.VMEM((SUBLANES, tq), jnp.int32)],
        compiler_params=_params(("arbitrary",)),
        name="indexer_mask",
    )(q_idx_t, idx_w_t, k_lo, k_hi)


def _attention_body(q_ref, k_ref, v_ref, bias_ref, o_ref, m_ref, acc_ref):
    tq = q_ref.shape[0]
    chunk_keys = bias_ref.shape[2]
    i = pl.program_id(0)
    n_chunks = (i * tq + tq - 1) // chunk_keys + 1

    m_ref[...] = jnp.full(m_ref.shape, M_INIT, F32)
    acc_ref[...] = jnp.zeros(acc_ref.shape, F32)

    def step(first_chunk, width):
        kc = width * chunk_keys
        sub = kc // LANES
        k0 = pl.multiple_of(first_chunk * chunk_keys, chunk_keys)
        bias = jnp.concatenate([bias_ref[first_chunk + c] for c in range(width)], axis=1)
        ones = jnp.ones((kc, HEAD_DIM), v_ref.dtype)
        for h in range(ATTN_HEADS):
            cols = slice(h * HEAD_DIM, (h + 1) * HEAD_DIM)
            s = lax.dot_general(q_ref[:, cols], k_ref[pl.ds(k0, kc), cols],
                                (((1,), (1,)), ((), ())), preferred_element_type=F32)
            sb = s.astype(BF16) + bias
            part = sb[:, :LANES]
            for c in range(1, sub):
                part = jnp.maximum(part, sb[:, c * LANES:(c + 1) * LANES])
            m_prev = m_ref[h]
            m_next = jnp.maximum(m_prev, jnp.max(part.astype(F32), axis=1, keepdims=True))
            p = jnp.exp2(sb - jnp.concatenate([m_next.astype(BF16)] * sub, axis=1))
            alpha = jnp.exp2(m_prev - m_next)
            m_ref[h] = m_next
            pv = jnp.dot(p, jnp.concatenate([v_ref[pl.ds(k0, kc), cols], ones], axis=1),
                         preferred_element_type=F32)
            acc_ref[h] = jnp.concatenate([alpha, alpha], axis=1) * acc_ref[h] + pv

    def wide_step(j, carry):
        step(j * ATT_STEP_CHUNKS, ATT_STEP_CHUNKS)
        return carry

    n_wide = n_chunks // ATT_STEP_CHUNKS
    lax.fori_loop(0, n_wide, wide_step, 0)
    for r in range(1, ATT_STEP_CHUNKS):
        @pl.when(n_chunks % ATT_STEP_CHUNKS >= r)
        def _():
            step(n_wide * ATT_STEP_CHUNKS + r - 1, 1)

    for h in range(ATTN_HEADS):
        acc = acc_ref[h]
        o_ref[:, h * HEAD_DIM:(h + 1) * HEAD_DIM] = (acc[:, :HEAD_DIM] / acc[:, HEAD_DIM:]).astype(o_ref.dtype)


def _attention(q, k, v, bias):
    s, width = q.shape
    tq = SEL_TQ
    n_chunks, _, kc = bias.shape
    resident = dict(pipeline_mode=pl.Buffered(1))
    return pl.pallas_call(
        _attention_body,
        grid=(s // tq,),
        in_specs=[pl.BlockSpec((tq, width), lambda i: (i, 0)),
                  pl.BlockSpec((s, width), lambda i: (0, 0), **resident),
                  pl.BlockSpec((s, width), lambda i: (0, 0), **resident),
                  pl.BlockSpec((n_chunks, tq, kc), lambda i: (0, i, 0))],
        out_specs=pl.BlockSpec((tq, width), lambda i: (i, 0)),
        out_shape=jax.ShapeDtypeStruct((s, width), BF16),
        scratch_shapes=[pltpu.VMEM((ATTN_HEADS, tq, LANES), F32),
                        pltpu.VMEM((ATTN_HEADS, tq, 2 * HEAD_DIM), F32)],
        compiler_params=_params(("arbitrary",)),
        name="masked_attention",
    )(q, k, v, bias)


def _ffn_down_body(a_ref, w_ref, x_ref, g_ref, gate_ref, o_ref):
    f = jnp.dot(a_ref[...], w_ref[...], preferred_element_type=F32)
    o_ref[...] = x_ref[...] + gate_ref[...] * _rms(f, g_ref[...])


def _ffn_down(a, w, x1, g_post, gate):
    m, kdim = a.shape
    d = w.shape[1]
    tm = FFN_DOWN_TM
    row = pl.BlockSpec((1, d), lambda i: (0, 0))
    return pl.pallas_call(
        _ffn_down_body,
        grid=(m // tm,),
        in_specs=[pl.BlockSpec((tm, kdim), lambda i: (i, 0)),
                  pl.BlockSpec((kdim, d), lambda i: (0, 0), pipeline_mode=pl.Buffered(1)),
                  pl.BlockSpec((tm, d), lambda i: (i, 0)), row, row],
        out_specs=pl.BlockSpec((tm, d), lambda i: (i, 0)),
        out_shape=jax.ShapeDtypeStruct((m, d), F32),
        compiler_params=_params(("arbitrary",)),
        name="ffn_down_residual",
    )(a, w, x1, g_post, gate)


def _rope_angles(seq, dim, repeat=1):
    inv = jnp.tile(1.0 / (ROPE_THETA ** (jnp.arange(0, dim, 2, dtype=F32) / dim)), repeat)
    coarse = (jnp.arange(seq // ROPE_BLOCK, dtype=F32) * ROPE_BLOCK)[:, None] * inv[None, :]
    fine = jnp.arange(ROPE_BLOCK, dtype=F32)[:, None] * inv[None, :]
    cos_a, sin_a = jnp.cos(coarse)[:, None, :], jnp.sin(coarse)[:, None, :]
    cos_b, sin_b = jnp.cos(fine)[None, :, :], jnp.sin(fine)[None, :, :]
    cos = (cos_a * cos_b - sin_a * sin_b).reshape(seq, inv.shape[0])
    sin = (sin_a * cos_b + cos_a * sin_b).reshape(seq, inv.shape[0])
    return cos, sin


def _attn_rope_tables(seq):
    cos, sin = _rope_angles(seq, HEAD_DIM, repeat=2)
    sign = jnp.concatenate([-jnp.ones((HEAD_DIM // 2,), F32), jnp.ones((HEAD_DIM // 2,), F32)])
    return cos, sin * sign


def _index_rope_tables(seq):
    cos, sin = _rope_angles(seq, IDX_ROPE_DIM)
    return cos, sin, cos.T, sin.T


def _layer(x, mod, g_pre_mix, g_post_mix, w_in_t, gmlp_ln_g, gmlp_ln_b, gmlp_w_s, gmlp_b_s,
           q_lat_norm_g, w_q_up, w_qidx_up, kidx_ln_g, kidx_ln_b, w_proj_a, w_proj_b, w_out,
           g_pre_ffn, g_post_ffn, w_ffn_gate, w_ffn_up, w_ffn_down, tables):
    s, d = x.shape
    cc, ss, icos, isin, icos_t, isin_t = tables
    row = lambda a: a.reshape(1, -1)
    shift_m, scale_m, gate_m, shift_f, scale_f, gate_f = (mod[:, n * d:(n + 1) * d] for n in range(N_MOD))

    tm = MM_TM
    vec_d = lambda a: (a, (1, d), lambda j, i: (0, 0))
    u, h = _matmul([x], [(w_in_t, 0, False)], [0], [vec_d(row(g_pre_mix)), vec_d(scale_m), vec_d(shift_m)],
                   [((s, GMLP_WIDTH), BF16, (tm, GMLP_WIDTH), _tile), ((s, d), BF16, (tm, d), _rowblk)],
                   _ep_gelu, n=GMLP_WIDTH, tm=tm, tn=GMLP_WIDTH, sub_n=EPILOGUE_SUB_N,
                   sub_m=EPILOGUE_SUB_N, prologue=_pro_prenorm, name="prenorm_in_proj_u")
    in_proj = functools.partial(_simple_mm, h, w_in_t, kn=False)
    vn = in_proj(_ep_gelu_layernorm,
                 [(row(gmlp_ln_g), (1, GMLP_WIDTH), _colblk), (row(gmlp_ln_b), (1, GMLP_WIDTH), _colblk)],
                 BF16, n=GMLP_WIDTH, first=GMLP_WIDTH, tm=tm, tn=GMLP_WIDTH, sub_m=EPILOGUE_SUB_M,
                 name="in_proj_v_ln")
    gmlp_bias = jnp.repeat(gmlp_b_s.T, LANES, axis=1)
    y_a = _gmlp(u, vn, gmlp_w_s, gmlp_bias)

    q_lat = in_proj(_ep_rmsnorm, [(row(q_lat_norm_g), (1, Q_LORA_RANK), _colblk)],
                    BF16, n=Q_LORA_RANK, first=COL_QLAT, tm=tm, tn=Q_LORA_RANK, name="in_proj_qlat")
    q_scale = HEAD_DIM ** -0.5 * math.log2(math.e)
    rope_extras = [(cc, (tm, HEAD_DIM), _rowblk), (ss, (tm, HEAD_DIM), _rowblk)]
    q = _simple_mm(q_lat, w_q_up, functools.partial(_ep_rope, scale=q_scale), rope_extras,
                   BF16, n=ATTN_WIDTH, tm=tm, tn=ATTN_WIDTH, sub_n=EPILOGUE_SUB_N, name="q_up_rope")
    k = in_proj(functools.partial(_ep_rope, scale=1.0), rope_extras, BF16,
                n=ATTN_WIDTH, first=COL_K, tm=tm, tn=ATTN_WIDTH, sub_n=EPILOGUE_SUB_N,
                name="in_proj_k_rope")
    v = in_proj(_ep_cast, [], BF16, n=ATTN_WIDTH, first=COL_V, tm=tm, tn=ATTN_WIDTH,
                sub_n=EPILOGUE_SUB_N, name="in_proj_v")
    half = IDX_ROPE_DIM // 2
    q_idx_t = _simple_mm(q_lat, w_qidx_up, _ep_partial_rope_t,
                         [(icos_t, (half, tm), _colblk_t), (isin_t, (half, tm), _colblk_t)],
                         BF16, n=IDX_HEADS * IDX_HEAD_DIM, tm=tm, tn=QIDX_TN, name="qidx_up_rope",
                         out_t=True)
    pad = lambda a: jnp.pad(a, (0, LANES - a.shape[0])).reshape(1, LANES)
    idx_w_scale = (IDX_HEADS ** -0.5) * (IDX_HEAD_DIM ** -0.5)
    lane_tile = lambda dtype: ((s, LANES), dtype, (tm, LANES), _tile)
    k_lo, k_hi, idx_w_t = _matmul(
        [h], [(w_in_t, COL_KIDX, False)], [0],
        [(pad(kidx_ln_g), (1, LANES), _colblk), (pad(kidx_ln_b), (1, LANES), _colblk),
         (icos, (tm, half), _rowblk), (isin, (tm, half), _rowblk)],
        [lane_tile(BF16), lane_tile(BF16), ((IDX_HEADS, s), F32, (IDX_HEADS, tm), _colblk_t)],
        functools.partial(_ep_index_keys, w_scale=idx_w_scale), n=LANES, tm=tm, tn=LANES,
        name="in_proj_index_keys")
    bias = _indexer_mask(q_idx_t, idx_w_t, k_lo, k_hi)
    y_b = _attention(q, k, v, bias)

    gates = in_proj(_ep_sigmoid, [], BF16, n=2 * d, first=COL_GATE, tm=tm, tn=MM_TN,
                    sub_n=EPILOGUE_SUB_N, name="in_proj_gates")
    n_gate_blocks = d // MM_TN
    merged = _matmul([y_a, y_b], [(w_proj_a, 0, True), (w_proj_b, 0, True)], [0, 1],
                     [(gates, (tm, MM_TN), _tile),
                      (gates, (tm, MM_TN), lambda j, i: (i, j + n_gate_blocks))],
                     [((s, d), BF16, (tm, MM_TN), _tile)], _ep_merge, n=d, tm=tm, tn=MM_TN,
                     sub_n=EPILOGUE_SUB_N, name="branch_merge")[0]
    tm_full = ROWNORM_TM
    vec = lambda a: (a, (1, d), _colblk)
    x1, h2 = _matmul([merged], [(w_out, 0, True)], [0],
                     [(x, (tm_full, d), _rowblk), vec(row(g_post_mix)), vec(gate_m),
                      vec(row(g_pre_ffn)), vec(scale_f), vec(shift_f)],
                     [((s, d), F32, (tm_full, d), _tile), ((s, d), BF16, (tm_full, d), _tile)],
                     _ep_mix_residual, n=d, tm=tm_full, tn=d, sub_m=EPILOGUE_SUB_M, name="out_proj_residual")

    ffn_hidden = w_ffn_gate.shape[1]
    tn_ffn = FFN_UP_TN
    n_row_blocks = s // tm
    down_rows = ffn_hidden // ((ffn_hidden // tn_ffn) * n_row_blocks)
    down_blk = ((down_rows, d), lambda j, i: (j * n_row_blocks + i, 0))
    act, w_down = _matmul([h2], [(w_ffn_gate, 0, True), (w_ffn_up, 0, True)], [0, 0],
                          [(w_ffn_down,) + down_blk],
                          [((s, ffn_hidden), BF16, (tm, tn_ffn), _tile), (w_ffn_down.shape, BF16) + down_blk],
                          _ep_swiglu, n=ffn_hidden, tm=tm, tn=tn_ffn, name="ffn_up")
    return _ffn_down(act, w_down, x1, row(g_post_ffn), gate_f)


def kernel(x, c, w_mod, b_mod, g_pre_mix, g_post_mix, w_in, gmlp_ln_g, gmlp_ln_b, gmlp_w_s, gmlp_b_s, q_lat_norm_g, w_q_up, w_qidx_up, kidx_ln_g, kidx_ln_b, w_proj_a, w_proj_b, w_out, g_pre_ffn, g_post_ffn, w_ffn_gate, w_ffn_up, w_ffn_down):
    batch, seq, d = x.shape
    assert batch == 1 and d == D_MODEL
    tables = _attn_rope_tables(seq) + _index_rope_tables(seq)
    y = x[0]
    for l in range(w_mod.shape[0]):
        mod = _modulation(c, w_mod[l], b_mod[l])
        y = _layer(y, mod, g_pre_mix[l], g_post_mix[l], w_in[l].T, gmlp_ln_g[l], gmlp_ln_b[l],
                   gmlp_w_s[l], gmlp_b_s[l], q_lat_norm_g[l], w_q_up[l], w_qidx_up[l],
                   kidx_ln_g[l], kidx_ln_b[l], w_proj_a[l], w_proj_b[l], w_out[l],
                   g_pre_ffn[l], g_post_ffn[l], w_ffn_gate[l], w_ffn_up[l], w_ffn_down[l], tables)
    return y[None]
```

```python
import functools
import math

import jax
import jax.numpy as jnp
from jax import lax
from jax.experimental import pallas as pl
from jax.experimental.pallas import tpu as pltpu

F32 = jnp.float32
BF16 = jnp.bfloat16

D_MODEL = 2048
GMLP_WIDTH = 1024
GMLP_GROUPS = 8
CHUNK = 128
ATTN_HEADS = 8
HEAD_DIM = 128
ATTN_WIDTH = ATTN_HEADS * HEAD_DIM
Q_LORA_RANK = 512
IDX_HEADS = 16
IDX_HEAD_DIM = 64
IDX_ROPE_DIM = 32
INDEX_TOPK = 256
ROPE_THETA = 10000.0
N_MOD = 6
NORM_EPS = 1e-6

COL_QLAT = 2 * GMLP_WIDTH
COL_K = COL_QLAT + Q_LORA_RANK
COL_V = COL_K + ATTN_WIDTH
COL_KIDX = COL_V + ATTN_WIDTH
COL_IDXW = COL_KIDX + IDX_HEAD_DIM
COL_GATE = COL_IDXW + IDX_HEADS

LANES = 128
SUBLANES = 8
VMEM_LIMIT = 56 * 1024 * 1024

MASK_BIAS = -(2.0 ** 100)
M_INIT = -(2.0 ** 99)
INT_MIN = -2 ** 31
KEY_NEG_INF = -2139095041

SEL_TQ = 256
KEY_BITS = 32
PLANE_KEYS = KEY_BITS * SUBLANES
ATT_KC = 2 * PLANE_KEYS
ATT_STEP_CHUNKS = 2
EPILOGUE_SUB_N = 512
EPILOGUE_SUB_M = 256
ROPE_BLOCK = 64

MM_TM = 1024
MM_TN = 1024
ROWNORM_TM = 512
FFN_UP_TN = 512
FFN_UP_TM = 2048
FFN_DOWN_TM = 256
QIDX_TN = 512
STREAM_TM = 1024
MOD_TN = 1024


def _params(sem=None):
    return pltpu.CompilerParams(dimension_semantics=sem, vmem_limit_bytes=VMEM_LIMIT)


def _mod_body(cb_ref, w_ref, b_ref, o_ref):
    cb = cb_ref[...]
    tn = o_ref.shape[1]
    parts = [jnp.sum(w_ref[:, p * LANES:(p + 1) * LANES] * cb, axis=0, keepdims=True)
             for p in range(tn // LANES)]
    o_ref[...] = jnp.concatenate(parts, axis=1) + b_ref[...]


def _modulation(c, w_mod, b_mod):
    k, n = w_mod.shape
    tn = MOD_TN
    cb = jnp.broadcast_to(c.reshape(k, 1), (k, LANES))
    return pl.pallas_call(
        _mod_body,
        grid=(n // tn,),
        in_specs=[pl.BlockSpec((k, LANES), lambda j: (0, 0)),
                  pl.BlockSpec((k, tn), lambda j: (0, j)),
                  pl.BlockSpec((1, tn), lambda j: (0, j))],
        out_specs=pl.BlockSpec((1, tn), lambda j: (0, j)),
        out_shape=jax.ShapeDtypeStruct((1, n), F32),
        compiler_params=_params(("arbitrary",)),
        name="modulation",
    )(cb, w_mod, b_mod.reshape(1, n))


def _rms(x, g):
    return x * lax.rsqrt(jnp.mean(x * x, axis=-1, keepdims=True) + NORM_EPS) * g


def _mm_body(*refs, n_x, pairs, w_kn, out_t, sub_m, sub_n, n_extra, n_out, epilogue, prologue):
    n_w = len(pairs)
    x_refs = refs[:n_x]
    w_refs = refs[n_x:n_x + n_w]
    e_refs = refs[n_x + n_w:n_x + n_w + n_extra]
    o_refs = refs[n_x + n_w + n_extra:n_x + n_w + n_extra + n_out]
    wb_refs = refs[n_x + n_w + n_extra + n_out:]

    @pl.when(pl.program_id(1) == 0)
    def _():
        for w_ref, wb_ref, kn in zip(w_refs, wb_refs, w_kn):
            w = w_ref[...]
            if kn == out_t:
                w = w.T
            wb_ref[...] = w.astype(wb_ref.dtype)

    if out_t:
        accs = [lax.dot_general(wb_ref[...], x_refs[xi][...], (((1,), (1,)), ((), ())),
                                preferred_element_type=F32)
                for xi, wb_ref in zip(pairs, wb_refs)]
        epilogue(accs, e_refs, o_refs, slice(0, wb_refs[0].shape[0]), slice(0, x_refs[0].shape[0]))
        return
    tm, tn = x_refs[0].shape[0], wb_refs[0].shape[1]
    for r0 in range(0, tm, sub_m):
        rs = slice(r0, r0 + sub_m)
        if prologue is None:
            lhs = [x_ref[rs, :] for x_ref in x_refs]
        else:
            lhs = prologue(x_refs, e_refs, o_refs, rs)
        for c0 in range(0, tn, sub_n):
            cs = slice(c0, c0 + sub_n)
            accs = [jnp.dot(lhs[xi], wb_ref[:, cs], preferred_element_type=F32)
                    for xi, wb_ref in zip(pairs, wb_refs)]
            epilogue(accs, e_refs, o_refs, cs, rs)


def _matmul(xs, ws, pairs, extras, outs, epilogue, *, n, tm, tn, name, out_t=False, sub_n=None,
            sub_m=None, prologue=None):
    m = xs[0].shape[0]
    n_col = n // tn
    w_mode = dict(pipeline_mode=pl.Buffered(1)) if n_col == 1 else {}
    in_specs = [pl.BlockSpec((tm, x.shape[1]), lambda j, i: (i, 0)) for x in xs]
    scratch = []
    for w, first, kn in ws:
        if kn:
            kdim = w.shape[0]
            assert first % tn == 0
            in_specs.append(pl.BlockSpec((kdim, tn), functools.partial(_w_cols, first=first // tn), **w_mode))
        else:
            kdim = w.shape[1]
            in_specs.append(pl.BlockSpec((pl.Element(tn), pl.Element(kdim)),
                                         functools.partial(_w_rows, first=first, tn=tn), **w_mode))
        scratch.append(pltpu.VMEM((tn, kdim) if out_t else (kdim, tn), BF16))
    in_specs += [pl.BlockSpec(bs, im) for _, bs, im in extras]
    body = functools.partial(_mm_body, n_x=len(xs), pairs=tuple(pairs), w_kn=tuple(kn for _, _, kn in ws),
                             out_t=out_t, sub_m=sub_m or tm, sub_n=sub_n or tn, n_extra=len(extras),
                             n_out=len(outs), epilogue=epilogue, prologue=prologue)
    return pl.pallas_call(
        body,
        grid=(n_col, m // tm),
        in_specs=in_specs,
        out_specs=[pl.BlockSpec(bs, im) for _, _, bs, im in outs],
        out_shape=[jax.ShapeDtypeStruct(s, d) for s, d, _, _ in outs],
        scratch_shapes=scratch,
        compiler_params=_params(("arbitrary", "arbitrary")),
        name=name,
    )(*xs, *[w for w, _, _ in ws], *[a for a, _, _ in extras])


def _w_cols(j, i, *, first):
    return (0, first + j)


def _w_rows(j, i, *, first, tn):
    assert first % SUBLANES == 0 and tn % SUBLANES == 0
    return (pl.multiple_of(first + j * tn, SUBLANES), 0)


def _tile(j, i):
    return (i, j)


def _tile_t(j, i):
    return (j, i)


def _rowblk(j, i):
    return (i, 0)


def _colblk(j, i):
    return (0, j)


def _colblk_t(j, i):
    return (0, i)


def _pro_prenorm(x_refs, e_refs, o_refs, rows):
    g, scale, shift = e_refs[0][...], e_refs[1][...], e_refs[2][...]
    h = (_rms(x_refs[0][rows, :], g) * (1.0 + scale) + shift).astype(o_refs[1].dtype)
    o_refs[1][rows, :] = h
    return [h]


def _ep_gelu(accs, e_refs, o_refs, cols, rows):
    o_refs[0][rows, cols] = jax.nn.gelu(accs[0]).astype(o_refs[0].dtype)


def _ep_gelu_layernorm(accs, e_refs, o_refs, cols, rows):
    z = jax.nn.gelu(accs[0])
    mu = jnp.mean(z, axis=-1, keepdims=True)
    zc = z - mu
    var = jnp.mean(zc * zc, axis=-1, keepdims=True)
    y = zc * lax.rsqrt(var + NORM_EPS) * e_refs[0][...] + e_refs[1][...]
    o_refs[0][rows, :] = y.astype(o_refs[0].dtype)


def _ep_rmsnorm(accs, e_refs, o_refs, cols, rows):
    o_refs[0][rows, :] = _rms(accs[0], e_refs[0][...]).astype(o_refs[0].dtype)


def _ep_cast(accs, e_refs, o_refs, cols, rows):
    o_refs[0][rows, cols] = accs[0].astype(o_refs[0].dtype)


def _ep_sigmoid(accs, e_refs, o_refs, cols, rows):
    o_refs[0][rows, cols] = (0.5 * jnp.tanh(0.5 * accs[0]) + 0.5).astype(o_refs[0].dtype)


def _ep_rope(accs, e_refs, o_refs, cols, rows, *, scale):
    cc = e_refs[0][rows, :]
    ss = e_refs[1][rows, :]
    acc = accs[0]
    for h in range(acc.shape[1] // HEAD_DIM):
        xh = acc[:, h * HEAD_DIM:(h + 1) * HEAD_DIM]
        r = xh * cc + pltpu.roll(xh, HEAD_DIM // 2, 1) * ss
        if scale != 1.0:
            r = r * scale
        c0 = cols.start + h * HEAD_DIM
        o_refs[0][rows, c0:c0 + HEAD_DIM] = r.astype(o_refs[0].dtype)


def _ep_partial_rope_t(accs, e_refs, o_refs, cols, rows):
    c = e_refs[0][...]
    s = e_refs[1][...]
    acc = accs[0]
    half = IDX_ROPE_DIM // 2
    for h in range(acc.shape[0] // IDX_HEAD_DIM):
        r0 = h * IDX_HEAD_DIM
        x1 = acc[r0:r0 + half, :]
        x2 = acc[r0 + half:r0 + 2 * half, :]
        o_refs[0][r0:r0 + half, :] = (x1 * c - x2 * s).astype(o_refs[0].dtype)
        o_refs[0][r0 + half:r0 + 2 * half, :] = (x2 * c + x1 * s).astype(o_refs[0].dtype)
        o_refs[0][r0 + 2 * half:r0 + IDX_HEAD_DIM, :] = (
            acc[r0 + 2 * half:r0 + IDX_HEAD_DIM, :].astype(o_refs[0].dtype))


def _partial_rope(x, c, a, b):
    half = IDX_ROPE_DIM // 2
    return x * c + pltpu.roll(x, LANES - half, 1) * a + pltpu.roll(x, half, 1) * b


def _ep_index_keys(accs, e_refs, o_refs, cols, rows, *, w_scale):
    g, bb = e_refs[0][...], e_refs[1][...]
    cos, sin = e_refs[2][rows, :], e_refs[3][rows, :]
    acc = accs[0]
    n, half = acc.shape[0], IDX_ROPE_DIM // 2
    c = jnp.concatenate([cos, cos, jnp.ones((n, LANES - 2 * half), F32)], axis=1)
    a = jnp.concatenate([-sin, jnp.zeros((n, LANES - half), F32)], axis=1)
    b = jnp.concatenate([jnp.zeros((n, half), F32), sin, jnp.zeros((n, LANES - 2 * half), F32)], axis=1)
    lane = lax.broadcasted_iota(jnp.int32, acc.shape, 1)
    is_key = lane < IDX_HEAD_DIM
    mu = jnp.sum(jnp.where(is_key, acc, 0.0), axis=-1, keepdims=True) / IDX_HEAD_DIM
    xc = jnp.where(is_key, acc - mu, 0.0)
    var = jnp.sum(xc * xc, axis=-1, keepdims=True) / IDX_HEAD_DIM
    y = xc * lax.rsqrt(var + NORM_EPS) * g + bb
    y = jnp.where(is_key, _partial_rope(y, c, a, b), 0.0)
    o_refs[0][rows, :] = y.astype(o_refs[0].dtype)
    o_refs[1][rows, :] = pltpu.roll(y, IDX_HEAD_DIM, 1).astype(o_refs[1].dtype)
    o_refs[2][:, rows] = (acc * w_scale).T[IDX_HEAD_DIM:IDX_HEAD_DIM + IDX_HEADS, :]


def _ep_merge(accs, e_refs, o_refs, cols, rows):
    o = e_refs[0][rows, cols].astype(F32) * accs[0] + e_refs[1][rows, cols].astype(F32) * accs[1]
    o_refs[0][rows, cols] = o.astype(o_refs[0].dtype)


def _ep_mix_residual(accs, e_refs, o_refs, cols, rows):
    x_ref, g_post, gate, g_pre, scale, shift = e_refs
    x1 = x_ref[rows, :] + gate[...] * _rms(accs[0], g_post[...])
    o_refs[0][rows, :] = x1
    h2 = _rms(x1, g_pre[...]) * (1.0 + scale[...]) + shift[...]
    o_refs[1][rows, :] = h2.astype(o_refs[1].dtype)


def _ep_swiglu(accs, e_refs, o_refs, cols, rows):
    o_refs[0][rows, cols] = (jax.nn.silu(accs[0]) * accs[1]).astype(o_refs[0].dtype)

    if cols.start == 0 and rows.start == 0:
        o_refs[1][...] = e_refs[0][...].astype(o_refs[1].dtype)


def _simple_mm(x, w, epilogue, extras, out_dtype, *, n, tm, tn, name, first=0, kn=True, out_t=False,
               sub_n=None, sub_m=None):
    m = x.shape[0]
    out = ((n, m), out_dtype, (tn, tm), _tile_t) if out_t else ((m, n), out_dtype, (tm, tn), _tile)
    return _matmul([x], [(w, first, kn)], [0], extras, [out], epilogue, n=n, tm=tm, tn=tn,
                   name=name, out_t=out_t, sub_n=sub_n, sub_m=sub_m)[0]


def _gmlp_body(u_ref, v_ref, w_ref, b_ref, o_ref):
    t = lax.broadcasted_iota(jnp.int32, (CHUNK, CHUNK), 0)
    s = lax.broadcasted_iota(jnp.int32, (CHUNK, CHUNK), 1)
    causal = s <= t
    for g in range(GMLP_GROUPS):
        w = jnp.where(causal, w_ref[g], 0.0).astype(BF16)
        cols = slice(g * LANES, (g + 1) * LANES)
        bias = b_ref[:, cols]
        for c in range(u_ref.shape[0] // CHUNK):
            rows = slice(c * CHUNK, (c + 1) * CHUNK)
            sv = jnp.dot(w, v_ref[rows, cols], preferred_element_type=F32) + bias
            o_ref[rows, cols] = (u_ref[rows, cols].astype(F32) * sv).astype(o_ref.dtype)


def _gmlp(u, vn, w_s, bias):
    m, n = u.shape
    tm = STREAM_TM
    blk = pl.BlockSpec((tm, n), lambda i: (i, 0))
    return pl.pallas_call(
        _gmlp_body,
        grid=(m // tm,),
        in_specs=[blk, blk,
                  pl.BlockSpec(w_s.shape, lambda i: (0, 0, 0)),
                  pl.BlockSpec(bias.shape, lambda i: (0, 0))],
        out_specs=blk,
        out_shape=jax.ShapeDtypeStruct((m, n), BF16),
        compiler_params=_params(("arbitrary",)),
        name="gmlp_gating",
    )(u, vn, w_s, bias)


def _bit_planes(words):
    a = list(words)
    j, mask = KEY_BITS // 2, 0x0000FFFF
    while j:
        for k in range(KEY_BITS):
            if k & j:
                continue
            t = (a[k] ^ (a[k + j] >> j)) & jnp.int32(mask)
            a[k] = a[k] ^ t
            a[k + j] = a[k + j] ^ (t << j)
        j //= 2
        mask = (mask ^ (mask << j)) & 0xFFFFFFFF
    return a


def _score_key(score):
    bits = lax.bitcast_convert_type(score, jnp.int32)
    return bits ^ ((bits >> 31) & jnp.int32(0x7FFFFFFF))


def _key_score(key):
    return lax.bitcast_convert_type(key ^ ((key >> 31) & jnp.int32(0x7FFFFFFF)), F32)


def _indexer_body(qt_ref, wt_ref, klo_ref, khi_ref, bias_ref, score_ref, plane_ref, live_ref, kth_ref):
    tq = qt_ref.shape[1]
    n_chunks_total = bias_ref.shape[0]
    i = pl.program_id(0)
    n_chunks = ((i + 1) * tq + ATT_KC - 1) // ATT_KC
    blocks_per_chunk = ATT_KC // PLANE_KEYS
    n_blocks = n_chunks * blocks_per_chunk
    heads_per_group = LANES // IDX_HEAD_DIM
    k_refs = (klo_ref, khi_ref)

    @pl.when(i == 0)
    def _():
        plane_ref[...] = jnp.zeros(plane_ref.shape, jnp.int32)

    q_pos = i * tq + lax.broadcasted_iota(jnp.int32, (ATT_KC, tq), 1)
    k_off = lax.broadcasted_iota(jnp.int32, (ATT_KC, tq), 0)

    def score_chunk(c):
        k0 = pl.multiple_of(c * ATT_KC, ATT_KC)
        score = jnp.zeros((ATT_KC, tq), F32)
        for h in range(IDX_HEADS):
            grp, r = divmod(h, heads_per_group)
            logit = jnp.dot(k_refs[r][pl.ds(k0, ATT_KC), :], qt_ref[grp * LANES:(grp + 1) * LANES, :],
                            preferred_element_type=F32)
            score = score + jnp.maximum(logit, 0.0) * wt_ref[h:h + 1, :]
        score_ref[pl.ds(k0, ATT_KC), :] = jnp.where(k0 + k_off <= q_pos, score, -jnp.inf)

    def slice_chunk(c):
        for hh in range(blocks_per_chunk):
            blk = c * blocks_per_chunk + hh
            k0 = pl.multiple_of(blk * PLANE_KEYS, PLANE_KEYS)
            bits = lax.bitcast_convert_type(score_ref[pl.ds(k0, PLANE_KEYS), :], jnp.int32)
            planes = _bit_planes([bits[w * SUBLANES:(w + 1) * SUBLANES, :] for w in range(KEY_BITS)])
            r0 = pl.multiple_of(blk * SUBLANES, SUBLANES)
            sign = planes[0]
            plane_ref[0, pl.ds(r0, SUBLANES), :] = ~sign
            for r in range(1, KEY_BITS):
                plane_ref[r, pl.ds(r0, SUBLANES), :] = planes[r] ^ sign

    def score_and_slice(c, carry):
        slice_chunk(c - 1)
        score_chunk(c)
        return carry

    score_chunk(0)
    lax.fori_loop(1, n_chunks, score_and_slice, 0)
    slice_chunk(n_chunks - 1)

    def count_bits(words):
        ones = lax.population_count(words)
        part = jnp.sum(ones.reshape(words.shape[0] // SUBLANES, SUBLANES, tq), axis=0)
        return jnp.sum(part.astype(F32), axis=0, keepdims=True)

    def radix_select(rows):
        row_id = lax.broadcasted_iota(jnp.int32, (rows, tq), 0)
        live_ref[:rows] = jnp.where(row_id < n_blocks * SUBLANES, jnp.int32(-1), jnp.int32(0))

        def search_bit(r, carry):
            need, kth = carry
            total = count_bits(live_ref[:rows] & plane_ref[r, :rows])
            take = total >= need
            keep_clear = jnp.where(take, jnp.int32(0), jnp.int32(-1))
            live_ref[:rows] = live_ref[:rows] & (plane_ref[r, :rows] ^ keep_clear)
            need = jnp.where(take, need, need - total)
            kth = jnp.where(take, kth | jnp.left_shift(jnp.int32(1), KEY_BITS - 1 - r), kth)
            return need, kth

        _, kth = lax.fori_loop(0, KEY_BITS, search_bit,
                               (jnp.full((1, tq), float(INDEX_TOPK), F32), jnp.zeros((1, tq), jnp.int32)))
        kth_ref[0:1] = kth ^ jnp.int32(INT_MIN)

    plane_rows = live_ref.shape[0]
    quarter = plane_rows // 4
    for rows in range(quarter, plane_rows + 1, quarter):
        @pl.when((n_blocks * SUBLANES > rows - quarter) & (n_blocks * SUBLANES <= rows))
        def _():
            radix_select(rows)

    def count_scores(pred):
        group = 4 * SUBLANES

        def add_chunk(c, acc):
            k0 = pl.multiple_of(c * ATT_KC, ATT_KC)
            hit = jnp.where(pred(score_ref[pl.ds(k0, ATT_KC), :]), 1.0, 0.0)
            return acc + jnp.sum(hit.reshape(ATT_KC // group, group, tq), axis=0)

        acc = lax.fori_loop(0, n_chunks, add_chunk, jnp.zeros((group, tq), F32))
        return jnp.sum(acc, axis=0, keepdims=True)

    few_keys = i * tq + lax.broadcasted_iota(jnp.int32, (1, tq), 1) + 1 < INDEX_TOPK
    lowest = float(jnp.finfo(F32).min)
    thr = jnp.where(few_keys, lowest, _key_score(jnp.maximum(kth_ref[0:1], KEY_NEG_INF + 1)))

    def bias_tile(sel):
        return jnp.where(sel, 0.0, MASK_BIAS).T

    def write_chunk(c, kept):
        parts = []
        for hh in range(blocks_per_chunk):
            k0 = pl.multiple_of(c * ATT_KC + hh * PLANE_KEYS, PLANE_KEYS)
            sel = score_ref[pl.ds(k0, PLANE_KEYS), :] >= thr
            parts.append(bias_tile(sel))
            hit = jnp.where(sel, 1.0, 0.0)
            kept = kept + jnp.sum(hit.reshape(PLANE_KEYS // kept.shape[0], kept.shape[0], tq), axis=0)
        bias_ref[c] = jnp.concatenate(parts, axis=1).astype(bias_ref.dtype)
        return kept

    kept = lax.fori_loop(0, n_chunks, write_chunk, jnp.zeros((4 * SUBLANES, tq), F32))
    kept = jnp.sum(kept, axis=0, keepdims=True)
    unsettled = jnp.max(jnp.where(few_keys | (kept == float(INDEX_TOPK)), 0.0, 1.0))

    @pl.when(unsettled > 0.0)
    def _():
        def search_bit(b, prefix):
            cand = prefix ^ jnp.left_shift(jnp.int32(1), KEY_BITS - 1 - b)
            cand_score = _key_score(jnp.maximum(cand, KEY_NEG_INF))
            admitted = count_scores(lambda sc: sc >= cand_score)
            return jnp.where(admitted >= float(INDEX_TOPK), cand, prefix)

        kth = lax.fori_loop(0, KEY_BITS, search_bit, jnp.full((1, tq), INT_MIN, jnp.int32))
        kth_score = jnp.where(few_keys, lowest, _key_score(jnp.maximum(kth, KEY_NEG_INF + 1)))
        above = count_scores(lambda sc: sc > kth_score)
        need = jnp.where(few_keys, float(2 ** 24), float(INDEX_TOPK) - above)

        row = lax.broadcasted_iota(jnp.int32, (PLANE_KEYS, PLANE_KEYS), 0)
        col = lax.broadcasted_iota(jnp.int32, (PLANE_KEYS, PLANE_KEYS), 1)
        prefix_sum = jnp.where(col <= row, 1.0, 0.0).astype(BF16)

        def write_chunk(c, seen):
            parts = []
            for hh in range(blocks_per_chunk):
                k0 = pl.multiple_of(c * ATT_KC + hh * PLANE_KEYS, PLANE_KEYS)
                sc = score_ref[pl.ds(k0, PLANE_KEYS), :]
                tie = sc == kth_score
                rank = seen + jnp.dot(prefix_sum, jnp.where(tie, 1.0, 0.0).astype(BF16),
                                      preferred_element_type=F32)
                parts.append(bias_tile((sc > kth_score) | (tie & (rank <= need))))
                seen = rank[PLANE_KEYS - 1:PLANE_KEYS, :]
            bias_ref[c] = jnp.concatenate(parts, axis=1).astype(bias_ref.dtype)
            return seen

        lax.fori_loop(0, n_chunks, write_chunk, jnp.zeros((1, tq), F32))

    def fill_chunk(c, carry):
        bias_ref[c] = jnp.full(bias_ref.shape[1:], MASK_BIAS, bias_ref.dtype)
        return carry

    lax.fori_loop(n_chunks, n_chunks_total, fill_chunk, 0)


def _indexer_mask(q_idx_t, idx_w_t, k_lo, k_hi):
    s = q_idx_t.shape[1]
    tq = SEL_TQ
    assert tq % PLANE_KEYS == 0 and s % (KEY_BITS * SUBLANES) == 0
    return pl.pallas_call(
        _indexer_body,
        grid=(s // tq,),
        in_specs=[pl.BlockSpec((q_idx_t.shape[0], tq), lambda i: (0, i)),
                  pl.BlockSpec((IDX_HEADS, tq), lambda i: (0, i)),
                  pl.BlockSpec(k_lo.shape, lambda i: (0, 0)),
                  pl.BlockSpec(k_hi.shape, lambda i: (0, 0))],
        out_specs=pl.BlockSpec((s // ATT_KC, tq, ATT_KC), lambda i: (0, i, 0)),
        out_shape=jax.ShapeDtypeStruct((s // ATT_KC, s, ATT_KC), BF16),
        scratch_shapes=[pltpu.VMEM((s, tq), F32),
                        pltpu.VMEM((KEY_BITS, s // KEY_BITS, tq), jnp.int32),
                        pltpu.VMEM((s // KEY_BITS, tq), jnp.int32),
                        pltpu.VMEM((SUBLANES, tq), jnp.int32)],
        compiler_params=_params(("arbitrary",)),
        name="indexer_mask",
    )(q_idx_t, idx_w_t, k_lo, k_hi)


def _attention_body(q_ref, k_ref, v_ref, bias_ref, o_ref, m_ref, acc_ref):
    tq = q_ref.shape[0]
    chunk_keys = bias_ref.shape[2]
    i = pl.program_id(0)
    n_chunks = (i * tq + tq - 1) // chunk_keys + 1

    m_ref[...] = jnp.full(m_ref.shape, M_INIT, F32)
    acc_ref[...] = jnp.zeros(acc_ref.shape, F32)

    def step(first_chunk, width):
        kc = width * chunk_keys
        sub = kc // LANES
        k0 = pl.multiple_of(first_chunk * chunk_keys, chunk_keys)
        bias = jnp.concatenate([bias_ref[first_chunk + c] for c in range(width)], axis=1)
        ones = jnp.ones((kc, HEAD_DIM), v_ref.dtype)
        for h in range(ATTN_HEADS):
            cols = slice(h * HEAD_DIM, (h + 1) * HEAD_DIM)
            s = lax.dot_general(q_ref[:, cols], k_ref[pl.ds(k0, kc), cols],
                                (((1,), (1,)), ((), ())), preferred_element_type=F32)
            sb = s.astype(BF16) + bias
            part = sb[:, :LANES]
            for c in range(1, sub):
                part = jnp.maximum(part, sb[:, c * LANES:(c + 1) * LANES])
            m_prev = m_ref[h]
            m_next = jnp.maximum(m_prev, jnp.max(part.astype(F32), axis=1, keepdims=True))
            p = jnp.exp2(sb - jnp.concatenate([m_next.astype(BF16)] * sub, axis=1))
            alpha = jnp.exp2(m_prev - m_next)
            m_ref[h] = m_next
            pv = jnp.dot(p, jnp.concatenate([v_ref[pl.ds(k0, kc), cols], ones], axis=1),
                         preferred_element_type=F32)
            acc_ref[h] = jnp.concatenate([alpha, alpha], axis=1) * acc_ref[h] + pv

    def wide_step(j, carry):
        step(j * ATT_STEP_CHUNKS, ATT_STEP_CHUNKS)
        return carry

    n_wide = n_chunks // ATT_STEP_CHUNKS
    lax.fori_loop(0, n_wide, wide_step, 0)
    for r in range(1, ATT_STEP_CHUNKS):
        @pl.when(n_chunks % ATT_STEP_CHUNKS >= r)
        def _():
            step(n_wide * ATT_STEP_CHUNKS + r - 1, 1)

    for h in range(ATTN_HEADS):
        acc = acc_ref[h]
        o_ref[:, h * HEAD_DIM:(h + 1) * HEAD_DIM] = (acc[:, :HEAD_DIM] / acc[:, HEAD_DIM:]).astype(o_ref.dtype)


def _attention(q, k, v, bias):
    s, width = q.shape
    tq = SEL_TQ
    n_chunks, _, kc = bias.shape
    resident = dict(pipeline_mode=pl.Buffered(1))
    return pl.pallas_call(
        _attention_body,
        grid=(s // tq,),
        in_specs=[pl.BlockSpec((tq, width), lambda i: (i, 0)),
                  pl.BlockSpec((s, width), lambda i: (0, 0), **resident),
                  pl.BlockSpec((s, width), lambda i: (0, 0), **resident),
                  pl.BlockSpec((n_chunks, tq, kc), lambda i: (0, i, 0))],
        out_specs=pl.BlockSpec((tq, width), lambda i: (i, 0)),
        out_shape=jax.ShapeDtypeStruct((s, width), BF16),
        scratch_shapes=[pltpu.VMEM((ATTN_HEADS, tq, LANES), F32),
                        pltpu.VMEM((ATTN_HEADS, tq, 2 * HEAD_DIM), F32)],
        compiler_params=_params(("arbitrary",)),
        name="masked_attention",
    )(q, k, v, bias)


def _ffn_down_body(a_ref, w_ref, x_ref, g_ref, gate_ref, o_ref):
    f = jnp.dot(a_ref[...], w_ref[...], preferred_element_type=F32)
    o_ref[...] = x_ref[...] + gate_ref[...] * _rms(f, g_ref[...])


def _ffn_down(a, w, x1, g_post, gate):
    m, kdim = a.shape
    d = w.shape[1]
    tm = FFN_DOWN_TM
    row = pl.BlockSpec((1, d), lambda i: (0, 0))
    return pl.pallas_call(
        _ffn_down_body,
        grid=(m // tm,),
        in_specs=[pl.BlockSpec((tm, kdim), lambda i: (i, 0)),
                  pl.BlockSpec((kdim, d), lambda i: (0, 0), pipeline_mode=pl.Buffered(1)),
                  pl.BlockSpec((tm, d), lambda i: (i, 0)), row, row],
        out_specs=pl.BlockSpec((tm, d), lambda i: (i, 0)),
        out_shape=jax.ShapeDtypeStruct((m, d), F32),
        compiler_params=_params(("arbitrary",)),
        name="ffn_down_residual",
    )(a, w, x1, g_post, gate)


def _rope_angles(seq, dim, repeat=1):
    inv = jnp.tile(1.0 / (ROPE_THETA ** (jnp.arange(0, dim, 2, dtype=F32) / dim)), repeat)
    coarse = (jnp.arange(seq // ROPE_BLOCK, dtype=F32) * ROPE_BLOCK)[:, None] * inv[None, :]
    fine = jnp.arange(ROPE_BLOCK, dtype=F32)[:, None] * inv[None, :]
    cos_a, sin_a = jnp.cos(coarse)[:, None, :], jnp.sin(coarse)[:, None, :]
    cos_b, sin_b = jnp.cos(fine)[None, :, :], jnp.sin(fine)[None, :, :]
    cos = (cos_a * cos_b - sin_a * sin_b).reshape(seq, inv.shape[0])
    sin = (sin_a * cos_b + cos_a * sin_b).reshape(seq, inv.shape[0])
    return cos, sin


def _attn_rope_tables(seq):
    cos, sin = _rope_angles(seq, HEAD_DIM, repeat=2)
    sign = jnp.concatenate([-jnp.ones((HEAD_DIM // 2,), F32), jnp.ones((HEAD_DIM // 2,), F32)])
    return cos, sin * sign


def _index_rope_tables(seq):
    cos, sin = _rope_angles(seq, IDX_ROPE_DIM)
    return cos, sin, cos.T, sin.T


def _layer(x, mod, g_pre_mix, g_post_mix, w_in_t, gmlp_ln_g, gmlp_ln_b, gmlp_w_s, gmlp_b_s,
           q_lat_norm_g, w_q_up, w_qidx_up, kidx_ln_g, kidx_ln_b, w_proj_a, w_proj_b, w_out,
           g_pre_ffn, g_post_ffn, w_ffn_gate, w_ffn_up, w_ffn_down, tables):
    s, d = x.shape
    cc, ss, icos, isin, icos_t, isin_t = tables
    row = lambda a: a.reshape(1, -1)
    shift_m, scale_m, gate_m, shift_f, scale_f, gate_f = (mod[:, n * d:(n + 1) * d] for n in range(N_MOD))

    tm = MM_TM
    vec_d = lambda a: (a, (1, d), lambda j, i: (0, 0))
    u, h = _matmul([x], [(w_in_t, 0, False)], [0], [vec_d(row(g_pre_mix)), vec_d(scale_m), vec_d(shift_m)],
                   [((s, GMLP_WIDTH), BF16, (tm, GMLP_WIDTH), _tile), ((s, d), BF16, (tm, d), _rowblk)],
                   _ep_gelu, n=GMLP_WIDTH, tm=tm, tn=GMLP_WIDTH, sub_n=EPILOGUE_SUB_N,
                   sub_m=EPILOGUE_SUB_N, prologue=_pro_prenorm, name="prenorm_in_proj_u")
    in_proj = functools.partial(_simple_mm, h, w_in_t, kn=False)
    vn = in_proj(_ep_gelu_layernorm,
                 [(row(gmlp_ln_g), (1, GMLP_WIDTH), _colblk), (row(gmlp_ln_b), (1, GMLP_WIDTH), _colblk)],
                 BF16, n=GMLP_WIDTH, first=GMLP_WIDTH, tm=tm, tn=GMLP_WIDTH, sub_m=EPILOGUE_SUB_M,
                 name="in_proj_v_ln")
    gmlp_bias = jnp.repeat(gmlp_b_s.T, LANES, axis=1)
    y_a = _gmlp(u, vn, gmlp_w_s, gmlp_bias)

    q_lat = in_proj(_ep_rmsnorm, [(row(q_lat_norm_g), (1, Q_LORA_RANK), _colblk)],
                    BF16, n=Q_LORA_RANK, first=COL_QLAT, tm=tm, tn=Q_LORA_RANK, name="in_proj_qlat")
    q_scale = HEAD_DIM ** -0.5 * math.log2(math.e)
    rope_extras = [(cc, (tm, HEAD_DIM), _rowblk), (ss, (tm, HEAD_DIM), _rowblk)]
    q = _simple_mm(q_lat, w_q_up, functools.partial(_ep_rope, scale=q_scale), rope_extras,
                   BF16, n=ATTN_WIDTH, tm=tm, tn=ATTN_WIDTH, sub_n=EPILOGUE_SUB_N, name="q_up_rope")
    k = in_proj(functools.partial(_ep_rope, scale=1.0), rope_extras, BF16,
                n=ATTN_WIDTH, first=COL_K, tm=tm, tn=ATTN_WIDTH, sub_n=EPILOGUE_SUB_N,
                name="in_proj_k_rope")
    v = in_proj(_ep_cast, [], BF16, n=ATTN_WIDTH, first=COL_V, tm=tm, tn=ATTN_WIDTH,
                sub_n=EPILOGUE_SUB_N, name="in_proj_v")
    half = IDX_ROPE_DIM // 2
    q_idx_t = _simple_mm(q_lat, w_qidx_up, _ep_partial_rope_t,
                         [(icos_t, (half, tm), _colblk_t), (isin_t, (half, tm), _colblk_t)],
                         BF16, n=IDX_HEADS * IDX_HEAD_DIM, tm=tm, tn=QIDX_TN, name="qidx_up_rope",
                         out_t=True)
    pad = lambda a: jnp.pad(a, (0, LANES - a.shape[0])).reshape(1, LANES)
    idx_w_scale = (IDX_HEADS ** -0.5) * (IDX_HEAD_DIM ** -0.5)
    lane_tile = lambda dtype: ((s, LANES), dtype, (tm, LANES), _tile)
    k_lo, k_hi, idx_w_t = _matmul(
        [h], [(w_in_t, COL_KIDX, False)], [0],
        [(pad(kidx_ln_g), (1, LANES), _colblk), (pad(kidx_ln_b), (1, LANES), _colblk),
         (icos, (tm, half), _rowblk), (isin, (tm, half), _rowblk)],
        [lane_tile(BF16), lane_tile(BF16), ((IDX_HEADS, s), F32, (IDX_HEADS, tm), _colblk_t)],
        functools.partial(_ep_index_keys, w_scale=idx_w_scale), n=LANES, tm=tm, tn=LANES,
        name="in_proj_index_keys")
    bias = _indexer_mask(q_idx_t, idx_w_t, k_lo, k_hi)
    y_b = _attention(q, k, v, bias)

    gates = in_proj(_ep_sigmoid, [], BF16, n=2 * d, first=COL_GATE, tm=tm, tn=MM_TN,
                    sub_n=EPILOGUE_SUB_N, name="in_proj_gates")
    n_gate_blocks = d // MM_TN
    merged = _matmul([y_a, y_b], [(w_proj_a, 0, True), (w_proj_b, 0, True)], [0, 1],
                     [(gates, (tm, MM_TN), _tile),
                      (gates, (tm, MM_TN), lambda j, i: (i, j + n_gate_blocks))],
                     [((s, d), BF16, (tm, MM_TN), _tile)], _ep_merge, n=d, tm=tm, tn=MM_TN,
                     sub_n=EPILOGUE_SUB_N, name="branch_merge")[0]
    tm_full = ROWNORM_TM
    vec = lambda a: (a, (1, d), _colblk)
    x1, h2 = _matmul([merged], [(w_out, 0, True)], [0],
                     [(x, (tm_full, d), _rowblk), vec(row(g_post_mix)), vec(gate_m),
                      vec(row(g_pre_ffn)), vec(scale_f), vec(shift_f)],
                     [((s, d), F32, (tm_full, d), _tile), ((s, d), BF16, (tm_full, d), _tile)],
                     _ep_mix_residual, n=d, tm=tm_full, tn=d, sub_m=EPILOGUE_SUB_M, name="out_proj_residual")

    ffn_hidden = w_ffn_gate.shape[1]
    tn_ffn, tm_ffn = FFN_UP_TN, FFN_UP_TM
    n_row_blocks = s // tm_ffn
    down_rows = ffn_hidden // ((ffn_hidden // tn_ffn) * n_row_blocks)
    down_blk = ((down_rows, d), lambda j, i: (j * n_row_blocks + i, 0))
    act, w_down = _matmul([h2], [(w_ffn_gate, 0, True), (w_ffn_up, 0, True)], [0, 0],
                          [(w_ffn_down,) + down_blk],
                          [((s, ffn_hidden), BF16, (tm_ffn, tn_ffn), _tile), (w_ffn_down.shape, BF16) + down_blk],
                          _ep_swiglu, n=ffn_hidden, tm=tm_ffn, tn=tn_ffn, sub_m=MM_TM, name="ffn_up")
    return _ffn_down(act, w_down, x1, row(g_post_ffn), gate_f)


def kernel(x, c, w_mod, b_mod, g_pre_mix, g_post_mix, w_in, gmlp_ln_g, gmlp_ln_b, gmlp_w_s, gmlp_b_s, q_lat_norm_g, w_q_up, w_qidx_up, kidx_ln_g, kidx_ln_b, w_proj_a, w_proj_b, w_out, g_pre_ffn, g_post_ffn, w_ffn_gate, w_ffn_up, w_ffn_down):
    batch, seq, d = x.shape
    assert batch == 1 and d == D_MODEL
    tables = _attn_rope_tables(seq) + _index_rope_tables(seq)
    y = x[0]
    for l in range(w_mod.shape[0]):
        mod = _modulation(c, w_mod[l], b_mod[l])
        y = _layer(y, mod, g_pre_mix[l], g_post_mix[l], w_in[l].T, gmlp_ln_g[l], gmlp_ln_b[l],
                   gmlp_w_s[l], gmlp_b_s[l], q_lat_norm_g[l], w_q_up[l], w_qidx_up[l],
                   kidx_ln_g[l], kidx_ln_b[l], w_proj_a[l], w_proj_b[l], w_out[l],
                   g_pre_ffn[l], g_post_ffn[l], w_ffn_gate[l], w_ffn_up[l], w_ffn_down[l], tables)
    return y[None]
```

```python
import functools
import math

import jax
import jax.numpy as jnp
from jax import lax
from jax.experimental import pallas as pl
from jax.experimental.pallas import tpu as pltpu

F32 = jnp.float32
BF16 = jnp.bfloat16

D_MODEL = 2048
GMLP_WIDTH = 1024
GMLP_GROUPS = 8
CHUNK = 128
ATTN_HEADS = 8
HEAD_DIM = 128
ATTN_WIDTH = ATTN_HEADS * HEAD_DIM
Q_LORA_RANK = 512
IDX_HEADS = 16
IDX_HEAD_DIM = 64
IDX_ROPE_DIM = 32
INDEX_TOPK = 256
ROPE_THETA = 10000.0
N_MOD = 6
NORM_EPS = 1e-6

COL_QLAT = 2 * GMLP_WIDTH
COL_K = COL_QLAT + Q_LORA_RANK
COL_V = COL_K + ATTN_WIDTH
COL_KIDX = COL_V + ATTN_WIDTH
COL_IDXW = COL_KIDX + IDX_HEAD_DIM
COL_GATE = COL_IDXW + IDX_HEADS

LANES = 128
SUBLANES = 8
VMEM_LIMIT = 56 * 1024 * 1024

MASK_BIAS = -(2.0 ** 100)
M_INIT = -(2.0 ** 99)
INT_MIN = -2 ** 31
KEY_NEG_INF = -2139095041

SEL_TQ = 256
KEY_BITS = 32
PLANE_KEYS = KEY_BITS * SUBLANES
ATT_KC = 2 * PLANE_KEYS
ATT_STEP_CHUNKS = 2
EPILOGUE_SUB_N = 512
EPILOGUE_SUB_M = 256
ROPE_BLOCK = 64

MM_TM = 1024
MM_TN = 1024
ROWNORM_TM = 512
FFN_UP_TN = 512
MM_TM_WIDE = 2 * MM_TM
FFN_DOWN_TM = 256
QIDX_TN = 512
STREAM_TM = 1024
MOD_TN = 1024


def _params(sem=None):
    return pltpu.CompilerParams(dimension_semantics=sem, vmem_limit_bytes=VMEM_LIMIT)


def _mod_body(cb_ref, w_ref, b_ref, o_ref):
    cb = cb_ref[...]
    tn = o_ref.shape[1]
    parts = [jnp.sum(w_ref[:, p * LANES:(p + 1) * LANES] * cb, axis=0, keepdims=True)
             for p in range(tn // LANES)]
    o_ref[...] = jnp.concatenate(parts, axis=1) + b_ref[...]


def _modulation(c, w_mod, b_mod):
    k, n = w_mod.shape
    tn = MOD_TN
    cb = jnp.broadcast_to(c.reshape(k, 1), (k, LANES))
    return pl.pallas_call(
        _mod_body,
        grid=(n // tn,),
        in_specs=[pl.BlockSpec((k, LANES), lambda j: (0, 0)),
                  pl.BlockSpec((k, tn), lambda j: (0, j)),
                  pl.BlockSpec((1, tn), lambda j: (0, j))],
        out_specs=pl.BlockSpec((1, tn), lambda j: (0, j)),
        out_shape=jax.ShapeDtypeStruct((1, n), F32),
        compiler_params=_params(("arbitrary",)),
        name="modulation",
    )(cb, w_mod, b_mod.reshape(1, n))


def _rms(x, g):
    return x * lax.rsqrt(jnp.mean(x * x, axis=-1, keepdims=True) + NORM_EPS) * g


def _mm_body(*refs, n_x, pairs, w_kn, out_t, sub_m, sub_n, n_extra, n_out, epilogue, prologue):
    n_w = len(pairs)
    x_refs = refs[:n_x]
    w_refs = refs[n_x:n_x + n_w]
    e_refs = refs[n_x + n_w:n_x + n_w + n_extra]
    o_refs = refs[n_x + n_w + n_extra:n_x + n_w + n_extra + n_out]
    wb_refs = refs[n_x + n_w + n_extra + n_out:]

    @pl.when(pl.program_id(1) == 0)
    def _():
        for w_ref, wb_ref, kn in zip(w_refs, wb_refs, w_kn):
            w = w_ref[...]
            if kn == out_t:
                w = w.T
            wb_ref[...] = w.astype(wb_ref.dtype)

    if out_t:
        accs = [lax.dot_general(wb_ref[...], x_refs[xi][...], (((1,), (1,)), ((), ())),
                                preferred_element_type=F32)
                for xi, wb_ref in zip(pairs, wb_refs)]
        epilogue(accs, e_refs, o_refs, slice(0, wb_refs[0].shape[0]), slice(0, x_refs[0].shape[0]))
        return
    tm, tn = x_refs[0].shape[0], wb_refs[0].shape[1]
    for r0 in range(0, tm, sub_m):
        rs = slice(r0, r0 + sub_m)
        if prologue is None:
            lhs = [x_ref[rs, :] for x_ref in x_refs]
        else:
            lhs = prologue(x_refs, e_refs, o_refs, rs)
        for c0 in range(0, tn, sub_n):
            cs = slice(c0, c0 + sub_n)
            accs = [jnp.dot(lhs[xi], wb_ref[:, cs], preferred_element_type=F32)
                    for xi, wb_ref in zip(pairs, wb_refs)]
            epilogue(accs, e_refs, o_refs, cs, rs)


def _matmul(xs, ws, pairs, extras, outs, epilogue, *, n, tm, tn, name, out_t=False, sub_n=None,
            sub_m=None, prologue=None):
    m = xs[0].shape[0]
    n_col = n // tn
    w_mode = dict(pipeline_mode=pl.Buffered(1)) if n_col == 1 else {}
    in_specs = [pl.BlockSpec((tm, x.shape[1]), lambda j, i: (i, 0)) for x in xs]
    scratch = []
    for w, first, kn in ws:
        if kn:
            kdim = w.shape[0]
            assert first % tn == 0
            in_specs.append(pl.BlockSpec((kdim, tn), functools.partial(_w_cols, first=first // tn), **w_mode))
        else:
            kdim = w.shape[1]
            in_specs.append(pl.BlockSpec((pl.Element(tn), pl.Element(kdim)),
                                         functools.partial(_w_rows, first=first, tn=tn), **w_mode))
        scratch.append(pltpu.VMEM((tn, kdim) if out_t else (kdim, tn), BF16))
    in_specs += [pl.BlockSpec(bs, im) for _, bs, im in extras]
    body = functools.partial(_mm_body, n_x=len(xs), pairs=tuple(pairs), w_kn=tuple(kn for _, _, kn in ws),
                             out_t=out_t, sub_m=sub_m or tm, sub_n=sub_n or tn, n_extra=len(extras),
                             n_out=len(outs), epilogue=epilogue, prologue=prologue)
    return pl.pallas_call(
        body,
        grid=(n_col, m // tm),
        in_specs=in_specs,
        out_specs=[pl.BlockSpec(bs, im) for _, _, bs, im in outs],
        out_shape=[jax.ShapeDtypeStruct(s, d) for s, d, _, _ in outs],
        scratch_shapes=scratch,
        compiler_params=_params(("arbitrary", "arbitrary")),
        name=name,
    )(*xs, *[w for w, _, _ in ws], *[a for a, _, _ in extras])


def _w_cols(j, i, *, first):
    return (0, first + j)


def _w_rows(j, i, *, first, tn):
    assert first % SUBLANES == 0 and tn % SUBLANES == 0
    return (pl.multiple_of(first + j * tn, SUBLANES), 0)


def _tile(j, i):
    return (i, j)


def _tile_t(j, i):
    return (j, i)


def _rowblk(j, i):
    return (i, 0)


def _colblk(j, i):
    return (0, j)


def _colblk_t(j, i):
    return (0, i)


def _pro_prenorm(x_refs, e_refs, o_refs, rows):
    g, scale, shift = e_refs[0][...], e_refs[1][...], e_refs[2][...]
    h = (_rms(x_refs[0][rows, :], g) * (1.0 + scale) + shift).astype(o_refs[1].dtype)
    o_refs[1][rows, :] = h
    return [h]


def _ep_gelu(accs, e_refs, o_refs, cols, rows):
    o_refs[0][rows, cols] = jax.nn.gelu(accs[0]).astype(o_refs[0].dtype)


def _ep_gelu_layernorm(accs, e_refs, o_refs, cols, rows):
    z = jax.nn.gelu(accs[0])
    mu = jnp.mean(z, axis=-1, keepdims=True)
    zc = z - mu
    var = jnp.mean(zc * zc, axis=-1, keepdims=True)
    y = zc * lax.rsqrt(var + NORM_EPS) * e_refs[0][...] + e_refs[1][...]
    o_refs[0][rows, :] = y.astype(o_refs[0].dtype)


def _ep_rmsnorm(accs, e_refs, o_refs, cols, rows):
    o_refs[0][rows, :] = _rms(accs[0], e_refs[0][...]).astype(o_refs[0].dtype)


def _ep_cast(accs, e_refs, o_refs, cols, rows):
    o_refs[0][rows, cols] = accs[0].astype(o_refs[0].dtype)


def _ep_sigmoid(accs, e_refs, o_refs, cols, rows):
    o_refs[0][rows, cols] = (0.5 * jnp.tanh(0.5 * accs[0]) + 0.5).astype(o_refs[0].dtype)


def _ep_rope(accs, e_refs, o_refs, cols, rows, *, scale):
    cc = e_refs[0][rows, :]
    ss = e_refs[1][rows, :]
    acc = accs[0]
    for h in range(acc.shape[1] // HEAD_DIM):
        xh = acc[:, h * HEAD_DIM:(h + 1) * HEAD_DIM]
        r = xh * cc + pltpu.roll(xh, HEAD_DIM // 2, 1) * ss
        if scale != 1.0:
            r = r * scale
        c0 = cols.start + h * HEAD_DIM
        o_refs[0][rows, c0:c0 + HEAD_DIM] = r.astype(o_refs[0].dtype)


def _ep_partial_rope_t(accs, e_refs, o_refs, cols, rows):
    c = e_refs[0][...]
    s = e_refs[1][...]
    acc = accs[0]
    half = IDX_ROPE_DIM // 2
    for h in range(acc.shape[0] // IDX_HEAD_DIM):
        r0 = h * IDX_HEAD_DIM
        x1 = acc[r0:r0 + half, :]
        x2 = acc[r0 + half:r0 + 2 * half, :]
        o_refs[0][r0:r0 + half, :] = (x1 * c - x2 * s).astype(o_refs[0].dtype)
        o_refs[0][r0 + half:r0 + 2 * half, :] = (x2 * c + x1 * s).astype(o_refs[0].dtype)
        o_refs[0][r0 + 2 * half:r0 + IDX_HEAD_DIM, :] = (
            acc[r0 + 2 * half:r0 + IDX_HEAD_DIM, :].astype(o_refs[0].dtype))


def _partial_rope(x, c, a, b):
    half = IDX_ROPE_DIM // 2
    return x * c + pltpu.roll(x, LANES - half, 1) * a + pltpu.roll(x, half, 1) * b


def _ep_index_keys(accs, e_refs, o_refs, cols, rows, *, w_scale):
    g, bb = e_refs[0][...], e_refs[1][...]
    cos, sin = e_refs[2][rows, :], e_refs[3][rows, :]
    acc = accs[0]
    n, half = acc.shape[0], IDX_ROPE_DIM // 2
    c = jnp.concatenate([cos, cos, jnp.ones((n, LANES - 2 * half), F32)], axis=1)
    a = jnp.concatenate([-sin, jnp.zeros((n, LANES - half), F32)], axis=1)
    b = jnp.concatenate([jnp.zeros((n, half), F32), sin, jnp.zeros((n, LANES - 2 * half), F32)], axis=1)
    lane = lax.broadcasted_iota(jnp.int32, acc.shape, 1)
    is_key = lane < IDX_HEAD_DIM
    mu = jnp.sum(jnp.where(is_key, acc, 0.0), axis=-1, keepdims=True) / IDX_HEAD_DIM
    xc = jnp.where(is_key, acc - mu, 0.0)
    var = jnp.sum(xc * xc, axis=-1, keepdims=True) / IDX_HEAD_DIM
    y = xc * lax.rsqrt(var + NORM_EPS) * g + bb
    y = jnp.where(is_key, _partial_rope(y, c, a, b), 0.0)
    o_refs[0][rows, :] = y.astype(o_refs[0].dtype)
    o_refs[1][rows, :] = pltpu.roll(y, IDX_HEAD_DIM, 1).astype(o_refs[1].dtype)
    o_refs[2][:, rows] = (acc * w_scale).T[IDX_HEAD_DIM:IDX_HEAD_DIM + IDX_HEADS, :]


def _ep_merge(accs, e_refs, o_refs, cols, rows):
    o = e_refs[0][rows, cols].astype(F32) * accs[0] + e_refs[1][rows, cols].astype(F32) * accs[1]
    o_refs[0][rows, cols] = o.astype(o_refs[0].dtype)


def _ep_mix_residual(accs, e_refs, o_refs, cols, rows):
    x_ref, g_post, gate, g_pre, scale, shift = e_refs
    x1 = x_ref[rows, :] + gate[...] * _rms(accs[0], g_post[...])
    o_refs[0][rows, :] = x1
    h2 = _rms(x1, g_pre[...]) * (1.0 + scale[...]) + shift[...]
    o_refs[1][rows, :] = h2.astype(o_refs[1].dtype)


def _ep_swiglu(accs, e_refs, o_refs, cols, rows):
    o_refs[0][rows, cols] = (jax.nn.silu(accs[0]) * accs[1]).astype(o_refs[0].dtype)

    if cols.start == 0 and rows.start == 0:
        o_refs[1][...] = e_refs[0][...].astype(o_refs[1].dtype)


def _simple_mm(x, w, epilogue, extras, out_dtype, *, n, tm, tn, name, first=0, kn=True, out_t=False,
               sub_n=None, sub_m=None):
    m = x.shape[0]
    out = ((n, m), out_dtype, (tn, tm), _tile_t) if out_t else ((m, n), out_dtype, (tm, tn), _tile)
    return _matmul([x], [(w, first, kn)], [0], extras, [out], epilogue, n=n, tm=tm, tn=tn,
                   name=name, out_t=out_t, sub_n=sub_n, sub_m=sub_m)[0]


def _gmlp_body(u_ref, v_ref, w_ref, b_ref, o_ref):
    t = lax.broadcasted_iota(jnp.int32, (CHUNK, CHUNK), 0)
    s = lax.broadcasted_iota(jnp.int32, (CHUNK, CHUNK), 1)
    causal = s <= t
    for g in range(GMLP_GROUPS):
        w = jnp.where(causal, w_ref[g], 0.0).astype(BF16)
        cols = slice(g * LANES, (g + 1) * LANES)
        bias = b_ref[:, cols]
        for c in range(u_ref.shape[0] // CHUNK):
            rows = slice(c * CHUNK, (c + 1) * CHUNK)
            sv = jnp.dot(w, v_ref[rows, cols], preferred_element_type=F32) + bias
            o_ref[rows, cols] = (u_ref[rows, cols].astype(F32) * sv).astype(o_ref.dtype)


def _gmlp(u, vn, w_s, bias):
    m, n = u.shape
    tm = STREAM_TM
    blk = pl.BlockSpec((tm, n), lambda i: (i, 0))
    return pl.pallas_call(
        _gmlp_body,
        grid=(m // tm,),
        in_specs=[blk, blk,
                  pl.BlockSpec(w_s.shape, lambda i: (0, 0, 0)),
                  pl.BlockSpec(bias.shape, lambda i: (0, 0))],
        out_specs=blk,
        out_shape=jax.ShapeDtypeStruct((m, n), BF16),
        compiler_params=_params(("arbitrary",)),
        name="gmlp_gating",
    )(u, vn, w_s, bias)


def _bit_planes(words):
    a = list(words)
    j, mask = KEY_BITS // 2, 0x0000FFFF
    while j:
        for k in range(KEY_BITS):
            if k & j:
                continue
            t = (a[k] ^ (a[k + j] >> j)) & jnp.int32(mask)
            a[k] = a[k] ^ t
            a[k + j] = a[k + j] ^ (t << j)
        j //= 2
        mask = (mask ^ (mask << j)) & 0xFFFFFFFF
    return a


def _score_key(score):
    bits = lax.bitcast_convert_type(score, jnp.int32)
    return bits ^ ((bits >> 31) & jnp.int32(0x7FFFFFFF))


def _key_score(key):
    return lax.bitcast_convert_type(key ^ ((key >> 31) & jnp.int32(0x7FFFFFFF)), F32)


def _indexer_body(qt_ref, wt_ref, klo_ref, khi_ref, bias_ref, score_ref, plane_ref, live_ref, kth_ref):
    tq = qt_ref.shape[1]
    n_chunks_total = bias_ref.shape[0]
    i = pl.program_id(0)
    n_chunks = ((i + 1) * tq + ATT_KC - 1) // ATT_KC
    blocks_per_chunk = ATT_KC // PLANE_KEYS
    n_blocks = n_chunks * blocks_per_chunk
    heads_per_group = LANES // IDX_HEAD_DIM
    k_refs = (klo_ref, khi_ref)

    @pl.when(i == 0)
    def _():
        plane_ref[...] = jnp.zeros(plane_ref.shape, jnp.int32)

    q_pos = i * tq + lax.broadcasted_iota(jnp.int32, (ATT_KC, tq), 1)
    k_off = lax.broadcasted_iota(jnp.int32, (ATT_KC, tq), 0)

    def score_chunk(c):
        k0 = pl.multiple_of(c * ATT_KC, ATT_KC)
        score = jnp.zeros((ATT_KC, tq), F32)
        for h in range(IDX_HEADS):
            grp, r = divmod(h, heads_per_group)
            logit = jnp.dot(k_refs[r][pl.ds(k0, ATT_KC), :], qt_ref[grp * LANES:(grp + 1) * LANES, :],
                            preferred_element_type=F32)
            score = score + jnp.maximum(logit, 0.0) * wt_ref[h:h + 1, :]
        score_ref[pl.ds(k0, ATT_KC), :] = jnp.where(k0 + k_off <= q_pos, score, -jnp.inf)

    def slice_chunk(c):
        for hh in range(blocks_per_chunk):
            blk = c * blocks_per_chunk + hh
            k0 = pl.multiple_of(blk * PLANE_KEYS, PLANE_KEYS)
            bits = lax.bitcast_convert_type(score_ref[pl.ds(k0, PLANE_KEYS), :], jnp.int32)
            planes = _bit_planes([bits[w * SUBLANES:(w + 1) * SUBLANES, :] for w in range(KEY_BITS)])
            r0 = pl.multiple_of(blk * SUBLANES, SUBLANES)
            sign = planes[0]
            plane_ref[0, pl.ds(r0, SUBLANES), :] = ~sign
            for r in range(1, KEY_BITS):
                plane_ref[r, pl.ds(r0, SUBLANES), :] = planes[r] ^ sign

    def score_and_slice(c, carry):
        slice_chunk(c - 1)
        score_chunk(c)
        return carry

    score_chunk(0)
    lax.fori_loop(1, n_chunks, score_and_slice, 0)
    slice_chunk(n_chunks - 1)

    def count_bits(words):
        ones = lax.population_count(words)
        part = jnp.sum(ones.reshape(words.shape[0] // SUBLANES, SUBLANES, tq), axis=0)
        return jnp.sum(part.astype(F32), axis=0, keepdims=True)

    def radix_select(rows):
        row_id = lax.broadcasted_iota(jnp.int32, (rows, tq), 0)
        live_ref[:rows] = jnp.where(row_id < n_blocks * SUBLANES, jnp.int32(-1), jnp.int32(0))

        def search_bit(r, carry):
            need, kth = carry
            total = count_bits(live_ref[:rows] & plane_ref[r, :rows])
            take = total >= need
            keep_clear = jnp.where(take, jnp.int32(0), jnp.int32(-1))
            live_ref[:rows] = live_ref[:rows] & (plane_ref[r, :rows] ^ keep_clear)
            need = jnp.where(take, need, need - total)
            kth = jnp.where(take, kth | jnp.left_shift(jnp.int32(1), KEY_BITS - 1 - r), kth)
            return need, kth

        _, kth = lax.fori_loop(0, KEY_BITS, search_bit,
                               (jnp.full((1, tq), float(INDEX_TOPK), F32), jnp.zeros((1, tq), jnp.int32)))
        kth_ref[0:1] = kth ^ jnp.int32(INT_MIN)

    plane_rows = live_ref.shape[0]
    quarter = plane_rows // 4
    for rows in range(quarter, plane_rows + 1, quarter):
        @pl.when((n_blocks * SUBLANES > rows - quarter) & (n_blocks * SUBLANES <= rows))
        def _():
            radix_select(rows)

    def count_scores(pred):
        group = 4 * SUBLANES

        def add_chunk(c, acc):
            k0 = pl.multiple_of(c * ATT_KC, ATT_KC)
            hit = jnp.where(pred(score_ref[pl.ds(k0, ATT_KC), :]), 1.0, 0.0)
            return acc + jnp.sum(hit.reshape(ATT_KC // group, group, tq), axis=0)

        acc = lax.fori_loop(0, n_chunks, add_chunk, jnp.zeros((group, tq), F32))
        return jnp.sum(acc, axis=0, keepdims=True)

    few_keys = i * tq + lax.broadcasted_iota(jnp.int32, (1, tq), 1) + 1 < INDEX_TOPK
    lowest = float(jnp.finfo(F32).min)
    thr = jnp.where(few_keys, lowest, _key_score(jnp.maximum(kth_ref[0:1], KEY_NEG_INF + 1)))

    def bias_tile(sel):
        return jnp.where(sel, 0.0, MASK_BIAS).T

    def write_chunk(c, kept):
        parts = []
        for hh in range(blocks_per_chunk):
            k0 = pl.multiple_of(c * ATT_KC + hh * PLANE_KEYS, PLANE_KEYS)
            sel = score_ref[pl.ds(k0, PLANE_KEYS), :] >= thr
            parts.append(bias_tile(sel))
            hit = jnp.where(sel, 1.0, 0.0)
            kept = kept + jnp.sum(hit.reshape(PLANE_KEYS // kept.shape[0], kept.shape[0], tq), axis=0)
        bias_ref[c] = jnp.concatenate(parts, axis=1).astype(bias_ref.dtype)
        return kept

    kept = lax.fori_loop(0, n_chunks, write_chunk, jnp.zeros((4 * SUBLANES, tq), F32))
    kept = jnp.sum(kept, axis=0, keepdims=True)
    unsettled = jnp.max(jnp.where(few_keys | (kept == float(INDEX_TOPK)), 0.0, 1.0))

    @pl.when(unsettled > 0.0)
    def _():
        def search_bit(b, prefix):
            cand = prefix ^ jnp.left_shift(jnp.int32(1), KEY_BITS - 1 - b)
            cand_score = _key_score(jnp.maximum(cand, KEY_NEG_INF))
            admitted = count_scores(lambda sc: sc >= cand_score)
            return jnp.where(admitted >= float(INDEX_TOPK), cand, prefix)

        kth = lax.fori_loop(0, KEY_BITS, search_bit, jnp.full((1, tq), INT_MIN, jnp.int32))
        kth_score = jnp.where(few_keys, lowest, _key_score(jnp.maximum(kth, KEY_NEG_INF + 1)))
        above = count_scores(lambda sc: sc > kth_score)
        need = jnp.where(few_keys, float(2 ** 24), float(INDEX_TOPK) - above)

        row = lax.broadcasted_iota(jnp.int32, (PLANE_KEYS, PLANE_KEYS), 0)
        col = lax.broadcasted_iota(jnp.int32, (PLANE_KEYS, PLANE_KEYS), 1)
        prefix_sum = jnp.where(col <= row, 1.0, 0.0).astype(BF16)

        def write_chunk(c, seen):
            parts = []
            for hh in range(blocks_per_chunk):
                k0 = pl.multiple_of(c * ATT_KC + hh * PLANE_KEYS, PLANE_KEYS)
                sc = score_ref[pl.ds(k0, PLANE_KEYS), :]
                tie = sc == kth_score
                rank = seen + jnp.dot(prefix_sum, jnp.where(tie, 1.0, 0.0).astype(BF16),
                                      preferred_element_type=F32)
                parts.append(bias_tile((sc > kth_score) | (tie & (rank <= need))))
                seen = rank[PLANE_KEYS - 1:PLANE_KEYS, :]
            bias_ref[c] = jnp.concatenate(parts, axis=1).astype(bias_ref.dtype)
            return seen

        lax.fori_loop(0, n_chunks, write_chunk, jnp.zeros((1, tq), F32))

    def fill_chunk(c, carry):
        bias_ref[c] = jnp.full(bias_ref.shape[1:], MASK_BIAS, bias_ref.dtype)
        return carry

    lax.fori_loop(n_chunks, n_chunks_total, fill_chunk, 0)


def _indexer_mask(q_idx_t, idx_w_t, k_lo, k_hi):
    s = q_idx_t.shape[1]
    tq = SEL_TQ
    assert tq % PLANE_KEYS == 0 and s % (KEY_BITS * SUBLANES) == 0
    return pl.pallas_call(
        _indexer_body,
        grid=(s // tq,),
        in_specs=[pl.BlockSpec((q_idx_t.shape[0], tq), lambda i: (0, i)),
                  pl.BlockSpec((IDX_HEADS, tq), lambda i: (0, i)),
                  pl.BlockSpec(k_lo.shape, lambda i: (0, 0)),
                  pl.BlockSpec(k_hi.shape, lambda i: (0, 0))],
        out_specs=pl.BlockSpec((s // ATT_KC, tq, ATT_KC), lambda i: (0, i, 0)),
        out_shape=jax.ShapeDtypeStruct((s // ATT_KC, s, ATT_KC), BF16),
        scratch_shapes=[pltpu.VMEM((s, tq), F32),
                        pltpu.VMEM((KEY_BITS, s // KEY_BITS, tq), jnp.int32),
                        pltpu.VMEM((s // KEY_BITS, tq), jnp.int32),
                        pltpu.VMEM((SUBLANES, tq), jnp.int32)],
        compiler_params=_params(("arbitrary",)),
        name="indexer_mask",
    )(q_idx_t, idx_w_t, k_lo, k_hi)


def _attention_body(q_ref, k_ref, v_ref, bias_ref, o_ref, m_ref, acc_ref):
    tq = q_ref.shape[0]
    chunk_keys = bias_ref.shape[2]
    i = pl.program_id(0)
    n_chunks = (i * tq + tq - 1) // chunk_keys + 1

    m_ref[...] = jnp.full(m_ref.shape, M_INIT, F32)
    acc_ref[...] = jnp.zeros(acc_ref.shape, F32)

    def step(first_chunk, width):
        kc = width * chunk_keys
        sub = kc // LANES
        k0 = pl.multiple_of(first_chunk * chunk_keys, chunk_keys)
        bias = jnp.concatenate([bias_ref[first_chunk + c] for c in range(width)], axis=1)
        ones = jnp.ones((kc, HEAD_DIM), v_ref.dtype)
        for h in range(ATTN_HEADS):
            cols = slice(h * HEAD_DIM, (h + 1) * HEAD_DIM)
            s = lax.dot_general(q_ref[:, cols], k_ref[pl.ds(k0, kc), cols],
                                (((1,), (1,)), ((), ())), preferred_element_type=F32)
            sb = s.astype(BF16) + bias
            part = sb[:, :LANES]
            for c in range(1, sub):
                part = jnp.maximum(part, sb[:, c * LANES:(c + 1) * LANES])
            m_prev = m_ref[h]
            m_next = jnp.maximum(m_prev, jnp.max(part.astype(F32), axis=1, keepdims=True))
            p = jnp.exp2(sb - jnp.concatenate([m_next.astype(BF16)] * sub, axis=1))
            alpha = jnp.exp2(m_prev - m_next)
            m_ref[h] = m_next
            pv = jnp.dot(p, jnp.concatenate([v_ref[pl.ds(k0, kc), cols], ones], axis=1),
                         preferred_element_type=F32)
            acc_ref[h] = jnp.concatenate([alpha, alpha], axis=1) * acc_ref[h] + pv

    def wide_step(j, carry):
        step(j * ATT_STEP_CHUNKS, ATT_STEP_CHUNKS)
        return carry

    n_wide = n_chunks // ATT_STEP_CHUNKS
    lax.fori_loop(0, n_wide, wide_step, 0)
    for r in range(1, ATT_STEP_CHUNKS):
        @pl.when(n_chunks % ATT_STEP_CHUNKS >= r)
        def _():
            step(n_wide * ATT_STEP_CHUNKS + r - 1, 1)

    for h in range(ATTN_HEADS):
        acc = acc_ref[h]
        o_ref[:, h * HEAD_DIM:(h + 1) * HEAD_DIM] = (acc[:, :HEAD_DIM] / acc[:, HEAD_DIM:]).astype(o_ref.dtype)


def _attention(q, k, v, bias):
    s, width = q.shape
    tq = SEL_TQ
    n_chunks, _, kc = bias.shape
    resident = dict(pipeline_mode=pl.Buffered(1))
    return pl.pallas_call(
        _attention_body,
        grid=(s // tq,),
        in_specs=[pl.BlockSpec((tq, width), lambda i: (i, 0)),
                  pl.BlockSpec((s, width), lambda i: (0, 0), **resident),
                  pl.BlockSpec((s, width), lambda i: (0, 0), **resident),
                  pl.BlockSpec((n_chunks, tq, kc), lambda i: (0, i, 0))],
        out_specs=pl.BlockSpec((tq, width), lambda i: (i, 0)),
        out_shape=jax.ShapeDtypeStruct((s, width), BF16),
        scratch_shapes=[pltpu.VMEM((ATTN_HEADS, tq, LANES), F32),
                        pltpu.VMEM((ATTN_HEADS, tq, 2 * HEAD_DIM), F32)],
        compiler_params=_params(("arbitrary",)),
        name="masked_attention",
    )(q, k, v, bias)


def _ffn_down_body(a_ref, w_ref, x_ref, g_ref, gate_ref, o_ref):
    f = jnp.dot(a_ref[...], w_ref[...], preferred_element_type=F32)
    o_ref[...] = x_ref[...] + gate_ref[...] * _rms(f, g_ref[...])


def _ffn_down(a, w, x1, g_post, gate):
    m, kdim = a.shape
    d = w.shape[1]
    tm = FFN_DOWN_TM
    row = pl.BlockSpec((1, d), lambda i: (0, 0))
    return pl.pallas_call(
        _ffn_down_body,
        grid=(m // tm,),
        in_specs=[pl.BlockSpec((tm, kdim), lambda i: (i, 0)),
                  pl.BlockSpec((kdim, d), lambda i: (0, 0), pipeline_mode=pl.Buffered(1)),
                  pl.BlockSpec((tm, d), lambda i: (i, 0)), row, row],
        out_specs=pl.BlockSpec((tm, d), lambda i: (i, 0)),
        out_shape=jax.ShapeDtypeStruct((m, d), F32),
        compiler_params=_params(("arbitrary",)),
        name="ffn_down_residual",
    )(a, w, x1, g_post, gate)


def _rope_angles(seq, dim, repeat=1):
    inv = jnp.tile(1.0 / (ROPE_THETA ** (jnp.arange(0, dim, 2, dtype=F32) / dim)), repeat)
    coarse = (jnp.arange(seq // ROPE_BLOCK, dtype=F32) * ROPE_BLOCK)[:, None] * inv[None, :]
    fine = jnp.arange(ROPE_BLOCK, dtype=F32)[:, None] * inv[None, :]
    cos_a, sin_a = jnp.cos(coarse)[:, None, :], jnp.sin(coarse)[:, None, :]
    cos_b, sin_b = jnp.cos(fine)[None, :, :], jnp.sin(fine)[None, :, :]
    cos = (cos_a * cos_b - sin_a * sin_b).reshape(seq, inv.shape[0])
    sin = (sin_a * cos_b + cos_a * sin_b).reshape(seq, inv.shape[0])
    return cos, sin


def _attn_rope_tables(seq):
    cos, sin = _rope_angles(seq, HEAD_DIM, repeat=2)
    sign = jnp.concatenate([-jnp.ones((HEAD_DIM // 2,), F32), jnp.ones((HEAD_DIM // 2,), F32)])
    return cos, sin * sign


def _index_rope_tables(seq):
    cos, sin = _rope_angles(seq, IDX_ROPE_DIM)
    return cos, sin, cos.T, sin.T


def _layer(x, mod, g_pre_mix, g_post_mix, w_in_t, gmlp_ln_g, gmlp_ln_b, gmlp_w_s, gmlp_b_s,
           q_lat_norm_g, w_q_up, w_qidx_up, kidx_ln_g, kidx_ln_b, w_proj_a, w_proj_b, w_out,
           g_pre_ffn, g_post_ffn, w_ffn_gate, w_ffn_up, w_ffn_down, tables):
    s, d = x.shape
    cc, ss, icos, isin, icos_t, isin_t = tables
    row = lambda a: a.reshape(1, -1)
    shift_m, scale_m, gate_m, shift_f, scale_f, gate_f = (mod[:, n * d:(n + 1) * d] for n in range(N_MOD))

    tm = MM_TM
    vec_d = lambda a: (a, (1, d), lambda j, i: (0, 0))
    u, h = _matmul([x], [(w_in_t, 0, False)], [0], [vec_d(row(g_pre_mix)), vec_d(scale_m), vec_d(shift_m)],
                   [((s, GMLP_WIDTH), BF16, (tm, GMLP_WIDTH), _tile), ((s, d), BF16, (tm, d), _rowblk)],
                   _ep_gelu, n=GMLP_WIDTH, tm=tm, tn=GMLP_WIDTH, sub_n=EPILOGUE_SUB_N,
                   sub_m=EPILOGUE_SUB_N, prologue=_pro_prenorm, name="prenorm_in_proj_u")
    in_proj = functools.partial(_simple_mm, h, w_in_t, kn=False)
    vn = in_proj(_ep_gelu_layernorm,
                 [(row(gmlp_ln_g), (1, GMLP_WIDTH), _colblk), (row(gmlp_ln_b), (1, GMLP_WIDTH), _colblk)],
                 BF16, n=GMLP_WIDTH, first=GMLP_WIDTH, tm=tm, tn=GMLP_WIDTH, sub_m=EPILOGUE_SUB_M,
                 name="in_proj_v_ln")
    gmlp_bias = jnp.repeat(gmlp_b_s.T, LANES, axis=1)
    y_a = _gmlp(u, vn, gmlp_w_s, gmlp_bias)

    q_lat = in_proj(_ep_rmsnorm, [(row(q_lat_norm_g), (1, Q_LORA_RANK), _colblk)],
                    BF16, n=Q_LORA_RANK, first=COL_QLAT, tm=tm, tn=Q_LORA_RANK, name="in_proj_qlat")
    q_scale = HEAD_DIM ** -0.5 * math.log2(math.e)
    rope_extras = [(cc, (tm, HEAD_DIM), _rowblk), (ss, (tm, HEAD_DIM), _rowblk)]
    q = _simple_mm(q_lat, w_q_up, functools.partial(_ep_rope, scale=q_scale), rope_extras,
                   BF16, n=ATTN_WIDTH, tm=tm, tn=ATTN_WIDTH, sub_n=EPILOGUE_SUB_N, name="q_up_rope")
    tm2 = MM_TM_WIDE
    k = in_proj(functools.partial(_ep_rope, scale=1.0),
                [(cc, (tm2, HEAD_DIM), _rowblk), (ss, (tm2, HEAD_DIM), _rowblk)], BF16,
                n=ATTN_WIDTH, first=COL_K, tm=tm2, tn=ATTN_WIDTH, sub_n=EPILOGUE_SUB_N, sub_m=tm,
                name="in_proj_k_rope")
    v = in_proj(_ep_cast, [], BF16, n=ATTN_WIDTH, first=COL_V, tm=tm2, tn=ATTN_WIDTH,
                sub_n=EPILOGUE_SUB_N, sub_m=tm, name="in_proj_v")
    half = IDX_ROPE_DIM // 2
    q_idx_t = _simple_mm(q_lat, w_qidx_up, _ep_partial_rope_t,
                         [(icos_t, (half, tm), _colblk_t), (isin_t, (half, tm), _colblk_t)],
                         BF16, n=IDX_HEADS * IDX_HEAD_DIM, tm=tm, tn=QIDX_TN, name="qidx_up_rope",
                         out_t=True)
    pad = lambda a: jnp.pad(a, (0, LANES - a.shape[0])).reshape(1, LANES)
    idx_w_scale = (IDX_HEADS ** -0.5) * (IDX_HEAD_DIM ** -0.5)
    lane_tile = lambda dtype: ((s, LANES), dtype, (tm, LANES), _tile)
    k_lo, k_hi, idx_w_t = _matmul(
        [h], [(w_in_t, COL_KIDX, False)], [0],
        [(pad(kidx_ln_g), (1, LANES), _colblk), (pad(kidx_ln_b), (1, LANES), _colblk),
         (icos, (tm, half), _rowblk), (isin, (tm, half), _rowblk)],
        [lane_tile(BF16), lane_tile(BF16), ((IDX_HEADS, s), F32, (IDX_HEADS, tm), _colblk_t)],
        functools.partial(_ep_index_keys, w_scale=idx_w_scale), n=LANES, tm=tm, tn=LANES,
        name="in_proj_index_keys")
    bias = _indexer_mask(q_idx_t, idx_w_t, k_lo, k_hi)
    y_b = _attention(q, k, v, bias)

    gates = in_proj(_ep_sigmoid, [], BF16, n=2 * d, first=COL_GATE, tm=tm2, tn=MM_TN,
                    sub_n=EPILOGUE_SUB_N, sub_m=tm, name="in_proj_gates")
    n_gate_blocks = d // MM_TN
    merged = _matmul([y_a, y_b], [(w_proj_a, 0, True), (w_proj_b, 0, True)], [0, 1],
                     [(gates, (tm, MM_TN), _tile),
                      (gates, (tm, MM_TN), lambda j, i: (i, j + n_gate_blocks))],
                     [((s, d), BF16, (tm, MM_TN), _tile)], _ep_merge, n=d, tm=tm, tn=MM_TN,
                     sub_n=EPILOGUE_SUB_N, name="branch_merge")[0]
    tm_full = ROWNORM_TM
    vec = lambda a: (a, (1, d), _colblk)
    x1, h2 = _matmul([merged], [(w_out, 0, True)], [0],
                     [(x, (tm_full, d), _rowblk), vec(row(g_post_mix)), vec(gate_m),
                      vec(row(g_pre_ffn)), vec(scale_f), vec(shift_f)],
                     [((s, d), F32, (tm_full, d), _tile), ((s, d), BF16, (tm_full, d), _tile)],
                     _ep_mix_residual, n=d, tm=tm_full, tn=d, sub_m=EPILOGUE_SUB_M, name="out_proj_residual")

    ffn_hidden = w_ffn_gate.shape[1]
    tn_ffn, tm_ffn = FFN_UP_TN, MM_TM_WIDE
    n_row_blocks = s // tm_ffn
    down_rows = ffn_hidden // ((ffn_hidden // tn_ffn) * n_row_blocks)
    down_blk = ((down_rows, d), lambda j, i: (j * n_row_blocks + i, 0))
    act, w_down = _matmul([h2], [(w_ffn_gate, 0, True), (w_ffn_up, 0, True)], [0, 0],
                          [(w_ffn_down,) + down_blk],
                          [((s, ffn_hidden), BF16, (tm_ffn, tn_ffn), _tile), (w_ffn_down.shape, BF16) + down_blk],
                          _ep_swiglu, n=ffn_hidden, tm=tm_ffn, tn=tn_ffn, sub_m=MM_TM, name="ffn_up")
    return _ffn_down(act, w_down, x1, row(g_post_ffn), gate_f)


def kernel(x, c, w_mod, b_mod, g_pre_mix, g_post_mix, w_in, gmlp_ln_g, gmlp_ln_b, gmlp_w_s, gmlp_b_s, q_lat_norm_g, w_q_up, w_qidx_up, kidx_ln_g, kidx_ln_b, w_proj_a, w_proj_b, w_out, g_pre_ffn, g_post_ffn, w_ffn_gate, w_ffn_up, w_ffn_down):
    batch, seq, d = x.shape
    assert batch == 1 and d == D_MODEL
    tables = _attn_rope_tables(seq) + _index_rope_tables(seq)
    y = x[0]
    for l in range(w_mod.shape[0]):
        mod = _modulation(c, w_mod[l], b_mod[l])
        y = _layer(y, mod, g_pre_mix[l], g_post_mix[l], w_in[l].T, gmlp_ln_g[l], gmlp_ln_b[l],
                   gmlp_w_s[l], gmlp_b_s[l], q_lat_norm_g[l], w_q_up[l], w_qidx_up[l],
                   kidx_ln_g[l], kidx_ln_b[l], w_proj_a[l], w_proj_b[l], w_out[l],
                   g_pre_ffn[l], g_post_ffn[l], w_ffn_gate[l], w_ffn_up[l], w_ffn_down[l], tables)
    return y[None]
```

```python
import functools
import math

import jax
import jax.numpy as jnp
from jax import lax
from jax.experimental import pallas as pl
from jax.experimental.pallas import tpu as pltpu

F32 = jnp.float32
BF16 = jnp.bfloat16

D_MODEL = 2048
GMLP_WIDTH = 1024
GMLP_GROUPS = 8
CHUNK = 128
ATTN_HEADS = 8
HEAD_DIM = 128
ATTN_WIDTH = ATTN_HEADS * HEAD_DIM
Q_LORA_RANK = 512
IDX_HEADS = 16
IDX_HEAD_DIM = 64
IDX_ROPE_DIM = 32
INDEX_TOPK = 256
ROPE_THETA = 10000.0
N_MOD = 6
NORM_EPS = 1e-6

COL_QLAT = 2 * GMLP_WIDTH
COL_K = COL_QLAT + Q_LORA_RANK
COL_V = COL_K + ATTN_WIDTH
COL_KIDX = COL_V + ATTN_WIDTH
COL_IDXW = COL_KIDX + IDX_HEAD_DIM
COL_GATE = COL_IDXW + IDX_HEADS

LANES = 128
SUBLANES = 8
VMEM_LIMIT = 56 * 1024 * 1024

MASK_BIAS = -(2.0 ** 100)
M_INIT = -(2.0 ** 99)
INT_MIN = -2 ** 31
KEY_NEG_INF = -2139095041

SEL_TQ = 256
KEY_BITS = 32
PLANE_KEYS = KEY_BITS * SUBLANES
ATT_KC = 2 * PLANE_KEYS
ATT_STEP_CHUNKS = 2
EPILOGUE_SUB_N = 512
EPILOGUE_SUB_M = 256
ROPE_BLOCK = 64

MM_TM = 1024
MM_TN = 1024
ROWNORM_TM = 512
FFN_UP_TN = 512
MM_TM_WIDE = 2 * MM_TM
FFN_DOWN_TM = 256
QIDX_TN = 512
STREAM_TM = 1024
MOD_TN = 1024


def _params(sem=None):
    return pltpu.CompilerParams(dimension_semantics=sem, vmem_limit_bytes=VMEM_LIMIT)


def _mod_body(cb_ref, w_ref, b_ref, o_ref):
    cb = cb_ref[...]
    tn = o_ref.shape[1]
    parts = [jnp.sum(w_ref[:, p * LANES:(p + 1) * LANES] * cb, axis=0, keepdims=True)
             for p in range(tn // LANES)]
    o_ref[...] = jnp.concatenate(parts, axis=1) + b_ref[...]


def _modulation(c, w_mod, b_mod):
    k, n = w_mod.shape
    tn = MOD_TN
    cb = jnp.broadcast_to(c.reshape(k, 1), (k, LANES))
    return pl.pallas_call(
        _mod_body,
        grid=(n // tn,),
        in_specs=[pl.BlockSpec((k, LANES), lambda j: (0, 0)),
                  pl.BlockSpec((k, tn), lambda j: (0, j)),
                  pl.BlockSpec((1, tn), lambda j: (0, j))],
        out_specs=pl.BlockSpec((1, tn), lambda j: (0, j)),
        out_shape=jax.ShapeDtypeStruct((1, n), F32),
        compiler_params=_params(("arbitrary",)),
        name="modulation",
    )(cb, w_mod, b_mod.reshape(1, n))


def _rms(x, g):
    return x * lax.rsqrt(jnp.mean(x * x, axis=-1, keepdims=True) + NORM_EPS) * g


def _mm_body(*refs, n_x, pairs, w_kn, out_t, sub_m, sub_n, n_extra, n_out, epilogue, prologue):
    n_w = len(pairs)
    x_refs = refs[:n_x]
    w_refs = refs[n_x:n_x + n_w]
    e_refs = refs[n_x + n_w:n_x + n_w + n_extra]
    o_refs = refs[n_x + n_w + n_extra:n_x + n_w + n_extra + n_out]
    wb_refs = refs[n_x + n_w + n_extra + n_out:]

    @pl.when(pl.program_id(1) == 0)
    def _():
        for w_ref, wb_ref, kn in zip(w_refs, wb_refs, w_kn):
            w = w_ref[...]
            if kn == out_t:
                w = w.T
            wb_ref[...] = w.astype(wb_ref.dtype)

    if out_t:
        accs = [lax.dot_general(wb_ref[...], x_refs[xi][...], (((1,), (1,)), ((), ())),
                                preferred_element_type=F32)
                for xi, wb_ref in zip(pairs, wb_refs)]
        epilogue(accs, e_refs, o_refs, slice(0, wb_refs[0].shape[0]), slice(0, x_refs[0].shape[0]))
        return
    tm, tn = x_refs[0].shape[0], wb_refs[0].shape[1]
    for r0 in range(0, tm, sub_m):
        rs = slice(r0, r0 + sub_m)
        if prologue is None:
            lhs = [x_ref[rs, :] for x_ref in x_refs]
        else:
            lhs = prologue(x_refs, e_refs, o_refs, rs)
        for c0 in range(0, tn, sub_n):
            cs = slice(c0, c0 + sub_n)
            accs = [jnp.dot(lhs[xi], wb_ref[:, cs], preferred_element_type=F32)
                    for xi, wb_ref in zip(pairs, wb_refs)]
            epilogue(accs, e_refs, o_refs, cs, rs)


def _matmul(xs, ws, pairs, extras, outs, epilogue, *, n, tm, tn, name, out_t=False, sub_n=None,
            sub_m=None, prologue=None):
    m = xs[0].shape[0]
    n_col = n // tn
    w_mode = dict(pipeline_mode=pl.Buffered(1)) if n_col == 1 else {}
    in_specs = [pl.BlockSpec((tm, x.shape[1]), lambda j, i: (i, 0)) for x in xs]
    scratch = []
    for w, first, kn in ws:
        if kn:
            kdim = w.shape[0]
            assert first % tn == 0
            in_specs.append(pl.BlockSpec((kdim, tn), functools.partial(_w_cols, first=first // tn), **w_mode))
        else:
            kdim = w.shape[1]
            in_specs.append(pl.BlockSpec((pl.Element(tn), pl.Element(kdim)),
                                         functools.partial(_w_rows, first=first, tn=tn), **w_mode))
        scratch.append(pltpu.VMEM((tn, kdim) if out_t else (kdim, tn), BF16))
    in_specs += [pl.BlockSpec(bs, im) for _, bs, im in extras]
    body = functools.partial(_mm_body, n_x=len(xs), pairs=tuple(pairs), w_kn=tuple(kn for _, _, kn in ws),
                             out_t=out_t, sub_m=sub_m or tm, sub_n=sub_n or tn, n_extra=len(extras),
                             n_out=len(outs), epilogue=epilogue, prologue=prologue)
    return pl.pallas_call(
        body,
        grid=(n_col, m // tm),
        in_specs=in_specs,
        out_specs=[pl.BlockSpec(bs, im) for _, _, bs, im in outs],
        out_shape=[jax.ShapeDtypeStruct(s, d) for s, d, _, _ in outs],
        scratch_shapes=scratch,
        compiler_params=_params(("arbitrary", "arbitrary")),
        name=name,
    )(*xs, *[w for w, _, _ in ws], *[a for a, _, _ in extras])


def _w_cols(j, i, *, first):
    return (0, first + j)


def _w_rows(j, i, *, first, tn):
    assert first % SUBLANES == 0 and tn % SUBLANES == 0
    return (pl.multiple_of(first + j * tn, SUBLANES), 0)


def _tile(j, i):
    return (i, j)


def _tile_t(j, i):
    return (j, i)


def _rowblk(j, i):
    return (i, 0)


def _colblk(j, i):
    return (0, j)


def _colblk_t(j, i):
    return (0, i)


def _pro_prenorm(x_refs, e_refs, o_refs, rows):
    g, scale, shift = e_refs[0][...], e_refs[1][...], e_refs[2][...]
    h = (_rms(x_refs[0][rows, :], g) * (1.0 + scale) + shift).astype(o_refs[1].dtype)
    o_refs[1][rows, :] = h
    return [h]


def _ep_gelu(accs, e_refs, o_refs, cols, rows):
    o_refs[0][rows, cols] = jax.nn.gelu(accs[0]).astype(o_refs[0].dtype)


def _ep_gelu_layernorm(accs, e_refs, o_refs, cols, rows):
    z = jax.nn.gelu(accs[0])
    mu = jnp.mean(z, axis=-1, keepdims=True)
    zc = z - mu
    var = jnp.mean(zc * zc, axis=-1, keepdims=True)
    y = zc * lax.rsqrt(var + NORM_EPS) * e_refs[0][...] + e_refs[1][...]
    o_refs[0][rows, :] = y.astype(o_refs[0].dtype)


def _ep_rmsnorm(accs, e_refs, o_refs, cols, rows):
    o_refs[0][rows, :] = _rms(accs[0], e_refs[0][...]).astype(o_refs[0].dtype)


def _ep_cast(accs, e_refs, o_refs, cols, rows):
    o_refs[0][rows, cols] = accs[0].astype(o_refs[0].dtype)


def _ep_sigmoid(accs, e_refs, o_refs, cols, rows):
    o_refs[0][rows, cols] = (0.5 * jnp.tanh(0.5 * accs[0]) + 0.5).astype(o_refs[0].dtype)


def _ep_rope(accs, e_refs, o_refs, cols, rows, *, scale):
    cc = e_refs[0][rows, :]
    ss = e_refs[1][rows, :]
    acc = accs[0]
    for h in range(acc.shape[1] // HEAD_DIM):
        xh = acc[:, h * HEAD_DIM:(h + 1) * HEAD_DIM]
        r = xh * cc + pltpu.roll(xh, HEAD_DIM // 2, 1) * ss
        if scale != 1.0:
            r = r * scale
        c0 = cols.start + h * HEAD_DIM
        o_refs[0][rows, c0:c0 + HEAD_DIM] = r.astype(o_refs[0].dtype)


def _ep_rope_and_cast(accs, e_refs, o_refs, cols, rows):
    _ep_rope(accs[:1], e_refs, o_refs[:1], cols, rows, scale=1.0)
    o_refs[1][rows, cols] = accs[1].astype(o_refs[1].dtype)


def _ep_partial_rope_t(accs, e_refs, o_refs, cols, rows):
    c = e_refs[0][...]
    s = e_refs[1][...]
    acc = accs[0]
    half = IDX_ROPE_DIM // 2
    for h in range(acc.shape[0] // IDX_HEAD_DIM):
        r0 = h * IDX_HEAD_DIM
        x1 = acc[r0:r0 + half, :]
        x2 = acc[r0 + half:r0 + 2 * half, :]
        o_refs[0][r0:r0 + half, :] = (x1 * c - x2 * s).astype(o_refs[0].dtype)
        o_refs[0][r0 + half:r0 + 2 * half, :] = (x2 * c + x1 * s).astype(o_refs[0].dtype)
        o_refs[0][r0 + 2 * half:r0 + IDX_HEAD_DIM, :] = (
            acc[r0 + 2 * half:r0 + IDX_HEAD_DIM, :].astype(o_refs[0].dtype))


def _partial_rope(x, c, a, b):
    half = IDX_ROPE_DIM // 2
    return x * c + pltpu.roll(x, LANES - half, 1) * a + pltpu.roll(x, half, 1) * b


def _ep_index_keys(accs, e_refs, o_refs, cols, rows, *, w_scale):
    g, bb = e_refs[0][...], e_refs[1][...]
    cos, sin = e_refs[2][rows, :], e_refs[3][rows, :]
    acc = accs[0]
    n, half = acc.shape[0], IDX_ROPE_DIM // 2
    c = jnp.concatenate([cos, cos, jnp.ones((n, LANES - 2 * half), F32)], axis=1)
    a = jnp.concatenate([-sin, jnp.zeros((n, LANES - half), F32)], axis=1)
    b = jnp.concatenate([jnp.zeros((n, half), F32), sin, jnp.zeros((n, LANES - 2 * half), F32)], axis=1)
    lane = lax.broadcasted_iota(jnp.int32, acc.shape, 1)
    is_key = lane < IDX_HEAD_DIM
    mu = jnp.sum(jnp.where(is_key, acc, 0.0), axis=-1, keepdims=True) / IDX_HEAD_DIM
    xc = jnp.where(is_key, acc - mu, 0.0)
    var = jnp.sum(xc * xc, axis=-1, keepdims=True) / IDX_HEAD_DIM
    y = xc * lax.rsqrt(var + NORM_EPS) * g + bb
    y = jnp.where(is_key, _partial_rope(y, c, a, b), 0.0)
    o_refs[0][rows, :] = y.astype(o_refs[0].dtype)
    o_refs[1][rows, :] = pltpu.roll(y, IDX_HEAD_DIM, 1).astype(o_refs[1].dtype)
    o_refs[2][:, rows] = (acc * w_scale).T[IDX_HEAD_DIM:IDX_HEAD_DIM + IDX_HEADS, :]


def _ep_merge(accs, e_refs, o_refs, cols, rows):
    o = e_refs[0][rows, cols].astype(F32) * accs[0] + e_refs[1][rows, cols].astype(F32) * accs[1]
    o_refs[0][rows, cols] = o.astype(o_refs[0].dtype)


def _ep_mix_residual(accs, e_refs, o_refs, cols, rows):
    x_ref, g_post, gate, g_pre, scale, shift = e_refs
    x1 = x_ref[rows, :] + gate[...] * _rms(accs[0], g_post[...])
    o_refs[0][rows, :] = x1
    h2 = _rms(x1, g_pre[...]) * (1.0 + scale[...]) + shift[...]
    o_refs[1][rows, :] = h2.astype(o_refs[1].dtype)


def _ep_swiglu(accs, e_refs, o_refs, cols, rows):
    o_refs[0][rows, cols] = (jax.nn.silu(accs[0]) * accs[1]).astype(o_refs[0].dtype)

    if cols.start == 0 and rows.start == 0:
        o_refs[1][...] = e_refs[0][...].astype(o_refs[1].dtype)


def _simple_mm(x, w, epilogue, extras, out_dtype, *, n, tm, tn, name, first=0, kn=True, out_t=False,
               sub_n=None, sub_m=None):
    m = x.shape[0]
    out = ((n, m), out_dtype, (tn, tm), _tile_t) if out_t else ((m, n), out_dtype, (tm, tn), _tile)
    return _matmul([x], [(w, first, kn)], [0], extras, [out], epilogue, n=n, tm=tm, tn=tn,
                   name=name, out_t=out_t, sub_n=sub_n, sub_m=sub_m)[0]


def _gmlp_body(u_ref, v_ref, w_ref, b_ref, o_ref):
    t = lax.broadcasted_iota(jnp.int32, (CHUNK, CHUNK), 0)
    s = lax.broadcasted_iota(jnp.int32, (CHUNK, CHUNK), 1)
    causal = s <= t
    for g in range(GMLP_GROUPS):
        w = jnp.where(causal, w_ref[g], 0.0).astype(BF16)
        cols = slice(g * LANES, (g + 1) * LANES)
        bias = b_ref[:, cols]
        for c in range(u_ref.shape[0] // CHUNK):
            rows = slice(c * CHUNK, (c + 1) * CHUNK)
            sv = jnp.dot(w, v_ref[rows, cols], preferred_element_type=F32) + bias
            o_ref[rows, cols] = (u_ref[rows, cols].astype(F32) * sv).astype(o_ref.dtype)


def _gmlp(u, vn, w_s, bias):
    m, n = u.shape
    tm = STREAM_TM
    blk = pl.BlockSpec((tm, n), lambda i: (i, 0))
    return pl.pallas_call(
        _gmlp_body,
        grid=(m // tm,),
        in_specs=[blk, blk,
                  pl.BlockSpec(w_s.shape, lambda i: (0, 0, 0)),
                  pl.BlockSpec(bias.shape, lambda i: (0, 0))],
        out_specs=blk,
        out_shape=jax.ShapeDtypeStruct((m, n), BF16),
        compiler_params=_params(("arbitrary",)),
        name="gmlp_gating",
    )(u, vn, w_s, bias)


def _bit_planes(words):
    a = list(words)
    j, mask = KEY_BITS // 2, 0x0000FFFF
    while j:
        for k in range(KEY_BITS):
            if k & j:
                continue
            t = (a[k] ^ (a[k + j] >> j)) & jnp.int32(mask)
            a[k] = a[k] ^ t
            a[k + j] = a[k + j] ^ (t << j)
        j //= 2
        mask = (mask ^ (mask << j)) & 0xFFFFFFFF
    return a


def _score_key(score):
    bits = lax.bitcast_convert_type(score, jnp.int32)
    return bits ^ ((bits >> 31) & jnp.int32(0x7FFFFFFF))


def _key_score(key):
    return lax.bitcast_convert_type(key ^ ((key >> 31) & jnp.int32(0x7FFFFFFF)), F32)


def _indexer_body(qt_ref, wt_ref, klo_ref, khi_ref, bias_ref, score_ref, plane_ref, live_ref, kth_ref):
    tq = qt_ref.shape[1]
    n_chunks_total = bias_ref.shape[0]
    i = pl.program_id(0)
    n_chunks = ((i + 1) * tq + ATT_KC - 1) // ATT_KC
    blocks_per_chunk = ATT_KC // PLANE_KEYS
    n_blocks = n_chunks * blocks_per_chunk
    heads_per_group = LANES // IDX_HEAD_DIM
    k_refs = (klo_ref, khi_ref)

    @pl.when(i == 0)
    def _():
        plane_ref[...] = jnp.zeros(plane_ref.shape, jnp.int32)

    q_pos = i * tq + lax.broadcasted_iota(jnp.int32, (ATT_KC, tq), 1)
    k_off = lax.broadcasted_iota(jnp.int32, (ATT_KC, tq), 0)

    def score_chunk(c):
        k0 = pl.multiple_of(c * ATT_KC, ATT_KC)
        score = jnp.zeros((ATT_KC, tq), F32)
        for h in range(IDX_HEADS):
            grp, r = divmod(h, heads_per_group)
            logit = jnp.dot(k_refs[r][pl.ds(k0, ATT_KC), :], qt_ref[grp * LANES:(grp + 1) * LANES, :],
                            preferred_element_type=F32)
            score = score + jnp.maximum(logit, 0.0) * wt_ref[h:h + 1, :]
        score_ref[pl.ds(k0, ATT_KC), :] = jnp.where(k0 + k_off <= q_pos, score, -jnp.inf)

    def slice_chunk(c):
        for hh in range(blocks_per_chunk):
            blk = c * blocks_per_chunk + hh
            k0 = pl.multiple_of(blk * PLANE_KEYS, PLANE_KEYS)
            bits = lax.bitcast_convert_type(score_ref[pl.ds(k0, PLANE_KEYS), :], jnp.int32)
            planes = _bit_planes([bits[w * SUBLANES:(w + 1) * SUBLANES, :] for w in range(KEY_BITS)])
            r0 = pl.multiple_of(blk * SUBLANES, SUBLANES)
            sign = planes[0]
            plane_ref[0, pl.ds(r0, SUBLANES), :] = ~sign
            for r in range(1, KEY_BITS):
                plane_ref[r, pl.ds(r0, SUBLANES), :] = planes[r] ^ sign

    def score_and_slice(c, carry):
        slice_chunk(c - 1)
        score_chunk(c)
        return carry

    score_chunk(0)
    lax.fori_loop(1, n_chunks, score_and_slice, 0)
    slice_chunk(n_chunks - 1)

    def count_bits(words):
        ones = lax.population_count(words)
        part = jnp.sum(ones.reshape(words.shape[0] // SUBLANES, SUBLANES, tq), axis=0)
        return jnp.sum(part.astype(F32), axis=0, keepdims=True)

    def radix_select(rows):
        row_id = lax.broadcasted_iota(jnp.int32, (rows, tq), 0)
        live_ref[:rows] = jnp.where(row_id < n_blocks * SUBLANES, jnp.int32(-1), jnp.int32(0))

        def search_bit(r, carry):
            need, kth = carry
            total = count_bits(live_ref[:rows] & plane_ref[r, :rows])
            take = total >= need
            keep_clear = jnp.where(take, jnp.int32(0), jnp.int32(-1))
            live_ref[:rows] = live_ref[:rows] & (plane_ref[r, :rows] ^ keep_clear)
            need = jnp.where(take, need, need - total)
            kth = jnp.where(take, kth | jnp.left_shift(jnp.int32(1), KEY_BITS - 1 - r), kth)
            return need, kth

        _, kth = lax.fori_loop(0, KEY_BITS, search_bit,
                               (jnp.full((1, tq), float(INDEX_TOPK), F32), jnp.zeros((1, tq), jnp.int32)))
        kth_ref[0:1] = kth ^ jnp.int32(INT_MIN)

    plane_rows = live_ref.shape[0]
    quarter = plane_rows // 4
    for rows in range(quarter, plane_rows + 1, quarter):
        @pl.when((n_blocks * SUBLANES > rows - quarter) & (n_blocks * SUBLANES <= rows))
        def _():
            radix_select(rows)

    def count_scores(pred):
        group = 4 * SUBLANES

        def add_chunk(c, acc):
            k0 = pl.multiple_of(c * ATT_KC, ATT_KC)
            hit = jnp.where(pred(score_ref[pl.ds(k0, ATT_KC), :]), 1.0, 0.0)
            return acc + jnp.sum(hit.reshape(ATT_KC // group, group, tq), axis=0)

        acc = lax.fori_loop(0, n_chunks, add_chunk, jnp.zeros((group, tq), F32))
        return jnp.sum(acc, axis=0, keepdims=True)

    few_keys = i * tq + lax.broadcasted_iota(jnp.int32, (1, tq), 1) + 1 < INDEX_TOPK
    lowest = float(jnp.finfo(F32).min)
    thr = jnp.where(few_keys, lowest, _key_score(jnp.maximum(kth_ref[0:1], KEY_NEG_INF + 1)))

    def bias_tile(sel):
        return jnp.where(sel, 0.0, MASK_BIAS).T

    def write_chunk(c, kept):
        parts = []
        for hh in range(blocks_per_chunk):
            k0 = pl.multiple_of(c * ATT_KC + hh * PLANE_KEYS, PLANE_KEYS)
            sel = score_ref[pl.ds(k0, PLANE_KEYS), :] >= thr
            parts.append(bias_tile(sel))
            hit = jnp.where(sel, 1.0, 0.0)
            kept = kept + jnp.sum(hit.reshape(PLANE_KEYS // kept.shape[0], kept.shape[0], tq), axis=0)
        bias_ref[c] = jnp.concatenate(parts, axis=1).astype(bias_ref.dtype)
        return kept

    kept = lax.fori_loop(0, n_chunks, write_chunk, jnp.zeros((4 * SUBLANES, tq), F32))
    kept = jnp.sum(kept, axis=0, keepdims=True)
    unsettled = jnp.max(jnp.where(few_keys | (kept == float(INDEX_TOPK)), 0.0, 1.0))

    @pl.when(unsettled > 0.0)
    def _():
        def search_bit(b, prefix):
            cand = prefix ^ jnp.left_shift(jnp.int32(1), KEY_BITS - 1 - b)
            cand_score = _key_score(jnp.maximum(cand, KEY_NEG_INF))
            admitted = count_scores(lambda sc: sc >= cand_score)
            return jnp.where(admitted >= float(INDEX_TOPK), cand, prefix)

        kth = lax.fori_loop(0, KEY_BITS, search_bit, jnp.full((1, tq), INT_MIN, jnp.int32))
        kth_score = jnp.where(few_keys, lowest, _key_score(jnp.maximum(kth, KEY_NEG_INF + 1)))
        above = count_scores(lambda sc: sc > kth_score)
        need = jnp.where(few_keys, float(2 ** 24), float(INDEX_TOPK) - above)

        row = lax.broadcasted_iota(jnp.int32, (PLANE_KEYS, PLANE_KEYS), 0)
        col = lax.broadcasted_iota(jnp.int32, (PLANE_KEYS, PLANE_KEYS), 1)
        prefix_sum = jnp.where(col <= row, 1.0, 0.0).astype(BF16)

        def write_chunk(c, seen):
            parts = []
            for hh in range(blocks_per_chunk):
                k0 = pl.multiple_of(c * ATT_KC + hh * PLANE_KEYS, PLANE_KEYS)
                sc = score_ref[pl.ds(k0, PLANE_KEYS), :]
                tie = sc == kth_score
                rank = seen + jnp.dot(prefix_sum, jnp.where(tie, 1.0, 0.0).astype(BF16),
                                      preferred_element_type=F32)
                parts.append(bias_tile((sc > kth_score) | (tie & (rank <= need))))
                seen = rank[PLANE_KEYS - 1:PLANE_KEYS, :]
            bias_ref[c] = jnp.concatenate(parts, axis=1).astype(bias_ref.dtype)
            return seen

        lax.fori_loop(0, n_chunks, write_chunk, jnp.zeros((1, tq), F32))

    def fill_chunk(c, carry):
        bias_ref[c] = jnp.full(bias_ref.shape[1:], MASK_BIAS, bias_ref.dtype)
        return carry

    lax.fori_loop(n_chunks, n_chunks_total, fill_chunk, 0)


def _indexer_mask(q_idx_t, idx_w_t, k_lo, k_hi):
    s = q_idx_t.shape[1]
    tq = SEL_TQ
    assert tq % PLANE_KEYS == 0 and s % (KEY_BITS * SUBLANES) == 0
    return pl.pallas_call(
        _indexer_body,
        grid=(s // tq,),
        in_specs=[pl.BlockSpec((q_idx_t.shape[0], tq), lambda i: (0, i)),
                  pl.BlockSpec((IDX_HEADS, tq), lambda i: (0, i)),
                  pl.BlockSpec(k_lo.shape, lambda i: (0, 0)),
                  pl.BlockSpec(k_hi.shape, lambda i: (0, 0))],
        out_specs=pl.BlockSpec((s // ATT_KC, tq, ATT_KC), lambda i: (0, i, 0)),
        out_shape=jax.ShapeDtypeStruct((s // ATT_KC, s, ATT_KC), BF16),
        scratch_shapes=[pltpu.VMEM((s, tq), F32),
                        pltpu.VMEM((KEY_BITS, s // KEY_BITS, tq), jnp.int32),
                        pltpu.VMEM((s // KEY_BITS, tq), jnp.int32),
                        pltpu.VMEM((SUBLANES, tq), jnp.int32)],
        compiler_params=_params(("arbitrary",)),
        name="indexer_mask",
    )(q_idx_t, idx_w_t, k_lo, k_hi)


def _attention_body(q_ref, k_ref, v_ref, bias_ref, o_ref, m_ref, acc_ref):
    tq = q_ref.shape[0]
    chunk_keys = bias_ref.shape[2]
    i = pl.program_id(0)
    n_chunks = (i * tq + tq - 1) // chunk_keys + 1

    m_ref[...] = jnp.full(m_ref.shape, M_INIT, F32)
    acc_ref[...] = jnp.zeros(acc_ref.shape, F32)

    def step(first_chunk, width):
        kc = width * chunk_keys
        sub = kc // LANES
        k0 = pl.multiple_of(first_chunk * chunk_keys, chunk_keys)
        bias = jnp.concatenate([bias_ref[first_chunk + c] for c in range(width)], axis=1)
        ones = jnp.ones((kc, HEAD_DIM), v_ref.dtype)
        for h in range(ATTN_HEADS):
            cols = slice(h * HEAD_DIM, (h + 1) * HEAD_DIM)
            s = lax.dot_general(q_ref[:, cols], k_ref[pl.ds(k0, kc), cols],
                                (((1,), (1,)), ((), ())), preferred_element_type=F32)
            sb = s.astype(BF16) + bias
            part = sb[:, :LANES]
            for c in range(1, sub):
                part = jnp.maximum(part, sb[:, c * LANES:(c + 1) * LANES])
            m_prev = m_ref[h]
            m_next = jnp.maximum(m_prev, jnp.max(part.astype(F32), axis=1, keepdims=True))
            p = jnp.exp2(sb - jnp.concatenate([m_next.astype(BF16)] * sub, axis=1))
            alpha = jnp.exp2(m_prev - m_next)
            m_ref[h] = m_next
            pv = jnp.dot(p, jnp.concatenate([v_ref[pl.ds(k0, kc), cols], ones], axis=1),
                         preferred_element_type=F32)
            acc_ref[h] = jnp.concatenate([alpha, alpha], axis=1) * acc_ref[h] + pv

    def wide_step(j, carry):
        step(j * ATT_STEP_CHUNKS, ATT_STEP_CHUNKS)
        return carry

    n_wide = n_chunks // ATT_STEP_CHUNKS
    lax.fori_loop(0, n_wide, wide_step, 0)
    for r in range(1, ATT_STEP_CHUNKS):
        @pl.when(n_chunks % ATT_STEP_CHUNKS >= r)
        def _():
            step(n_wide * ATT_STEP_CHUNKS + r - 1, 1)

    for h in range(ATTN_HEADS):
        acc = acc_ref[h]
        o_ref[:, h * HEAD_DIM:(h + 1) * HEAD_DIM] = (acc[:, :HEAD_DIM] / acc[:, HEAD_DIM:]).astype(o_ref.dtype)


def _attention(q, k, v, bias):
    s, width = q.shape
    tq = SEL_TQ
    n_chunks, _, kc = bias.shape
    resident = dict(pipeline_mode=pl.Buffered(1))
    return pl.pallas_call(
        _attention_body,
        grid=(s // tq,),
        in_specs=[pl.BlockSpec((tq, width), lambda i: (i, 0)),
                  pl.BlockSpec((s, width), lambda i: (0, 0), **resident),
                  pl.BlockSpec((s, width), lambda i: (0, 0), **resident),
                  pl.BlockSpec((n_chunks, tq, kc), lambda i: (0, i, 0))],
        out_specs=pl.BlockSpec((tq, width), lambda i: (i, 0)),
        out_shape=jax.ShapeDtypeStruct((s, width), BF16),
        scratch_shapes=[pltpu.VMEM((ATTN_HEADS, tq, LANES), F32),
                        pltpu.VMEM((ATTN_HEADS, tq, 2 * HEAD_DIM), F32)],
        compiler_params=_params(("arbitrary",)),
        name="masked_attention",
    )(q, k, v, bias)


def _ffn_down_body(a_ref, w_ref, x_ref, g_ref, gate_ref, o_ref):
    f = jnp.dot(a_ref[...], w_ref[...], preferred_element_type=F32)
    o_ref[...] = x_ref[...] + gate_ref[...] * _rms(f, g_ref[...])


def _ffn_down(a, w, x1, g_post, gate):
    m, kdim = a.shape
    d = w.shape[1]
    tm = FFN_DOWN_TM
    row = pl.BlockSpec((1, d), lambda i: (0, 0))
    return pl.pallas_call(
        _ffn_down_body,
        grid=(m // tm,),
        in_specs=[pl.BlockSpec((tm, kdim), lambda i: (i, 0)),
                  pl.BlockSpec((kdim, d), lambda i: (0, 0), pipeline_mode=pl.Buffered(1)),
                  pl.BlockSpec((tm, d), lambda i: (i, 0)), row, row],
        out_specs=pl.BlockSpec((tm, d), lambda i: (i, 0)),
        out_shape=jax.ShapeDtypeStruct((m, d), F32),
        compiler_params=_params(("arbitrary",)),
        name="ffn_down_residual",
    )(a, w, x1, g_post, gate)


def _rope_angles(seq, dim, repeat=1):
    inv = jnp.tile(1.0 / (ROPE_THETA ** (jnp.arange(0, dim, 2, dtype=F32) / dim)), repeat)
    coarse = (jnp.arange(seq // ROPE_BLOCK, dtype=F32) * ROPE_BLOCK)[:, None] * inv[None, :]
    fine = jnp.arange(ROPE_BLOCK, dtype=F32)[:, None] * inv[None, :]
    cos_a, sin_a = jnp.cos(coarse)[:, None, :], jnp.sin(coarse)[:, None, :]
    cos_b, sin_b = jnp.cos(fine)[None, :, :], jnp.sin(fine)[None, :, :]
    cos = (cos_a * cos_b - sin_a * sin_b).reshape(seq, inv.shape[0])
    sin = (sin_a * cos_b + cos_a * sin_b).reshape(seq, inv.shape[0])
    return cos, sin


def _attn_rope_tables(seq):
    cos, sin = _rope_angles(seq, HEAD_DIM, repeat=2)
    sign = jnp.concatenate([-jnp.ones((HEAD_DIM // 2,), F32), jnp.ones((HEAD_DIM // 2,), F32)])
    return cos, sin * sign


def _index_rope_tables(seq):
    cos, sin = _rope_angles(seq, IDX_ROPE_DIM)
    return cos, sin, cos.T, sin.T


def _layer(x, mod, g_pre_mix, g_post_mix, w_in_t, gmlp_ln_g, gmlp_ln_b, gmlp_w_s, gmlp_b_s,
           q_lat_norm_g, w_q_up, w_qidx_up, kidx_ln_g, kidx_ln_b, w_proj_a, w_proj_b, w_out,
           g_pre_ffn, g_post_ffn, w_ffn_gate, w_ffn_up, w_ffn_down, tables):
    s, d = x.shape
    cc, ss, icos, isin, icos_t, isin_t = tables
    row = lambda a: a.reshape(1, -1)
    shift_m, scale_m, gate_m, shift_f, scale_f, gate_f = (mod[:, n * d:(n + 1) * d] for n in range(N_MOD))

    tm = MM_TM
    vec_d = lambda a: (a, (1, d), lambda j, i: (0, 0))
    u, h = _matmul([x], [(w_in_t, 0, False)], [0], [vec_d(row(g_pre_mix)), vec_d(scale_m), vec_d(shift_m)],
                   [((s, GMLP_WIDTH), BF16, (tm, GMLP_WIDTH), _tile), ((s, d), BF16, (tm, d), _rowblk)],
                   _ep_gelu, n=GMLP_WIDTH, tm=tm, tn=GMLP_WIDTH, sub_n=EPILOGUE_SUB_N,
                   sub_m=EPILOGUE_SUB_N, prologue=_pro_prenorm, name="prenorm_in_proj_u")
    in_proj = functools.partial(_simple_mm, h, w_in_t, kn=False)
    vn = in_proj(_ep_gelu_layernorm,
                 [(row(gmlp_ln_g), (1, GMLP_WIDTH), _colblk), (row(gmlp_ln_b), (1, GMLP_WIDTH), _colblk)],
                 BF16, n=GMLP_WIDTH, first=GMLP_WIDTH, tm=tm, tn=GMLP_WIDTH, sub_m=EPILOGUE_SUB_M,
                 name="in_proj_v_ln")
    gmlp_bias = jnp.repeat(gmlp_b_s.T, LANES, axis=1)
    y_a = _gmlp(u, vn, gmlp_w_s, gmlp_bias)

    q_lat = in_proj(_ep_rmsnorm, [(row(q_lat_norm_g), (1, Q_LORA_RANK), _colblk)],
                    BF16, n=Q_LORA_RANK, first=COL_QLAT, tm=tm, tn=Q_LORA_RANK, name="in_proj_qlat")
    q_scale = HEAD_DIM ** -0.5 * math.log2(math.e)
    rope_extras = [(cc, (tm, HEAD_DIM), _rowblk), (ss, (tm, HEAD_DIM), _rowblk)]
    q = _simple_mm(q_lat, w_q_up, functools.partial(_ep_rope, scale=q_scale), rope_extras,
                   BF16, n=ATTN_WIDTH, tm=tm, tn=ATTN_WIDTH, sub_n=EPILOGUE_SUB_N, name="q_up_rope")
    tm2 = MM_TM_WIDE
    kv_tile = ((s, ATTN_WIDTH), BF16, (tm, ATTN_WIDTH), _tile)
    k, v = _matmul([h], [(w_in_t, COL_K, False), (w_in_t, COL_V, False)], [0, 0], rope_extras,
                   [kv_tile, kv_tile], _ep_rope_and_cast, n=ATTN_WIDTH, tm=tm, tn=ATTN_WIDTH,
                   sub_n=EPILOGUE_SUB_N, name="in_proj_k_rope_v")
    half = IDX_ROPE_DIM // 2
    q_idx_t = _simple_mm(q_lat, w_qidx_up, _ep_partial_rope_t,
                         [(icos_t, (half, tm), _colblk_t), (isin_t, (half, tm), _colblk_t)],
                         BF16, n=IDX_HEADS * IDX_HEAD_DIM, tm=tm, tn=QIDX_TN, name="qidx_up_rope",
                         out_t=True)
    pad = lambda a: jnp.pad(a, (0, LANES - a.shape[0])).reshape(1, LANES)
    idx_w_scale = (IDX_HEADS ** -0.5) * (IDX_HEAD_DIM ** -0.5)
    lane_tile = lambda dtype: ((s, LANES), dtype, (tm, LANES), _tile)
    k_lo, k_hi, idx_w_t = _matmul(
        [h], [(w_in_t, COL_KIDX, False)], [0],
        [(pad(kidx_ln_g), (1, LANES), _colblk), (pad(kidx_ln_b), (1, LANES), _colblk),
         (icos, (tm, half), _rowblk), (isin, (tm, half), _rowblk)],
        [lane_tile(BF16), lane_tile(BF16), ((IDX_HEADS, s), F32, (IDX_HEADS, tm), _colblk_t)],
        functools.partial(_ep_index_keys, w_scale=idx_w_scale), n=LANES, tm=tm, tn=LANES,
        name="in_proj_index_keys")
    bias = _indexer_mask(q_idx_t, idx_w_t, k_lo, k_hi)
    y_b = _attention(q, k, v, bias)

    gates = in_proj(_ep_sigmoid, [], BF16, n=2 * d, first=COL_GATE, tm=tm2, tn=MM_TN,
                    sub_n=EPILOGUE_SUB_N, sub_m=tm, name="in_proj_gates")
    n_gate_blocks = d // MM_TN
    merged = _matmul([y_a, y_b], [(w_proj_a, 0, True), (w_proj_b, 0, True)], [0, 1],
                     [(gates, (tm, MM_TN), _tile),
                      (gates, (tm, MM_TN), lambda j, i: (i, j + n_gate_blocks))],
                     [((s, d), BF16, (tm, MM_TN), _tile)], _ep_merge, n=d, tm=tm, tn=MM_TN,
                     sub_n=EPILOGUE_SUB_N, name="branch_merge")[0]
    tm_full = ROWNORM_TM
    vec = lambda a: (a, (1, d), _colblk)
    x1, h2 = _matmul([merged], [(w_out, 0, True)], [0],
                     [(x, (tm_full, d), _rowblk), vec(row(g_post_mix)), vec(gate_m),
                      vec(row(g_pre_ffn)), vec(scale_f), vec(shift_f)],
                     [((s, d), F32, (tm_full, d), _tile), ((s, d), BF16, (tm_full, d), _tile)],
                     _ep_mix_residual, n=d, tm=tm_full, tn=d, sub_m=EPILOGUE_SUB_M, name="out_proj_residual")

    ffn_hidden = w_ffn_gate.shape[1]
    tn_ffn, tm_ffn = FFN_UP_TN, MM_TM_WIDE
    n_row_blocks = s // tm_ffn
    down_rows = ffn_hidden // ((ffn_hidden // tn_ffn) * n_row_blocks)
    down_blk = ((down_rows, d), lambda j, i: (j * n_row_blocks + i, 0))
    act, w_down = _matmul([h2], [(w_ffn_gate, 0, True), (w_ffn_up, 0, True)], [0, 0],
                          [(w_ffn_down,) + down_blk],
                          [((s, ffn_hidden), BF16, (tm_ffn, tn_ffn), _tile), (w_ffn_down.shape, BF16) + down_blk],
                          _ep_swiglu, n=ffn_hidden, tm=tm_ffn, tn=tn_ffn, sub_m=MM_TM, name="ffn_up")
    return _ffn_down(act, w_down, x1, row(g_post_ffn), gate_f)


def kernel(x, c, w_mod, b_mod, g_pre_mix, g_post_mix, w_in, gmlp_ln_g, gmlp_ln_b, gmlp_w_s, gmlp_b_s, q_lat_norm_g, w_q_up, w_qidx_up, kidx_ln_g, kidx_ln_b, w_proj_a, w_proj_b, w_out, g_pre_ffn, g_post_ffn, w_ffn_gate, w_ffn_up, w_ffn_down):
    batch, seq, d = x.shape
    assert batch == 1 and d == D_MODEL
    tables = _attn_rope_tables(seq) + _index_rope_tables(seq)
    y = x[0]
    for l in range(w_mod.shape[0]):
        mod = _modulation(c, w_mod[l], b_mod[l])
        y = _layer(y, mod, g_pre_mix[l], g_post_mix[l], w_in[l].T, gmlp_ln_g[l], gmlp_ln_b[l],
                   gmlp_w_s[l], gmlp_b_s[l], q_lat_norm_g[l], w_q_up[l], w_qidx_up[l],
                   kidx_ln_g[l], kidx_ln_b[l], w_proj_a[l], w_proj_b[l], w_out[l],
                   g_pre_ffn[l], g_post_ffn[l], w_ffn_gate[l], w_ffn_up[l], w_ffn_down[l], tables)
    return y[None]
```
